```python
import jax, jax.numpy as jnp
from jax import lax
import numpy as np

D_MODEL = 1024
BATCH = 8
SEQ = 4096
DEPTH = 4

MLA_HEADS = 8
MLA_NOPE_DIM = 64
MLA_ROPE_DIM = 32
MLA_V_DIM = 64
MLA_Q_RANK = 256
MLA_KV_RANK = 128
ROPE_THETA = 10000.0
FOX_HEADS = 4
FOX_HEAD_DIM = 64
CONV_CHANNELS = 256
CONV_GROUPS = 4
CONV_WIDTH = 31
MLA_WIDTH = MLA_HEADS * MLA_V_DIM
FOX_WIDTH = FOX_HEADS * FOX_HEAD_DIM
MIX_WIDTH = MLA_WIDTH + FOX_WIDTH + CONV_CHANNELS
IN_SIZES = (MLA_Q_RANK, MLA_KV_RANK, MLA_ROPE_DIM, FOX_WIDTH, FOX_WIDTH, FOX_WIDTH, FOX_HEADS, 2 * CONV_CHANNELS)
IN_COLS = sum(IN_SIZES)
D_FF_DENSE = 2816
N_EXPERTS = 8
TOP_K = 2
D_FF_EXPERT = 1408
Q_BLOCK = 128
ALPHA = (2.0 * DEPTH) ** 0.25
BETA = (8.0 * DEPTH) ** -0.25
NORM_EPS = 1e-5

kernel_name = 'hybrid_mla_fox_conformer_moe_deepnorm'


def layer_norm(x, g, b):
    xf = x.astype(jnp.float32)
    mu = jnp.mean(xf, axis=-1, keepdims=True)
    var = jnp.mean(jnp.square(xf - mu), axis=-1, keepdims=True)
    return ((xf - mu) * lax.rsqrt(var + NORM_EPS) * g.astype(jnp.float32) + b.astype(jnp.float32)).astype(x.dtype)


def rms_norm(x, g):
    xf = x.astype(jnp.float32)
    ms = jnp.mean(jnp.square(xf), axis=-1, keepdims=True)
    return (xf * lax.rsqrt(ms + NORM_EPS) * g.astype(jnp.float32)).astype(x.dtype)


def group_norm_per_position(x, g, b):
    bsz, s, c = x.shape
    xf = x.astype(jnp.float32).reshape(bsz, s, CONV_GROUPS, c // CONV_GROUPS)
    mu = jnp.mean(xf, axis=-1, keepdims=True)
    var = jnp.mean(jnp.square(xf - mu), axis=-1, keepdims=True)
    xn = ((xf - mu) * lax.rsqrt(var + NORM_EPS)).reshape(bsz, s, c)
    return (xn * g.astype(jnp.float32) + b.astype(jnp.float32)).astype(x.dtype)


def rope_tables(positions):
    inv_freq = ROPE_THETA ** (-jnp.arange(0, MLA_ROPE_DIM, 2, dtype=jnp.float32) / MLA_ROPE_DIM)
    ang = positions.astype(jnp.float32)[..., None] * inv_freq
    return jnp.cos(ang), jnp.sin(ang)


def apply_rope(x, cos, sin):
    x1, x2 = jnp.split(x.astype(jnp.float32), 2, axis=-1)
    return jnp.concatenate([x1 * cos - x2 * sin, x1 * sin + x2 * cos], axis=-1).astype(x.dtype)


def causal_block_attention(q, k, v, scale, log_decay_cum=None):
    bsz, s, h, _ = q.shape
    n_blocks = s // Q_BLOCK
    kf = k.astype(jnp.float32)
    vf = v.astype(jnp.float32)
    k_pos = jnp.arange(s)
    cum_t = None if log_decay_cum is None else jnp.transpose(log_decay_cum.astype(jnp.float32), (0, 2, 1))

    def one_block(i):
        start = i * Q_BLOCK
        qb = lax.dynamic_slice_in_dim(q, start, Q_BLOCK, axis=1).astype(jnp.float32)
        logits = jnp.einsum('bqhd,bkhd->bhqk', qb, kf) * scale
        if cum_t is not None:
            cq = lax.dynamic_slice_in_dim(cum_t, start, Q_BLOCK, axis=2)
            logits = logits + cq[..., None] - cum_t[:, :, None, :]
        q_pos = start + jnp.arange(Q_BLOCK)
        mask = k_pos[None, :] <= q_pos[:, None]
        logits = jnp.where(mask, logits, -jnp.inf)
        p = jax.nn.softmax(logits, axis=-1)
        return jnp.einsum('bhqk,bkhd->bqhd', p, vf)

    out = lax.map(one_block, jnp.arange(n_blocks))
    out = jnp.moveaxis(out, 0, 1).reshape(bsz, s, h, vf.shape[-1])
    return out.astype(v.dtype)


def hybrid_mixer(x, cos, sin, w_in, mla_q_norm_g, w_uq, mla_kv_norm_g, w_ukv, fox_forget_b,
                 conv_w, conv_b, conv_norm_g, conv_norm_b, mla_out_norm_g, fox_out_norm_g, w_out):
    bsz, s, _ = x.shape
    proj = jnp.einsum('bsd,dc->bsc', x, w_in)
    offsets = np.cumsum(IN_SIZES)[:-1].tolist()
    c_q, c_kv, k_rope, fq, fk, fv, f_logit, conv_in = jnp.split(proj, offsets, axis=-1)

    q = jnp.einsum('bsr,rc->bsc', rms_norm(c_q, mla_q_norm_g), w_uq)
    q = q.reshape(bsz, s, MLA_HEADS, MLA_NOPE_DIM + MLA_ROPE_DIM)
    q_nope, q_rope = q[..., :MLA_NOPE_DIM], q[..., MLA_NOPE_DIM:]
    kv = jnp.einsum('bsr,rc->bsc', rms_norm(c_kv, mla_kv_norm_g), w_ukv)
    kv = kv.reshape(bsz, s, MLA_HEADS, MLA_NOPE_DIM + MLA_V_DIM)
    k_nope, v_mla = kv[..., :MLA_NOPE_DIM], kv[..., MLA_NOPE_DIM:]
    q_rope = apply_rope(q_rope, cos[:, :, None, :], sin[:, :, None, :])
    k_rope = apply_rope(k_rope, cos, sin)
    q_mla = jnp.concatenate([q_nope, q_rope], axis=-1)
    k_mla = jnp.concatenate(
        [k_nope, jnp.broadcast_to(k_rope[:, :, None, :], (bsz, s, MLA_HEADS, MLA_ROPE_DIM))], axis=-1)
    mla = causal_block_attention(q_mla, k_mla, v_mla, (MLA_NOPE_DIM + MLA_ROPE_DIM) ** -0.5)
    mla = mla.reshape(bsz, s, MLA_WIDTH)

    log_f = jax.nn.log_sigmoid(f_logit.astype(jnp.float32) + fox_forget_b.astype(jnp.float32))
    cum = jnp.cumsum(log_f, axis=1)
    fox = causal_block_attention(
        fq.reshape(bsz, s, FOX_HEADS, FOX_HEAD_DIM),
        fk.reshape(bsz, s, FOX_HEADS, FOX_HEAD_DIM),
        fv.reshape(bsz, s, FOX_HEADS, FOX_HEAD_DIM),
        FOX_HEAD_DIM ** -0.5, cum)
    fox = fox.reshape(bsz, s, FOX_WIDTH)

    a, g = jnp.split(conv_in, 2, axis=-1)
    h = a * jax.nn.sigmoid(g)
    h = lax.conv_general_dilated(
        h, conv_w[:, None, :], window_strides=(1,), padding=[(CONV_WIDTH - 1, 0)],
        dimension_numbers=('NWC', 'WIO', 'NWC'), feature_group_count=CONV_CHANNELS) + conv_b
    h = jax.nn.silu(group_norm_per_position(h, conv_norm_g, conv_norm_b))

    mixed = jnp.concatenate([rms_norm(mla, mla_out_norm_g), rms_norm(fox, fox_out_norm_g), h], axis=-1)
    return jnp.einsum('bsc,cd->bsd', mixed, w_out)


def swiglu(x, w1, w3, w2):
    hid = jax.nn.silu(jnp.einsum('bsd,df->bsf', x, w1)) * jnp.einsum('bsd,df->bsf', x, w3)
    return jnp.einsum('bsf,fd->bsd', hid, w2)


def moe_swiglu(x, router_w, w1, w3, w2):
    logits = jnp.einsum('bsd,de->bse', x, router_w).astype(jnp.float32)
    top_val, top_idx = lax.top_k(logits, TOP_K)
    top_w = jax.nn.softmax(top_val, axis=-1)
    gates = jnp.sum(jax.nn.one_hot(top_idx, N_EXPERTS, dtype=jnp.float32) * top_w[..., None], axis=-2)
    y = jnp.zeros_like(x)
    for e in range(N_EXPERTS):
        y = y + gates[..., e:e + 1].astype(x.dtype) * swiglu(x, w1[e], w3[e], w2[e])
    return y


def setup_inputs(seed: int = 0) -> dict:
    key = jax.random.key(seed)
    keys = jax.random.split(key, 40)
    ks = [keys[i] for i in range(40)]
    f32 = jnp.float32
    n_dense = (DEPTH + 1) // 2
    n_moe = DEPTH // 2

    def nrm(shape, scale):
        return jax.random.normal(ks.pop(), shape, f32) * scale

    def gain(shape):
        return 1.0 + 0.02 * jax.random.normal(ks.pop(), shape, f32)

    x = jax.random.normal(ks.pop(), (BATCH, SEQ, D_MODEL), f32)
    positions = jnp.tile(jnp.arange(SEQ, dtype=jnp.int32)[None, :], (BATCH, 1))
    return {
        'x': x,
        'positions': positions,
        'w_in': nrm((DEPTH, D_MODEL, IN_COLS), D_MODEL ** -0.5),
        'mla_q_norm_g': gain((DEPTH, MLA_Q_RANK)),
        'w_uq': nrm((DEPTH, MLA_Q_RANK, MLA_HEADS * (MLA_NOPE_DIM + MLA_ROPE_DIM)), MLA_Q_RANK ** -0.5),
        'mla_kv_norm_g': gain((DEPTH, MLA_KV_RANK)),
        'w_ukv': nrm((DEPTH, MLA_KV_RANK, MLA_HEADS * (MLA_NOPE_DIM + MLA_V_DIM)), MLA_KV_RANK ** -0.5),
        'fox_forget_b': jax.random.uniform(ks.pop(), (DEPTH, FOX_HEADS), f32, minval=1.0, maxval=5.0),
        'conv_w': nrm((DEPTH, CONV_WIDTH, CONV_CHANNELS), CONV_WIDTH ** -0.5),
        'conv_b': nrm((DEPTH, CONV_CHANNELS), 0.02),
        'conv_norm_g': gain((DEPTH, CONV_CHANNELS)),
        'conv_norm_b': nrm((DEPTH, CONV_CHANNELS), 0.02),
        'mla_out_norm_g': gain((DEPTH, MLA_WIDTH)),
        'fox_out_norm_g': gain((DEPTH, FOX_WIDTH)),
        'w_out': nrm((DEPTH, MIX_WIDTH, D_MODEL), BETA * MIX_WIDTH ** -0.5),
        'ln1_g': gain((DEPTH, D_MODEL)),
        'ln1_b': nrm((DEPTH, D_MODEL), 0.02),
        'dense_w1': nrm((n_dense, D_MODEL, D_FF_DENSE), D_MODEL ** -0.5),
        'dense_w3': nrm((n_dense, D_MODEL, D_FF_DENSE), D_MODEL ** -0.5),
        'dense_w2': nrm((n_dense, D_FF_DENSE, D_MODEL), BETA * D_FF_DENSE ** -0.5),
        'router_w': nrm((n_moe, D_MODEL, N_EXPERTS), D_MODEL ** -0.5),
        'expert_w1': nrm((n_moe, N_EXPERTS, D_MODEL, D_FF_EXPERT), D_MODEL ** -0.5),
        'expert_w3': nrm((n_moe, N_EXPERTS, D_MODEL, D_FF_EXPERT), D_MODEL ** -0.5),
        'expert_w2': nrm((n_moe, N_EXPERTS, D_FF_EXPERT, D_MODEL), BETA * D_FF_EXPERT ** -0.5),
        'ln2_g': gain((DEPTH, D_MODEL)),
        'ln2_b': nrm((DEPTH, D_MODEL), 0.02),
    }


def reference(x, positions, w_in, mla_q_norm_g, w_uq, mla_kv_norm_g, w_ukv, fox_forget_b,
              conv_w, conv_b, conv_norm_g, conv_norm_b, mla_out_norm_g, fox_out_norm_g, w_out,
              ln1_g, ln1_b, dense_w1, dense_w3, dense_w2, router_w, expert_w1, expert_w3,
              expert_w2, ln2_g, ln2_b):
    cos, sin = rope_tables(positions)
    for layer in range(DEPTH):
        mix = hybrid_mixer(x, cos, sin, w_in[layer], mla_q_norm_g[layer], w_uq[layer],
                           mla_kv_norm_g[layer], w_ukv[layer], fox_forget_b[layer],
                           conv_w[layer], conv_b[layer], conv_norm_g[layer], conv_norm_b[layer],
                           mla_out_norm_g[layer], fox_out_norm_g[layer], w_out[layer])
        x = layer_norm(ALPHA * x + mix, ln1_g[layer], ln1_b[layer])
        j = layer // 2
        if layer % 2 == 0:
            ff = swiglu(x, dense_w1[j], dense_w3[j], dense_w2[j])
        else:
            ff = moe_swiglu(x, router_w[j], expert_w1[j], expert_w3[j], expert_w2[j])
        x = layer_norm(ALPHA * x + ff, ln2_g[layer], ln2_b[layer])
    return x
```

```python
import functools
import math

import numpy as np
import jax
import jax.numpy as jnp
from jax import lax
from jax.experimental import pallas as pl
from jax.experimental.pallas import tpu as pltpu

F32 = jnp.float32
BF16 = jnp.bfloat16

D_MODEL = 1024
DEPTH = 4
MLA_HEADS = 8
MLA_NOPE = 64
MLA_ROPE = 32
MLA_V = 64
MLA_Q_RANK = 256
MLA_KV_RANK = 128
ROPE_THETA = 10000.0
FOX_HEADS = 4
FOX_DIM = 64
CONV_CH = 256
CONV_GROUPS = 4
CONV_WIDTH = 31
MLA_WIDTH = MLA_HEADS * MLA_V
FOX_WIDTH = FOX_HEADS * FOX_DIM
N_EXPERTS = 8
ALPHA = (2.0 * DEPTH) ** 0.25
NORM_EPS = 1e-5
LOG2E = math.log2(math.e)

HEAD_PAD = 128
N_HEADS = MLA_HEADS + FOX_HEADS
V_DIM = 64
CONV_HALO = 32
VMEM_LIMIT = 56 * 1024 * 1024

TQ = 256
TM_IN = 512
TM_OUT = 512
TM_FFN = 512


def _nt_dot(a, b):
    return lax.dot_general(a, b, (((1,), (1,)), ((), ())), preferred_element_type=F32)


def _dot(a, b):
    return jnp.dot(a, b, preferred_element_type=F32)


def _split2_dot(a, m_bf16):
    hi = a.astype(BF16)
    lo = (a - hi.astype(F32)).astype(BF16)
    return _dot(hi, m_bf16) + _dot(lo, m_bf16)


def _split3(a):
    hi = a.astype(BF16).astype(F32)
    r1 = a - hi
    mid = r1.astype(BF16).astype(F32)
    lo = (r1 - mid).astype(BF16).astype(F32)
    return hi, mid, lo


def _const_spec(shape):
    nd = len(shape)
    return pl.BlockSpec(shape, lambda *_: (0,) * nd, pipeline_mode=pl.Buffered(1))


def _rope_kernel(pos_ref, invf_ref, c_ref, s_ref, ct_ref, st_ref):
    pos = pos_ref[...].astype(F32)
    ang = invf_ref[...] * pos
    cos = jnp.cos(ang)
    sin = jnp.sin(ang)
    tn = pos.shape[1]
    ct = jnp.concatenate([jnp.ones((MLA_NOPE, tn), F32), cos, cos, jnp.zeros((32, tn), F32)], axis=0)
    st = jnp.concatenate([jnp.zeros((MLA_NOPE, tn), F32), sin, sin, jnp.zeros((32, tn), F32)], axis=0)
    ct_ref[...] = ct
    st_ref[...] = st
    c_ref[...] = ct.T
    s_ref[...] = st.T


def _rope_tables(positions):
    n = positions.size
    tn = min(512, n)
    inv_freq = ROPE_THETA ** (-jnp.arange(0, MLA_ROPE, 2, dtype=F32) / MLA_ROPE)
    return pl.pallas_call(
        _rope_kernel,
        grid=(n // tn,),
        in_specs=[pl.BlockSpec((1, tn), lambda i: (0, i)),
                  pl.BlockSpec((MLA_ROPE // 2, 1), lambda i: (0, 0))],
        out_specs=[pl.BlockSpec((tn, HEAD_PAD), lambda i: (i, 0)),
                   pl.BlockSpec((tn, HEAD_PAD), lambda i: (i, 0)),
                   pl.BlockSpec((HEAD_PAD, tn), lambda i: (0, i)),
                   pl.BlockSpec((HEAD_PAD, tn), lambda i: (0, i))],
        out_shape=[jax.ShapeDtypeStruct((n, HEAD_PAD), F32),
                   jax.ShapeDtypeStruct((n, HEAD_PAD), F32),
                   jax.ShapeDtypeStruct((HEAD_PAD, n), F32),
                   jax.ShapeDtypeStruct((HEAD_PAD, n), F32)],
        name="rope_tables",
    )(positions.reshape(1, n), inv_freq.reshape(-1, 1))


_A_CQ = 0
_A_CKV = _A_CQ + MLA_Q_RANK
_A_KR = _A_CKV + MLA_KV_RANK
_A_KRR = _A_KR + HEAD_PAD
_A_FK = _A_KRR + HEAD_PAD
_A_CA = _A_FK + FOX_HEADS * HEAD_PAD
_A_CG = _A_CA + CONV_CH
_A_COLS = _A_CG + CONV_CH
_AUG_ROWS = 8
_F_ROWS = 16


def _rms(x, g):
    ms = jnp.mean(jnp.square(x), axis=-1, keepdims=True)
    return x * lax.rsqrt(ms + NORM_EPS) * g


def _inproj_kernel(x_ref, c_ref, s_ref, ct_ref, st_ref, wa_ref, wfq_ref, wfv_ref, wf_ref, fb_ref,
                   gq_ref, wuq_ref, wuqr_ref, gkv_ref, wuk_ref, wuv_ref,
                   cw_ref, cb_ref, cng_ref, cnb_ref, gmat_ref,
                   qt_ref, k_ref, vt_ref, hc_ref,
                   hbuf, cbuf, fcarry, *, tiles_per_seq, tm):
    i = pl.program_id(0)

    @pl.when(i % tiles_per_seq == 0)
    def _():
        hbuf[0:CONV_HALO, :] = jnp.zeros((CONV_HALO, CONV_CH), F32)
        fcarry[...] = jnp.zeros_like(fcarry)

    xb = x_ref[...].astype(BF16)
    p1 = _dot(xb, wa_ref[...])
    cos_t = c_ref[...]
    sin_t = s_ref[...]
    cos_tt = ct_ref[...]
    sin_tt = st_ref[...]

    cqn = _rms(p1[:, _A_CQ:_A_CQ + MLA_Q_RANK], gq_ref[...]).astype(BF16)
    q_t = _nt_dot(wuq_ref[...], cqn)
    q_rot_t = _nt_dot(wuqr_ref[...], cqn)
    mla_scale = (MLA_NOPE + MLA_ROPE) ** -0.5 * LOG2E
    for h in range(MLA_HEADS):
        rows = slice(h * HEAD_PAD, (h + 1) * HEAD_PAD)
        qh = (q_t[rows, :] * cos_tt + q_rot_t[rows, :] * sin_tt) * mla_scale
        for c in range(tm // TQ):
            qt_ref[c, rows, :] = qh[:, c * TQ:(c + 1) * TQ].astype(BF16)

    ckvn = _rms(p1[:, _A_CKV:_A_CKV + MLA_KV_RANK], gkv_ref[...]).astype(BF16)
    k_nope = _dot(ckvn, wuk_ref[...])
    k_rope = p1[:, _A_KR:_A_KR + HEAD_PAD] * cos_t + p1[:, _A_KRR:_A_KRR + HEAD_PAD] * sin_t
    for h in range(MLA_HEADS):
        cols = slice(h * HEAD_PAD, (h + 1) * HEAD_PAD)
        k_ref[:, cols] = (k_nope[:, cols] + k_rope).astype(BF16)
    v_t = _nt_dot(wuv_ref[...], ckvn)
    fv_t = _nt_dot(wfv_ref[...], xb)
    for c in range(tm // TQ):
        cs = slice(c * TQ, (c + 1) * TQ)
        vt_ref[c, 0:MLA_WIDTH, :] = v_t[:, cs].astype(BF16)
        vt_ref[c, MLA_WIDTH:MLA_WIDTH + FOX_WIDTH, :] = fv_t[:, cs].astype(BF16)

    z = _nt_dot(wf_ref[...], xb) + fb_ref[...]
    logf = (jnp.minimum(z, 0.0) - jnp.log1p(jnp.exp(-jnp.abs(z)))) * LOG2E
    r_i = lax.broadcasted_iota(jnp.int32, (tm, tm), 0)
    c_i = lax.broadcasted_iota(jnp.int32, (tm, tm), 1)
    upper = jnp.where(r_i <= c_i, 1.0, 0.0).astype(BF16)
    l_hi, l_mid, l_lo = _split3(logf)
    cum = (_dot(l_hi.astype(BF16), upper) + _dot(l_mid.astype(BF16), upper)
           + _dot(l_lo.astype(BF16), upper))
    f_cum = cum + fcarry[:, 0:1]
    fcarry[...] = jnp.broadcast_to(f_cum[:, tm - 1:tm], fcarry.shape)
    f_hi, f_mid, f_lo = _split3(f_cum)

    fq_t = _nt_dot(wfq_ref[...], xb)
    row8 = lax.broadcasted_iota(jnp.int32, (_AUG_ROWS, tm), 0)
    fox_scale = FOX_DIM ** -0.5 * LOG2E
    for h in range(FOX_HEADS):
        bh = lambda a: jnp.broadcast_to(a[h:h + 1, :], (_AUG_ROWS, tm))
        aug_q = jnp.where(row8 == 0, bh(f_hi), jnp.where(row8 == 1, bh(f_mid), jnp.where(
            row8 == 2, bh(f_lo), jnp.where(row8 < 6, 1.0, 0.0))))
        aug_k = jnp.where(row8 < 3, 1.0, jnp.where(row8 == 3, -bh(f_hi), jnp.where(
            row8 == 4, -bh(f_mid), jnp.where(row8 == 5, -bh(f_lo), 0.0))))
        pad = jnp.zeros((HEAD_PAD - FOX_DIM - _AUG_ROWS, tm), F32)
        qh = jnp.concatenate(
            [fq_t[h * HEAD_PAD:h * HEAD_PAD + FOX_DIM, :] * fox_scale, aug_q, pad], axis=0)
        rows = slice((MLA_HEADS + h) * HEAD_PAD, (MLA_HEADS + h + 1) * HEAD_PAD)
        for c in range(tm // TQ):
            qt_ref[c, rows, :] = qh[:, c * TQ:(c + 1) * TQ].astype(BF16)
        kaug_t = jnp.concatenate([jnp.zeros((FOX_DIM, tm), F32), aug_k, pad], axis=0)
        fk = p1[:, _A_FK + h * HEAD_PAD:_A_FK + (h + 1) * HEAD_PAD]
        k_ref[:, rows] = (fk + kaug_t.T).astype(BF16)

    a = p1[:, _A_CA:_A_CA + CONV_CH]
    g = p1[:, _A_CG:_A_CG + CONV_CH]
    hbuf[CONV_HALO:CONV_HALO + tm, :] = a * jax.nn.sigmoid(g)
    chunk = 64
    first = CONV_HALO - (CONV_WIDTH - 1)
    for c0 in range(0, tm, chunk):
        acc = jnp.zeros((chunk, CONV_CH), F32)
        for r in range(8):
            offs = [o for o in range(first, first + CONV_WIDTH) if o % 8 == r]
            seg = hbuf[c0 + r:c0 + max(offs) + chunk, :]
            for o in offs:
                j = o - first
                a8 = o - r
                acc = acc + cw_ref[j:j + 1, :] * seg[a8:a8 + chunk, :]
        cbuf[c0:c0 + chunk, :] = acc
    hbuf[0:CONV_HALO, :] = hbuf[tm:tm + CONV_HALO, :]
    hv = cbuf[...] + cb_ref[...]
    gm = gmat_ref[...]
    mu = _split2_dot(hv, gm)
    d = hv - mu
    var = _split2_dot(d * d, gm)
    hn = d * lax.rsqrt(var + NORM_EPS) * cng_ref[...] + cnb_ref[...]
    hc_ref[...] = (hn * jax.nn.sigmoid(hn)).astype(BF16)


def _prep_inproj_weights(w_in, w_uq, w_ukv, fox_forget_b):
    o = np.cumsum((0, MLA_Q_RANK, MLA_KV_RANK, MLA_ROPE, FOX_WIDTH, FOX_WIDTH, FOX_WIDTH, FOX_HEADS,
                   2 * CONV_CH))
    w_cq, w_ckv, w_kr, w_fq, w_fk, w_fv, w_f, w_cv = (w_in[:, o[i]:o[i + 1]] for i in range(8))
    d = w_in.shape[0]
    half = MLA_ROPE // 2

    def rot_cols(w):
        return jnp.concatenate([-w[..., half:], w[..., :half]], axis=-1)

    def rope_block(w):
        return jnp.pad(w, ((0, 0), (MLA_NOPE, HEAD_PAD - MLA_NOPE - MLA_ROPE)))

    w_fk_pad = jnp.pad(w_fk.reshape(d, FOX_HEADS, FOX_DIM), ((0, 0), (0, 0), (0, HEAD_PAD - FOX_DIM)))
    wa = jnp.concatenate([w_cq, w_ckv, rope_block(w_kr), rope_block(rot_cols(w_kr)),
                          w_fk_pad.reshape(d, FOX_HEADS * HEAD_PAD), w_cv], axis=1)
    w_fq_pad = jnp.pad(w_fq.reshape(d, FOX_HEADS, FOX_DIM), ((0, 0), (0, 0), (0, HEAD_PAD - FOX_DIM)))
    wfq_t = w_fq_pad.reshape(d, FOX_HEADS * HEAD_PAD).T
    wfv_t = w_fv.T
    wf_t = jnp.pad(w_f, ((0, 0), (0, _F_ROWS - FOX_HEADS))).T
    fb = jnp.pad(fox_forget_b, (0, _F_ROWS - FOX_HEADS)).reshape(_F_ROWS, 1)

    uq = w_uq.reshape(MLA_Q_RANK, MLA_HEADS, MLA_NOPE + MLA_ROPE)
    uq_nope, uq_rope = uq[..., :MLA_NOPE], uq[..., MLA_NOPE:]
    tail = ((0, 0), (0, 0), (0, HEAD_PAD - MLA_NOPE - MLA_ROPE))
    uq_pad = jnp.pad(jnp.concatenate([uq_nope, uq_rope], axis=-1), tail)
    uq_rot_pad = jnp.pad(jnp.concatenate([jnp.zeros_like(uq_nope), rot_cols(uq_rope)], axis=-1), tail)
    wuq_t = uq_pad.reshape(MLA_Q_RANK, MLA_HEADS * HEAD_PAD).T
    wuqr_t = uq_rot_pad.reshape(MLA_Q_RANK, MLA_HEADS * HEAD_PAD).T
    ukv = w_ukv.reshape(MLA_KV_RANK, MLA_HEADS, MLA_NOPE + MLA_V)
    wuk = jnp.pad(ukv[..., :MLA_NOPE], ((0, 0), (0, 0), (0, HEAD_PAD - MLA_NOPE))).reshape(
        MLA_KV_RANK, MLA_HEADS * HEAD_PAD)
    wuv_t = ukv[..., MLA_NOPE:].reshape(MLA_KV_RANK, MLA_WIDTH).T
    bf = lambda a: a.astype(BF16)
    return dict(wa=bf(wa), wfq=bf(wfq_t), wfv=bf(wfv_t), wf=bf(wf_t), fb=fb, wuq=bf(wuq_t),
                wuqr=bf(wuqr_t), wuk=bf(wuk), wuv=bf(wuv_t))


def _input_projection(x2d, tabs, pw, gq, gkv, conv_w, conv_b, conv_ng, conv_nb, seq):
    n, d = x2d.shape
    tm = min(TM_IN, seq)
    cos_t, sin_t, cos_tt, sin_tt = tabs
    gidx = np.arange(CONV_CH) // (CONV_CH // CONV_GROUPS)
    gmat = jnp.asarray((gidx[:, None] == gidx[None, :]) / (CONV_CH // CONV_GROUPS), BF16)
    cw = jnp.pad(conv_w, ((0, 32 - CONV_WIDTH), (0, 0)))
    row = lambda a: a.reshape(1, -1)
    tok = lambda w: pl.BlockSpec((tm, w), lambda i: (i, 0))
    tok_t = lambda r: pl.BlockSpec((r, tm), lambda i: (0, i))
    consts = [pw["wa"], pw["wfq"], pw["wfv"], pw["wf"], pw["fb"], row(gq), pw["wuq"], pw["wuqr"],
              row(gkv), pw["wuk"], pw["wuv"], cw, row(conv_b), row(conv_ng), row(conv_nb), gmat]
    kern = functools.partial(_inproj_kernel, tiles_per_seq=seq // tm, tm=tm)
    return pl.pallas_call(
        kern,
        grid=(n // tm,),
        in_specs=[tok(d), tok(HEAD_PAD), tok(HEAD_PAD), tok_t(HEAD_PAD), tok_t(HEAD_PAD)]
        + [_const_spec(c.shape) for c in consts],
        out_specs=[pl.BlockSpec((tm // TQ, N_HEADS * HEAD_PAD, TQ), lambda i: (i, 0, 0)),
                   tok(N_HEADS * HEAD_PAD),
                   pl.BlockSpec((tm // TQ, N_HEADS * V_DIM, TQ), lambda i: (i, 0, 0)),
                   tok(CONV_CH)],
        out_shape=[jax.ShapeDtypeStruct((n // TQ, N_HEADS * HEAD_PAD, TQ), BF16),
                   jax.ShapeDtypeStruct((n, N_HEADS * HEAD_PAD), BF16),
                   jax.ShapeDtypeStruct((n // TQ, N_HEADS * V_DIM, TQ), BF16),
                   jax.ShapeDtypeStruct((n, CONV_CH), BF16)],
        scratch_shapes=[pltpu.VMEM((CONV_HALO + tm, CONV_CH), F32),
                        pltpu.VMEM((tm, CONV_CH), F32),
                        pltpu.VMEM((_F_ROWS, 128), F32)],
        compiler_params=pltpu.CompilerParams(dimension_semantics=("arbitrary",),
                                             vmem_limit_bytes=VMEM_LIMIT),
        name="input_projection",
    )(x2d, cos_t, sin_t, cos_tt, sin_tt, *consts)


def _attn_kernel(qt_ref, k_ref, vt_ref, o_ref, *, n_tiles):
    r_i = lax.broadcasted_iota(jnp.int32, (TQ, TQ), 0)
    c_i = lax.broadcasted_iota(jnp.int32, (TQ, TQ), 1)
    causal = r_i <= c_i

    def q_tile(i, carry):
        q_t = [qt_ref[i, h * HEAD_PAD:(h + 1) * HEAD_PAD, :] for h in range(2)]

        def block(j, state, masked):
            new = []
            row0 = pl.multiple_of(j * TQ, TQ)
            for h in range(2):
                m, l, acc = state[h]
                kb = k_ref[pl.ds(row0, TQ), h * HEAD_PAD:(h + 1) * HEAD_PAD]
                s = _dot(kb, q_t[h])
                if masked:
                    s = jnp.where(causal, s, -jnp.inf)
                m_new = jnp.maximum(m, jnp.max(s, axis=0, keepdims=True))
                alpha = jnp.exp2(m - m_new)
                p = jnp.exp2(s - m_new)
                l = alpha * l + jnp.sum(p, axis=0, keepdims=True)
                vb = vt_ref[j, h * V_DIM:(h + 1) * V_DIM, :]
                acc = alpha * acc + _dot(vb, p.astype(BF16))
                new.append((m_new, l, acc))
            return tuple(new)

        init = tuple((jnp.full((1, TQ), -1e30, F32), jnp.zeros((1, TQ), F32),
                      jnp.zeros((V_DIM, TQ), F32)) for _ in range(2))
        state = lax.fori_loop(0, i, lambda j, st: block(j, st, False), init)
        state = block(i, state, True)
        out_t = jnp.concatenate([acc / l for (_, l, acc) in state], axis=0)
        o_ref[pl.ds(pl.multiple_of(i * TQ, TQ), TQ), :] = out_t.T.astype(o_ref.dtype)
        return carry

    lax.fori_loop(0, n_tiles, q_tile, 0)


def _attention(q_t, k, v_t, batch, seq):
    n = k.shape[0]
    n_tiles = seq // TQ
    pairs = N_HEADS // 2
    return pl.pallas_call(
        functools.partial(_attn_kernel, n_tiles=n_tiles),
        grid=(batch, pairs),
        in_specs=[pl.BlockSpec((n_tiles, 2 * HEAD_PAD, TQ), lambda b, p: (b, p, 0)),
                  pl.BlockSpec((seq, 2 * HEAD_PAD), lambda b, p: (b, p)),
                  pl.BlockSpec((n_tiles, 2 * V_DIM, TQ), lambda b, p: (b, p, 0))],
        out_specs=pl.BlockSpec((seq, 2 * V_DIM), lambda b, p: (b, p)),
        out_shape=jax.ShapeDtypeStruct((n, N_HEADS * V_DIM), BF16),
        compiler_params=pltpu.CompilerParams(dimension_semantics=("arbitrary", "arbitrary"),
                                             vmem_limit_bytes=VMEM_LIMIT),
        name="attention",
    )(q_t, k, v_t)


def _layer_norm(x, g, b):
    mu = jnp.mean(x, axis=-1, keepdims=True)
    d = x - mu
    var = jnp.mean(jnp.square(d), axis=-1, keepdims=True)
    return d * lax.rsqrt(var + NORM_EPS) * g + b


def _outproj_kernel(o_ref, hc_ref, x_ref, gm_ref, gf_ref, wo_ref, g1_ref, b1_ref, *rest, with_router):
    if with_router:
        rw_ref, x1_ref, gates_ref = rest
    else:
        (x1_ref,) = rest
    o = o_ref[...].astype(F32)
    mla = _rms(o[:, :MLA_WIDTH], gm_ref[...])
    fox = _rms(o[:, MLA_WIDTH:], gf_ref[...])
    mixed = jnp.concatenate([mla.astype(BF16), fox.astype(BF16), hc_ref[...]], axis=-1)
    y = _dot(mixed, wo_ref[...])
    x1 = _layer_norm(ALPHA * x_ref[...] + y, g1_ref[...], b1_ref[...])
    x1_ref[...] = x1
    if with_router:
        rw = rw_ref[...]
        x_hi = x1.astype(BF16)
        x_lo = (x1 - x_hi.astype(F32)).astype(BF16)
        w_hi = rw.astype(BF16)
        w_lo = (rw - w_hi.astype(F32)).astype(BF16)
        logits = _dot(x_hi, w_hi) + (_dot(x_lo, w_hi) + _dot(x_hi, w_lo))
        lane = lax.broadcasted_iota(jnp.int32, logits.shape, 1)
        logits = jnp.where(lane < N_EXPERTS, logits, -jnp.inf)
        v1 = jnp.max(logits, axis=-1, keepdims=True)
        i1 = jnp.min(jnp.where(logits == v1, lane, 128), axis=-1, keepdims=True)
        rest_l = jnp.where(lane == i1, -jnp.inf, logits)
        v2 = jnp.max(rest_l, axis=-1, keepdims=True)
        i2 = jnp.min(jnp.where(rest_l == v2, lane, 128), axis=-1, keepdims=True)
        e2 = jnp.exp(v2 - v1)
        den = 1.0 + e2
        gates_ref[...] = jnp.where(lane == i1, 1.0 / den, jnp.where(lane == i2, e2 / den, 0.0))


def _output_projection(o, hc, x2d, gm, gf, w_out, g1, b1, router_w=None):
    n, d = x2d.shape
    tm = min(TM_OUT, n)
    row = lambda a: a.reshape(1, -1)
    tok = lambda w: pl.BlockSpec((tm, w), lambda i: (i, 0))
    consts = [row(gm), row(gf), w_out.astype(BF16), row(g1), row(b1)]
    out_specs = [tok(d)]
    out_shape = [jax.ShapeDtypeStruct((n, d), F32)]
    if router_w is not None:
        consts.append(jnp.pad(router_w, ((0, 0), (0, 128 - N_EXPERTS))))
        out_specs.append(tok(128))
        out_shape.append(jax.ShapeDtypeStruct((n, 128), F32))
    return pl.pallas_call(
        functools.partial(_outproj_kernel, with_router=router_w is not None),
        grid=(n // tm,),
        in_specs=[tok(o.shape[1]), tok(CONV_CH), tok(d)] + [_const_spec(c.shape) for c in consts],
        out_specs=out_specs,
        out_shape=out_shape,
        compiler_params=pltpu.CompilerParams(dimension_semantics=("arbitrary",),
                                             vmem_limit_bytes=VMEM_LIMIT),
        name="output_projection",
    )(o, hc, x2d, *consts)


def _swiglu_tile(xb, w1, w3, w2):
    h1 = _dot(xb, w1)
    h3 = _dot(xb, w3)
    hid = (h1 * jax.nn.sigmoid(h1) * h3).astype(BF16)
    return _dot(hid, w2)


def _dense_ffn_kernel(x_ref, w1_ref, w3_ref, w2_ref, g_ref, b_ref, o_ref, *, f_chunk):
    x = x_ref[...]
    xb = x.astype(BF16)
    ff = None
    for c0 in range(0, w1_ref.shape[1], f_chunk):
        part = _swiglu_tile(xb, w1_ref[:, c0:c0 + f_chunk], w3_ref[:, c0:c0 + f_chunk],
                            w2_ref[c0:c0 + f_chunk, :])
        ff = part if ff is None else ff + part
    o_ref[...] = _layer_norm(ALPHA * x + ff, g_ref[...], b_ref[...])


def _dense_ffn(x2d, w1, w3, w2, g, b):
    n, d = x2d.shape
    tm = min(TM_FFN, n)
    f = w1.shape[1]
    f_chunk = f // 2 if (f // 2) % 128 == 0 else f
    row = lambda a: a.reshape(1, -1)
    consts = [w1.astype(BF16), w3.astype(BF16), w2.astype(BF16), row(g), row(b)]
    return pl.pallas_call(
        functools.partial(_dense_ffn_kernel, f_chunk=f_chunk),
        grid=(n // tm,),
        in_specs=[pl.BlockSpec((tm, d), lambda i: (i, 0))] + [_const_spec(c.shape) for c in consts],
        out_specs=pl.BlockSpec((tm, d), lambda i: (i, 0)),
        out_shape=jax.ShapeDtypeStruct((n, d), F32),
        compiler_params=pltpu.CompilerParams(dimension_semantics=("arbitrary",),
                                             vmem_limit_bytes=VMEM_LIMIT),
        name="dense_ffn",
    )(x2d, *consts)


def _moe_kernel(x_ref, gates_ref, w1_ref, w3_ref, w2_ref, g_ref, b_ref, o_ref, acc_ref):
    e = pl.program_id(1)

    @pl.when(e == 0)
    def _():
        acc_ref[...] = jnp.zeros_like(acc_ref)

    gates = gates_ref[...]
    lane = lax.broadcasted_iota(jnp.int32, gates.shape, 1)
    gate = jnp.sum(jnp.where(lane == e, gates, 0.0), axis=-1, keepdims=True)
    y = _swiglu_tile(x_ref[...].astype(BF16), w1_ref[0], w3_ref[0], w2_ref[0])
    acc_ref[...] += gate * y

    @pl.when(e == pl.num_programs(1) - 1)
    def _():
        o_ref[...] = _layer_norm(ALPHA * x_ref[...] + acc_ref[...], g_ref[...], b_ref[...])


def _moe_ffn(x2d, gates, w1, w3, w2, g, b):
    n, d = x2d.shape
    tm = min(TM_FFN, n)
    n_exp, _, f = w1.shape
    row = lambda a: a.reshape(1, -1)
    return pl.pallas_call(
        _moe_kernel,
        grid=(n // tm, n_exp),
        in_specs=[pl.BlockSpec((tm, d), lambda i, e: (i, 0)),
                  pl.BlockSpec((tm, 128), lambda i, e: (i, 0)),
                  pl.BlockSpec((1, d, f), lambda i, e: (e, 0, 0)),
                  pl.BlockSpec((1, d, f), lambda i, e: (e, 0, 0)),
                  pl.BlockSpec((1, f, d), lambda i, e: (e, 0, 0)),
                  _const_spec((1, d)), _const_spec((1, d))],
        out_specs=pl.BlockSpec((tm, d), lambda i, e: (i, 0)),
        out_shape=jax.ShapeDtypeStruct((n, d), F32),
        scratch_shapes=[pltpu.VMEM((tm, d), F32)],
        compiler_params=pltpu.CompilerParams(dimension_semantics=("arbitrary", "arbitrary"),
                                             vmem_limit_bytes=VMEM_LIMIT),
        name="moe_ffn",
    )(x2d, gates, w1.astype(BF16), w3.astype(BF16), w2.astype(BF16), row(g), row(b))


def kernel(x, positions, w_in, mla_q_norm_g, w_uq, mla_kv_norm_g, w_ukv, fox_forget_b, conv_w, conv_b,
           conv_norm_g, conv_norm_b, mla_out_norm_g, fox_out_norm_g, w_out, ln1_g, ln1_b, dense_w1,
           dense_w3, dense_w2, router_w, expert_w1, expert_w3, expert_w2, ln2_g, ln2_b):
    batch, seq, d = x.shape
    assert d == D_MODEL and seq % TQ == 0 and seq % min(TM_IN, seq) == 0
    depth = w_in.shape[0]
    tabs = _rope_tables(positions)
    h = x.reshape(batch * seq, d)
    for layer in range(depth):
        pw = _prep_inproj_weights(w_in[layer], w_uq[layer], w_ukv[layer], fox_forget_b[layer])
        q_t, k, v_t, hc = _input_projection(
            h, tabs, pw, mla_q_norm_g[layer], mla_kv_norm_g[layer], conv_w[layer], conv_b[layer],
            conv_norm_g[layer], conv_norm_b[layer], seq)
        o = _attention(q_t, k, v_t, batch, seq)
        j = layer // 2
        if layer % 2 == 0:
            h = _output_projection(o, hc, h, mla_out_norm_g[layer], fox_out_norm_g[layer], w_out[layer],
                                   ln1_g[layer], ln1_b[layer])[0]
            h = _dense_ffn(h, dense_w1[j], dense_w3[j], dense_w2[j], ln2_g[layer], ln2_b[layer])
        else:
            h, gates = _output_projection(o, hc, h, mla_out_norm_g[layer], fox_out_norm_g[layer],
                                          w_out[layer], ln1_g[layer], ln1_b[layer], router_w[j])
            h = _moe_ffn(h, gates, expert_w1[j], expert_w3[j], expert_w2[j], ln2_g[layer], ln2_b[layer])
    return h.reshape(batch, seq, d)
```

```python
import functools
import math

import numpy as np
import jax
import jax.numpy as jnp
from jax import lax
from jax.experimental import pallas as pl
from jax.experimental.pallas import tpu as pltpu

F32 = jnp.float32
BF16 = jnp.bfloat16

D_MODEL = 1024
DEPTH = 4
MLA_HEADS = 8
MLA_NOPE = 64
MLA_ROPE = 32
MLA_V = 64
MLA_Q_RANK = 256
MLA_KV_RANK = 128
ROPE_THETA = 10000.0
FOX_HEADS = 4
FOX_DIM = 64
CONV_CH = 256
CONV_GROUPS = 4
CONV_WIDTH = 31
MLA_WIDTH = MLA_HEADS * MLA_V
FOX_WIDTH = FOX_HEADS * FOX_DIM
N_EXPERTS = 8
ALPHA = (2.0 * DEPTH) ** 0.25
NORM_EPS = 1e-5
LOG2E = math.log2(math.e)

HEAD_PAD = 128
N_HEADS = MLA_HEADS + FOX_HEADS
V_DIM = 64
CONV_HALO = 32
VMEM_LIMIT = 56 * 1024 * 1024

TQ = 512
TK = 256
TM_IN = 512
TM_OUT = 512
TM_FFN = 512


def _nt_dot(a, b):
    return lax.dot_general(a, b, (((1,), (1,)), ((), ())), preferred_element_type=F32)


def _dot(a, b):
    return jnp.dot(a, b, preferred_element_type=F32)


def _split2_dot(a, m_bf16):
    hi = a.astype(BF16)
    lo = (a - hi.astype(F32)).astype(BF16)
    return _dot(hi, m_bf16) + _dot(lo, m_bf16)


def _split3(a):
    hi = a.astype(BF16).astype(F32)
    r1 = a - hi
    mid = r1.astype(BF16).astype(F32)
    lo = (r1 - mid).astype(BF16).astype(F32)
    return hi, mid, lo


def _const_spec(shape):
    nd = len(shape)
    return pl.BlockSpec(shape, lambda *_: (0,) * nd, pipeline_mode=pl.Buffered(1))


def _rope_kernel(pos_ref, invf_ref, c_ref, s_ref, ct_ref, st_ref):
    pos = pos_ref[...].astype(F32)
    ang = invf_ref[...] * pos
    cos = jnp.cos(ang)
    sin = jnp.sin(ang)
    tn = pos.shape[1]
    ct = jnp.concatenate([jnp.ones((MLA_NOPE, tn), F32), cos, cos, jnp.zeros((32, tn), F32)], axis=0)
    st = jnp.concatenate([jnp.zeros((MLA_NOPE, tn), F32), sin, sin, jnp.zeros((32, tn), F32)], axis=0)
    ct_ref[...] = ct
    st_ref[...] = st
    c_ref[...] = ct.T
    s_ref[...] = st.T


def _rope_tables(positions):
    n = positions.size
    tn = min(512, n)
    inv_freq = ROPE_THETA ** (-jnp.arange(0, MLA_ROPE, 2, dtype=F32) / MLA_ROPE)
    return pl.pallas_call(
        _rope_kernel,
        grid=(n // tn,),
        in_specs=[pl.BlockSpec((1, tn), lambda i: (0, i)),
                  pl.BlockSpec((MLA_ROPE // 2, 1), lambda i: (0, 0))],
        out_specs=[pl.BlockSpec((tn, HEAD_PAD), lambda i: (i, 0)),
                   pl.BlockSpec((tn, HEAD_PAD), lambda i: (i, 0)),
                   pl.BlockSpec((HEAD_PAD, tn), lambda i: (0, i)),
                   pl.BlockSpec((HEAD_PAD, tn), lambda i: (0, i))],
        out_shape=[jax.ShapeDtypeStruct((n, HEAD_PAD), F32),
                   jax.ShapeDtypeStruct((n, HEAD_PAD), F32),
                   jax.ShapeDtypeStruct((HEAD_PAD, n), F32),
                   jax.ShapeDtypeStruct((HEAD_PAD, n), F32)],
        name="rope_tables",
    )(positions.reshape(1, n), inv_freq.reshape(-1, 1))


_A_CQ = 0
_A_CKV = _A_CQ + MLA_Q_RANK
_A_KR = _A_CKV + MLA_KV_RANK
_A_KRR = _A_KR + HEAD_PAD
_A_FK = _A_KRR + HEAD_PAD
_A_CA = _A_FK + FOX_HEADS * HEAD_PAD
_A_CG = _A_CA + CONV_CH
_A_COLS = _A_CG + CONV_CH
_AUG_ROWS = 8
_F_ROWS = 16


def _rms(x, g):
    ms = jnp.mean(jnp.square(x), axis=-1, keepdims=True)
    return x * lax.rsqrt(ms + NORM_EPS) * g


def _inproj_kernel(x_ref, c_ref, s_ref, ct_ref, st_ref, wa_ref, wfq_ref, wfv_ref, wf_ref, fb_ref,
                   gq_ref, wuq_ref, wuqr_ref, gkv_ref, wuk_ref, wuv_ref,
                   cw_ref, cb_ref, cng_ref, cnb_ref, gmat_ref,
                   qt_ref, k_ref, vt_ref, hc_ref,
                   hbuf, cbuf, fcarry, *, tiles_per_seq, tm):
    i = pl.program_id(0)

    @pl.when(i % tiles_per_seq == 0)
    def _():
        hbuf[0:CONV_HALO, :] = jnp.zeros((CONV_HALO, CONV_CH), F32)
        fcarry[...] = jnp.zeros_like(fcarry)

    xb = x_ref[...].astype(BF16)
    p1 = _dot(xb, wa_ref[...])
    cos_t = c_ref[...]
    sin_t = s_ref[...]
    cos_tt = ct_ref[...]
    sin_tt = st_ref[...]

    cqn = _rms(p1[:, _A_CQ:_A_CQ + MLA_Q_RANK], gq_ref[...]).astype(BF16)
    q_t = _nt_dot(wuq_ref[...], cqn)
    q_rot_t = _nt_dot(wuqr_ref[...], cqn)
    mla_scale = (MLA_NOPE + MLA_ROPE) ** -0.5 * LOG2E
    for h in range(MLA_HEADS):
        rows = slice(h * HEAD_PAD, (h + 1) * HEAD_PAD)
        qh = (q_t[rows, :] * cos_tt + q_rot_t[rows, :] * sin_tt) * mla_scale
        for c in range(tm // TQ):
            qt_ref[c, rows, :] = qh[:, c * TQ:(c + 1) * TQ].astype(BF16)

    ckvn = _rms(p1[:, _A_CKV:_A_CKV + MLA_KV_RANK], gkv_ref[...]).astype(BF16)
    k_nope = _dot(ckvn, wuk_ref[...])
    k_rope = p1[:, _A_KR:_A_KR + HEAD_PAD] * cos_t + p1[:, _A_KRR:_A_KRR + HEAD_PAD] * sin_t
    for h in range(MLA_HEADS):
        cols = slice(h * HEAD_PAD, (h + 1) * HEAD_PAD)
        k_ref[:, cols] = (k_nope[:, cols] + k_rope).astype(BF16)
    v_t = _nt_dot(wuv_ref[...], ckvn)
    fv_t = _nt_dot(wfv_ref[...], xb)
    for c in range(tm // TK):
        cs = slice(c * TK, (c + 1) * TK)
        vt_ref[c, 0:MLA_WIDTH, :] = v_t[:, cs].astype(BF16)
        vt_ref[c, MLA_WIDTH:MLA_WIDTH + FOX_WIDTH, :] = fv_t[:, cs].astype(BF16)

    z = _nt_dot(wf_ref[...], xb) + fb_ref[...]
    logf = (jnp.minimum(z, 0.0) - jnp.log1p(jnp.exp(-jnp.abs(z)))) * LOG2E
    r_i = lax.broadcasted_iota(jnp.int32, (tm, tm), 0)
    c_i = lax.broadcasted_iota(jnp.int32, (tm, tm), 1)
    upper = jnp.where(r_i <= c_i, 1.0, 0.0).astype(BF16)
    l_hi, l_mid, l_lo = _split3(logf)
    cum = (_dot(l_hi.astype(BF16), upper) + _dot(l_mid.astype(BF16), upper)
           + _dot(l_lo.astype(BF16), upper))
    f_cum = cum + fcarry[:, 0:1]
    fcarry[...] = jnp.broadcast_to(f_cum[:, tm - 1:tm], fcarry.shape)
    f_hi, f_mid, f_lo = _split3(f_cum)

    fq_t = _nt_dot(wfq_ref[...], xb)
    row8 = lax.broadcasted_iota(jnp.int32, (_AUG_ROWS, tm), 0)
    fox_scale = FOX_DIM ** -0.5 * LOG2E
    for h in range(FOX_HEADS):
        bh = lambda a: jnp.broadcast_to(a[h:h + 1, :], (_AUG_ROWS, tm))
        aug_q = jnp.where(row8 == 0, bh(f_hi), jnp.where(row8 == 1, bh(f_mid), jnp.where(
            row8 == 2, bh(f_lo), jnp.where(row8 < 6, 1.0, 0.0))))
        aug_k = jnp.where(row8 < 3, 1.0, jnp.where(row8 == 3, -bh(f_hi), jnp.where(
            row8 == 4, -bh(f_mid), jnp.where(row8 == 5, -bh(f_lo), 0.0))))
        pad = jnp.zeros((HEAD_PAD - FOX_DIM - _AUG_ROWS, tm), F32)
        qh = jnp.concatenate(
            [fq_t[h * HEAD_PAD:h * HEAD_PAD + FOX_DIM, :] * fox_scale, aug_q, pad], axis=0)
        rows = slice((MLA_HEADS + h) * HEAD_PAD, (MLA_HEADS + h + 1) * HEAD_PAD)
        for c in range(tm // TQ):
            qt_ref[c, rows, :] = qh[:, c * TQ:(c + 1) * TQ].astype(BF16)
        kaug_t = jnp.concatenate([jnp.zeros((FOX_DIM, tm), F32), aug_k, pad], axis=0)
        fk = p1[:, _A_FK + h * HEAD_PAD:_A_FK + (h + 1) * HEAD_PAD]
        k_ref[:, rows] = (fk + kaug_t.T).astype(BF16)

    a = p1[:, _A_CA:_A_CA + CONV_CH]
    g = p1[:, _A_CG:_A_CG + CONV_CH]
    hbuf[CONV_HALO:CONV_HALO + tm, :] = a * jax.nn.sigmoid(g)
    chunk = 64
    first = CONV_HALO - (CONV_WIDTH - 1)
    for c0 in range(0, tm, chunk):
        acc = jnp.zeros((chunk, CONV_CH), F32)
        for r in range(8):
            offs = [o for o in range(first, first + CONV_WIDTH) if o % 8 == r]
            seg = hbuf[c0 + r:c0 + max(offs) + chunk, :]
            for o in offs:
                j = o - first
                a8 = o - r
                acc = acc + cw_ref[j:j + 1, :] * seg[a8:a8 + chunk, :]
        cbuf[c0:c0 + chunk, :] = acc
    hbuf[0:CONV_HALO, :] = hbuf[tm:tm + CONV_HALO, :]
    hv = cbuf[...] + cb_ref[...]
    gm = gmat_ref[...]
    mu = _split2_dot(hv, gm)
    d = hv - mu
    var = _split2_dot(d * d, gm)
    hn = d * lax.rsqrt(var + NORM_EPS) * cng_ref[...] + cnb_ref[...]
    hc_ref[...] = (hn * jax.nn.sigmoid(hn)).astype(BF16)


def _prep_inproj_weights(w_in, w_uq, w_ukv, fox_forget_b):
    o = np.cumsum((0, MLA_Q_RANK, MLA_KV_RANK, MLA_ROPE, FOX_WIDTH, FOX_WIDTH, FOX_WIDTH, FOX_HEADS,
                   2 * CONV_CH))
    w_cq, w_ckv, w_kr, w_fq, w_fk, w_fv, w_f, w_cv = (w_in[:, o[i]:o[i + 1]] for i in range(8))
    d = w_in.shape[0]
    half = MLA_ROPE // 2

    def rot_cols(w):
        return jnp.concatenate([-w[..., half:], w[..., :half]], axis=-1)

    def rope_block(w):
        return jnp.pad(w, ((0, 0), (MLA_NOPE, HEAD_PAD - MLA_NOPE - MLA_ROPE)))

    w_fk_pad = jnp.pad(w_fk.reshape(d, FOX_HEADS, FOX_DIM), ((0, 0), (0, 0), (0, HEAD_PAD - FOX_DIM)))
    wa = jnp.concatenate([w_cq, w_ckv, rope_block(w_kr), rope_block(rot_cols(w_kr)),
                          w_fk_pad.reshape(d, FOX_HEADS * HEAD_PAD), w_cv], axis=1)
    w_fq_pad = jnp.pad(w_fq.reshape(d, FOX_HEADS, FOX_DIM), ((0, 0), (0, 0), (0, HEAD_PAD - FOX_DIM)))
    wfq_t = w_fq_pad.reshape(d, FOX_HEADS * HEAD_PAD).T
    wfv_t = w_fv.T
    wf_t = jnp.pad(w_f, ((0, 0), (0, _F_ROWS - FOX_HEADS))).T
    fb = jnp.pad(fox_forget_b, (0, _F_ROWS - FOX_HEADS)).reshape(_F_ROWS, 1)

    uq = w_uq.reshape(MLA_Q_RANK, MLA_HEADS, MLA_NOPE + MLA_ROPE)
    uq_nope, uq_rope = uq[..., :MLA_NOPE], uq[..., MLA_NOPE:]
    tail = ((0, 0), (0, 0), (0, HEAD_PAD - MLA_NOPE - MLA_ROPE))
    uq_pad = jnp.pad(jnp.concatenate([uq_nope, uq_rope], axis=-1), tail)
    uq_rot_pad = jnp.pad(jnp.concatenate([jnp.zeros_like(uq_nope), rot_cols(uq_rope)], axis=-1), tail)
    wuq_t = uq_pad.reshape(MLA_Q_RANK, MLA_HEADS * HEAD_PAD).T
    wuqr_t = uq_rot_pad.reshape(MLA_Q_RANK, MLA_HEADS * HEAD_PAD).T
    ukv = w_ukv.reshape(MLA_KV_RANK, MLA_HEADS, MLA_NOPE + MLA_V)
    wuk = jnp.pad(ukv[..., :MLA_NOPE], ((0, 0), (0, 0), (0, HEAD_PAD - MLA_NOPE))).reshape(
        MLA_KV_RANK, MLA_HEADS * HEAD_PAD)
    wuv_t = ukv[..., MLA_NOPE:].reshape(MLA_KV_RANK, MLA_WIDTH).T
    bf = lambda a: a.astype(BF16)
    return dict(wa=bf(wa), wfq=bf(wfq_t), wfv=bf(wfv_t), wf=bf(wf_t), fb=fb, wuq=bf(wuq_t),
                wuqr=bf(wuqr_t), wuk=bf(wuk), wuv=bf(wuv_t))


def _input_projection(x2d, tabs, pw, gq, gkv, conv_w, conv_b, conv_ng, conv_nb, seq):
    n, d = x2d.shape
    tm = min(TM_IN, seq)
    cos_t, sin_t, cos_tt, sin_tt = tabs
    gidx = np.arange(CONV_CH) // (CONV_CH // CONV_GROUPS)
    gmat = jnp.asarray((gidx[:, None] == gidx[None, :]) / (CONV_CH // CONV_GROUPS), BF16)
    cw = jnp.pad(conv_w, ((0, 32 - CONV_WIDTH), (0, 0)))
    row = lambda a: a.reshape(1, -1)
    tok = lambda w: pl.BlockSpec((tm, w), lambda i: (i, 0))
    tok_t = lambda r: pl.BlockSpec((r, tm), lambda i: (0, i))
    consts = [pw["wa"], pw["wfq"], pw["wfv"], pw["wf"], pw["fb"], row(gq), pw["wuq"], pw["wuqr"],
              row(gkv), pw["wuk"], pw["wuv"], cw, row(conv_b), row(conv_ng), row(conv_nb), gmat]
    kern = functools.partial(_inproj_kernel, tiles_per_seq=seq // tm, tm=tm)
    return pl.pallas_call(
        kern,
        grid=(n // tm,),
        in_specs=[tok(d), tok(HEAD_PAD), tok(HEAD_PAD), tok_t(HEAD_PAD), tok_t(HEAD_PAD)]
        + [_const_spec(c.shape) for c in consts],
        out_specs=[pl.BlockSpec((tm // TQ, N_HEADS * HEAD_PAD, TQ), lambda i: (i, 0, 0)),
                   tok(N_HEADS * HEAD_PAD),
                   pl.BlockSpec((tm // TK, N_HEADS * V_DIM, TK), lambda i: (i, 0, 0)),
                   tok(CONV_CH)],
        out_shape=[jax.ShapeDtypeStruct((n // TQ, N_HEADS * HEAD_PAD, TQ), BF16),
                   jax.ShapeDtypeStruct((n, N_HEADS * HEAD_PAD), BF16),
                   jax.ShapeDtypeStruct((n // TK, N_HEADS * V_DIM, TK), BF16),
                   jax.ShapeDtypeStruct((n, CONV_CH), BF16)],
        scratch_shapes=[pltpu.VMEM((CONV_HALO + tm, CONV_CH), F32),
                        pltpu.VMEM((tm, CONV_CH), F32),
                        pltpu.VMEM((_F_ROWS, 128), F32)],
        compiler_params=pltpu.CompilerParams(dimension_semantics=("arbitrary",),
                                             vmem_limit_bytes=VMEM_LIMIT),
        name="input_projection",
    )(x2d, cos_t, sin_t, cos_tt, sin_tt, *consts)


def _attn_kernel(qt_ref, k_ref, vt_ref, o_ref, *, n_tiles):
    r_i = lax.broadcasted_iota(jnp.int32, (TK, TQ), 0)
    c_i = lax.broadcasted_iota(jnp.int32, (TK, TQ), 1)
    kv_per_q = TQ // TK

    def q_tile(i, carry):
        q_t = [qt_ref[i, h * HEAD_PAD:(h + 1) * HEAD_PAD, :] for h in range(2)]

        def block(j, state, diag):
            row0 = pl.multiple_of(j * TK, TK)
            scores = [_dot(k_ref[pl.ds(row0, TK), h * HEAD_PAD:(h + 1) * HEAD_PAD], q_t[h])
                      for h in range(2)]
            probs = []
            for h in range(2):
                m, l, acc = state[h]
                s = scores[h]
                if diag is not None:
                    s = jnp.where(r_i + diag * TK <= c_i, s, -jnp.inf)
                m_new = jnp.maximum(m, jnp.max(s, axis=0, keepdims=True))
                alpha = jnp.exp2(m - m_new)
                p = jnp.exp2(s - m_new)
                l = alpha * l + jnp.sum(p, axis=0, keepdims=True)
                probs.append((m_new, l, alpha, p.astype(BF16)))
            new = []
            for h in range(2):
                m_new, l, alpha, p = probs[h]
                vb = vt_ref[j, h * V_DIM:(h + 1) * V_DIM, :]
                new.append((m_new, l, alpha * state[h][2] + _dot(vb, p)))
            return tuple(new)

        init = tuple((jnp.full((1, TQ), -1e30, F32), jnp.zeros((1, TQ), F32),
                      jnp.zeros((V_DIM, TQ), F32)) for _ in range(2))
        state = lax.fori_loop(0, i * kv_per_q, lambda j, st: block(j, st, None), init)
        for dg in range(kv_per_q):
            state = block(i * kv_per_q + dg, state, dg)
        out_t = jnp.concatenate([acc / l for (_, l, acc) in state], axis=0)
        o_ref[pl.ds(pl.multiple_of(i * TQ, TQ), TQ), :] = out_t.T.astype(o_ref.dtype)
        return carry

    lax.fori_loop(0, n_tiles, q_tile, 0)


def _attention(q_t, k, v_t, batch, seq):
    n = k.shape[0]
    pairs = N_HEADS // 2
    return pl.pallas_call(
        functools.partial(_attn_kernel, n_tiles=seq // TQ),
        grid=(batch, pairs),
        in_specs=[pl.BlockSpec((seq // TQ, 2 * HEAD_PAD, TQ), lambda b, p: (b, p, 0)),
                  pl.BlockSpec((seq, 2 * HEAD_PAD), lambda b, p: (b, p)),
                  pl.BlockSpec((seq // TK, 2 * V_DIM, TK), lambda b, p: (b, p, 0))],
        out_specs=pl.BlockSpec((seq, 2 * V_DIM), lambda b, p: (b, p)),
        out_shape=jax.ShapeDtypeStruct((n, N_HEADS * V_DIM), BF16),
        compiler_params=pltpu.CompilerParams(dimension_semantics=("arbitrary", "arbitrary"),
                                             vmem_limit_bytes=VMEM_LIMIT),
        name="attention",
    )(q_t, k, v_t)


def _layer_norm(x, g, b):
    mu = jnp.mean(x, axis=-1, keepdims=True)
    d = x - mu
    var = jnp.mean(jnp.square(d), axis=-1, keepdims=True)
    return d * lax.rsqrt(var + NORM_EPS) * g + b


def _outproj_kernel(o_ref, hc_ref, x_ref, gm_ref, gf_ref, wo_ref, g1_ref, b1_ref, *rest, with_router):
    if with_router:
        rw_ref, x1_ref, gates_ref = rest
    else:
        (x1_ref,) = rest
    o = o_ref[...].astype(F32)
    mla = _rms(o[:, :MLA_WIDTH], gm_ref[...])
    fox = _rms(o[:, MLA_WIDTH:], gf_ref[...])
    mixed = jnp.concatenate([mla.astype(BF16), fox.astype(BF16), hc_ref[...]], axis=-1)
    y = _dot(mixed, wo_ref[...])
    x1 = _layer_norm(ALPHA * x_ref[...] + y, g1_ref[...], b1_ref[...])
    x1_ref[...] = x1
    if with_router:
        rw = rw_ref[...]
        x_hi = x1.astype(BF16)
        x_lo = (x1 - x_hi.astype(F32)).astype(BF16)
        w_hi = rw.astype(BF16)
        w_lo = (rw - w_hi.astype(F32)).astype(BF16)
        logits = _dot(x_hi, w_hi) + (_dot(x_lo, w_hi) + _dot(x_hi, w_lo))
        lane = lax.broadcasted_iota(jnp.int32, logits.shape, 1)
        logits = jnp.where(lane < N_EXPERTS, logits, -jnp.inf)
        v1 = jnp.max(logits, axis=-1, keepdims=True)
        i1 = jnp.min(jnp.where(logits == v1, lane, 128), axis=-1, keepdims=True)
        rest_l = jnp.where(lane == i1, -jnp.inf, logits)
        v2 = jnp.max(rest_l, axis=-1, keepdims=True)
        i2 = jnp.min(jnp.where(rest_l == v2, lane, 128), axis=-1, keepdims=True)
        e2 = jnp.exp(v2 - v1)
        den = 1.0 + e2
        gates_ref[...] = jnp.where(lane == i1, 1.0 / den, jnp.where(lane == i2, e2 / den, 0.0))


def _output_projection(o, hc, x2d, gm, gf, w_out, g1, b1, router_w=None):
    n, d = x2d.shape
    tm = min(TM_OUT, n)
    row = lambda a: a.reshape(1, -1)
    tok = lambda w: pl.BlockSpec((tm, w), lambda i: (i, 0))
    consts = [row(gm), row(gf), w_out.astype(BF16), row(g1), row(b1)]
    out_specs = [tok(d)]
    out_shape = [jax.ShapeDtypeStruct((n, d), F32)]
    if router_w is not None:
        consts.append(jnp.pad(router_w, ((0, 0), (0, 128 - N_EXPERTS))))
        out_specs.append(tok(128))
        out_shape.append(jax.ShapeDtypeStruct((n, 128), F32))
    return pl.pallas_call(
        functools.partial(_outproj_kernel, with_router=router_w is not None),
        grid=(n // tm,),
        in_specs=[tok(o.shape[1]), tok(CONV_CH), tok(d)] + [_const_spec(c.shape) for c in consts],
        out_specs=out_specs,
        out_shape=out_shape,
        compiler_params=pltpu.CompilerParams(dimension_semantics=("arbitrary",),
                                             vmem_limit_bytes=VMEM_LIMIT),
        name="output_projection",
    )(o, hc, x2d, *consts)


def _swiglu_tile(xb, w1, w3, w2):
    h1 = _dot(xb, w1)
    h3 = _dot(xb, w3)
    hid = (h1 * jax.nn.sigmoid(h1) * h3).astype(BF16)
    return _dot(hid, w2)


def _dense_ffn_kernel(x_ref, w1_ref, w3_ref, w2_ref, g_ref, b_ref, o_ref, *, f_chunk):
    x = x_ref[...]
    xb = x.astype(BF16)
    ff = None
    for c0 in range(0, w1_ref.shape[1], f_chunk):
        part = _swiglu_tile(xb, w1_ref[:, c0:c0 + f_chunk], w3_ref[:, c0:c0 + f_chunk],
                            w2_ref[c0:c0 + f_chunk, :])
        ff = part if ff is None else ff + part
    o_ref[...] = _layer_norm(ALPHA * x + ff, g_ref[...], b_ref[...])


def _dense_ffn(x2d, w1, w3, w2, g, b):
    n, d = x2d.shape
    tm = min(TM_FFN, n)
    f = w1.shape[1]
    f_chunk = f // 2 if (f // 2) % 128 == 0 else f
    row = lambda a: a.reshape(1, -1)
    consts = [w1.astype(BF16), w3.astype(BF16), w2.astype(BF16), row(g), row(b)]
    return pl.pallas_call(
        functools.partial(_dense_ffn_kernel, f_chunk=f_chunk),
        grid=(n // tm,),
        in_specs=[pl.BlockSpec((tm, d), lambda i: (i, 0))] + [_const_spec(c.shape) for c in consts],
        out_specs=pl.BlockSpec((tm, d), lambda i: (i, 0)),
        out_shape=jax.ShapeDtypeStruct((n, d), F32),
        compiler_params=pltpu.CompilerParams(dimension_semantics=("arbitrary",),
                                             vmem_limit_bytes=VMEM_LIMIT),
        name="dense_ffn",
    )(x2d, *consts)


def _moe_kernel(x_ref, gates_ref, w1_ref, w3_ref, w2_ref, g_ref, b_ref, o_ref, acc_ref):
    e = pl.program_id(1)

    @pl.when(e == 0)
    def _():
        acc_ref[...] = jnp.zeros_like(acc_ref)

    gates = gates_ref[...]
    lane = lax.broadcasted_iota(jnp.int32, gates.shape, 1)
    gate = jnp.sum(jnp.where(lane == e, gates, 0.0), axis=-1, keepdims=True)
    y = _swiglu_tile(x_ref[...].astype(BF16), w1_ref[0], w3_ref[0], w2_ref[0])
    acc_ref[...] += gate * y

    @pl.when(e == pl.num_programs(1) - 1)
    def _():
        o_ref[...] = _layer_norm(ALPHA * x_ref[...] + acc_ref[...], g_ref[...], b_ref[...])


def _moe_ffn(x2d, gates, w1, w3, w2, g, b):
    n, d = x2d.shape
    tm = min(TM_FFN, n)
    n_exp, _, f = w1.shape
    row = lambda a: a.reshape(1, -1)
    return pl.pallas_call(
        _moe_kernel,
        grid=(n // tm, n_exp),
        in_specs=[pl.BlockSpec((tm, d), lambda i, e: (i, 0)),
                  pl.BlockSpec((tm, 128), lambda i, e: (i, 0)),
                  pl.BlockSpec((1, d, f), lambda i, e: (e, 0, 0)),
                  pl.BlockSpec((1, d, f), lambda i, e: (e, 0, 0)),
                  pl.BlockSpec((1, f, d), lambda i, e: (e, 0, 0)),
                  _const_spec((1, d)), _const_spec((1, d))],
        out_specs=pl.BlockSpec((tm, d), lambda i, e: (i, 0)),
        out_shape=jax.ShapeDtypeStruct((n, d), F32),
        scratch_shapes=[pltpu.VMEM((tm, d), F32)],
        compiler_params=pltpu.CompilerParams(dimension_semantics=("arbitrary", "arbitrary"),
                                             vmem_limit_bytes=VMEM_LIMIT),
        name="moe_ffn",
    )(x2d, gates, w1.astype(BF16), w3.astype(BF16), w2.astype(BF16), row(g), row(b))


def kernel(x, positions, w_in, mla_q_norm_g, w_uq, mla_kv_norm_g, w_ukv, fox_forget_b, conv_w, conv_b,
           conv_norm_g, conv_norm_b, mla_out_norm_g, fox_out_norm_g, w_out, ln1_g, ln1_b, dense_w1,
           dense_w3, dense_w2, router_w, expert_w1, expert_w3, expert_w2, ln2_g, ln2_b):
    batch, seq, d = x.shape
    assert d == D_MODEL and seq % TQ == 0 and seq % min(TM_IN, seq) == 0
    depth = w_in.shape[0]
    tabs = _rope_tables(positions)
    h = x.reshape(batch * seq, d)
    for layer in range(depth):
        pw = _prep_inproj_weights(w_in[layer], w_uq[layer], w_ukv[layer], fox_forget_b[layer])
        q_t, k, v_t, hc = _input_projection(
            h, tabs, pw, mla_q_norm_g[layer], mla_kv_norm_g[layer], conv_w[layer], conv_b[layer],
            conv_norm_g[layer], conv_norm_b[layer], seq)
        o = _attention(q_t, k, v_t, batch, seq)
        j = layer // 2
        if layer % 2 == 0:
            h = _output_projection(o, hc, h, mla_out_norm_g[layer], fox_out_norm_g[layer], w_out[layer],
                                   ln1_g[layer], ln1_b[layer])[0]
            h = _dense_ffn(h, dense_w1[j], dense_w3[j], dense_w2[j], ln2_g[layer], ln2_b[layer])
        else:
            h, gates = _output_projection(o, hc, h, mla_out_norm_g[layer], fox_out_norm_g[layer],
                                          w_out[layer], ln1_g[layer], ln1_b[layer], router_w[j])
            h = _moe_ffn(h, gates, expert_w1[j], expert_w3[j], expert_w2[j], ln2_g[layer], ln2_b[layer])
    return h.reshape(batch, seq, d)
```

```python
import functools
import math

import numpy as np
import jax
import jax.numpy as jnp
from jax import lax
from jax.experimental import pallas as pl
from jax.experimental.pallas import tpu as pltpu

F32 = jnp.float32
BF16 = jnp.bfloat16

D_MODEL = 1024
DEPTH = 4
MLA_HEADS = 8
MLA_NOPE = 64
MLA_ROPE = 32
MLA_V = 64
MLA_Q_RANK = 256
MLA_KV_RANK = 128
ROPE_THETA = 10000.0
FOX_HEADS = 4
FOX_DIM = 64
CONV_CH = 256
CONV_GROUPS = 4
CONV_WIDTH = 31
MLA_WIDTH = MLA_HEADS * MLA_V
FOX_WIDTH = FOX_HEADS * FOX_DIM
N_EXPERTS = 8
ALPHA = (2.0 * DEPTH) ** 0.25
NORM_EPS = 1e-5
LOG2E = math.log2(math.e)

HEAD_PAD = 128
N_HEADS = MLA_HEADS + FOX_HEADS
V_DIM = 64
V_ROWS = 80
CONV_HALO = 32
VMEM_LIMIT = 56 * 1024 * 1024

TQ = 512
TK = 256
TM_IN = 512
TM_OUT = 512
TM_FFN = 512


def _nt_dot(a, b):
    return lax.dot_general(a, b, (((1,), (1,)), ((), ())), preferred_element_type=F32)


def _dot(a, b):
    return jnp.dot(a, b, preferred_element_type=F32)


def _split2_dot(a, m_bf16):
    hi = a.astype(BF16)
    lo = (a - hi.astype(F32)).astype(BF16)
    return _dot(hi, m_bf16) + _dot(lo, m_bf16)


def _split3(a):
    hi = a.astype(BF16).astype(F32)
    r1 = a - hi
    mid = r1.astype(BF16).astype(F32)
    lo = (r1 - mid).astype(BF16).astype(F32)
    return hi, mid, lo


def _const_spec(shape):
    nd = len(shape)
    return pl.BlockSpec(shape, lambda *_: (0,) * nd, pipeline_mode=pl.Buffered(1))


def _rope_kernel(pos_ref, invf_ref, c_ref, s_ref, ct_ref, st_ref):
    pos = pos_ref[...].astype(F32)
    ang = invf_ref[...] * pos
    cos = jnp.cos(ang)
    sin = jnp.sin(ang)
    tn = pos.shape[1]
    ct = jnp.concatenate([jnp.ones((MLA_NOPE, tn), F32), cos, cos, jnp.zeros((32, tn), F32)], axis=0)
    st = jnp.concatenate([jnp.zeros((MLA_NOPE, tn), F32), sin, sin, jnp.zeros((32, tn), F32)], axis=0)
    ct_ref[...] = ct
    st_ref[...] = st
    c_ref[...] = ct.T
    s_ref[...] = st.T


def _rope_tables(positions):
    n = positions.size
    tn = min(512, n)
    inv_freq = ROPE_THETA ** (-jnp.arange(0, MLA_ROPE, 2, dtype=F32) / MLA_ROPE)
    return pl.pallas_call(
        _rope_kernel,
        grid=(n // tn,),
        in_specs=[pl.BlockSpec((1, tn), lambda i: (0, i)),
                  pl.BlockSpec((MLA_ROPE // 2, 1), lambda i: (0, 0))],
        out_specs=[pl.BlockSpec((tn, HEAD_PAD), lambda i: (i, 0)),
                   pl.BlockSpec((tn, HEAD_PAD), lambda i: (i, 0)),
                   pl.BlockSpec((HEAD_PAD, tn), lambda i: (0, i)),
                   pl.BlockSpec((HEAD_PAD, tn), lambda i: (0, i))],
        out_shape=[jax.ShapeDtypeStruct((n, HEAD_PAD), F32),
                   jax.ShapeDtypeStruct((n, HEAD_PAD), F32),
                   jax.ShapeDtypeStruct((HEAD_PAD, n), F32),
                   jax.ShapeDtypeStruct((HEAD_PAD, n), F32)],
        name="rope_tables",
    )(positions.reshape(1, n), inv_freq.reshape(-1, 1))


_A_CQ = 0
_A_CKV = _A_CQ + MLA_Q_RANK
_A_KR = _A_CKV + MLA_KV_RANK
_A_KRR = _A_KR + HEAD_PAD
_A_FK = _A_KRR + HEAD_PAD
_A_CA = _A_FK + FOX_HEADS * HEAD_PAD
_A_CG = _A_CA + CONV_CH
_A_COLS = _A_CG + CONV_CH
_AUG_ROWS = 8
_F_ROWS = 16


def _rms(x, g):
    ms = jnp.mean(jnp.square(x), axis=-1, keepdims=True)
    return x * lax.rsqrt(ms + NORM_EPS) * g


def _inproj_kernel(x_ref, c_ref, s_ref, ct_ref, st_ref, wa_ref, wfq_ref, wfv_ref, wf_ref, fb_ref,
                   gq_ref, wuq_ref, wuqr_ref, gkv_ref, wuk_ref, wuv_ref,
                   cw_ref, cb_ref, cng_ref, cnb_ref, gmat_ref,
                   qt_ref, k_ref, vt_ref, hc_ref,
                   hbuf, cbuf, fcarry, *, tiles_per_seq, tm):
    i = pl.program_id(0)

    @pl.when(i % tiles_per_seq == 0)
    def _():
        hbuf[0:CONV_HALO, :] = jnp.zeros((CONV_HALO, CONV_CH), F32)
        fcarry[...] = jnp.zeros_like(fcarry)

    xb = x_ref[...].astype(BF16)
    p1 = _dot(xb, wa_ref[...])
    cos_t = c_ref[...]
    sin_t = s_ref[...]
    cos_tt = ct_ref[...]
    sin_tt = st_ref[...]

    cqn = _rms(p1[:, _A_CQ:_A_CQ + MLA_Q_RANK], gq_ref[...]).astype(BF16)
    q_t = _nt_dot(wuq_ref[...], cqn)
    q_rot_t = _nt_dot(wuqr_ref[...], cqn)
    mla_scale = (MLA_NOPE + MLA_ROPE) ** -0.5 * LOG2E
    for h in range(MLA_HEADS):
        rows = slice(h * HEAD_PAD, (h + 1) * HEAD_PAD)
        qh = (q_t[rows, :] * cos_tt + q_rot_t[rows, :] * sin_tt) * mla_scale
        for c in range(tm // TQ):
            qt_ref[c, rows, :] = qh[:, c * TQ:(c + 1) * TQ].astype(BF16)

    ckvn = _rms(p1[:, _A_CKV:_A_CKV + MLA_KV_RANK], gkv_ref[...]).astype(BF16)
    k_nope = _dot(ckvn, wuk_ref[...])
    k_rope = p1[:, _A_KR:_A_KR + HEAD_PAD] * cos_t + p1[:, _A_KRR:_A_KRR + HEAD_PAD] * sin_t
    for h in range(MLA_HEADS):
        cols = slice(h * HEAD_PAD, (h + 1) * HEAD_PAD)
        k_ref[:, cols] = (k_nope[:, cols] + k_rope).astype(BF16)
    v_t = _nt_dot(wuv_ref[...], ckvn)
    fv_t = _nt_dot(wfv_ref[...], xb)
    ones_blk = jnp.where(lax.broadcasted_iota(jnp.int32, (V_ROWS - V_DIM, tm), 0) == 0, 1.0, 0.0)
    for h in range(N_HEADS):
        src = v_t if h < MLA_HEADS else fv_t
        r0 = (h if h < MLA_HEADS else h - MLA_HEADS) * V_DIM
        vh = jnp.concatenate([src[r0:r0 + V_DIM, :], ones_blk], axis=0).astype(BF16)
        for c in range(tm // TK):
            vt_ref[c, h * V_ROWS:(h + 1) * V_ROWS, :] = vh[:, c * TK:(c + 1) * TK]

    z = _nt_dot(wf_ref[...], xb) + fb_ref[...]
    logf = (jnp.minimum(z, 0.0) - jnp.log1p(jnp.exp(-jnp.abs(z)))) * LOG2E
    r_i = lax.broadcasted_iota(jnp.int32, (tm, tm), 0)
    c_i = lax.broadcasted_iota(jnp.int32, (tm, tm), 1)
    upper = jnp.where(r_i <= c_i, 1.0, 0.0).astype(BF16)
    l_hi, l_mid, l_lo = _split3(logf)
    cum = (_dot(l_hi.astype(BF16), upper) + _dot(l_mid.astype(BF16), upper)
           + _dot(l_lo.astype(BF16), upper))
    f_cum = cum + fcarry[:, 0:1]
    fcarry[...] = jnp.broadcast_to(f_cum[:, tm - 1:tm], fcarry.shape)
    f_hi, f_mid, f_lo = _split3(f_cum)

    fq_t = _nt_dot(wfq_ref[...], xb)
    row8 = lax.broadcasted_iota(jnp.int32, (_AUG_ROWS, tm), 0)
    fox_scale = FOX_DIM ** -0.5 * LOG2E
    for h in range(FOX_HEADS):
        bh = lambda a: jnp.broadcast_to(a[h:h + 1, :], (_AUG_ROWS, tm))
        aug_q = jnp.where(row8 == 0, bh(f_hi), jnp.where(row8 == 1, bh(f_mid), jnp.where(
            row8 == 2, bh(f_lo), jnp.where(row8 < 6, 1.0, 0.0))))
        aug_k = jnp.where(row8 < 3, 1.0, jnp.where(row8 == 3, -bh(f_hi), jnp.where(
            row8 == 4, -bh(f_mid), jnp.where(row8 == 5, -bh(f_lo), 0.0))))
        pad = jnp.zeros((HEAD_PAD - FOX_DIM - _AUG_ROWS, tm), F32)
        qh = jnp.concatenate(
            [fq_t[h * HEAD_PAD:h * HEAD_PAD + FOX_DIM, :] * fox_scale, aug_q, pad], axis=0)
        rows = slice((MLA_HEADS + h) * HEAD_PAD, (MLA_HEADS + h + 1) * HEAD_PAD)
        for c in range(tm // TQ):
            qt_ref[c, rows, :] = qh[:, c * TQ:(c + 1) * TQ].astype(BF16)
        kaug_t = jnp.concatenate([jnp.zeros((FOX_DIM, tm), F32), aug_k, pad], axis=0)
        fk = p1[:, _A_FK + h * HEAD_PAD:_A_FK + (h + 1) * HEAD_PAD]
        k_ref[:, rows] = (fk + kaug_t.T).astype(BF16)

    a = p1[:, _A_CA:_A_CA + CONV_CH]
    g = p1[:, _A_CG:_A_CG + CONV_CH]
    hbuf[CONV_HALO:CONV_HALO + tm, :] = a * jax.nn.sigmoid(g)
    chunk = 64
    first = CONV_HALO - (CONV_WIDTH - 1)
    for c0 in range(0, tm, chunk):
        acc = jnp.zeros((chunk, CONV_CH), F32)
        for r in range(8):
            offs = [o for o in range(first, first + CONV_WIDTH) if o % 8 == r]
            seg = hbuf[c0 + r:c0 + max(offs) + chunk, :]
            for o in offs:
                j = o - first
                a8 = o - r
                acc = acc + cw_ref[j:j + 1, :] * seg[a8:a8 + chunk, :]
        cbuf[c0:c0 + chunk, :] = acc
    hbuf[0:CONV_HALO, :] = hbuf[tm:tm + CONV_HALO, :]
    hv = cbuf[...] + cb_ref[...]
    gm = gmat_ref[...]
    mu = _split2_dot(hv, gm)
    d = hv - mu
    var = _split2_dot(d * d, gm)
    hn = d * lax.rsqrt(var + NORM_EPS) * cng_ref[...] + cnb_ref[...]
    hc_ref[...] = (hn * jax.nn.sigmoid(hn)).astype(BF16)


def _prep_inproj_weights(w_in, w_uq, w_ukv, fox_forget_b):
    o = np.cumsum((0, MLA_Q_RANK, MLA_KV_RANK, MLA_ROPE, FOX_WIDTH, FOX_WIDTH, FOX_WIDTH, FOX_HEADS,
                   2 * CONV_CH))
    w_cq, w_ckv, w_kr, w_fq, w_fk, w_fv, w_f, w_cv = (w_in[:, o[i]:o[i + 1]] for i in range(8))
    d = w_in.shape[0]
    half = MLA_ROPE // 2

    def rot_cols(w):
        return jnp.concatenate([-w[..., half:], w[..., :half]], axis=-1)

    def rope_block(w):
        return jnp.pad(w, ((0, 0), (MLA_NOPE, HEAD_PAD - MLA_NOPE - MLA_ROPE)))

    w_fk_pad = jnp.pad(w_fk.reshape(d, FOX_HEADS, FOX_DIM), ((0, 0), (0, 0), (0, HEAD_PAD - FOX_DIM)))
    wa = jnp.concatenate([w_cq, w_ckv, rope_block(w_kr), rope_block(rot_cols(w_kr)),
                          w_fk_pad.reshape(d, FOX_HEADS * HEAD_PAD), w_cv], axis=1)
    w_fq_pad = jnp.pad(w_fq.reshape(d, FOX_HEADS, FOX_DIM), ((0, 0), (0, 0), (0, HEAD_PAD - FOX_DIM)))
    wfq_t = w_fq_pad.reshape(d, FOX_HEADS * HEAD_PAD).T
    wfv_t = w_fv.T
    wf_t = jnp.pad(w_f, ((0, 0), (0, _F_ROWS - FOX_HEADS))).T
    fb = jnp.pad(fox_forget_b, (0, _F_ROWS - FOX_HEADS)).reshape(_F_ROWS, 1)

    uq = w_uq.reshape(MLA_Q_RANK, MLA_HEADS, MLA_NOPE + MLA_ROPE)
    uq_nope, uq_rope = uq[..., :MLA_NOPE], uq[..., MLA_NOPE:]
    tail = ((0, 0), (0, 0), (0, HEAD_PAD - MLA_NOPE - MLA_ROPE))
    uq_pad = jnp.pad(jnp.concatenate([uq_nope, uq_rope], axis=-1), tail)
    uq_rot_pad = jnp.pad(jnp.concatenate([jnp.zeros_like(uq_nope), rot_cols(uq_rope)], axis=-1), tail)
    wuq_t = uq_pad.reshape(MLA_Q_RANK, MLA_HEADS * HEAD_PAD).T
    wuqr_t = uq_rot_pad.reshape(MLA_Q_RANK, MLA_HEADS * HEAD_PAD).T
    ukv = w_ukv.reshape(MLA_KV_RANK, MLA_HEADS, MLA_NOPE + MLA_V)
    wuk = jnp.pad(ukv[..., :MLA_NOPE], ((0, 0), (0, 0), (0, HEAD_PAD - MLA_NOPE))).reshape(
        MLA_KV_RANK, MLA_HEADS * HEAD_PAD)
    wuv_t = ukv[..., MLA_NOPE:].reshape(MLA_KV_RANK, MLA_WIDTH).T
    bf = lambda a: a.astype(BF16)
    return dict(wa=bf(wa), wfq=bf(wfq_t), wfv=bf(wfv_t), wf=bf(wf_t), fb=fb, wuq=bf(wuq_t),
                wuqr=bf(wuqr_t), wuk=bf(wuk), wuv=bf(wuv_t))


def _input_projection(x2d, tabs, pw, gq, gkv, conv_w, conv_b, conv_ng, conv_nb, seq):
    n, d = x2d.shape
    tm = min(TM_IN, seq)
    cos_t, sin_t, cos_tt, sin_tt = tabs
    gidx = np.arange(CONV_CH) // (CONV_CH // CONV_GROUPS)
    gmat = jnp.asarray((gidx[:, None] == gidx[None, :]) / (CONV_CH // CONV_GROUPS), BF16)
    cw = jnp.pad(conv_w, ((0, 32 - CONV_WIDTH), (0, 0)))
    row = lambda a: a.reshape(1, -1)
    tok = lambda w: pl.BlockSpec((tm, w), lambda i: (i, 0))
    tok_t = lambda r: pl.BlockSpec((r, tm), lambda i: (0, i))
    consts = [pw["wa"], pw["wfq"], pw["wfv"], pw["wf"], pw["fb"], row(gq), pw["wuq"], pw["wuqr"],
              row(gkv), pw["wuk"], pw["wuv"], cw, row(conv_b), row(conv_ng), row(conv_nb), gmat]
    kern = functools.partial(_inproj_kernel, tiles_per_seq=seq // tm, tm=tm)
    return pl.pallas_call(
        kern,
        grid=(n // tm,),
        in_specs=[tok(d), tok(HEAD_PAD), tok(HEAD_PAD), tok_t(HEAD_PAD), tok_t(HEAD_PAD)]
        + [_const_spec(c.shape) for c in consts],
        out_specs=[pl.BlockSpec((tm // TQ, N_HEADS * HEAD_PAD, TQ), lambda i: (i, 0, 0)),
                   tok(N_HEADS * HEAD_PAD),
                   pl.BlockSpec((tm // TK, N_HEADS * V_ROWS, TK), lambda i: (i, 0, 0)),
                   tok(CONV_CH)],
        out_shape=[jax.ShapeDtypeStruct((n // TQ, N_HEADS * HEAD_PAD, TQ), BF16),
                   jax.ShapeDtypeStruct((n, N_HEADS * HEAD_PAD), BF16),
                   jax.ShapeDtypeStruct((n // TK, N_HEADS * V_ROWS, TK), BF16),
                   jax.ShapeDtypeStruct((n, CONV_CH), BF16)],
        scratch_shapes=[pltpu.VMEM((CONV_HALO + tm, CONV_CH), F32),
                        pltpu.VMEM((tm, CONV_CH), F32),
                        pltpu.VMEM((_F_ROWS, 128), F32)],
        compiler_params=pltpu.CompilerParams(dimension_semantics=("arbitrary",),
                                             vmem_limit_bytes=VMEM_LIMIT),
        name="input_projection",
    )(x2d, cos_t, sin_t, cos_tt, sin_tt, *consts)


def _attn_kernel(qt_ref, k_ref, vt_ref, o_ref, s00, s01, s10, s11, p00, p01, p10, p11, acc0, acc1, diff_ref,
                 *, n_tiles):
    assert TQ == 2 * TK
    s_ref = ((s00, s01), (s10, s11))
    p_ref = ((p00, p01), (p10, p11))
    acc_ref = (acc0, acc1)
    diff_ref[...] = (lax.broadcasted_iota(jnp.int32, (TK, TQ), 1)
                     - lax.broadcasted_iota(jnp.int32, (TK, TQ), 0))

    def q_tile(i, carry):

        def scores(j, slot):
            row0 = pl.multiple_of(j * TK, TK)
            block_max = []
            for h in range(2):
                s = _dot(k_ref[pl.ds(row0, TK), h * HEAD_PAD:(h + 1) * HEAD_PAD],
                         qt_ref[i, h * HEAD_PAD:(h + 1) * HEAD_PAD, :])
                s_ref[slot][h][...] = s
                block_max.append(jnp.max(s, axis=0, keepdims=True))
            return block_max

        def softmax(slot, m, block_max):
            m_new = [jnp.maximum(m[h], block_max[h]) for h in range(2)]
            for h in range(2):
                p_ref[slot][h][...] = jnp.exp2(s_ref[slot][h][...] - m_new[h]).astype(BF16)
            return m_new, [jnp.exp2(m[h] - m_new[h]) for h in range(2)]

        def softmax_diag(slot, m, dg):
            m_new, alpha = [], []
            for h in range(2):
                s = jnp.where(diff_ref[...] >= dg * TK, s_ref[slot][h][...], -jnp.inf)
                m_new.append(jnp.maximum(m[h], jnp.max(s, axis=0, keepdims=True)))
                p_ref[slot][h][...] = jnp.exp2(s - m_new[h]).astype(BF16)
                alpha.append(jnp.exp2(m[h] - m_new[h]))
            return m_new, alpha

        def values(j, slot, alpha):
            for h in range(2):
                acc_ref[h][...] = alpha[h] * acc_ref[h][...] + _dot(
                    vt_ref[j, h * V_ROWS:(h + 1) * V_ROWS, :], p_ref[slot][h][...])

        for h in range(2):
            acc_ref[h][...] = jnp.zeros_like(acc_ref[h])
            p_ref[1][h][...] = jnp.zeros_like(p_ref[1][h])
        bm0 = scores(0, 0)
        m0 = [jnp.full((1, TQ), -1e30, F32)] * 2
        a0 = [jnp.ones((1, TQ), F32)] * 2

        def pair(u, state):
            m, alpha, bm_t = list(state[0:2]), list(state[2:4]), list(state[4:6])
            t = 2 * u
            m, alpha_t = softmax(0, m, bm_t)
            values(jnp.maximum(t - 1, 0), 1, alpha)
            bm_t1 = scores(t + 1, 1)
            m, alpha_t1 = softmax(1, m, bm_t1)
            values(t, 0, alpha_t)
            bm_t2 = scores(t + 2, 0)
            return (*m, *alpha_t1, *bm_t2)

        state = lax.fori_loop(0, i, pair, (*m0, *a0, *bm0))
        m, alpha = list(state[0:2]), list(state[2:4])
        d0 = 2 * i
        values(jnp.maximum(d0 - 1, 0), 1, alpha)
        scores(d0 + 1, 1)
        m, alpha_d0 = softmax_diag(0, m, 0)
        m, alpha_d1 = softmax_diag(1, m, 1)
        values(d0, 0, alpha_d0)
        values(d0 + 1, 1, alpha_d1)
        out_t = jnp.concatenate(
            [acc_ref[h][0:V_DIM, :] / acc_ref[h][V_DIM:V_DIM + 1, :] for h in range(2)], axis=0)
        o_ref[pl.ds(pl.multiple_of(i * TQ, TQ), TQ), :] = out_t.T.astype(o_ref.dtype)
        return carry

    lax.fori_loop(0, n_tiles, q_tile, 0)


def _attention(q_t, k, v_t, batch, seq):
    n = k.shape[0]
    pairs = N_HEADS // 2
    return pl.pallas_call(
        functools.partial(_attn_kernel, n_tiles=seq // TQ),
        grid=(batch, pairs),
        in_specs=[pl.BlockSpec((seq // TQ, 2 * HEAD_PAD, TQ), lambda b, p: (b, p, 0)),
                  pl.BlockSpec((seq, 2 * HEAD_PAD), lambda b, p: (b, p)),
                  pl.BlockSpec((seq // TK, 2 * V_ROWS, TK), lambda b, p: (b, p, 0))],
        out_specs=pl.BlockSpec((seq, 2 * V_DIM), lambda b, p: (b, p)),
        out_shape=jax.ShapeDtypeStruct((n, N_HEADS * V_DIM), BF16),
        scratch_shapes=[pltpu.VMEM((TK, TQ), F32)] * 4 + [pltpu.VMEM((TK, TQ), BF16)] * 4
        + [pltpu.VMEM((V_ROWS, TQ), F32)] * 2 + [pltpu.VMEM((TK, TQ), jnp.int32)],
        compiler_params=pltpu.CompilerParams(dimension_semantics=("arbitrary", "arbitrary"),
                                             vmem_limit_bytes=VMEM_LIMIT),
        name="attention",
    )(q_t, k, v_t)


def _layer_norm(x, g, b):
    mu = jnp.mean(x, axis=-1, keepdims=True)
    d = x - mu
    var = jnp.mean(jnp.square(d), axis=-1, keepdims=True)
    return d * lax.rsqrt(var + NORM_EPS) * g + b


def _outproj_kernel(o_ref, hc_ref, x_ref, gm_ref, gf_ref, wo_ref, g1_ref, b1_ref, *rest, with_router):
    if with_router:
        rw_ref, x1_ref, gates_ref = rest
    else:
        (x1_ref,) = rest
    o = o_ref[...].astype(F32)
    mla = _rms(o[:, :MLA_WIDTH], gm_ref[...])
    fox = _rms(o[:, MLA_WIDTH:], gf_ref[...])
    mixed = jnp.concatenate([mla.astype(BF16), fox.astype(BF16), hc_ref[...]], axis=-1)
    y = _dot(mixed, wo_ref[...])
    x1 = _layer_norm(ALPHA * x_ref[...] + y, g1_ref[...], b1_ref[...])
    x1_ref[...] = x1
    if with_router:
        rw = rw_ref[...]
        x_hi = x1.astype(BF16)
        x_lo = (x1 - x_hi.astype(F32)).astype(BF16)
        w_hi = rw.astype(BF16)
        w_lo = (rw - w_hi.astype(F32)).astype(BF16)
        logits = _dot(x_hi, w_hi) + (_dot(x_lo, w_hi) + _dot(x_hi, w_lo))
        lane = lax.broadcasted_iota(jnp.int32, logits.shape, 1)
        logits = jnp.where(lane < N_EXPERTS, logits, -jnp.inf)
        v1 = jnp.max(logits, axis=-1, keepdims=True)
        i1 = jnp.min(jnp.where(logits == v1, lane, 128), axis=-1, keepdims=True)
        rest_l = jnp.where(lane == i1, -jnp.inf, logits)
        v2 = jnp.max(rest_l, axis=-1, keepdims=True)
        i2 = jnp.min(jnp.where(rest_l == v2, lane, 128), axis=-1, keepdims=True)
        e2 = jnp.exp(v2 - v1)
        den = 1.0 + e2
        gates_ref[...] = jnp.where(lane == i1, 1.0 / den, jnp.where(lane == i2, e2 / den, 0.0))


def _output_projection(o, hc, x2d, gm, gf, w_out, g1, b1, router_w=None):
    n, d = x2d.shape
    tm = min(TM_OUT, n)
    row = lambda a: a.reshape(1, -1)
    tok = lambda w: pl.BlockSpec((tm, w), lambda i: (i, 0))
    consts = [row(gm), row(gf), w_out.astype(BF16), row(g1), row(b1)]
    out_specs = [tok(d)]
    out_shape = [jax.ShapeDtypeStruct((n, d), F32)]
    if router_w is not None:
        consts.append(jnp.pad(router_w, ((0, 0), (0, 128 - N_EXPERTS))))
        out_specs.append(tok(128))
        out_shape.append(jax.ShapeDtypeStruct((n, 128), F32))
    return pl.pallas_call(
        functools.partial(_outproj_kernel, with_router=router_w is not None),
        grid=(n // tm,),
        in_specs=[tok(o.shape[1]), tok(CONV_CH), tok(d)] + [_const_spec(c.shape) for c in consts],
        out_specs=out_specs,
        out_shape=out_shape,
        compiler_params=pltpu.CompilerParams(dimension_semantics=("arbitrary",),
                                             vmem_limit_bytes=VMEM_LIMIT),
        name="output_projection",
    )(o, hc, x2d, *consts)


def _swiglu_tile(xb, w1, w3, w2):
    h1 = _dot(xb, w1)
    h3 = _dot(xb, w3)
    hid = (h1 * jax.nn.sigmoid(h1) * h3).astype(BF16)
    return _dot(hid, w2)


def _dense_ffn_kernel(x_ref, w1_ref, w3_ref, w2_ref, g_ref, b_ref, o_ref, *, f_chunk):
    x = x_ref[...]
    xb = x.astype(BF16)
    ff = None
    for c0 in range(0, w1_ref.shape[1], f_chunk):
        part = _swiglu_tile(xb, w1_ref[:, c0:c0 + f_chunk], w3_ref[:, c0:c0 + f_chunk],
                            w2_ref[c0:c0 + f_chunk, :])
        ff = part if ff is None else ff + part
    o_ref[...] = _layer_norm(ALPHA * x + ff, g_ref[...], b_ref[...])


def _dense_ffn(x2d, w1, w3, w2, g, b):
    n, d = x2d.shape
    tm = min(TM_FFN, n)
    f = w1.shape[1]
    f_chunk = f // 2 if (f // 2) % 128 == 0 else f
    row = lambda a: a.reshape(1, -1)
    consts = [w1.astype(BF16), w3.astype(BF16), w2.astype(BF16), row(g), row(b)]
    return pl.pallas_call(
        functools.partial(_dense_ffn_kernel, f_chunk=f_chunk),
        grid=(n // tm,),
        in_specs=[pl.BlockSpec((tm, d), lambda i: (i, 0))] + [_const_spec(c.shape) for c in consts],
        out_specs=pl.BlockSpec((tm, d), lambda i: (i, 0)),
        out_shape=jax.ShapeDtypeStruct((n, d), F32),
        compiler_params=pltpu.CompilerParams(dimension_semantics=("arbitrary",),
                                             vmem_limit_bytes=VMEM_LIMIT),
        name="dense_ffn",
    )(x2d, *consts)


def _moe_kernel(x_ref, gates_ref, w1_ref, w3_ref, w2_ref, g_ref, b_ref, o_ref, acc_ref):
    e = pl.program_id(1)

    @pl.when(e == 0)
    def _():
        acc_ref[...] = jnp.zeros_like(acc_ref)

    gates = gates_ref[...]
    lane = lax.broadcasted_iota(jnp.int32, gates.shape, 1)
    gate = jnp.sum(jnp.where(lane == e, gates, 0.0), axis=-1, keepdims=True)
    y = _swiglu_tile(x_ref[...].astype(BF16), w1_ref[0], w3_ref[0], w2_ref[0])
    acc_ref[...] += gate * y

    @pl.when(e == pl.num_programs(1) - 1)
    def _():
        o_ref[...] = _layer_norm(ALPHA * x_ref[...] + acc_ref[...], g_ref[...], b_ref[...])


def _moe_ffn(x2d, gates, w1, w3, w2, g, b):
    n, d = x2d.shape
    tm = min(TM_FFN, n)
    n_exp, _, f = w1.shape
    row = lambda a: a.reshape(1, -1)
    return pl.pallas_call(
        _moe_kernel,
        grid=(n // tm, n_exp),
        in_specs=[pl.BlockSpec((tm, d), lambda i, e: (i, 0)),
                  pl.BlockSpec((tm, 128), lambda i, e: (i, 0)),
                  pl.BlockSpec((1, d, f), lambda i, e: (e, 0, 0)),
                  pl.BlockSpec((1, d, f), lambda i, e: (e, 0, 0)),
                  pl.BlockSpec((1, f, d), lambda i, e: (e, 0, 0)),
                  _const_spec((1, d)), _const_spec((1, d))],
        out_specs=pl.BlockSpec((tm, d), lambda i, e: (i, 0)),
        out_shape=jax.ShapeDtypeStruct((n, d), F32),
        scratch_shapes=[pltpu.VMEM((tm, d), F32)],
        compiler_params=pltpu.CompilerParams(dimension_semantics=("arbitrary", "arbitrary"),
                                             vmem_limit_bytes=VMEM_LIMIT),
        name="moe_ffn",
    )(x2d, gates, w1.astype(BF16), w3.astype(BF16), w2.astype(BF16), row(g), row(b))


def kernel(x, positions, w_in, mla_q_norm_g, w_uq, mla_kv_norm_g, w_ukv, fox_forget_b, conv_w, conv_b,
           conv_norm_g, conv_norm_b, mla_out_norm_g, fox_out_norm_g, w_out, ln1_g, ln1_b, dense_w1,
           dense_w3, dense_w2, router_w, expert_w1, expert_w3, expert_w2, ln2_g, ln2_b):
    batch, seq, d = x.shape
    assert d == D_MODEL and seq % TQ == 0 and seq % min(TM_IN, seq) == 0
    depth = w_in.shape[0]
    tabs = _rope_tables(positions)
    h = x.reshape(batch * seq, d)
    for layer in range(depth):
        pw = _prep_inproj_weights(w_in[layer], w_uq[layer], w_ukv[layer], fox_forget_b[layer])
        q_t, k, v_t, hc = _input_projection(
            h, tabs, pw, mla_q_norm_g[layer], mla_kv_norm_g[layer], conv_w[layer], conv_b[layer],
            conv_norm_g[layer], conv_norm_b[layer], seq)
        o = _attention(q_t, k, v_t, batch, seq)
        j = layer // 2
        if layer % 2 == 0:
            h = _output_projection(o, hc, h, mla_out_norm_g[layer], fox_out_norm_g[layer], w_out[layer],
                                   ln1_g[layer], ln1_b[layer])[0]
            h = _dense_ffn(h, dense_w1[j], dense_w3[j], dense_w2[j], ln2_g[layer], ln2_b[layer])
        else:
            h, gates = _output_projection(o, hc, h, mla_out_norm_g[layer], fox_out_norm_g[layer],
                                          w_out[layer], ln1_g[layer], ln1_b[layer], router_w[j])
            h = _moe_ffn(h, gates, expert_w1[j], expert_w3[j], expert_w2[j], ln2_g[layer], ln2_b[layer])
    return h.reshape(batch, seq, d)
```

```python
import functools
import math

import numpy as np
import jax
import jax.numpy as jnp
from jax import lax
from jax.experimental import pallas as pl
from jax.experimental.pallas import tpu as pltpu

F32 = jnp.float32
BF16 = jnp.bfloat16

D_MODEL = 1024
DEPTH = 4
MLA_HEADS = 8
MLA_NOPE = 64
MLA_ROPE = 32
MLA_V = 64
MLA_Q_RANK = 256
MLA_KV_RANK = 128
ROPE_THETA = 10000.0
FOX_HEADS = 4
FOX_DIM = 64
CONV_CH = 256
CONV_GROUPS = 4
CONV_WIDTH = 31
MLA_WIDTH = MLA_HEADS * MLA_V
FOX_WIDTH = FOX_HEADS * FOX_DIM
N_EXPERTS = 8
ALPHA = (2.0 * DEPTH) ** 0.25
NORM_EPS = 1e-5
LOG2E = math.log2(math.e)

HEAD_PAD = 128
N_HEADS = MLA_HEADS + FOX_HEADS
V_DIM = 64
V_ROWS = 80
CONV_HALO = 32
VMEM_LIMIT = 56 * 1024 * 1024

TQ = 512
TK = 256
TM_IN = 512
TM_OUT = 512
TM_FFN = 512
TR_MOE = 512
TM_MOE = 256


def _nt_dot(a, b):
    return lax.dot_general(a, b, (((1,), (1,)), ((), ())), preferred_element_type=F32)


def _dot(a, b):
    return jnp.dot(a, b, preferred_element_type=F32)


def _split2_dot(a, m_bf16):
    hi = a.astype(BF16)
    lo = (a - hi.astype(F32)).astype(BF16)
    return _dot(hi, m_bf16) + _dot(lo, m_bf16)


def _split3(a):
    hi = a.astype(BF16).astype(F32)
    r1 = a - hi
    mid = r1.astype(BF16).astype(F32)
    lo = (r1 - mid).astype(BF16).astype(F32)
    return hi, mid, lo


def _const_spec(shape):
    nd = len(shape)
    return pl.BlockSpec(shape, lambda *_: (0,) * nd, pipeline_mode=pl.Buffered(1))


def _rope_kernel(pos_ref, invf_ref, c_ref, s_ref, ct_ref, st_ref):
    pos = pos_ref[...].astype(F32)
    ang = invf_ref[...] * pos
    cos = jnp.cos(ang)
    sin = jnp.sin(ang)
    tn = pos.shape[1]
    ct = jnp.concatenate([jnp.ones((MLA_NOPE, tn), F32), cos, cos, jnp.zeros((32, tn), F32)], axis=0)
    st = jnp.concatenate([jnp.zeros((MLA_NOPE, tn), F32), sin, sin, jnp.zeros((32, tn), F32)], axis=0)
    ct_ref[...] = ct
    st_ref[...] = st
    c_ref[...] = ct.T
    s_ref[...] = st.T


def _rope_tables(positions):
    n = positions.size
    tn = min(512, n)
    inv_freq = ROPE_THETA ** (-jnp.arange(0, MLA_ROPE, 2, dtype=F32) / MLA_ROPE)
    return pl.pallas_call(
        _rope_kernel,
        grid=(n // tn,),
        in_specs=[pl.BlockSpec((1, tn), lambda i: (0, i)),
                  pl.BlockSpec((MLA_ROPE // 2, 1), lambda i: (0, 0))],
        out_specs=[pl.BlockSpec((tn, HEAD_PAD), lambda i: (i, 0)),
                   pl.BlockSpec((tn, HEAD_PAD), lambda i: (i, 0)),
                   pl.BlockSpec((HEAD_PAD, tn), lambda i: (0, i)),
                   pl.BlockSpec((HEAD_PAD, tn), lambda i: (0, i))],
        out_shape=[jax.ShapeDtypeStruct((n, HEAD_PAD), F32),
                   jax.ShapeDtypeStruct((n, HEAD_PAD), F32),
                   jax.ShapeDtypeStruct((HEAD_PAD, n), F32),
                   jax.ShapeDtypeStruct((HEAD_PAD, n), F32)],
        name="rope_tables",
    )(positions.reshape(1, n), inv_freq.reshape(-1, 1))


_A_CQ = 0
_A_CKV = _A_CQ + MLA_Q_RANK
_A_KR = _A_CKV + MLA_KV_RANK
_A_KRR = _A_KR + HEAD_PAD
_A_FK = _A_KRR + HEAD_PAD
_A_CA = _A_FK + FOX_HEADS * HEAD_PAD
_A_CG = _A_CA + CONV_CH
_A_COLS = _A_CG + CONV_CH
_AUG_ROWS = 8
_F_ROWS = 16


def _rms(x, g):
    ms = jnp.mean(jnp.square(x), axis=-1, keepdims=True)
    return x * lax.rsqrt(ms + NORM_EPS) * g


def _inproj_kernel(x_ref, c_ref, s_ref, ct_ref, st_ref, wa_ref, wfq_ref, wfv_ref, wf_ref, fb_ref,
                   gq_ref, wuq_ref, wuqr_ref, gkv_ref, wuk_ref, wuv_ref,
                   cw_ref, cb_ref, cng_ref, cnb_ref, gmat_ref,
                   qt_ref, k_ref, vt_ref, hc_ref,
                   hbuf, cbuf, fcarry, *, tiles_per_seq, tm):
    i = pl.program_id(0)

    @pl.when(i % tiles_per_seq == 0)
    def _():
        hbuf[0:CONV_HALO, :] = jnp.zeros((CONV_HALO, CONV_CH), F32)
        fcarry[...] = jnp.zeros_like(fcarry)

    xb = x_ref[...].astype(BF16)
    p1 = _dot(xb, wa_ref[...])
    cos_t = c_ref[...]
    sin_t = s_ref[...]
    cos_tt = ct_ref[...]
    sin_tt = st_ref[...]

    cqn = _rms(p1[:, _A_CQ:_A_CQ + MLA_Q_RANK], gq_ref[...]).astype(BF16)
    q_t = _nt_dot(wuq_ref[...], cqn)
    q_rot_t = _nt_dot(wuqr_ref[...], cqn)
    mla_scale = (MLA_NOPE + MLA_ROPE) ** -0.5 * LOG2E
    for h in range(MLA_HEADS):
        rows = slice(h * HEAD_PAD, (h + 1) * HEAD_PAD)
        qh = (q_t[rows, :] * cos_tt + q_rot_t[rows, :] * sin_tt) * mla_scale
        for c in range(tm // TQ):
            qt_ref[c, rows, :] = qh[:, c * TQ:(c + 1) * TQ].astype(BF16)

    ckvn = _rms(p1[:, _A_CKV:_A_CKV + MLA_KV_RANK], gkv_ref[...]).astype(BF16)
    k_nope = _dot(ckvn, wuk_ref[...])
    k_rope = p1[:, _A_KR:_A_KR + HEAD_PAD] * cos_t + p1[:, _A_KRR:_A_KRR + HEAD_PAD] * sin_t
    for h in range(MLA_HEADS):
        cols = slice(h * HEAD_PAD, (h + 1) * HEAD_PAD)
        k_ref[:, cols] = (k_nope[:, cols] + k_rope).astype(BF16)
    v_t = _nt_dot(wuv_ref[...], ckvn)
    fv_t = _nt_dot(wfv_ref[...], xb)
    ones_blk = jnp.where(lax.broadcasted_iota(jnp.int32, (V_ROWS - V_DIM, tm), 0) == 0, 1.0, 0.0)
    for h in range(N_HEADS):
        src = v_t if h < MLA_HEADS else fv_t
        r0 = (h if h < MLA_HEADS else h - MLA_HEADS) * V_DIM
        vh = jnp.concatenate([src[r0:r0 + V_DIM, :], ones_blk], axis=0).astype(BF16)
        for c in range(tm // TK):
            vt_ref[c, h * V_ROWS:(h + 1) * V_ROWS, :] = vh[:, c * TK:(c + 1) * TK]

    z = _nt_dot(wf_ref[...], xb) + fb_ref[...]
    logf = (jnp.minimum(z, 0.0) - jnp.log1p(jnp.exp(-jnp.abs(z)))) * LOG2E
    r_i = lax.broadcasted_iota(jnp.int32, (tm, tm), 0)
    c_i = lax.broadcasted_iota(jnp.int32, (tm, tm), 1)
    upper = jnp.where(r_i <= c_i, 1.0, 0.0).astype(BF16)
    l_hi, l_mid, l_lo = _split3(logf)
    cum = (_dot(l_hi.astype(BF16), upper) + _dot(l_mid.astype(BF16), upper)
           + _dot(l_lo.astype(BF16), upper))
    f_cum = cum + fcarry[:, 0:1]
    fcarry[...] = jnp.broadcast_to(f_cum[:, tm - 1:tm], fcarry.shape)
    f_hi, f_mid, f_lo = _split3(f_cum)

    fq_t = _nt_dot(wfq_ref[...], xb)
    row8 = lax.broadcasted_iota(jnp.int32, (_AUG_ROWS, tm), 0)
    fox_scale = FOX_DIM ** -0.5 * LOG2E
    for h in range(FOX_HEADS):
        bh = lambda a: jnp.broadcast_to(a[h:h + 1, :], (_AUG_ROWS, tm))
        aug_q = jnp.where(row8 == 0, bh(f_hi), jnp.where(row8 == 1, bh(f_mid), jnp.where(
            row8 == 2, bh(f_lo), jnp.where(row8 < 6, 1.0, 0.0))))
        aug_k = jnp.where(row8 < 3, 1.0, jnp.where(row8 == 3, -bh(f_hi), jnp.where(
            row8 == 4, -bh(f_mid), jnp.where(row8 == 5, -bh(f_lo), 0.0))))
        pad = jnp.zeros((HEAD_PAD - FOX_DIM - _AUG_ROWS, tm), F32)
        qh = jnp.concatenate(
            [fq_t[h * HEAD_PAD:h * HEAD_PAD + FOX_DIM, :] * fox_scale, aug_q, pad], axis=0)
        rows = slice((MLA_HEADS + h) * HEAD_PAD, (MLA_HEADS + h + 1) * HEAD_PAD)
        for c in range(tm // TQ):
            qt_ref[c, rows, :] = qh[:, c * TQ:(c + 1) * TQ].astype(BF16)
        kaug_t = jnp.concatenate([jnp.zeros((FOX_DIM, tm), F32), aug_k, pad], axis=0)
        fk = p1[:, _A_FK + h * HEAD_PAD:_A_FK + (h + 1) * HEAD_PAD]
        k_ref[:, rows] = (fk + kaug_t.T).astype(BF16)

    a = p1[:, _A_CA:_A_CA + CONV_CH]
    g = p1[:, _A_CG:_A_CG + CONV_CH]
    hbuf[CONV_HALO:CONV_HALO + tm, :] = a * jax.nn.sigmoid(g)
    chunk = 64
    first = CONV_HALO - (CONV_WIDTH - 1)
    for c0 in range(0, tm, chunk):
        acc = jnp.zeros((chunk, CONV_CH), F32)
        for r in range(8):
            offs = [o for o in range(first, first + CONV_WIDTH) if o % 8 == r]
            seg = hbuf[c0 + r:c0 + max(offs) + chunk, :]
            for o in offs:
                j = o - first
                a8 = o - r
                acc = acc + cw_ref[j:j + 1, :] * seg[a8:a8 + chunk, :]
        cbuf[c0:c0 + chunk, :] = acc
    hbuf[0:CONV_HALO, :] = hbuf[tm:tm + CONV_HALO, :]
    hv = cbuf[...] + cb_ref[...]
    gm = gmat_ref[...]
    mu = _split2_dot(hv, gm)
    d = hv - mu
    var = _split2_dot(d * d, gm)
    hn = d * lax.rsqrt(var + NORM_EPS) * cng_ref[...] + cnb_ref[...]
    hc_ref[...] = (hn * jax.nn.sigmoid(hn)).astype(BF16)


def _prep_inproj_weights(w_in, w_uq, w_ukv, fox_forget_b):
    o = np.cumsum((0, MLA_Q_RANK, MLA_KV_RANK, MLA_ROPE, FOX_WIDTH, FOX_WIDTH, FOX_WIDTH, FOX_HEADS,
                   2 * CONV_CH))
    w_cq, w_ckv, w_kr, w_fq, w_fk, w_fv, w_f, w_cv = (w_in[:, o[i]:o[i + 1]] for i in range(8))
    d = w_in.shape[0]
    half = MLA_ROPE // 2

    def rot_cols(w):
        return jnp.concatenate([-w[..., half:], w[..., :half]], axis=-1)

    def rope_block(w):
        return jnp.pad(w, ((0, 0), (MLA_NOPE, HEAD_PAD - MLA_NOPE - MLA_ROPE)))

    w_fk_pad = jnp.pad(w_fk.reshape(d, FOX_HEADS, FOX_DIM), ((0, 0), (0, 0), (0, HEAD_PAD - FOX_DIM)))
    wa = jnp.concatenate([w_cq, w_ckv, rope_block(w_kr), rope_block(rot_cols(w_kr)),
                          w_fk_pad.reshape(d, FOX_HEADS * HEAD_PAD), w_cv], axis=1)
    w_fq_pad = jnp.pad(w_fq.reshape(d, FOX_HEADS, FOX_DIM), ((0, 0), (0, 0), (0, HEAD_PAD - FOX_DIM)))
    wfq_t = w_fq_pad.reshape(d, FOX_HEADS * HEAD_PAD).T
    wfv_t = w_fv.T
    wf_t = jnp.pad(w_f, ((0, 0), (0, _F_ROWS - FOX_HEADS))).T
    fb = jnp.pad(fox_forget_b, (0, _F_ROWS - FOX_HEADS)).reshape(_F_ROWS, 1)

    uq = w_uq.reshape(MLA_Q_RANK, MLA_HEADS, MLA_NOPE + MLA_ROPE)
    uq_nope, uq_rope = uq[..., :MLA_NOPE], uq[..., MLA_NOPE:]
    tail = ((0, 0), (0, 0), (0, HEAD_PAD - MLA_NOPE - MLA_ROPE))
    uq_pad = jnp.pad(jnp.concatenate([uq_nope, uq_rope], axis=-1), tail)
    uq_rot_pad = jnp.pad(jnp.concatenate([jnp.zeros_like(uq_nope), rot_cols(uq_rope)], axis=-1), tail)
    wuq_t = uq_pad.reshape(MLA_Q_RANK, MLA_HEADS * HEAD_PAD).T
    wuqr_t = uq_rot_pad.reshape(MLA_Q_RANK, MLA_HEADS * HEAD_PAD).T
    ukv = w_ukv.reshape(MLA_KV_RANK, MLA_HEADS, MLA_NOPE + MLA_V)
    wuk = jnp.pad(ukv[..., :MLA_NOPE], ((0, 0), (0, 0), (0, HEAD_PAD - MLA_NOPE))).reshape(
        MLA_KV_RANK, MLA_HEADS * HEAD_PAD)
    wuv_t = ukv[..., MLA_NOPE:].reshape(MLA_KV_RANK, MLA_WIDTH).T
    bf = lambda a: a.astype(BF16)
    return dict(wa=bf(wa), wfq=bf(wfq_t), wfv=bf(wfv_t), wf=bf(wf_t), fb=fb, wuq=bf(wuq_t),
                wuqr=bf(wuqr_t), wuk=bf(wuk), wuv=bf(wuv_t))


def _input_projection(x2d, tabs, pw, gq, gkv, conv_w, conv_b, conv_ng, conv_nb, seq):
    n, d = x2d.shape
    tm = min(TM_IN, seq)
    cos_t, sin_t, cos_tt, sin_tt = tabs
    gidx = np.arange(CONV_CH) // (CONV_CH // CONV_GROUPS)
    gmat = jnp.asarray((gidx[:, None] == gidx[None, :]) / (CONV_CH // CONV_GROUPS), BF16)
    cw = jnp.pad(conv_w, ((0, 32 - CONV_WIDTH), (0, 0)))
    row = lambda a: a.reshape(1, -1)
    tok = lambda w: pl.BlockSpec((tm, w), lambda i: (i, 0))
    tok_t = lambda r: pl.BlockSpec((r, tm), lambda i: (0, i))
    consts = [pw["wa"], pw["wfq"], pw["wfv"], pw["wf"], pw["fb"], row(gq), pw["wuq"], pw["wuqr"],
              row(gkv), pw["wuk"], pw["wuv"], cw, row(conv_b), row(conv_ng), row(conv_nb), gmat]
    kern = functools.partial(_inproj_kernel, tiles_per_seq=seq // tm, tm=tm)
    return pl.pallas_call(
        kern,
        grid=(n // tm,),
        in_specs=[tok(d), tok(HEAD_PAD), tok(HEAD_PAD), tok_t(HEAD_PAD), tok_t(HEAD_PAD)]
        + [_const_spec(c.shape) for c in consts],
        out_specs=[pl.BlockSpec((tm // TQ, N_HEADS * HEAD_PAD, TQ), lambda i: (i, 0, 0)),
                   tok(N_HEADS * HEAD_PAD),
                   pl.BlockSpec((tm // TK, N_HEADS * V_ROWS, TK), lambda i: (i, 0, 0)),
                   tok(CONV_CH)],
        out_shape=[jax.ShapeDtypeStruct((n // TQ, N_HEADS * HEAD_PAD, TQ), BF16),
                   jax.ShapeDtypeStruct((n, N_HEADS * HEAD_PAD), BF16),
                   jax.ShapeDtypeStruct((n // TK, N_HEADS * V_ROWS, TK), BF16),
                   jax.ShapeDtypeStruct((n, CONV_CH), BF16)],
        scratch_shapes=[pltpu.VMEM((CONV_HALO + tm, CONV_CH), F32),
                        pltpu.VMEM((tm, CONV_CH), F32),
                        pltpu.VMEM((_F_ROWS, 128), F32)],
        compiler_params=pltpu.CompilerParams(dimension_semantics=("arbitrary",),
                                             vmem_limit_bytes=VMEM_LIMIT),
        name="input_projection",
    )(x2d, cos_t, sin_t, cos_tt, sin_tt, *consts)


def _attn_kernel(qt_ref, k_ref, vt_ref, o_ref, s00, s01, s10, s11, p00, p01, p10, p11, acc0, acc1, diff_ref,
                 *, n_tiles):
    assert TQ == 2 * TK
    s_ref = ((s00, s01), (s10, s11))
    p_ref = ((p00, p01), (p10, p11))
    acc_ref = (acc0, acc1)
    diff_ref[...] = (lax.broadcasted_iota(jnp.int32, (TK, TQ), 1)
                     - lax.broadcasted_iota(jnp.int32, (TK, TQ), 0))

    def q_tile(i, carry):

        def scores(j, slot):
            row0 = pl.multiple_of(j * TK, TK)
            block_max = []
            for h in range(2):
                s = _dot(k_ref[pl.ds(row0, TK), h * HEAD_PAD:(h + 1) * HEAD_PAD],
                         qt_ref[i, h * HEAD_PAD:(h + 1) * HEAD_PAD, :])
                s_ref[slot][h][...] = s
                block_max.append(jnp.max(s, axis=0, keepdims=True))
            return block_max

        def softmax(slot, m, block_max):
            m_new = [jnp.maximum(m[h], block_max[h]) for h in range(2)]
            for h in range(2):
                p_ref[slot][h][...] = jnp.exp2(s_ref[slot][h][...] - m_new[h]).astype(BF16)
            return m_new, [jnp.exp2(m[h] - m_new[h]) for h in range(2)]

        def softmax_diag(slot, m, dg):
            m_new, alpha = [], []
            for h in range(2):
                s = jnp.where(diff_ref[...] >= dg * TK, s_ref[slot][h][...], -jnp.inf)
                m_new.append(jnp.maximum(m[h], jnp.max(s, axis=0, keepdims=True)))
                p_ref[slot][h][...] = jnp.exp2(s - m_new[h]).astype(BF16)
                alpha.append(jnp.exp2(m[h] - m_new[h]))
            return m_new, alpha

        def values(j, slot, alpha):
            for h in range(2):
                acc_ref[h][...] = alpha[h] * acc_ref[h][...] + _dot(
                    vt_ref[j, h * V_ROWS:(h + 1) * V_ROWS, :], p_ref[slot][h][...])

        for h in range(2):
            acc_ref[h][...] = jnp.zeros_like(acc_ref[h])
            p_ref[1][h][...] = jnp.zeros_like(p_ref[1][h])
        bm0 = scores(0, 0)
        m0 = [jnp.full((1, TQ), -1e30, F32)] * 2
        a0 = [jnp.ones((1, TQ), F32)] * 2

        def pair(u, state):
            m, alpha, bm_t = list(state[0:2]), list(state[2:4]), list(state[4:6])
            t = 2 * u
            m, alpha_t = softmax(0, m, bm_t)
            values(jnp.maximum(t - 1, 0), 1, alpha)
            bm_t1 = scores(t + 1, 1)
            m, alpha_t1 = softmax(1, m, bm_t1)
            values(t, 0, alpha_t)
            bm_t2 = scores(t + 2, 0)
            return (*m, *alpha_t1, *bm_t2)

        state = lax.fori_loop(0, i, pair, (*m0, *a0, *bm0))
        m, alpha = list(state[0:2]), list(state[2:4])
        d0 = 2 * i
        values(jnp.maximum(d0 - 1, 0), 1, alpha)
        scores(d0 + 1, 1)
        m, alpha_d0 = softmax_diag(0, m, 0)
        m, alpha_d1 = softmax_diag(1, m, 1)
        values(d0, 0, alpha_d0)
        values(d0 + 1, 1, alpha_d1)
        out_t = jnp.concatenate(
            [acc_ref[h][0:V_DIM, :] / acc_ref[h][V_DIM:V_DIM + 1, :] for h in range(2)], axis=0)
        o_ref[pl.ds(pl.multiple_of(i * TQ, TQ), TQ), :] = out_t.T.astype(o_ref.dtype)
        return carry

    lax.fori_loop(0, n_tiles, q_tile, 0)


def _attention(q_t, k, v_t, batch, seq):
    n = k.shape[0]
    pairs = N_HEADS // 2
    return pl.pallas_call(
        functools.partial(_attn_kernel, n_tiles=seq // TQ),
        grid=(batch, pairs),
        in_specs=[pl.BlockSpec((seq // TQ, 2 * HEAD_PAD, TQ), lambda b, p: (b, p, 0)),
                  pl.BlockSpec((seq, 2 * HEAD_PAD), lambda b, p: (b, p)),
                  pl.BlockSpec((seq // TK, 2 * V_ROWS, TK), lambda b, p: (b, p, 0))],
        out_specs=pl.BlockSpec((seq, 2 * V_DIM), lambda b, p: (b, p)),
        out_shape=jax.ShapeDtypeStruct((n, N_HEADS * V_DIM), BF16),
        scratch_shapes=[pltpu.VMEM((TK, TQ), F32)] * 4 + [pltpu.VMEM((TK, TQ), BF16)] * 4
        + [pltpu.VMEM((V_ROWS, TQ), F32)] * 2 + [pltpu.VMEM((TK, TQ), jnp.int32)],
        compiler_params=pltpu.CompilerParams(dimension_semantics=("arbitrary", "arbitrary"),
                                             vmem_limit_bytes=VMEM_LIMIT),
        name="attention",
    )(q_t, k, v_t)


def _layer_norm(x, g, b):
    mu = jnp.mean(x, axis=-1, keepdims=True)
    d = x - mu
    var = jnp.mean(jnp.square(d), axis=-1, keepdims=True)
    return d * lax.rsqrt(var + NORM_EPS) * g + b


def _outproj_kernel(o_ref, hc_ref, x_ref, gm_ref, gf_ref, wo_ref, g1_ref, b1_ref, *rest, with_router):
    if with_router:
        rw_ref, x1_ref, route_ref, counts_ref, cnt_ref = rest
    else:
        (x1_ref,) = rest
    o = o_ref[...].astype(F32)
    mla = _rms(o[:, :MLA_WIDTH], gm_ref[...])
    fox = _rms(o[:, MLA_WIDTH:], gf_ref[...])
    mixed = jnp.concatenate([mla.astype(BF16), fox.astype(BF16), hc_ref[...]], axis=-1)
    y = _dot(mixed, wo_ref[...])
    x1 = _layer_norm(ALPHA * x_ref[...] + y, g1_ref[...], b1_ref[...])
    x1_ref[...] = x1
    if with_router:
        rw = rw_ref[...]
        x_hi = x1.astype(BF16)
        x_lo = (x1 - x_hi.astype(F32)).astype(BF16)
        w_hi = rw.astype(BF16)
        w_lo = (rw - w_hi.astype(F32)).astype(BF16)
        logits = _dot(x_hi, w_hi) + (_dot(x_lo, w_hi) + _dot(x_hi, w_lo))
        lane = lax.broadcasted_iota(jnp.int32, logits.shape, 1)
        logits = jnp.where(lane < N_EXPERTS, logits, -jnp.inf)
        v1 = jnp.max(logits, axis=-1, keepdims=True)
        i1 = jnp.min(jnp.where(logits == v1, lane, 128), axis=-1, keepdims=True)
        rest_l = jnp.where(lane == i1, -jnp.inf, logits)
        v2 = jnp.max(rest_l, axis=-1, keepdims=True)
        i2 = jnp.min(jnp.where(rest_l == v2, lane, 128), axis=-1, keepdims=True)
        e2 = jnp.exp(v2 - v1)
        den = 1.0 + e2

        @pl.when(pl.program_id(0) == 0)
        def _():
            cnt_ref[...] = jnp.zeros_like(cnt_ref)

        tm = logits.shape[0]
        sel = jnp.where(lane == i1, 1.0, jnp.where(lane == i2, 1.0, 0.0))
        r_i = lax.broadcasted_iota(jnp.int32, (tm, tm), 0)
        c_i = lax.broadcasted_iota(jnp.int32, (tm, tm), 1)
        lower = jnp.where(c_i < r_i, 1.0, 0.0).astype(BF16)
        before = cnt_ref[0:1, :]
        rank = _dot(lower, sel.astype(BF16)) + before
        total = before + jnp.sum(sel, axis=0, keepdims=True)
        cnt_ref[...] = jnp.broadcast_to(total, cnt_ref.shape)
        counts_ref[...] = jnp.broadcast_to(total, counts_ref.shape)
        r1 = jnp.sum(jnp.where(lane == i1, rank, 0.0), axis=-1, keepdims=True)
        r2 = jnp.sum(jnp.where(lane == i2, rank, 0.0), axis=-1, keepdims=True)
        cols = (i1.astype(F32), i2.astype(F32), r1, r2, 1.0 / den, e2 / den)
        route = jnp.zeros(logits.shape, F32)
        for c, v in enumerate(cols):
            route = jnp.where(lane == c, v, route)
        route_ref[...] = route


def _output_projection(o, hc, x2d, gm, gf, w_out, g1, b1, router_w=None):
    n, d = x2d.shape
    tm = min(TM_OUT, n)
    row = lambda a: a.reshape(1, -1)
    tok = lambda w: pl.BlockSpec((tm, w), lambda i: (i, 0))
    consts = [row(gm), row(gf), w_out.astype(BF16), row(g1), row(b1)]
    out_specs = [tok(d)]
    out_shape = [jax.ShapeDtypeStruct((n, d), F32)]
    scratch = []
    if router_w is not None:
        consts.append(jnp.pad(router_w, ((0, 0), (0, 128 - N_EXPERTS))))
        out_specs += [tok(128), pl.BlockSpec((8, 128), lambda i: (0, 0))]
        out_shape += [jax.ShapeDtypeStruct((n, 128), F32), jax.ShapeDtypeStruct((8, 128), F32)]
        scratch = [pltpu.VMEM((8, 128), F32)]
    return pl.pallas_call(
        functools.partial(_outproj_kernel, with_router=router_w is not None),
        grid=(n // tm,),
        in_specs=[tok(o.shape[1]), tok(CONV_CH), tok(d)] + [_const_spec(c.shape) for c in consts],
        out_specs=out_specs,
        out_shape=out_shape,
        scratch_shapes=scratch,
        compiler_params=pltpu.CompilerParams(dimension_semantics=("arbitrary",),
                                             vmem_limit_bytes=VMEM_LIMIT),
        name="output_projection",
    )(o, hc, x2d, *consts)


def _swiglu_tile(xb, w1, w3, w2):
    h1 = _dot(xb, w1)
    h3 = _dot(xb, w3)
    hid = (h1 * jax.nn.sigmoid(h1) * h3).astype(BF16)
    return _dot(hid, w2)


def _dense_ffn_kernel(x_ref, w1_ref, w3_ref, w2_ref, g_ref, b_ref, o_ref, *, f_chunk):
    x = x_ref[...]
    xb = x.astype(BF16)
    ff = None
    for c0 in range(0, w1_ref.shape[1], f_chunk):
        part = _swiglu_tile(xb, w1_ref[:, c0:c0 + f_chunk], w3_ref[:, c0:c0 + f_chunk],
                            w2_ref[c0:c0 + f_chunk, :])
        ff = part if ff is None else ff + part
    o_ref[...] = _layer_norm(ALPHA * x + ff, g_ref[...], b_ref[...])


def _dense_ffn(x2d, w1, w3, w2, g, b):
    n, d = x2d.shape
    tm = min(TM_FFN, n)
    f = w1.shape[1]
    f_chunk = f // 2 if (f // 2) % 128 == 0 else f
    row = lambda a: a.reshape(1, -1)
    consts = [w1.astype(BF16), w3.astype(BF16), w2.astype(BF16), row(g), row(b)]
    return pl.pallas_call(
        functools.partial(_dense_ffn_kernel, f_chunk=f_chunk),
        grid=(n // tm,),
        in_specs=[pl.BlockSpec((tm, d), lambda i: (i, 0))] + [_const_spec(c.shape) for c in consts],
        out_specs=pl.BlockSpec((tm, d), lambda i: (i, 0)),
        out_shape=jax.ShapeDtypeStruct((n, d), F32),
        compiler_params=pltpu.CompilerParams(dimension_semantics=("arbitrary",),
                                             vmem_limit_bytes=VMEM_LIMIT),
        name="dense_ffn",
    )(x2d, *consts)


def _dispatch_kernel(d1_ref, d2_ref, x_ref, xs_init_ref, xs_ref, sem, *, tm):
    del xs_init_ref
    base = pl.program_id(0) * tm

    def start(r, c):
        src = x_ref.at[pl.ds(r, 1)]
        pltpu.make_async_copy(src, xs_ref.at[pl.ds(d1_ref[base + r], 1)], sem.at[0]).start()
        pltpu.make_async_copy(src, xs_ref.at[pl.ds(d2_ref[base + r], 1)], sem.at[1]).start()
        return c

    def wait(r, c):
        for k in range(2):
            pltpu.make_async_copy(x_ref.at[pl.ds(0, 1)], xs_ref.at[pl.ds(0, 1)], sem.at[k]).wait()
        return c

    lax.fori_loop(0, tm, start, 0, unroll=8)
    lax.fori_loop(0, tm, wait, 0, unroll=8)


def _expert_kernel(te_ref, blk_ref, nu_ref, xs_ref, w1_ref, w3_ref, w2_ref, ys_ref):
    del te_ref, blk_ref
    used = pl.program_id(0) < nu_ref[0]

    @pl.when(used)
    def _():
        ys_ref[...] = _swiglu_tile(xs_ref[...].astype(BF16), w1_ref[0], w3_ref[0], w2_ref[0])

    @pl.when(jnp.logical_not(used))
    def _():
        ys_ref[...] = jnp.zeros_like(ys_ref)


def _combine_kernel(d1_ref, d2_ref, x_ref, route_ref, g_ref, b_ref, ys_ref, o_ref, ybuf, sem, *, tm):
    i = pl.program_id(0)
    n_steps = pl.num_programs(0)

    def issue(tile, slot):
        base = tile * tm

        def start(r, c):
            pltpu.make_async_copy(ys_ref.at[pl.ds(d1_ref[base + r], 1)], ybuf.at[slot, 0, pl.ds(r, 1)],
                                  sem.at[slot]).start()
            pltpu.make_async_copy(ys_ref.at[pl.ds(d2_ref[base + r], 1)], ybuf.at[slot, 1, pl.ds(r, 1)],
                                  sem.at[slot]).start()
            return c

        lax.fori_loop(0, tm, start, 0, unroll=8)

    @pl.when(i == 0)
    def _():
        issue(0, 0)

    @pl.when(i + 1 < n_steps)
    def _():
        issue(i + 1, (i + 1) % 2)

    slot = i % 2

    def wait(r, c):
        for k in range(2):
            pltpu.make_async_copy(ys_ref.at[pl.ds(0, 1)], ybuf.at[slot, k, pl.ds(0, 1)], sem.at[slot]).wait()
        return c

    lax.fori_loop(0, tm, wait, 0, unroll=8)
    route = route_ref[...]
    ff = route[:, 4:5] * ybuf[slot, 0] + route[:, 5:6] * ybuf[slot, 1]
    o_ref[...] = _layer_norm(ALPHA * x_ref[...] + ff, g_ref[...], b_ref[...])


def _moe_ffn(x2d, route, counts, w1, w3, w2, g, b):
    n, d = x2d.shape
    n_exp, _, f = w1.shape
    tr = min(TR_MOE, n)
    tm = min(TM_MOE, n)
    n_pad = 2 * n + n_exp * tr
    n_tiles = n_pad // tr
    i32 = jnp.int32

    cnt = counts[0, :n_exp].astype(i32)
    seg = (cnt + tr - 1) // tr * tr
    seg_end = jnp.cumsum(seg)
    seg_start = seg_end - seg
    e1, e2 = route[:, 0].astype(i32), route[:, 1].astype(i32)
    dest1 = seg_start[e1] + route[:, 2].astype(i32)
    dest2 = seg_start[e2] + route[:, 3].astype(i32)
    n_used = jnp.maximum(seg_end[-1] // tr, 1)
    tile = jnp.minimum(jnp.arange(n_tiles, dtype=i32), n_used - 1)
    tile_expert = jnp.minimum(jnp.sum(tile[:, None] * tr >= seg_end[None, :], axis=1), n_exp - 1).astype(i32)

    cparams = pltpu.CompilerParams(dimension_semantics=("arbitrary",), vmem_limit_bytes=VMEM_LIMIT)
    xs = pl.pallas_call(
        functools.partial(_dispatch_kernel, tm=tm),
        grid_spec=pltpu.PrefetchScalarGridSpec(
            num_scalar_prefetch=2, grid=(n // tm,),
            in_specs=[pl.BlockSpec((tm, d), lambda i, *_: (i, 0)), pl.BlockSpec(memory_space=pl.ANY)],
            out_specs=pl.BlockSpec(memory_space=pl.ANY),
            scratch_shapes=[pltpu.SemaphoreType.DMA((2,))]),
        out_shape=jax.ShapeDtypeStruct((n_pad, d), F32),
        input_output_aliases={3: 0},
        compiler_params=cparams,
        name="moe_dispatch",
    )(dest1, dest2, x2d, jnp.zeros((n_pad, d), F32))

    ys = pl.pallas_call(
        _expert_kernel,
        grid_spec=pltpu.PrefetchScalarGridSpec(
            num_scalar_prefetch=3, grid=(n_tiles,),
            in_specs=[pl.BlockSpec((tr, d), lambda i, te, blk, nu: (blk[i], 0)),
                      pl.BlockSpec((1, d, f), lambda i, te, blk, nu: (te[i], 0, 0)),
                      pl.BlockSpec((1, d, f), lambda i, te, blk, nu: (te[i], 0, 0)),
                      pl.BlockSpec((1, f, d), lambda i, te, blk, nu: (te[i], 0, 0))],
            out_specs=pl.BlockSpec((tr, d), lambda i, te, blk, nu: (i, 0))),
        out_shape=jax.ShapeDtypeStruct((n_pad, d), F32),
        compiler_params=cparams,
        name="moe_experts",
    )(tile_expert, tile, n_used.reshape(1), xs, w1.astype(BF16), w3.astype(BF16), w2.astype(BF16))

    row = lambda a: a.reshape(1, -1)
    return pl.pallas_call(
        functools.partial(_combine_kernel, tm=tm),
        grid_spec=pltpu.PrefetchScalarGridSpec(
            num_scalar_prefetch=2, grid=(n // tm,),
            in_specs=[pl.BlockSpec((tm, d), lambda i, *_: (i, 0)),
                      pl.BlockSpec((tm, 128), lambda i, *_: (i, 0)),
                      pl.BlockSpec((1, d), lambda i, *_: (0, 0)),
                      pl.BlockSpec((1, d), lambda i, *_: (0, 0)),
                      pl.BlockSpec(memory_space=pl.ANY)],
            out_specs=pl.BlockSpec((tm, d), lambda i, *_: (i, 0)),
            scratch_shapes=[pltpu.VMEM((2, 2, tm, d), F32), pltpu.SemaphoreType.DMA((2,))]),
        out_shape=jax.ShapeDtypeStruct((n, d), F32),
        compiler_params=cparams,
        name="moe_combine",
    )(dest1, dest2, x2d, route, row(g), row(b), ys)


def kernel(x, positions, w_in, mla_q_norm_g, w_uq, mla_kv_norm_g, w_ukv, fox_forget_b, conv_w, conv_b,
           conv_norm_g, conv_norm_b, mla_out_norm_g, fox_out_norm_g, w_out, ln1_g, ln1_b, dense_w1,
           dense_w3, dense_w2, router_w, expert_w1, expert_w3, expert_w2, ln2_g, ln2_b):
    batch, seq, d = x.shape
    assert d == D_MODEL and seq % TQ == 0 and seq % min(TM_IN, seq) == 0
    depth = w_in.shape[0]
    tabs = _rope_tables(positions)
    h = x.reshape(batch * seq, d)
    for layer in range(depth):
        pw = _prep_inproj_weights(w_in[layer], w_uq[layer], w_ukv[layer], fox_forget_b[layer])
        q_t, k, v_t, hc = _input_projection(
            h, tabs, pw, mla_q_norm_g[layer], mla_kv_norm_g[layer], conv_w[layer], conv_b[layer],
            conv_norm_g[layer], conv_norm_b[layer], seq)
        o = _attention(q_t, k, v_t, batch, seq)
        j = layer // 2
        if layer % 2 == 0:
            h = _output_projection(o, hc, h, mla_out_norm_g[layer], fox_out_norm_g[layer], w_out[layer],
                                   ln1_g[layer], ln1_b[layer])[0]
            h = _dense_ffn(h, dense_w1[j], dense_w3[j], dense_w2[j], ln2_g[layer], ln2_b[layer])
        else:
            h, route, counts = _output_projection(o, hc, h, mla_out_norm_g[layer], fox_out_norm_g[layer],
                                                  w_out[layer], ln1_g[layer], ln1_b[layer], router_w[j])
            h = _moe_ffn(h, route, counts, expert_w1[j], expert_w3[j], expert_w2[j], ln2_g[layer],
                         ln2_b[layer])
    return h.reshape(batch, seq, d)
```

```python
import functools
import math

import numpy as np
import jax
import jax.numpy as jnp
from jax import lax
from jax.experimental import pallas as pl
from jax.experimental.pallas import tpu as pltpu

F32 = jnp.float32
BF16 = jnp.bfloat16

D_MODEL = 1024
DEPTH = 4
MLA_HEADS = 8
MLA_NOPE = 64
MLA_ROPE = 32
MLA_V = 64
MLA_Q_RANK = 256
MLA_KV_RANK = 128
ROPE_THETA = 10000.0
FOX_HEADS = 4
FOX_DIM = 64
CONV_CH = 256
CONV_GROUPS = 4
CONV_WIDTH = 31
MLA_WIDTH = MLA_HEADS * MLA_V
FOX_WIDTH = FOX_HEADS * FOX_DIM
N_EXPERTS = 8
ALPHA = (2.0 * DEPTH) ** 0.25
NORM_EPS = 1e-5
LOG2E = math.log2(math.e)

HEAD_PAD = 128
N_HEADS = MLA_HEADS + FOX_HEADS
V_DIM = 64
V_ROWS = 80
CONV_HALO = 32
VMEM_LIMIT = 56 * 1024 * 1024

TQ = 512
TK = 256
HPS = 2
TM_IN = 512
TM_OUT = 512
TM_FFN = 512
TR_MOE = 512
TM_MOE = 256


def _nt_dot(a, b):
    return lax.dot_general(a, b, (((1,), (1,)), ((), ())), preferred_element_type=F32)


def _dot(a, b):
    return jnp.dot(a, b, preferred_element_type=F32)


def _split2_dot(a, m_bf16):
    hi = a.astype(BF16)
    lo = (a - hi.astype(F32)).astype(BF16)
    return _dot(hi, m_bf16) + _dot(lo, m_bf16)


def _split3(a):
    hi = a.astype(BF16).astype(F32)
    r1 = a - hi
    mid = r1.astype(BF16).astype(F32)
    lo = (r1 - mid).astype(BF16).astype(F32)
    return hi, mid, lo


def _const_spec(shape):
    nd = len(shape)
    return pl.BlockSpec(shape, lambda *_: (0,) * nd, pipeline_mode=pl.Buffered(1))


def _rope_kernel(pos_ref, invf_ref, c_ref, s_ref, ct_ref, st_ref):
    pos = pos_ref[...].astype(F32)
    ang = invf_ref[...] * pos
    cos = jnp.cos(ang)
    sin = jnp.sin(ang)
    tn = pos.shape[1]
    ct = jnp.concatenate([jnp.ones((MLA_NOPE, tn), F32), cos, cos, jnp.zeros((32, tn), F32)], axis=0)
    st = jnp.concatenate([jnp.zeros((MLA_NOPE, tn), F32), sin, sin, jnp.zeros((32, tn), F32)], axis=0)
    ct_ref[...] = ct
    st_ref[...] = st
    c_ref[...] = ct.T
    s_ref[...] = st.T


def _rope_tables(positions):
    n = positions.size
    tn = min(512, n)
    inv_freq = ROPE_THETA ** (-jnp.arange(0, MLA_ROPE, 2, dtype=F32) / MLA_ROPE)
    return pl.pallas_call(
        _rope_kernel,
        grid=(n // tn,),
        in_specs=[pl.BlockSpec((1, tn), lambda i: (0, i)),
                  pl.BlockSpec((MLA_ROPE // 2, 1), lambda i: (0, 0))],
        out_specs=[pl.BlockSpec((tn, HEAD_PAD), lambda i: (i, 0)),
                   pl.BlockSpec((tn, HEAD_PAD), lambda i: (i, 0)),
                   pl.BlockSpec((HEAD_PAD, tn), lambda i: (0, i)),
                   pl.BlockSpec((HEAD_PAD, tn), lambda i: (0, i))],
        out_shape=[jax.ShapeDtypeStruct((n, HEAD_PAD), F32),
                   jax.ShapeDtypeStruct((n, HEAD_PAD), F32),
                   jax.ShapeDtypeStruct((HEAD_PAD, n), F32),
                   jax.ShapeDtypeStruct((HEAD_PAD, n), F32)],
        name="rope_tables",
    )(positions.reshape(1, n), inv_freq.reshape(-1, 1))


_A_CQ = 0
_A_CKV = _A_CQ + MLA_Q_RANK
_A_KR = _A_CKV + MLA_KV_RANK
_A_KRR = _A_KR + HEAD_PAD
_A_FK = _A_KRR + HEAD_PAD
_A_CA = _A_FK + FOX_HEADS * HEAD_PAD
_A_CG = _A_CA + CONV_CH
_A_COLS = _A_CG + CONV_CH
_AUG_ROWS = 8
_F_ROWS = 16


def _rms(x, g):
    ms = jnp.mean(jnp.square(x), axis=-1, keepdims=True)
    return x * lax.rsqrt(ms + NORM_EPS) * g


def _inproj_kernel(x_ref, c_ref, s_ref, ct_ref, st_ref, wa_ref, wfq_ref, wfv_ref, wf_ref, fb_ref,
                   gq_ref, wuq_ref, wuqr_ref, gkv_ref, wuk_ref, wuv_ref,
                   cw_ref, cb_ref, cng_ref, cnb_ref, gmat_ref,
                   qt_ref, k_ref, vt_ref, hc_ref,
                   hbuf, hsh, cbuf, fcarry, *, tiles_per_seq, tm):
    i = pl.program_id(0)

    @pl.when(i % tiles_per_seq == 0)
    def _():
        hbuf[0:CONV_HALO, :] = jnp.zeros((CONV_HALO, CONV_CH), F32)
        fcarry[...] = jnp.zeros_like(fcarry)

    xb = x_ref[...].astype(BF16)
    p1 = _dot(xb, wa_ref[...])
    cos_t = c_ref[...]
    sin_t = s_ref[...]
    cos_tt = ct_ref[...]
    sin_tt = st_ref[...]

    cqn = _rms(p1[:, _A_CQ:_A_CQ + MLA_Q_RANK], gq_ref[...]).astype(BF16)
    q_t = _nt_dot(wuq_ref[...], cqn)
    q_rot_t = _nt_dot(wuqr_ref[...], cqn)
    mla_scale = (MLA_NOPE + MLA_ROPE) ** -0.5 * LOG2E
    for h in range(MLA_HEADS):
        rows = slice(h * HEAD_PAD, (h + 1) * HEAD_PAD)
        qh = (q_t[rows, :] * cos_tt + q_rot_t[rows, :] * sin_tt) * mla_scale
        for c in range(tm // TQ):
            qt_ref[c, rows, :] = qh[:, c * TQ:(c + 1) * TQ].astype(BF16)

    ckvn = _rms(p1[:, _A_CKV:_A_CKV + MLA_KV_RANK], gkv_ref[...]).astype(BF16)
    k_nope = _dot(ckvn, wuk_ref[...])
    k_rope = p1[:, _A_KR:_A_KR + HEAD_PAD] * cos_t + p1[:, _A_KRR:_A_KRR + HEAD_PAD] * sin_t
    for h in range(MLA_HEADS):
        cols = slice(h * HEAD_PAD, (h + 1) * HEAD_PAD)
        k_ref[:, cols] = (k_nope[:, cols] + k_rope).astype(BF16)
    v_t = _nt_dot(wuv_ref[...], ckvn)
    fv_t = _nt_dot(wfv_ref[...], xb)
    ones_blk = jnp.where(lax.broadcasted_iota(jnp.int32, (V_ROWS - V_DIM, tm), 0) == 0, 1.0, 0.0)
    for h in range(N_HEADS):
        src = v_t if h < MLA_HEADS else fv_t
        r0 = (h if h < MLA_HEADS else h - MLA_HEADS) * V_DIM
        vh = jnp.concatenate([src[r0:r0 + V_DIM, :], ones_blk], axis=0).astype(BF16)
        for c in range(tm // TK):
            vt_ref[c, h * V_ROWS:(h + 1) * V_ROWS, :] = vh[:, c * TK:(c + 1) * TK]

    z = _nt_dot(wf_ref[...], xb) + fb_ref[...]
    logf = (jnp.minimum(z, 0.0) - jnp.log1p(jnp.exp(-jnp.abs(z)))) * LOG2E
    r_i = lax.broadcasted_iota(jnp.int32, (tm, tm), 0)
    c_i = lax.broadcasted_iota(jnp.int32, (tm, tm), 1)
    upper = jnp.where(r_i <= c_i, 1.0, 0.0).astype(BF16)
    l_hi, l_mid, l_lo = _split3(logf)
    cum = (_dot(l_hi.astype(BF16), upper) + _dot(l_mid.astype(BF16), upper)
           + _dot(l_lo.astype(BF16), upper))
    f_cum = cum + fcarry[:, 0:1]
    fcarry[...] = jnp.broadcast_to(f_cum[:, tm - 1:tm], fcarry.shape)
    f_hi, f_mid, f_lo = _split3(f_cum)

    fq_t = _nt_dot(wfq_ref[...], xb)
    row8 = lax.broadcasted_iota(jnp.int32, (_AUG_ROWS, tm), 0)
    fox_scale = FOX_DIM ** -0.5 * LOG2E
    for h in range(FOX_HEADS):
        bh = lambda a: jnp.broadcast_to(a[h:h + 1, :], (_AUG_ROWS, tm))
        aug_q = jnp.where(row8 == 0, bh(f_hi), jnp.where(row8 == 1, bh(f_mid), jnp.where(
            row8 == 2, bh(f_lo), jnp.where(row8 < 6, 1.0, 0.0))))
        aug_k = jnp.where(row8 < 3, 1.0, jnp.where(row8 == 3, -bh(f_hi), jnp.where(
            row8 == 4, -bh(f_mid), jnp.where(row8 == 5, -bh(f_lo), 0.0))))
        pad = jnp.zeros((HEAD_PAD - FOX_DIM - _AUG_ROWS, tm), F32)
        qh = jnp.concatenate(
            [fq_t[h * HEAD_PAD:h * HEAD_PAD + FOX_DIM, :] * fox_scale, aug_q, pad], axis=0)
        rows = slice((MLA_HEADS + h) * HEAD_PAD, (MLA_HEADS + h + 1) * HEAD_PAD)
        for c in range(tm // TQ):
            qt_ref[c, rows, :] = qh[:, c * TQ:(c + 1) * TQ].astype(BF16)
        kaug_t = jnp.concatenate([jnp.zeros((FOX_DIM, tm), F32), aug_k, pad], axis=0)
        fk = p1[:, _A_FK + h * HEAD_PAD:_A_FK + (h + 1) * HEAD_PAD]
        k_ref[:, rows] = (fk + kaug_t.T).astype(BF16)

    a = p1[:, _A_CA:_A_CA + CONV_CH]
    g = p1[:, _A_CG:_A_CG + CONV_CH]
    hbuf[CONV_HALO:CONV_HALO + tm, :] = a * jax.nn.sigmoid(g)
    chunk = 64
    first = CONV_HALO - (CONV_WIDTH - 1)
    for r in range(1, 8):
        hsh[r - 1] = hbuf[r:r + tm + CONV_HALO - 8, :]
    for c0 in range(0, tm, chunk):
        acc = jnp.zeros((chunk, CONV_CH), F32)
        for o in range(first, first + CONV_WIDTH):
            r = o % 8
            row = c0 + o - r
            seg = hbuf[row:row + chunk, :] if r == 0 else hsh[r - 1, row:row + chunk, :]
            acc = acc + cw_ref[o - first:o - first + 1, :] * seg
        cbuf[c0:c0 + chunk, :] = acc
    hbuf[0:CONV_HALO, :] = hbuf[tm:tm + CONV_HALO, :]
    hv = cbuf[...] + cb_ref[...]
    gm = gmat_ref[...]
    mu = _split2_dot(hv, gm)
    d = hv - mu
    var = _split2_dot(d * d, gm)
    hn = d * lax.rsqrt(var + NORM_EPS) * cng_ref[...] + cnb_ref[...]
    hc_ref[...] = (hn * jax.nn.sigmoid(hn)).astype(BF16)


def _prep_inproj_weights(w_in, w_uq, w_ukv, fox_forget_b):
    o = np.cumsum((0, MLA_Q_RANK, MLA_KV_RANK, MLA_ROPE, FOX_WIDTH, FOX_WIDTH, FOX_WIDTH, FOX_HEADS,
                   2 * CONV_CH))
    w_cq, w_ckv, w_kr, w_fq, w_fk, w_fv, w_f, w_cv = (w_in[:, o[i]:o[i + 1]] for i in range(8))
    d = w_in.shape[0]
    half = MLA_ROPE // 2

    def rot_cols(w):
        return jnp.concatenate([-w[..., half:], w[..., :half]], axis=-1)

    def rope_block(w):
        return jnp.pad(w, ((0, 0), (MLA_NOPE, HEAD_PAD - MLA_NOPE - MLA_ROPE)))

    w_fk_pad = jnp.pad(w_fk.reshape(d, FOX_HEADS, FOX_DIM), ((0, 0), (0, 0), (0, HEAD_PAD - FOX_DIM)))
    wa = jnp.concatenate([w_cq, w_ckv, rope_block(w_kr), rope_block(rot_cols(w_kr)),
                          w_fk_pad.reshape(d, FOX_HEADS * HEAD_PAD), w_cv], axis=1)
    w_fq_pad = jnp.pad(w_fq.reshape(d, FOX_HEADS, FOX_DIM), ((0, 0), (0, 0), (0, HEAD_PAD - FOX_DIM)))
    wfq_t = w_fq_pad.reshape(d, FOX_HEADS * HEAD_PAD).T
    wfv_t = w_fv.T
    wf_t = jnp.pad(w_f, ((0, 0), (0, _F_ROWS - FOX_HEADS))).T
    fb = jnp.pad(fox_forget_b, (0, _F_ROWS - FOX_HEADS)).reshape(_F_ROWS, 1)

    uq = w_uq.reshape(MLA_Q_RANK, MLA_HEADS, MLA_NOPE + MLA_ROPE)
    uq_nope, uq_rope = uq[..., :MLA_NOPE], uq[..., MLA_NOPE:]
    tail = ((0, 0), (0, 0), (0, HEAD_PAD - MLA_NOPE - MLA_ROPE))
    uq_pad = jnp.pad(jnp.concatenate([uq_nope, uq_rope], axis=-1), tail)
    uq_rot_pad = jnp.pad(jnp.concatenate([jnp.zeros_like(uq_nope), rot_cols(uq_rope)], axis=-1), tail)
    wuq_t = uq_pad.reshape(MLA_Q_RANK, MLA_HEADS * HEAD_PAD).T
    wuqr_t = uq_rot_pad.reshape(MLA_Q_RANK, MLA_HEADS * HEAD_PAD).T
    ukv = w_ukv.reshape(MLA_KV_RANK, MLA_HEADS, MLA_NOPE + MLA_V)
    wuk = jnp.pad(ukv[..., :MLA_NOPE], ((0, 0), (0, 0), (0, HEAD_PAD - MLA_NOPE))).reshape(
        MLA_KV_RANK, MLA_HEADS * HEAD_PAD)
    wuv_t = ukv[..., MLA_NOPE:].reshape(MLA_KV_RANK, MLA_WIDTH).T
    bf = lambda a: a.astype(BF16)
    return dict(wa=bf(wa), wfq=bf(wfq_t), wfv=bf(wfv_t), wf=bf(wf_t), fb=fb, wuq=bf(wuq_t),
                wuqr=bf(wuqr_t), wuk=bf(wuk), wuv=bf(wuv_t))


def _input_projection(x2d, tabs, pw, gq, gkv, conv_w, conv_b, conv_ng, conv_nb, seq):
    n, d = x2d.shape
    tm = min(TM_IN, seq)
    cos_t, sin_t, cos_tt, sin_tt = tabs
    gidx = np.arange(CONV_CH) // (CONV_CH // CONV_GROUPS)
    gmat = jnp.asarray((gidx[:, None] == gidx[None, :]) / (CONV_CH // CONV_GROUPS), BF16)
    cw = jnp.pad(conv_w, ((0, 32 - CONV_WIDTH), (0, 0)))
    row = lambda a: a.reshape(1, -1)
    tok = lambda w: pl.BlockSpec((tm, w), lambda i: (i, 0))
    tok_t = lambda r: pl.BlockSpec((r, tm), lambda i: (0, i))
    consts = [pw["wa"], pw["wfq"], pw["wfv"], pw["wf"], pw["fb"], row(gq), pw["wuq"], pw["wuqr"],
              row(gkv), pw["wuk"], pw["wuv"], cw, row(conv_b), row(conv_ng), row(conv_nb), gmat]
    kern = functools.partial(_inproj_kernel, tiles_per_seq=seq // tm, tm=tm)
    return pl.pallas_call(
        kern,
        grid=(n // tm,),
        in_specs=[tok(d), tok(HEAD_PAD), tok(HEAD_PAD), tok_t(HEAD_PAD), tok_t(HEAD_PAD)]
        + [_const_spec(c.shape) for c in consts],
        out_specs=[pl.BlockSpec((tm // TQ, N_HEADS * HEAD_PAD, TQ), lambda i: (i, 0, 0)),
                   tok(N_HEADS * HEAD_PAD),
                   pl.BlockSpec((tm // TK, N_HEADS * V_ROWS, TK), lambda i: (i, 0, 0)),
                   tok(CONV_CH)],
        out_shape=[jax.ShapeDtypeStruct((n // TQ, N_HEADS * HEAD_PAD, TQ), BF16),
                   jax.ShapeDtypeStruct((n, N_HEADS * HEAD_PAD), BF16),
                   jax.ShapeDtypeStruct((n // TK, N_HEADS * V_ROWS, TK), BF16),
                   jax.ShapeDtypeStruct((n, CONV_CH), BF16)],
        scratch_shapes=[pltpu.VMEM((CONV_HALO + tm, CONV_CH), F32),
                        pltpu.VMEM((7, tm + CONV_HALO - 8, CONV_CH), F32),
                        pltpu.VMEM((tm, CONV_CH), F32),
                        pltpu.VMEM((_F_ROWS, 128), F32)],
        compiler_params=pltpu.CompilerParams(dimension_semantics=("arbitrary",),
                                             vmem_limit_bytes=VMEM_LIMIT),
        name="input_projection",
    )(x2d, cos_t, sin_t, cos_tt, sin_tt, *consts)


def _attn_kernel(qt_ref, k_ref, vt_ref, o_ref, *scratch, n_tiles):
    assert TQ == 2 * TK
    s_ref = (scratch[0:HPS], scratch[HPS:2 * HPS], scratch[2 * HPS:3 * HPS])
    p_ref = (scratch[3 * HPS:4 * HPS], scratch[4 * HPS:5 * HPS])
    acc_ref = scratch[5 * HPS:6 * HPS]
    diff_ref = scratch[6 * HPS]
    diff_ref[...] = (lax.broadcasted_iota(jnp.int32, (TK, TQ), 1)
                     - lax.broadcasted_iota(jnp.int32, (TK, TQ), 0))

    def tile_scores(tile, j, slot):
        row0 = pl.multiple_of(j * TK, TK)
        block_max = []
        for h in range(HPS):
            s = _dot(k_ref[pl.ds(row0, TK), h * HEAD_PAD:(h + 1) * HEAD_PAD],
                     qt_ref[tile, h * HEAD_PAD:(h + 1) * HEAD_PAD, :])
            s_ref[slot][h][...] = s
            block_max.append(jnp.max(s, axis=0, keepdims=True))
        return block_max

    def q_tile(i, bm0):
        scores = functools.partial(tile_scores, i)

        def softmax(slot, m, block_max):
            m_new = [jnp.maximum(m[h], block_max[h]) for h in range(HPS)]
            for h in range(HPS):
                p_ref[slot][h][...] = jnp.exp2(s_ref[slot][h][...] - m_new[h]).astype(BF16)
            return m_new, [jnp.exp2(m[h] - m_new[h]) for h in range(HPS)]

        def softmax_diag(s_slot, p_slot, m, dg):
            m_new, alpha = [], []
            for h in range(HPS):
                s = jnp.where(diff_ref[...] >= dg * TK, s_ref[s_slot][h][...], -jnp.inf)
                m_new.append(jnp.maximum(m[h], jnp.max(s, axis=0, keepdims=True)))
                p_ref[p_slot][h][...] = jnp.exp2(s - m_new[h]).astype(BF16)
                alpha.append(jnp.exp2(m[h] - m_new[h]))
            return m_new, alpha

        def values(j, slot, alpha):
            for h in range(HPS):
                acc_ref[h][...] = alpha[h] * acc_ref[h][...] + _dot(
                    vt_ref[j, h * V_ROWS:(h + 1) * V_ROWS, :], p_ref[slot][h][...])

        for h in range(HPS):
            acc_ref[h][...] = jnp.zeros_like(acc_ref[h])
            p_ref[1][h][...] = jnp.zeros_like(p_ref[1][h])
        m0 = [jnp.full((1, TQ), -1e30, F32)] * HPS
        a0 = [jnp.ones((1, TQ), F32)] * HPS

        def pair(u, state):
            m, alpha, bm_t = list(state[0:HPS]), list(state[HPS:2 * HPS]), list(state[2 * HPS:3 * HPS])
            t = 2 * u
            m, alpha_t = softmax(0, m, bm_t)
            values(jnp.maximum(t - 1, 0), 1, alpha)
            bm_t1 = scores(t + 1, 1)
            m, alpha_t1 = softmax(1, m, bm_t1)
            values(t, 0, alpha_t)
            bm_t2 = scores(t + 2, 0)
            return (*m, *alpha_t1, *bm_t2)

        state = lax.fori_loop(0, i // 2, lambda v, st: pair(2 * v + 1, pair(2 * v, st)), (*m0, *a0, *bm0))
        state = lax.fori_loop(i - i % 2, i, pair, state)
        m, alpha = list(state[0:HPS]), list(state[HPS:2 * HPS])
        d0 = 2 * i
        values(jnp.maximum(d0 - 1, 0), 1, alpha)
        nxt = jnp.minimum(i + 1, n_tiles - 1)
        m, alpha_d0 = softmax_diag(0, 0, m, 0)
        bm_next = tile_scores(nxt, 0, 0)
        m, alpha_d1 = softmax_diag(2, 1, m, 1)
        tile_scores(nxt, 2 * nxt + 1, 2)
        values(d0, 0, alpha_d0)
        values(d0 + 1, 1, alpha_d1)
        out_t = jnp.concatenate(
            [acc_ref[h][0:V_DIM, :] / acc_ref[h][V_DIM:V_DIM + 1, :] for h in range(HPS)], axis=0)
        o_ref[pl.ds(pl.multiple_of(i * TQ, TQ), TQ), :] = out_t.T.astype(o_ref.dtype)
        return tuple(bm_next)

    tile_scores(0, 1, 2)
    lax.fori_loop(0, n_tiles, q_tile, tuple(tile_scores(0, 0, 0)))


def _attention(q_t, k, v_t, batch, seq):
    n = k.shape[0]
    groups = N_HEADS // HPS
    return pl.pallas_call(
        functools.partial(_attn_kernel, n_tiles=seq // TQ),
        grid=(batch, groups),
        in_specs=[pl.BlockSpec((seq // TQ, HPS * HEAD_PAD, TQ), lambda b, p: (b, p, 0)),
                  pl.BlockSpec((seq, HPS * HEAD_PAD), lambda b, p: (b, p)),
                  pl.BlockSpec((seq // TK, HPS * V_ROWS, TK), lambda b, p: (b, p, 0))],
        out_specs=pl.BlockSpec((seq, HPS * V_DIM), lambda b, p: (b, p)),
        out_shape=jax.ShapeDtypeStruct((n, N_HEADS * V_DIM), BF16),
        scratch_shapes=[pltpu.VMEM((TK, TQ), F32)] * (3 * HPS) + [pltpu.VMEM((TK, TQ), BF16)] * (2 * HPS)
        + [pltpu.VMEM((V_ROWS, TQ), F32)] * HPS + [pltpu.VMEM((TK, TQ), jnp.int32)],
        compiler_params=pltpu.CompilerParams(dimension_semantics=("arbitrary", "arbitrary"),
                                             vmem_limit_bytes=VMEM_LIMIT),
        name="attention",
    )(q_t, k, v_t)


def _layer_norm(x, g, b):
    mu = jnp.mean(x, axis=-1, keepdims=True)
    d = x - mu
    var = jnp.mean(jnp.square(d), axis=-1, keepdims=True)
    return d * lax.rsqrt(var + NORM_EPS) * g + b


def _outproj_kernel(o_ref, hc_ref, x_ref, gm_ref, gf_ref, wo_ref, g1_ref, b1_ref, *rest, with_router):
    if with_router:
        rw_ref, x1_ref, route_ref, counts_ref, cnt_ref = rest
    else:
        (x1_ref,) = rest
    o = o_ref[...].astype(F32)
    mla = _rms(o[:, :MLA_WIDTH], gm_ref[...])
    fox = _rms(o[:, MLA_WIDTH:], gf_ref[...])
    mixed = jnp.concatenate([mla.astype(BF16), fox.astype(BF16), hc_ref[...]], axis=-1)
    y = _dot(mixed, wo_ref[...])
    x1 = _layer_norm(ALPHA * x_ref[...] + y, g1_ref[...], b1_ref[...])
    x1_ref[...] = x1
    if with_router:
        rw = rw_ref[...]
        x_hi = x1.astype(BF16)
        x_lo = (x1 - x_hi.astype(F32)).astype(BF16)
        w_hi = rw.astype(BF16)
        w_lo = (rw - w_hi.astype(F32)).astype(BF16)
        logits = _dot(x_hi, w_hi) + (_dot(x_lo, w_hi) + _dot(x_hi, w_lo))
        lane = lax.broadcasted_iota(jnp.int32, logits.shape, 1)
        logits = jnp.where(lane < N_EXPERTS, logits, -jnp.inf)
        v1 = jnp.max(logits, axis=-1, keepdims=True)
        i1 = jnp.min(jnp.where(logits == v1, lane, 128), axis=-1, keepdims=True)
        rest_l = jnp.where(lane == i1, -jnp.inf, logits)
        v2 = jnp.max(rest_l, axis=-1, keepdims=True)
        i2 = jnp.min(jnp.where(rest_l == v2, lane, 128), axis=-1, keepdims=True)
        e2 = jnp.exp(v2 - v1)
        den = 1.0 + e2

        @pl.when(pl.program_id(0) == 0)
        def _():
            cnt_ref[...] = jnp.zeros_like(cnt_ref)

        tm = logits.shape[0]
        sel = jnp.where(lane == i1, 1.0, jnp.where(lane == i2, 1.0, 0.0))
        r_i = lax.broadcasted_iota(jnp.int32, (tm, tm), 0)
        c_i = lax.broadcasted_iota(jnp.int32, (tm, tm), 1)
        lower = jnp.where(c_i < r_i, 1.0, 0.0).astype(BF16)
        before = cnt_ref[0:1, :]
        rank = _dot(lower, sel.astype(BF16)) + before
        total = before + jnp.sum(sel, axis=0, keepdims=True)
        cnt_ref[...] = jnp.broadcast_to(total, cnt_ref.shape)
        counts_ref[...] = jnp.broadcast_to(total, counts_ref.shape)
        r1 = jnp.sum(jnp.where(lane == i1, rank, 0.0), axis=-1, keepdims=True)
        r2 = jnp.sum(jnp.where(lane == i2, rank, 0.0), axis=-1, keepdims=True)
        cols = (i1.astype(F32), i2.astype(F32), r1, r2, 1.0 / den, e2 / den)
        route = jnp.zeros(logits.shape, F32)
        for c, v in enumerate(cols):
            route = jnp.where(lane == c, v, route)
        route_ref[...] = route


def _output_projection(o, hc, x2d, gm, gf, w_out, g1, b1, router_w=None):
    n, d = x2d.shape
    tm = min(TM_OUT, n)
    row = lambda a: a.reshape(1, -1)
    tok = lambda w: pl.BlockSpec((tm, w), lambda i: (i, 0))
    consts = [row(gm), row(gf), w_out.astype(BF16), row(g1), row(b1)]
    out_specs = [tok(d)]
    out_shape = [jax.ShapeDtypeStruct((n, d), F32)]
    scratch = []
    if router_w is not None:
        consts.append(jnp.pad(router_w, ((0, 0), (0, 128 - N_EXPERTS))))
        out_specs += [tok(128), pl.BlockSpec((8, 128), lambda i: (0, 0))]
        out_shape += [jax.ShapeDtypeStruct((n, 128), F32), jax.ShapeDtypeStruct((8, 128), F32)]
        scratch = [pltpu.VMEM((8, 128), F32)]
    return pl.pallas_call(
        functools.partial(_outproj_kernel, with_router=router_w is not None),
        grid=(n // tm,),
        in_specs=[tok(o.shape[1]), tok(CONV_CH), tok(d)] + [_const_spec(c.shape) for c in consts],
        out_specs=out_specs,
        out_shape=out_shape,
        scratch_shapes=scratch,
        compiler_params=pltpu.CompilerParams(dimension_semantics=("arbitrary",),
                                             vmem_limit_bytes=VMEM_LIMIT),
        name="output_projection",
    )(o, hc, x2d, *consts)


def _swiglu_tile(xb, w1, w3, w2):
    h1 = _dot(xb, w1)
    h3 = _dot(xb, w3)
    hid = (h1 * jax.nn.sigmoid(h1) * h3).astype(BF16)
    return _dot(hid, w2)


def _dense_ffn_kernel(x_ref, w1_ref, w3_ref, w2_ref, g_ref, b_ref, o_ref, *, f_chunk):
    x = x_ref[...]
    xb = x.astype(BF16)
    ff = None
    for c0 in range(0, w1_ref.shape[1], f_chunk):
        part = _swiglu_tile(xb, w1_ref[:, c0:c0 + f_chunk], w3_ref[:, c0:c0 + f_chunk],
                            w2_ref[c0:c0 + f_chunk, :])
        ff = part if ff is None else ff + part
    o_ref[...] = _layer_norm(ALPHA * x + ff, g_ref[...], b_ref[...])


def _dense_ffn(x2d, w1, w3, w2, g, b):
    n, d = x2d.shape
    tm = min(TM_FFN, n)
    f = w1.shape[1]
    f_chunk = f // 2 if (f // 2) % 128 == 0 else f
    row = lambda a: a.reshape(1, -1)
    consts = [w1.astype(BF16), w3.astype(BF16), w2.astype(BF16), row(g), row(b)]
    return pl.pallas_call(
        functools.partial(_dense_ffn_kernel, f_chunk=f_chunk),
        grid=(n // tm,),
        in_specs=[pl.BlockSpec((tm, d), lambda i: (i, 0))] + [_const_spec(c.shape) for c in consts],
        out_specs=pl.BlockSpec((tm, d), lambda i: (i, 0)),
        out_shape=jax.ShapeDtypeStruct((n, d), F32),
        compiler_params=pltpu.CompilerParams(dimension_semantics=("arbitrary",),
                                             vmem_limit_bytes=VMEM_LIMIT),
        name="dense_ffn",
    )(x2d, *consts)


def _dispatch_kernel(d1_ref, d2_ref, x_ref, xs_init_ref, xs_ref, sem, *, tm):
    del xs_init_ref
    base = pl.program_id(0) * tm

    def start(r, c):
        src = x_ref.at[pl.ds(r, 1)]
        pltpu.make_async_copy(src, xs_ref.at[pl.ds(d1_ref[base + r], 1)], sem.at[0]).start()
        pltpu.make_async_copy(src, xs_ref.at[pl.ds(d2_ref[base + r], 1)], sem.at[1]).start()
        return c

    def wait(r, c):
        for k in range(2):
            pltpu.make_async_copy(x_ref.at[pl.ds(0, 1)], xs_ref.at[pl.ds(0, 1)], sem.at[k]).wait()
        return c

    lax.fori_loop(0, tm, start, 0, unroll=8)
    lax.fori_loop(0, tm, wait, 0, unroll=8)


def _expert_kernel(te_ref, blk_ref, nu_ref, xs_ref, w1_ref, w3_ref, w2_ref, ys_ref):
    del te_ref, blk_ref
    used = pl.program_id(0) < nu_ref[0]

    @pl.when(used)
    def _():
        ys_ref[...] = _swiglu_tile(xs_ref[...].astype(BF16), w1_ref[0], w3_ref[0], w2_ref[0])

    @pl.when(jnp.logical_not(used))
    def _():
        ys_ref[...] = jnp.zeros_like(ys_ref)


def _combine_kernel(d1_ref, d2_ref, x_ref, route_ref, g_ref, b_ref, ys_ref, o_ref, ybuf, sem, *, tm):
    i = pl.program_id(0)
    n_steps = pl.num_programs(0)

    def issue(tile, slot):
        base = tile * tm

        def start(r, c):
            pltpu.make_async_copy(ys_ref.at[pl.ds(d1_ref[base + r], 1)], ybuf.at[slot, 0, pl.ds(r, 1)],
                                  sem.at[slot]).start()
            pltpu.make_async_copy(ys_ref.at[pl.ds(d2_ref[base + r], 1)], ybuf.at[slot, 1, pl.ds(r, 1)],
                                  sem.at[slot]).start()
            return c

        lax.fori_loop(0, tm, start, 0, unroll=8)

    @pl.when(i == 0)
    def _():
        issue(0, 0)

    @pl.when(i + 1 < n_steps)
    def _():
        issue(i + 1, (i + 1) % 2)

    slot = i % 2

    def wait(r, c):
        for k in range(2):
            pltpu.make_async_copy(ys_ref.at[pl.ds(0, 1)], ybuf.at[slot, k, pl.ds(0, 1)], sem.at[slot]).wait()
        return c

    lax.fori_loop(0, tm, wait, 0, unroll=8)
    route = route_ref[...]
    ff = route[:, 4:5] * ybuf[slot, 0] + route[:, 5:6] * ybuf[slot, 1]
    o_ref[...] = _layer_norm(ALPHA * x_ref[...] + ff, g_ref[...], b_ref[...])


def _moe_ffn(x2d, route, counts, w1, w3, w2, g, b):
    n, d = x2d.shape
    n_exp, _, f = w1.shape
    tr = min(TR_MOE, n)
    tm = min(TM_MOE, n)
    n_pad = 2 * n + n_exp * tr
    n_tiles = n_pad // tr
    i32 = jnp.int32

    cnt = counts[0, :n_exp].astype(i32)
    seg = (cnt + tr - 1) // tr * tr
    seg_end = jnp.cumsum(seg)
    seg_start = seg_end - seg
    e1, e2 = route[:, 0].astype(i32), route[:, 1].astype(i32)
    dest1 = seg_start[e1] + route[:, 2].astype(i32)
    dest2 = seg_start[e2] + route[:, 3].astype(i32)
    n_used = jnp.maximum(seg_end[-1] // tr, 1)
    tile = jnp.minimum(jnp.arange(n_tiles, dtype=i32), n_used - 1)
    tile_expert = jnp.minimum(jnp.sum(tile[:, None] * tr >= seg_end[None, :], axis=1), n_exp - 1).astype(i32)

    cparams = pltpu.CompilerParams(dimension_semantics=("arbitrary",), vmem_limit_bytes=VMEM_LIMIT)
    xs = pl.pallas_call(
        functools.partial(_dispatch_kernel, tm=tm),
        grid_spec=pltpu.PrefetchScalarGridSpec(
            num_scalar_prefetch=2, grid=(n // tm,),
            in_specs=[pl.BlockSpec((tm, d), lambda i, *_: (i, 0)), pl.BlockSpec(memory_space=pl.ANY)],
            out_specs=pl.BlockSpec(memory_space=pl.ANY),
            scratch_shapes=[pltpu.SemaphoreType.DMA((2,))]),
        out_shape=jax.ShapeDtypeStruct((n_pad, d), F32),
        input_output_aliases={3: 0},
        compiler_params=cparams,
        name="moe_dispatch",
    )(dest1, dest2, x2d, jnp.zeros((n_pad, d), F32))

    ys = pl.pallas_call(
        _expert_kernel,
        grid_spec=pltpu.PrefetchScalarGridSpec(
            num_scalar_prefetch=3, grid=(n_tiles,),
            in_specs=[pl.BlockSpec((tr, d), lambda i, te, blk, nu: (blk[i], 0)),
                      pl.BlockSpec((1, d, f), lambda i, te, blk, nu: (te[i], 0, 0)),
                      pl.BlockSpec((1, d, f), lambda i, te, blk, nu: (te[i], 0, 0)),
                      pl.BlockSpec((1, f, d), lambda i, te, blk, nu: (te[i], 0, 0))],
            out_specs=pl.BlockSpec((tr, d), lambda i, te, blk, nu: (i, 0))),
        out_shape=jax.ShapeDtypeStruct((n_pad, d), F32),
        compiler_params=cparams,
        name="moe_experts",
    )(tile_expert, tile, n_used.reshape(1), xs, w1.astype(BF16), w3.astype(BF16), w2.astype(BF16))

    row = lambda a: a.reshape(1, -1)
    return pl.pallas_call(
        functools.partial(_combine_kernel, tm=tm),
        grid_spec=pltpu.PrefetchScalarGridSpec(
            num_scalar_prefetch=2, grid=(n // tm,),
            in_specs=[pl.BlockSpec((tm, d), lambda i, *_: (i, 0)),
                      pl.BlockSpec((tm, 128), lambda i, *_: (i, 0)),
                      pl.BlockSpec((1, d), lambda i, *_: (0, 0)),
                      pl.BlockSpec((1, d), lambda i, *_: (0, 0)),
                      pl.BlockSpec(memory_space=pl.ANY)],
            out_specs=pl.BlockSpec((tm, d), lambda i, *_: (i, 0)),
            scratch_shapes=[pltpu.VMEM((2, 2, tm, d), F32), pltpu.SemaphoreType.DMA((2,))]),
        out_shape=jax.ShapeDtypeStruct((n, d), F32),
        compiler_params=cparams,
        name="moe_combine",
    )(dest1, dest2, x2d, route, row(g), row(b), ys)


def kernel(x, positions, w_in, mla_q_norm_g, w_uq, mla_kv_norm_g, w_ukv, fox_forget_b, conv_w, conv_b,
           conv_norm_g, conv_norm_b, mla_out_norm_g, fox_out_norm_g, w_out, ln1_g, ln1_b, dense_w1,
           dense_w3, dense_w2, router_w, expert_w1, expert_w3, expert_w2, ln2_g, ln2_b):
    batch, seq, d = x.shape
    assert d == D_MODEL and seq % TQ == 0 and seq % min(TM_IN, seq) == 0
    depth = w_in.shape[0]
    tabs = _rope_tables(positions)
    h = x.reshape(batch * seq, d)
    for layer in range(depth):
        pw = _prep_inproj_weights(w_in[layer], w_uq[layer], w_ukv[layer], fox_forget_b[layer])
        q_t, k, v_t, hc = _input_projection(
            h, tabs, pw, mla_q_norm_g[layer], mla_kv_norm_g[layer], conv_w[layer], conv_b[layer],
            conv_norm_g[layer], conv_norm_b[layer], seq)
        o = _attention(q_t, k, v_t, batch, seq)
        j = layer // 2
        if layer % 2 == 0:
            h = _output_projection(o, hc, h, mla_out_norm_g[layer], fox_out_norm_g[layer], w_out[layer],
                                   ln1_g[layer], ln1_b[layer])[0]
            h = _dense_ffn(h, dense_w1[j], dense_w3[j], dense_w2[j], ln2_g[layer], ln2_b[layer])
        else:
            h, route, counts = _output_projection(o, hc, h, mla_out_norm_g[layer], fox_out_norm_g[layer],
                                                  w_out[layer], ln1_g[layer], ln1_b[layer], router_w[j])
            h = _moe_ffn(h, route, counts, expert_w1[j], expert_w3[j], expert_w2[j], ln2_g[layer],
                         ln2_b[layer])
    return h.reshape(batch, seq, d)
```

```python
import functools
import math

import numpy as np
import jax
import jax.numpy as jnp
from jax import lax
from jax.experimental import pallas as pl
from jax.experimental.pallas import tpu as pltpu

F32 = jnp.float32
BF16 = jnp.bfloat16

D_MODEL = 1024
DEPTH = 4
MLA_HEADS = 8
MLA_NOPE = 64
MLA_ROPE = 32
MLA_V = 64
MLA_Q_RANK = 256
MLA_KV_RANK = 128
ROPE_THETA = 10000.0
FOX_HEADS = 4
FOX_DIM = 64
CONV_CH = 256
CONV_GROUPS = 4
CONV_WIDTH = 31
MLA_WIDTH = MLA_HEADS * MLA_V
FOX_WIDTH = FOX_HEADS * FOX_DIM
N_EXPERTS = 8
ALPHA = (2.0 * DEPTH) ** 0.25
NORM_EPS = 1e-5
LOG2E = math.log2(math.e)

HEAD_PAD = 128
N_HEADS = MLA_HEADS + FOX_HEADS
V_DIM = 64
V_ROWS = 80
CONV_HALO = 32
VMEM_LIMIT = 56 * 1024 * 1024

TQ = 512
TK = 256
HPS = 2
TM_IN = 512
TM_OUT = 512
TM_FFN = 512
TR_MOE = 512
TM_MOE = 256
ROW_TILE = 8


def _nt_dot(a, b):
    return lax.dot_general(a, b, (((1,), (1,)), ((), ())), preferred_element_type=F32)


def _dot(a, b):
    return jnp.dot(a, b, preferred_element_type=F32)


def _split2_dot(a, m_bf16):
    hi = a.astype(BF16)
    lo = (a - hi.astype(F32)).astype(BF16)
    return _dot(hi, m_bf16) + _dot(lo, m_bf16)


def _split3(a):
    hi = a.astype(BF16).astype(F32)
    r1 = a - hi
    mid = r1.astype(BF16).astype(F32)
    lo = (r1 - mid).astype(BF16).astype(F32)
    return hi, mid, lo


def _const_spec(shape):
    nd = len(shape)
    return pl.BlockSpec(shape, lambda *_: (0,) * nd, pipeline_mode=pl.Buffered(1))


def _rope_kernel(pos_ref, invf_ref, c_ref, s_ref, ct_ref, st_ref):
    pos = pos_ref[...].astype(F32)
    ang = invf_ref[...] * pos
    cos = jnp.cos(ang)
    sin = jnp.sin(ang)
    tn = pos.shape[1]
    ct = jnp.concatenate([jnp.ones((MLA_NOPE, tn), F32), cos, cos, jnp.zeros((32, tn), F32)], axis=0)
    st = jnp.concatenate([jnp.zeros((MLA_NOPE, tn), F32), sin, sin, jnp.zeros((32, tn), F32)], axis=0)
    ct_ref[...] = ct
    st_ref[...] = st
    c_ref[...] = ct.T
    s_ref[...] = st.T


def _rope_tables(positions):
    n = positions.size
    tn = min(512, n)
    inv_freq = ROPE_THETA ** (-jnp.arange(0, MLA_ROPE, 2, dtype=F32) / MLA_ROPE)
    return pl.pallas_call(
        _rope_kernel,
        grid=(n // tn,),
        in_specs=[pl.BlockSpec((1, tn), lambda i: (0, i)),
                  pl.BlockSpec((MLA_ROPE // 2, 1), lambda i: (0, 0))],
        out_specs=[pl.BlockSpec((tn, HEAD_PAD), lambda i: (i, 0)),
                   pl.BlockSpec((tn, HEAD_PAD), lambda i: (i, 0)),
                   pl.BlockSpec((HEAD_PAD, tn), lambda i: (0, i)),
                   pl.BlockSpec((HEAD_PAD, tn), lambda i: (0, i))],
        out_shape=[jax.ShapeDtypeStruct((n, HEAD_PAD), F32),
                   jax.ShapeDtypeStruct((n, HEAD_PAD), F32),
                   jax.ShapeDtypeStruct((HEAD_PAD, n), F32),
                   jax.ShapeDtypeStruct((HEAD_PAD, n), F32)],
        name="rope_tables",
    )(positions.reshape(1, n), inv_freq.reshape(-1, 1))


_A_CQ = 0
_A_CKV = _A_CQ + MLA_Q_RANK
_A_KR = _A_CKV + MLA_KV_RANK
_A_KRR = _A_KR + HEAD_PAD
_A_FK = _A_KRR + HEAD_PAD
_A_CA = _A_FK + FOX_HEADS * HEAD_PAD
_A_CG = _A_CA + CONV_CH
_A_COLS = _A_CG + CONV_CH
_AUG_ROWS = 8
_F_ROWS = 16


def _rms(x, g):
    ms = jnp.mean(jnp.square(x), axis=-1, keepdims=True)
    return x * lax.rsqrt(ms + NORM_EPS) * g


def _inproj_kernel(x_ref, c_ref, s_ref, ct_ref, st_ref, wa_ref, wfq_ref, wfv_ref, wf_ref, fb_ref,
                   gq_ref, wuq_ref, wuqr_ref, gkv_ref, wuk_ref, wuv_ref,
                   cw_ref, cb_ref, cng_ref, cnb_ref, gmat_ref,
                   qt_ref, k_ref, vt_ref, hc_ref,
                   hbuf, hsh, cbuf, fcarry, *, tiles_per_seq, tm):
    i = pl.program_id(0)

    @pl.when(i % tiles_per_seq == 0)
    def _():
        hbuf[0:CONV_HALO, :] = jnp.zeros((CONV_HALO, CONV_CH), F32)
        fcarry[...] = jnp.zeros_like(fcarry)

    xb = x_ref[...].astype(BF16)
    p1 = _dot(xb, wa_ref[...])
    cos_t = c_ref[...]
    sin_t = s_ref[...]
    cos_tt = ct_ref[...]
    sin_tt = st_ref[...]

    cqn = _rms(p1[:, _A_CQ:_A_CQ + MLA_Q_RANK], gq_ref[...]).astype(BF16)
    q_t = _nt_dot(wuq_ref[...], cqn)
    q_rot_t = _nt_dot(wuqr_ref[...], cqn)
    mla_scale = (MLA_NOPE + MLA_ROPE) ** -0.5 * LOG2E
    for h in range(MLA_HEADS):
        rows = slice(h * HEAD_PAD, (h + 1) * HEAD_PAD)
        qh = (q_t[rows, :] * cos_tt + q_rot_t[rows, :] * sin_tt) * mla_scale
        for c in range(tm // TQ):
            qt_ref[c, rows, :] = qh[:, c * TQ:(c + 1) * TQ].astype(BF16)

    ckvn = _rms(p1[:, _A_CKV:_A_CKV + MLA_KV_RANK], gkv_ref[...]).astype(BF16)
    k_nope = _dot(ckvn, wuk_ref[...])
    k_rope = p1[:, _A_KR:_A_KR + HEAD_PAD] * cos_t + p1[:, _A_KRR:_A_KRR + HEAD_PAD] * sin_t
    for h in range(MLA_HEADS):
        cols = slice(h * HEAD_PAD, (h + 1) * HEAD_PAD)
        k_ref[:, cols] = (k_nope[:, cols] + k_rope).astype(BF16)
    v_t = _nt_dot(wuv_ref[...], ckvn)
    fv_t = _nt_dot(wfv_ref[...], xb)
    ones_blk = jnp.where(lax.broadcasted_iota(jnp.int32, (V_ROWS - V_DIM, tm), 0) == 0, 1.0, 0.0)
    for h in range(N_HEADS):
        src = v_t if h < MLA_HEADS else fv_t
        r0 = (h if h < MLA_HEADS else h - MLA_HEADS) * V_DIM
        vh = jnp.concatenate([src[r0:r0 + V_DIM, :], ones_blk], axis=0).astype(BF16)
        for c in range(tm // TK):
            vt_ref[c, h * V_ROWS:(h + 1) * V_ROWS, :] = vh[:, c * TK:(c + 1) * TK]

    z = _nt_dot(wf_ref[...], xb) + fb_ref[...]
    logf = (jnp.minimum(z, 0.0) - jnp.log1p(jnp.exp(-jnp.abs(z)))) * LOG2E
    r_i = lax.broadcasted_iota(jnp.int32, (tm, tm), 0)
    c_i = lax.broadcasted_iota(jnp.int32, (tm, tm), 1)
    upper = jnp.where(r_i <= c_i, 1.0, 0.0).astype(BF16)
    l_hi, l_mid, l_lo = _split3(logf)
    cum = (_dot(l_hi.astype(BF16), upper) + _dot(l_mid.astype(BF16), upper)
           + _dot(l_lo.astype(BF16), upper))
    f_cum = cum + fcarry[:, 0:1]
    fcarry[...] = jnp.broadcast_to(f_cum[:, tm - 1:tm], fcarry.shape)
    f_hi, f_mid, f_lo = _split3(f_cum)

    fq_t = _nt_dot(wfq_ref[...], xb)
    row8 = lax.broadcasted_iota(jnp.int32, (_AUG_ROWS, tm), 0)
    fox_scale = FOX_DIM ** -0.5 * LOG2E
    for h in range(FOX_HEADS):
        bh = lambda a: jnp.broadcast_to(a[h:h + 1, :], (_AUG_ROWS, tm))
        aug_q = jnp.where(row8 == 0, bh(f_hi), jnp.where(row8 == 1, bh(f_mid), jnp.where(
            row8 == 2, bh(f_lo), jnp.where(row8 < 6, 1.0, 0.0))))
        aug_k = jnp.where(row8 < 3, 1.0, jnp.where(row8 == 3, -bh(f_hi), jnp.where(
            row8 == 4, -bh(f_mid), jnp.where(row8 == 5, -bh(f_lo), 0.0))))
        pad = jnp.zeros((HEAD_PAD - FOX_DIM - _AUG_ROWS, tm), F32)
        qh = jnp.concatenate(
            [fq_t[h * HEAD_PAD:h * HEAD_PAD + FOX_DIM, :] * fox_scale, aug_q, pad], axis=0)
        rows = slice((MLA_HEADS + h) * HEAD_PAD, (MLA_HEADS + h + 1) * HEAD_PAD)
        for c in range(tm // TQ):
            qt_ref[c, rows, :] = qh[:, c * TQ:(c + 1) * TQ].astype(BF16)
        kaug_t = jnp.concatenate([jnp.zeros((FOX_DIM, tm), F32), aug_k, pad], axis=0)
        fk = p1[:, _A_FK + h * HEAD_PAD:_A_FK + (h + 1) * HEAD_PAD]
        k_ref[:, rows] = (fk + kaug_t.T).astype(BF16)

    a = p1[:, _A_CA:_A_CA + CONV_CH]
    g = p1[:, _A_CG:_A_CG + CONV_CH]
    hbuf[CONV_HALO:CONV_HALO + tm, :] = a * jax.nn.sigmoid(g)
    chunk = 64
    first = CONV_HALO - (CONV_WIDTH - 1)
    for r in range(1, 8):
        hsh[r - 1] = hbuf[r:r + tm + CONV_HALO - 8, :]
    for c0 in range(0, tm, chunk):
        acc = jnp.zeros((chunk, CONV_CH), F32)
        for o in range(first, first + CONV_WIDTH):
            r = o % 8
            row = c0 + o - r
            seg = hbuf[row:row + chunk, :] if r == 0 else hsh[r - 1, row:row + chunk, :]
            acc = acc + cw_ref[o - first:o - first + 1, :] * seg
        cbuf[c0:c0 + chunk, :] = acc
    hbuf[0:CONV_HALO, :] = hbuf[tm:tm + CONV_HALO, :]
    hv = cbuf[...] + cb_ref[...]
    gm = gmat_ref[...]
    mu = _split2_dot(hv, gm)
    d = hv - mu
    var = _split2_dot(d * d, gm)
    hn = d * lax.rsqrt(var + NORM_EPS) * cng_ref[...] + cnb_ref[...]
    hc_ref[...] = (hn * jax.nn.sigmoid(hn)).astype(BF16)


def _prep_inproj_weights(w_in, w_uq, w_ukv, fox_forget_b):
    o = np.cumsum((0, MLA_Q_RANK, MLA_KV_RANK, MLA_ROPE, FOX_WIDTH, FOX_WIDTH, FOX_WIDTH, FOX_HEADS,
                   2 * CONV_CH))
    w_cq, w_ckv, w_kr, w_fq, w_fk, w_fv, w_f, w_cv = (w_in[:, o[i]:o[i + 1]] for i in range(8))
    d = w_in.shape[0]
    half = MLA_ROPE // 2

    def rot_cols(w):
        return jnp.concatenate([-w[..., half:], w[..., :half]], axis=-1)

    def rope_block(w):
        return jnp.pad(w, ((0, 0), (MLA_NOPE, HEAD_PAD - MLA_NOPE - MLA_ROPE)))

    w_fk_pad = jnp.pad(w_fk.reshape(d, FOX_HEADS, FOX_DIM), ((0, 0), (0, 0), (0, HEAD_PAD - FOX_DIM)))
    wa = jnp.concatenate([w_cq, w_ckv, rope_block(w_kr), rope_block(rot_cols(w_kr)),
                          w_fk_pad.reshape(d, FOX_HEADS * HEAD_PAD), w_cv], axis=1)
    w_fq_pad = jnp.pad(w_fq.reshape(d, FOX_HEADS, FOX_DIM), ((0, 0), (0, 0), (0, HEAD_PAD - FOX_DIM)))
    wfq_t = w_fq_pad.reshape(d, FOX_HEADS * HEAD_PAD).T
    wfv_t = w_fv.T
    wf_t = jnp.pad(w_f, ((0, 0), (0, _F_ROWS - FOX_HEADS))).T
    fb = jnp.pad(fox_forget_b, (0, _F_ROWS - FOX_HEADS)).reshape(_F_ROWS, 1)

    uq = w_uq.reshape(MLA_Q_RANK, MLA_HEADS, MLA_NOPE + MLA_ROPE)
    uq_nope, uq_rope = uq[..., :MLA_NOPE], uq[..., MLA_NOPE:]
    tail = ((0, 0), (0, 0), (0, HEAD_PAD - MLA_NOPE - MLA_ROPE))
    uq_pad = jnp.pad(jnp.concatenate([uq_nope, uq_rope], axis=-1), tail)
    uq_rot_pad = jnp.pad(jnp.concatenate([jnp.zeros_like(uq_nope), rot_cols(uq_rope)], axis=-1), tail)
    wuq_t = uq_pad.reshape(MLA_Q_RANK, MLA_HEADS * HEAD_PAD).T
    wuqr_t = uq_rot_pad.reshape(MLA_Q_RANK, MLA_HEADS * HEAD_PAD).T
    ukv = w_ukv.reshape(MLA_KV_RANK, MLA_HEADS, MLA_NOPE + MLA_V)
    wuk = jnp.pad(ukv[..., :MLA_NOPE], ((0, 0), (0, 0), (0, HEAD_PAD - MLA_NOPE))).reshape(
        MLA_KV_RANK, MLA_HEADS * HEAD_PAD)
    wuv_t = ukv[..., MLA_NOPE:].reshape(MLA_KV_RANK, MLA_WIDTH).T
    bf = lambda a: a.astype(BF16)
    return dict(wa=bf(wa), wfq=bf(wfq_t), wfv=bf(wfv_t), wf=bf(wf_t), fb=fb, wuq=bf(wuq_t),
                wuqr=bf(wuqr_t), wuk=bf(wuk), wuv=bf(wuv_t))


def _input_projection(x2d, tabs, pw, gq, gkv, conv_w, conv_b, conv_ng, conv_nb, seq):
    n, d = x2d.shape
    tm = min(TM_IN, seq)
    cos_t, sin_t, cos_tt, sin_tt = tabs
    gidx = np.arange(CONV_CH) // (CONV_CH // CONV_GROUPS)
    gmat = jnp.asarray((gidx[:, None] == gidx[None, :]) / (CONV_CH // CONV_GROUPS), BF16)
    cw = jnp.pad(conv_w, ((0, 32 - CONV_WIDTH), (0, 0)))
    row = lambda a: a.reshape(1, -1)
    tok = lambda w: pl.BlockSpec((tm, w), lambda i: (i, 0))
    tok_t = lambda r: pl.BlockSpec((r, tm), lambda i: (0, i))
    consts = [pw["wa"], pw["wfq"], pw["wfv"], pw["wf"], pw["fb"], row(gq), pw["wuq"], pw["wuqr"],
              row(gkv), pw["wuk"], pw["wuv"], cw, row(conv_b), row(conv_ng), row(conv_nb), gmat]
    kern = functools.partial(_inproj_kernel, tiles_per_seq=seq // tm, tm=tm)
    return pl.pallas_call(
        kern,
        grid=(n // tm,),
        in_specs=[tok(d), tok(HEAD_PAD), tok(HEAD_PAD), tok_t(HEAD_PAD), tok_t(HEAD_PAD)]
        + [_const_spec(c.shape) for c in consts],
        out_specs=[pl.BlockSpec((tm // TQ, N_HEADS * HEAD_PAD, TQ), lambda i: (i, 0, 0)),
                   tok(N_HEADS * HEAD_PAD),
                   pl.BlockSpec((tm // TK, N_HEADS * V_ROWS, TK), lambda i: (i, 0, 0)),
                   tok(CONV_CH)],
        out_shape=[jax.ShapeDtypeStruct((n // TQ, N_HEADS * HEAD_PAD, TQ), BF16),
                   jax.ShapeDtypeStruct((n, N_HEADS * HEAD_PAD), BF16),
                   jax.ShapeDtypeStruct((n // TK, N_HEADS * V_ROWS, TK), BF16),
                   jax.ShapeDtypeStruct((n, CONV_CH), BF16)],
        scratch_shapes=[pltpu.VMEM((CONV_HALO + tm, CONV_CH), F32),
                        pltpu.VMEM((7, tm + CONV_HALO - 8, CONV_CH), F32),
                        pltpu.VMEM((tm, CONV_CH), F32),
                        pltpu.VMEM((_F_ROWS, 128), F32)],
        compiler_params=pltpu.CompilerParams(dimension_semantics=("arbitrary",),
                                             vmem_limit_bytes=VMEM_LIMIT),
        name="input_projection",
    )(x2d, cos_t, sin_t, cos_tt, sin_tt, *consts)


def _attn_kernel(qt_ref, k_ref, vt_ref, o_ref, *scratch, n_tiles):
    assert TQ == 2 * TK
    s_ref = (scratch[0:HPS], scratch[HPS:2 * HPS], scratch[2 * HPS:3 * HPS])
    p_ref = (scratch[3 * HPS:4 * HPS], scratch[4 * HPS:5 * HPS])
    acc_ref = scratch[5 * HPS:6 * HPS]
    diff_ref = scratch[6 * HPS]
    diff_ref[...] = (lax.broadcasted_iota(jnp.int32, (TK, TQ), 1)
                     - lax.broadcasted_iota(jnp.int32, (TK, TQ), 0))

    def tile_scores(tile, j, slot):
        row0 = pl.multiple_of(j * TK, TK)
        block_max = []
        for h in range(HPS):
            s = _dot(k_ref[pl.ds(row0, TK), h * HEAD_PAD:(h + 1) * HEAD_PAD],
                     qt_ref[tile, h * HEAD_PAD:(h + 1) * HEAD_PAD, :])
            s_ref[slot][h][...] = s
            block_max.append(jnp.max(s, axis=0, keepdims=True))
        return block_max

    def q_tile(i, bm0):
        scores = functools.partial(tile_scores, i)

        def softmax(slot, m, block_max):
            m_new = [jnp.maximum(m[h], block_max[h]) for h in range(HPS)]
            for h in range(HPS):
                p_ref[slot][h][...] = jnp.exp2(s_ref[slot][h][...] - m_new[h]).astype(BF16)
            return m_new, [jnp.exp2(m[h] - m_new[h]) for h in range(HPS)]

        def softmax_diag(s_slot, p_slot, m, dg):
            m_new, alpha = [], []
            for h in range(HPS):
                s = jnp.where(diff_ref[...] >= dg * TK, s_ref[s_slot][h][...], -jnp.inf)
                m_new.append(jnp.maximum(m[h], jnp.max(s, axis=0, keepdims=True)))
                p_ref[p_slot][h][...] = jnp.exp2(s - m_new[h]).astype(BF16)
                alpha.append(jnp.exp2(m[h] - m_new[h]))
            return m_new, alpha

        def values(j, slot, alpha):
            for h in range(HPS):
                acc_ref[h][...] = alpha[h] * acc_ref[h][...] + _dot(
                    vt_ref[j, h * V_ROWS:(h + 1) * V_ROWS, :], p_ref[slot][h][...])

        for h in range(HPS):
            acc_ref[h][...] = jnp.zeros_like(acc_ref[h])
            p_ref[1][h][...] = jnp.zeros_like(p_ref[1][h])
        m0 = [jnp.full((1, TQ), -1e30, F32)] * HPS
        a0 = [jnp.ones((1, TQ), F32)] * HPS

        def pair(u, state):
            m, alpha, bm_t = list(state[0:HPS]), list(state[HPS:2 * HPS]), list(state[2 * HPS:3 * HPS])
            t = 2 * u
            m, alpha_t = softmax(0, m, bm_t)
            values(jnp.maximum(t - 1, 0), 1, alpha)
            bm_t1 = scores(t + 1, 1)
            m, alpha_t1 = softmax(1, m, bm_t1)
            values(t, 0, alpha_t)
            bm_t2 = scores(t + 2, 0)
            return (*m, *alpha_t1, *bm_t2)

        state = lax.fori_loop(0, i // 2, lambda v, st: pair(2 * v + 1, pair(2 * v, st)), (*m0, *a0, *bm0))
        state = lax.fori_loop(i - i % 2, i, pair, state)
        m, alpha = list(state[0:HPS]), list(state[HPS:2 * HPS])
        d0 = 2 * i
        values(jnp.maximum(d0 - 1, 0), 1, alpha)
        nxt = jnp.minimum(i + 1, n_tiles - 1)
        m, alpha_d0 = softmax_diag(0, 0, m, 0)
        bm_next = tile_scores(nxt, 0, 0)
        m, alpha_d1 = softmax_diag(2, 1, m, 1)
        tile_scores(nxt, 2 * nxt + 1, 2)
        values(d0, 0, alpha_d0)
        values(d0 + 1, 1, alpha_d1)
        out_t = jnp.concatenate(
            [acc_ref[h][0:V_DIM, :] / acc_ref[h][V_DIM:V_DIM + 1, :] for h in range(HPS)], axis=0)
        o_ref[pl.ds(pl.multiple_of(i * TQ, TQ), TQ), :] = out_t.T.astype(o_ref.dtype)
        return tuple(bm_next)

    tile_scores(0, 1, 2)
    lax.fori_loop(0, n_tiles, q_tile, tuple(tile_scores(0, 0, 0)))


def _attention(q_t, k, v_t, batch, seq):
    n = k.shape[0]
    groups = N_HEADS // HPS
    return pl.pallas_call(
        functools.partial(_attn_kernel, n_tiles=seq // TQ),
        grid=(batch, groups),
        in_specs=[pl.BlockSpec((seq // TQ, HPS * HEAD_PAD, TQ), lambda b, p: (b, p, 0)),
                  pl.BlockSpec((seq, HPS * HEAD_PAD), lambda b, p: (b, p)),
                  pl.BlockSpec((seq // TK, HPS * V_ROWS, TK), lambda b, p: (b, p, 0))],
        out_specs=pl.BlockSpec((seq, HPS * V_DIM), lambda b, p: (b, p)),
        out_shape=jax.ShapeDtypeStruct((n, N_HEADS * V_DIM), BF16),
        scratch_shapes=[pltpu.VMEM((TK, TQ), F32)] * (3 * HPS) + [pltpu.VMEM((TK, TQ), BF16)] * (2 * HPS)
        + [pltpu.VMEM((V_ROWS, TQ), F32)] * HPS + [pltpu.VMEM((TK, TQ), jnp.int32)],
        compiler_params=pltpu.CompilerParams(dimension_semantics=("arbitrary", "arbitrary"),
                                             vmem_limit_bytes=VMEM_LIMIT),
        name="attention",
    )(q_t, k, v_t)


def _layer_norm(x, g, b):
    mu = jnp.mean(x, axis=-1, keepdims=True)
    d = x - mu
    var = jnp.mean(jnp.square(d), axis=-1, keepdims=True)
    return d * lax.rsqrt(var + NORM_EPS) * g + b


def _outproj_kernel(o_ref, hc_ref, x_ref, gm_ref, gf_ref, wo_ref, g1_ref, b1_ref, *rest, with_router):
    if with_router:
        rw_ref, x1_ref, route_ref, counts_ref, cnt_ref = rest
    else:
        (x1_ref,) = rest
    o = o_ref[...].astype(F32)
    mla = _rms(o[:, :MLA_WIDTH], gm_ref[...])
    fox = _rms(o[:, MLA_WIDTH:], gf_ref[...])
    mixed = jnp.concatenate([mla.astype(BF16), fox.astype(BF16), hc_ref[...]], axis=-1)
    y = _dot(mixed, wo_ref[...])
    x1 = _layer_norm(ALPHA * x_ref[...] + y, g1_ref[...], b1_ref[...])
    x1_ref[...] = x1
    if with_router:
        rw = rw_ref[...]
        x_hi = x1.astype(BF16)
        x_lo = (x1 - x_hi.astype(F32)).astype(BF16)
        w_hi = rw.astype(BF16)
        w_lo = (rw - w_hi.astype(F32)).astype(BF16)
        logits = _dot(x_hi, w_hi) + (_dot(x_lo, w_hi) + _dot(x_hi, w_lo))
        lane = lax.broadcasted_iota(jnp.int32, logits.shape, 1)
        logits = jnp.where(lane < N_EXPERTS, logits, -jnp.inf)
        v1 = jnp.max(logits, axis=-1, keepdims=True)
        i1 = jnp.min(jnp.where(logits == v1, lane, 128), axis=-1, keepdims=True)
        rest_l = jnp.where(lane == i1, -jnp.inf, logits)
        v2 = jnp.max(rest_l, axis=-1, keepdims=True)
        i2 = jnp.min(jnp.where(rest_l == v2, lane, 128), axis=-1, keepdims=True)
        e2 = jnp.exp(v2 - v1)
        den = 1.0 + e2

        @pl.when(pl.program_id(0) == 0)
        def _():
            cnt_ref[...] = jnp.zeros_like(cnt_ref)

        tm = logits.shape[0]
        sel = jnp.where(lane == i1, 1.0, jnp.where(lane == i2, 1.0, 0.0))
        r_i = lax.broadcasted_iota(jnp.int32, (tm, tm), 0)
        c_i = lax.broadcasted_iota(jnp.int32, (tm, tm), 1)
        lower = jnp.where(c_i < r_i, 1.0, 0.0).astype(BF16)
        before = cnt_ref[0:1, :]
        rank = _dot(lower, sel.astype(BF16)) + before
        total = before + jnp.sum(sel, axis=0, keepdims=True)
        cnt_ref[...] = jnp.broadcast_to(total, cnt_ref.shape)
        counts_ref[...] = jnp.broadcast_to(total, counts_ref.shape)
        r1 = jnp.sum(jnp.where(lane == i1, rank, 0.0), axis=-1, keepdims=True)
        r2 = jnp.sum(jnp.where(lane == i2, rank, 0.0), axis=-1, keepdims=True)
        cols = (i1.astype(F32), i2.astype(F32), r1, r2, 1.0 / den, e2 / den)
        route = jnp.zeros(logits.shape, F32)
        for c, v in enumerate(cols):
            route = jnp.where(lane == c, v, route)
        route_ref[...] = route


def _output_projection(o, hc, x2d, gm, gf, w_out, g1, b1, router_w=None):
    n, d = x2d.shape
    tm = min(TM_OUT, n)
    row = lambda a: a.reshape(1, -1)
    tok = lambda w: pl.BlockSpec((tm, w), lambda i: (i, 0))
    consts = [row(gm), row(gf), w_out.astype(BF16), row(g1), row(b1)]
    out_specs = [tok(d)]
    out_shape = [jax.ShapeDtypeStruct((n, d), F32)]
    scratch = []
    if router_w is not None:
        consts.append(jnp.pad(router_w, ((0, 0), (0, 128 - N_EXPERTS))))
        out_specs += [tok(128), pl.BlockSpec((8, 128), lambda i: (0, 0))]
        out_shape += [jax.ShapeDtypeStruct((n, 128), F32), jax.ShapeDtypeStruct((8, 128), F32)]
        scratch = [pltpu.VMEM((8, 128), F32)]
    return pl.pallas_call(
        functools.partial(_outproj_kernel, with_router=router_w is not None),
        grid=(n // tm,),
        in_specs=[tok(o.shape[1]), tok(CONV_CH), tok(d)] + [_const_spec(c.shape) for c in consts],
        out_specs=out_specs,
        out_shape=out_shape,
        scratch_shapes=scratch,
        compiler_params=pltpu.CompilerParams(dimension_semantics=("arbitrary",),
                                             vmem_limit_bytes=VMEM_LIMIT),
        name="output_projection",
    )(o, hc, x2d, *consts)


def _swiglu_tile(xb, w1, w3, w2):
    h1 = _dot(xb, w1)
    h3 = _dot(xb, w3)
    hid = (h1 * jax.nn.sigmoid(h1) * h3).astype(BF16)
    return _dot(hid, w2)


def _dense_ffn_kernel(x_ref, w1_ref, w3_ref, w2_ref, g_ref, b_ref, o_ref, *, f_chunk):
    x = x_ref[...]
    xb = x.astype(BF16)
    ff = None
    for c0 in range(0, w1_ref.shape[1], f_chunk):
        part = _swiglu_tile(xb, w1_ref[:, c0:c0 + f_chunk], w3_ref[:, c0:c0 + f_chunk],
                            w2_ref[c0:c0 + f_chunk, :])
        ff = part if ff is None else ff + part
    o_ref[...] = _layer_norm(ALPHA * x + ff, g_ref[...], b_ref[...])


def _dense_ffn(x2d, w1, w3, w2, g, b):
    n, d = x2d.shape
    tm = min(TM_FFN, n)
    f = w1.shape[1]
    f_chunk = f // 2 if (f // 2) % 128 == 0 else f
    row = lambda a: a.reshape(1, -1)
    consts = [w1.astype(BF16), w3.astype(BF16), w2.astype(BF16), row(g), row(b)]
    return pl.pallas_call(
        functools.partial(_dense_ffn_kernel, f_chunk=f_chunk),
        grid=(n // tm,),
        in_specs=[pl.BlockSpec((tm, d), lambda i: (i, 0))] + [_const_spec(c.shape) for c in consts],
        out_specs=pl.BlockSpec((tm, d), lambda i: (i, 0)),
        out_shape=jax.ShapeDtypeStruct((n, d), F32),
        compiler_params=pltpu.CompilerParams(dimension_semantics=("arbitrary",),
                                             vmem_limit_bytes=VMEM_LIMIT),
        name="dense_ffn",
    )(x2d, *consts)


def _to_row_tiles(ref, x):
    for c in range(ROW_TILE):
        ref[pl.ds(c, x.shape[0], stride=ROW_TILE), :] = x[:, c * 128:(c + 1) * 128]


def _from_row_tiles(ref, t):
    return jnp.concatenate([ref[pl.ds(c, t, stride=ROW_TILE), :] for c in range(ROW_TILE)], axis=-1)


def _row_tile(ref, r):
    return ref.at[pl.ds(pl.multiple_of(r * ROW_TILE, ROW_TILE), ROW_TILE)]


def _dispatch_kernel(d1_ref, d2_ref, se_ref, x_ref, xs_ref, xr, zbuf, sem, *, tm, tr):
    i = pl.program_id(0)

    @pl.when(i == 0)
    def _():
        zbuf[...] = jnp.zeros_like(zbuf)
        for e in range(N_EXPERTS):
            end = se_ref[e]
            start_e = se_ref[e - 1] if e else 0

            for first, live in ((end - tr, end > start_e),
                                (se_ref[N_EXPERTS - 1] + e * tr,
                                 (se_ref[N_EXPERTS - 1] + e * tr) * ROW_TILE < xs_ref.shape[0])):
                @pl.when(live)
                def _():
                    rows = pl.ds(pl.multiple_of(first * ROW_TILE, ROW_TILE), tr * ROW_TILE)
                    fill = pltpu.make_async_copy(zbuf, xs_ref.at[rows], sem.at[2])
                    fill.start()
                    fill.wait()

    slot = i % 2
    _to_row_tiles(xr.at[slot], x_ref[...])
    base = i * tm

    def start(r, c):
        src = _row_tile(xr.at[slot], r)
        pltpu.make_async_copy(src, _row_tile(xs_ref, d1_ref[base + r]), sem.at[slot]).start()
        pltpu.make_async_copy(src, _row_tile(xs_ref, d2_ref[base + r]), sem.at[slot]).start(priority=1)
        return c

    def wait_step(s):
        def wait(r, c):
            for _ in range(2):
                pltpu.make_async_copy(_row_tile(xr.at[s], 0), _row_tile(xs_ref, 0), sem.at[s]).wait()
            return c

        lax.fori_loop(0, tm, wait, 0, unroll=8)

    lax.fori_loop(0, tm, start, 0, unroll=8)

    @pl.when(i > 0)
    def _():
        wait_step(1 - slot)

    @pl.when(i == pl.num_programs(0) - 1)
    def _():
        wait_step(slot)


def _expert_kernel(te_ref, blk_ref, nu_ref, xs_ref, w1_ref, w3_ref, w2_ref, ys_ref, *, tr):
    del te_ref, blk_ref
    used = pl.program_id(0) < nu_ref[0]

    @pl.when(used)
    def _():
        xb = _from_row_tiles(xs_ref, tr).astype(BF16)
        _to_row_tiles(ys_ref, _swiglu_tile(xb, w1_ref[0], w3_ref[0], w2_ref[0]))

    @pl.when(jnp.logical_not(used))
    def _():
        ys_ref[...] = jnp.zeros_like(ys_ref)


def _combine_kernel(d1_ref, d2_ref, x_ref, route_ref, g_ref, b_ref, ys_ref, o_ref, ybuf, sem, *, tm):
    i = pl.program_id(0)
    n_steps = pl.num_programs(0)

    def issue(tile, slot):
        base = tile * tm

        def start(r, c):
            pltpu.make_async_copy(_row_tile(ys_ref, d1_ref[base + r]), _row_tile(ybuf.at[slot, 0], r),
                                  sem.at[slot]).start()
            pltpu.make_async_copy(_row_tile(ys_ref, d2_ref[base + r]), _row_tile(ybuf.at[slot, 1], r),
                                  sem.at[slot]).start(priority=1)
            return c

        lax.fori_loop(0, tm, start, 0, unroll=8)

    @pl.when(i == 0)
    def _():
        issue(0, 0)

    @pl.when(i + 1 < n_steps)
    def _():
        issue(i + 1, (i + 1) % 2)

    slot = i % 2

    def wait(r, c):
        for k in range(2):
            pltpu.make_async_copy(_row_tile(ys_ref, 0), _row_tile(ybuf.at[slot, k], 0), sem.at[slot]).wait()
        return c

    lax.fori_loop(0, tm, wait, 0, unroll=8)
    route = route_ref[...]
    ff = (route[:, 4:5] * _from_row_tiles(ybuf.at[slot, 0], tm)
          + route[:, 5:6] * _from_row_tiles(ybuf.at[slot, 1], tm))
    o_ref[...] = _layer_norm(ALPHA * x_ref[...] + ff, g_ref[...], b_ref[...])


def _moe_ffn(x2d, route, counts, w1, w3, w2, g, b):
    n, d = x2d.shape
    n_exp, _, f = w1.shape
    tr = min(TR_MOE, n)
    tm = min(TM_MOE, n)
    n_pad = 2 * n + n_exp * tr
    n_tiles = n_pad // tr
    i32 = jnp.int32

    cnt = counts[0, :n_exp].astype(i32)
    seg = (cnt + tr - 1) // tr * tr
    seg_end = jnp.cumsum(seg)
    seg_start = seg_end - seg
    e1, e2 = route[:, 0].astype(i32), route[:, 1].astype(i32)
    dest1 = seg_start[e1] + route[:, 2].astype(i32)
    dest2 = seg_start[e2] + route[:, 3].astype(i32)
    n_used = jnp.maximum(seg_end[-1] // tr, 1)
    tile = jnp.minimum(jnp.arange(n_tiles, dtype=i32), n_used - 1)
    tile_expert = jnp.minimum(jnp.sum(tile[:, None] * tr >= seg_end[None, :], axis=1), n_exp - 1).astype(i32)

    cparams = pltpu.CompilerParams(dimension_semantics=("arbitrary",), vmem_limit_bytes=VMEM_LIMIT)
    assert d == ROW_TILE * 128
    xs = pl.pallas_call(
        functools.partial(_dispatch_kernel, tm=tm, tr=tr),
        grid_spec=pltpu.PrefetchScalarGridSpec(
            num_scalar_prefetch=3, grid=(n // tm,),
            in_specs=[pl.BlockSpec((tm, d), lambda i, *_: (i, 0))],
            out_specs=pl.BlockSpec(memory_space=pl.ANY),
            scratch_shapes=[pltpu.VMEM((2, tm * ROW_TILE, 128), F32), pltpu.VMEM((tr * ROW_TILE, 128), F32),
                            pltpu.SemaphoreType.DMA((3,))]),
        out_shape=jax.ShapeDtypeStruct((n_pad * ROW_TILE, 128), F32),
        compiler_params=cparams,
        name="moe_dispatch",
    )(dest1, dest2, seg_end.astype(i32), x2d)

    ys = pl.pallas_call(
        functools.partial(_expert_kernel, tr=tr),
        grid_spec=pltpu.PrefetchScalarGridSpec(
            num_scalar_prefetch=3, grid=(n_tiles,),
            in_specs=[pl.BlockSpec((tr * ROW_TILE, 128), lambda i, te, blk, nu: (blk[i], 0)),
                      pl.BlockSpec((1, d, f), lambda i, te, blk, nu: (te[i], 0, 0)),
                      pl.BlockSpec((1, d, f), lambda i, te, blk, nu: (te[i], 0, 0)),
                      pl.BlockSpec((1, f, d), lambda i, te, blk, nu: (te[i], 0, 0))],
            out_specs=pl.BlockSpec((tr * ROW_TILE, 128), lambda i, te, blk, nu: (i, 0))),
        out_shape=jax.ShapeDtypeStruct((n_pad * ROW_TILE, 128), F32),
        compiler_params=cparams,
        name="moe_experts",
    )(tile_expert, tile, n_used.reshape(1), xs, w1.astype(BF16), w3.astype(BF16), w2.astype(BF16))

    row = lambda a: a.reshape(1, -1)
    return pl.pallas_call(
        functools.partial(_combine_kernel, tm=tm),
        grid_spec=pltpu.PrefetchScalarGridSpec(
            num_scalar_prefetch=2, grid=(n // tm,),
            in_specs=[pl.BlockSpec((tm, d), lambda i, *_: (i, 0)),
                      pl.BlockSpec((tm, 128), lambda i, *_: (i, 0)),
                      pl.BlockSpec((1, d), lambda i, *_: (0, 0)),
                      pl.BlockSpec((1, d), lambda i, *_: (0, 0)),
                      pl.BlockSpec(memory_space=pl.ANY)],
            out_specs=pl.BlockSpec((tm, d), lambda i, *_: (i, 0)),
            scratch_shapes=[pltpu.VMEM((2, 2, tm * ROW_TILE, 128), F32), pltpu.SemaphoreType.DMA((2,))]),
        out_shape=jax.ShapeDtypeStruct((n, d), F32),
        compiler_params=cparams,
        name="moe_combine",
    )(dest1, dest2, x2d, route, row(g), row(b), ys)


def kernel(x, positions, w_in, mla_q_norm_g, w_uq, mla_kv_norm_g, w_ukv, fox_forget_b, conv_w, conv_b,
           conv_norm_g, conv_norm_b, mla_out_norm_g, fox_out_norm_g, w_out, ln1_g, ln1_b, dense_w1,
           dense_w3, dense_w2, router_w, expert_w1, expert_w3, expert_w2, ln2_g, ln2_b):
    batch, seq, d = x.shape
    assert d == D_MODEL and seq % TQ == 0 and seq % min(TM_IN, seq) == 0
    depth = w_in.shape[0]
    tabs = _rope_tables(positions)
    h = x.reshape(batch * seq, d)
    for layer in range(depth):
        pw = _prep_inproj_weights(w_in[layer], w_uq[layer], w_ukv[layer], fox_forget_b[layer])
        q_t, k, v_t, hc = _input_projection(
            h, tabs, pw, mla_q_norm_g[layer], mla_kv_norm_g[layer], conv_w[layer], conv_b[layer],
            conv_norm_g[layer], conv_norm_b[layer], seq)
        o = _attention(q_t, k, v_t, batch, seq)
        j = layer // 2
        if layer % 2 == 0:
            h = _output_projection(o, hc, h, mla_out_norm_g[layer], fox_out_norm_g[layer], w_out[layer],
                                   ln1_g[layer], ln1_b[layer])[0]
            h = _dense_ffn(h, dense_w1[j], dense_w3[j], dense_w2[j], ln2_g[layer], ln2_b[layer])
        else:
            h, route, counts = _output_projection(o, hc, h, mla_out_norm_g[layer], fox_out_norm_g[layer],
                                                  w_out[layer], ln1_g[layer], ln1_b[layer], router_w[j])
            h = _moe_ffn(h, route, counts, expert_w1[j], expert_w3[j], expert_w2[j], ln2_g[layer],
                         ln2_b[layer])
    return h.reshape(batch, seq, d)
```

```python
import functools
import math

import numpy as np
import jax
import jax.numpy as jnp
from jax import lax
from jax.experimental import pallas as pl
from jax.experimental.pallas import tpu as pltpu

F32 = jnp.float32
BF16 = jnp.bfloat16

D_MODEL = 1024
DEPTH = 4
MLA_HEADS = 8
MLA_NOPE = 64
MLA_ROPE = 32
MLA_V = 64
MLA_Q_RANK = 256
MLA_KV_RANK = 128
ROPE_THETA = 10000.0
FOX_HEADS = 4
FOX_DIM = 64
CONV_CH = 256
CONV_GROUPS = 4
CONV_WIDTH = 31
MLA_WIDTH = MLA_HEADS * MLA_V
FOX_WIDTH = FOX_HEADS * FOX_DIM
N_EXPERTS = 8
ALPHA = (2.0 * DEPTH) ** 0.25
NORM_EPS = 1e-5
LOG2E = math.log2(math.e)

HEAD_PAD = 128
N_HEADS = MLA_HEADS + FOX_HEADS
V_DIM = 64
V_ROWS = 80
CONV_HALO = 32
VMEM_LIMIT = 56 * 1024 * 1024

TQ = 512
TK = 256
HPS = 2
TM_IN = 512
TM_OUT = 512
TM_FFN = 1024
TR_MOE = 512
TM_MOE = 512
ROW_TILE = 8


def _nt_dot(a, b):
    return lax.dot_general(a, b, (((1,), (1,)), ((), ())), preferred_element_type=F32)


def _dot(a, b):
    return jnp.dot(a, b, preferred_element_type=F32)


def _split2_dot(a, m_bf16):
    hi = a.astype(BF16)
    lo = (a - hi.astype(F32)).astype(BF16)
    return _dot(hi, m_bf16) + _dot(lo, m_bf16)


def _split3(a):
    hi = a.astype(BF16).astype(F32)
    r1 = a - hi
    mid = r1.astype(BF16).astype(F32)
    lo = (r1 - mid).astype(BF16).astype(F32)
    return hi, mid, lo


def _const_spec(shape):
    nd = len(shape)
    return pl.BlockSpec(shape, lambda *_: (0,) * nd, pipeline_mode=pl.Buffered(1))


def _rope_kernel(pos_ref, invf_ref, c_ref, s_ref, ct_ref, st_ref):
    pos = pos_ref[...].astype(F32)
    ang = invf_ref[...] * pos
    cos = jnp.cos(ang)
    sin = jnp.sin(ang)
    tn = pos.shape[1]
    ct = jnp.concatenate([jnp.ones((MLA_NOPE, tn), F32), cos, cos, jnp.zeros((32, tn), F32)], axis=0)
    st = jnp.concatenate([jnp.zeros((MLA_NOPE, tn), F32), sin, sin, jnp.zeros((32, tn), F32)], axis=0)
    ct_ref[...] = ct
    st_ref[...] = st
    c_ref[...] = ct.T
    s_ref[...] = st.T


def _rope_tables(positions):
    n = positions.size
    tn = min(512, n)
    inv_freq = ROPE_THETA ** (-jnp.arange(0, MLA_ROPE, 2, dtype=F32) / MLA_ROPE)
    return pl.pallas_call(
        _rope_kernel,
        grid=(n // tn,),
        in_specs=[pl.BlockSpec((1, tn), lambda i: (0, i)),
                  pl.BlockSpec((MLA_ROPE // 2, 1), lambda i: (0, 0))],
        out_specs=[pl.BlockSpec((tn, HEAD_PAD), lambda i: (i, 0)),
                   pl.BlockSpec((tn, HEAD_PAD), lambda i: (i, 0)),
                   pl.BlockSpec((HEAD_PAD, tn), lambda i: (0, i)),
                   pl.BlockSpec((HEAD_PAD, tn), lambda i: (0, i))],
        out_shape=[jax.ShapeDtypeStruct((n, HEAD_PAD), F32),
                   jax.ShapeDtypeStruct((n, HEAD_PAD), F32),
                   jax.ShapeDtypeStruct((HEAD_PAD, n), F32),
                   jax.ShapeDtypeStruct((HEAD_PAD, n), F32)],
        name="rope_tables",
    )(positions.reshape(1, n), inv_freq.reshape(-1, 1))


_A_CQ = 0
_A_CKV = _A_CQ + MLA_Q_RANK
_A_KR = _A_CKV + MLA_KV_RANK
_A_KRR = _A_KR + HEAD_PAD
_A_FK = _A_KRR + HEAD_PAD
_A_CA = _A_FK + FOX_HEADS * HEAD_PAD
_A_CG = _A_CA + CONV_CH
_A_COLS = _A_CG + CONV_CH
_AUG_ROWS = 8
_F_ROWS = 16


def _rms(x, g):
    ms = jnp.mean(jnp.square(x), axis=-1, keepdims=True)
    return x * lax.rsqrt(ms + NORM_EPS) * g


def _inproj_kernel(x_ref, c_ref, s_ref, ct_ref, st_ref, wa_ref, wfq_ref, wfv_ref, wf_ref, fb_ref,
                   gq_ref, wuq_ref, wuqr_ref, gkv_ref, wuk_ref, wuv_ref,
                   cw_ref, cb_ref, cng_ref, cnb_ref, gmat_ref,
                   qt_ref, k_ref, vt_ref, hc_ref,
                   hbuf, hsh, cbuf, fcarry, *, tiles_per_seq, tm):
    i = pl.program_id(0)

    @pl.when(i % tiles_per_seq == 0)
    def _():
        hbuf[0:CONV_HALO, :] = jnp.zeros((CONV_HALO, CONV_CH), F32)
        fcarry[...] = jnp.zeros_like(fcarry)

    xb = x_ref[...].astype(BF16)
    p1 = _dot(xb, wa_ref[...])
    cos_t = c_ref[...]
    sin_t = s_ref[...]
    cos_tt = ct_ref[...]
    sin_tt = st_ref[...]

    cqn = _rms(p1[:, _A_CQ:_A_CQ + MLA_Q_RANK], gq_ref[...]).astype(BF16)
    q_t = _nt_dot(wuq_ref[...], cqn)
    q_rot_t = _nt_dot(wuqr_ref[...], cqn)
    mla_scale = (MLA_NOPE + MLA_ROPE) ** -0.5 * LOG2E
    for h in range(MLA_HEADS):
        rows = slice(h * HEAD_PAD, (h + 1) * HEAD_PAD)
        qh = (q_t[rows, :] * cos_tt + q_rot_t[rows, :] * sin_tt) * mla_scale
        for c in range(tm // TQ):
            qt_ref[c, rows, :] = qh[:, c * TQ:(c + 1) * TQ].astype(BF16)

    ckvn = _rms(p1[:, _A_CKV:_A_CKV + MLA_KV_RANK], gkv_ref[...]).astype(BF16)
    k_nope = _dot(ckvn, wuk_ref[...])
    k_rope = p1[:, _A_KR:_A_KR + HEAD_PAD] * cos_t + p1[:, _A_KRR:_A_KRR + HEAD_PAD] * sin_t
    for h in range(MLA_HEADS):
        cols = slice(h * HEAD_PAD, (h + 1) * HEAD_PAD)
        k_ref[:, cols] = (k_nope[:, cols] + k_rope).astype(BF16)
    v_t = _nt_dot(wuv_ref[...], ckvn)
    fv_t = _nt_dot(wfv_ref[...], xb)
    ones_blk = jnp.where(lax.broadcasted_iota(jnp.int32, (V_ROWS - V_DIM, tm), 0) == 0, 1.0, 0.0)
    for h in range(N_HEADS):
        src = v_t if h < MLA_HEADS else fv_t
        r0 = (h if h < MLA_HEADS else h - MLA_HEADS) * V_DIM
        vh = jnp.concatenate([src[r0:r0 + V_DIM, :], ones_blk], axis=0).astype(BF16)
        for c in range(tm // TK):
            vt_ref[c, h * V_ROWS:(h + 1) * V_ROWS, :] = vh[:, c * TK:(c + 1) * TK]

    z = _nt_dot(wf_ref[...], xb) + fb_ref[...]
    logf = (jnp.minimum(z, 0.0) - jnp.log1p(jnp.exp(-jnp.abs(z)))) * LOG2E
    r_i = lax.broadcasted_iota(jnp.int32, (tm, tm), 0)
    c_i = lax.broadcasted_iota(jnp.int32, (tm, tm), 1)
    upper = jnp.where(r_i <= c_i, 1.0, 0.0).astype(BF16)
    l_hi, l_mid, l_lo = _split3(logf)
    cum = (_dot(l_hi.astype(BF16), upper) + _dot(l_mid.astype(BF16), upper)
           + _dot(l_lo.astype(BF16), upper))
    f_cum = cum + fcarry[:, 0:1]
    fcarry[...] = jnp.broadcast_to(f_cum[:, tm - 1:tm], fcarry.shape)
    f_hi, f_mid, f_lo = _split3(f_cum)

    fq_t = _nt_dot(wfq_ref[...], xb)
    row8 = lax.broadcasted_iota(jnp.int32, (_AUG_ROWS, tm), 0)
    fox_scale = FOX_DIM ** -0.5 * LOG2E
    for h in range(FOX_HEADS):
        bh = lambda a: jnp.broadcast_to(a[h:h + 1, :], (_AUG_ROWS, tm))
        aug_q = jnp.where(row8 == 0, bh(f_hi), jnp.where(row8 == 1, bh(f_mid), jnp.where(
            row8 == 2, bh(f_lo), jnp.where(row8 < 6, 1.0, 0.0))))
        aug_k = jnp.where(row8 < 3, 1.0, jnp.where(row8 == 3, -bh(f_hi), jnp.where(
            row8 == 4, -bh(f_mid), jnp.where(row8 == 5, -bh(f_lo), 0.0))))
        pad = jnp.zeros((HEAD_PAD - FOX_DIM - _AUG_ROWS, tm), F32)
        qh = jnp.concatenate(
            [fq_t[h * HEAD_PAD:h * HEAD_PAD + FOX_DIM, :] * fox_scale, aug_q, pad], axis=0)
        rows = slice((MLA_HEADS + h) * HEAD_PAD, (MLA_HEADS + h + 1) * HEAD_PAD)
        for c in range(tm // TQ):
            qt_ref[c, rows, :] = qh[:, c * TQ:(c + 1) * TQ].astype(BF16)
        kaug_t = jnp.concatenate([jnp.zeros((FOX_DIM, tm), F32), aug_k, pad], axis=0)
        fk = p1[:, _A_FK + h * HEAD_PAD:_A_FK + (h + 1) * HEAD_PAD]
        k_ref[:, rows] = (fk + kaug_t.T).astype(BF16)

    a = p1[:, _A_CA:_A_CA + CONV_CH]
    g = p1[:, _A_CG:_A_CG + CONV_CH]
    hbuf[CONV_HALO:CONV_HALO + tm, :] = a * jax.nn.sigmoid(g)
    chunk = 64
    first = CONV_HALO - (CONV_WIDTH - 1)
    for r in range(1, 8):
        hsh[r - 1] = hbuf[r:r + tm + CONV_HALO - 8, :]
    for c0 in range(0, tm, chunk):
        acc = jnp.zeros((chunk, CONV_CH), F32)
        for o in range(first, first + CONV_WIDTH):
            r = o % 8
            row = c0 + o - r
            seg = hbuf[row:row + chunk, :] if r == 0 else hsh[r - 1, row:row + chunk, :]
            acc = acc + cw_ref[o - first:o - first + 1, :] * seg
        cbuf[c0:c0 + chunk, :] = acc
    hbuf[0:CONV_HALO, :] = hbuf[tm:tm + CONV_HALO, :]
    hv = cbuf[...] + cb_ref[...]
    gm = gmat_ref[...]
    mu = _split2_dot(hv, gm)
    d = hv - mu
    var = _split2_dot(d * d, gm)
    hn = d * lax.rsqrt(var + NORM_EPS) * cng_ref[...] + cnb_ref[...]
    hc_ref[...] = (hn * jax.nn.sigmoid(hn)).astype(BF16)


def _prep_inproj_weights(w_in, w_uq, w_ukv, fox_forget_b):
    o = np.cumsum((0, MLA_Q_RANK, MLA_KV_RANK, MLA_ROPE, FOX_WIDTH, FOX_WIDTH, FOX_WIDTH, FOX_HEADS,
                   2 * CONV_CH))
    w_cq, w_ckv, w_kr, w_fq, w_fk, w_fv, w_f, w_cv = (w_in[:, o[i]:o[i + 1]] for i in range(8))
    d = w_in.shape[0]
    half = MLA_ROPE // 2

    def rot_cols(w):
        return jnp.concatenate([-w[..., half:], w[..., :half]], axis=-1)

    def rope_block(w):
        return jnp.pad(w, ((0, 0), (MLA_NOPE, HEAD_PAD - MLA_NOPE - MLA_ROPE)))

    w_fk_pad = jnp.pad(w_fk.reshape(d, FOX_HEADS, FOX_DIM), ((0, 0), (0, 0), (0, HEAD_PAD - FOX_DIM)))
    wa = jnp.concatenate([w_cq, w_ckv, rope_block(w_kr), rope_block(rot_cols(w_kr)),
                          w_fk_pad.reshape(d, FOX_HEADS * HEAD_PAD), w_cv], axis=1)
    w_fq_pad = jnp.pad(w_fq.reshape(d, FOX_HEADS, FOX_DIM), ((0, 0), (0, 0), (0, HEAD_PAD - FOX_DIM)))
    wfq_t = w_fq_pad.reshape(d, FOX_HEADS * HEAD_PAD).T
    wfv_t = w_fv.T
    wf_t = jnp.pad(w_f, ((0, 0), (0, _F_ROWS - FOX_HEADS))).T
    fb = jnp.pad(fox_forget_b, (0, _F_ROWS - FOX_HEADS)).reshape(_F_ROWS, 1)

    uq = w_uq.reshape(MLA_Q_RANK, MLA_HEADS, MLA_NOPE + MLA_ROPE)
    uq_nope, uq_rope = uq[..., :MLA_NOPE], uq[..., MLA_NOPE:]
    tail = ((0, 0), (0, 0), (0, HEAD_PAD - MLA_NOPE - MLA_ROPE))
    uq_pad = jnp.pad(jnp.concatenate([uq_nope, uq_rope], axis=-1), tail)
    uq_rot_pad = jnp.pad(jnp.concatenate([jnp.zeros_like(uq_nope), rot_cols(uq_rope)], axis=-1), tail)
    wuq_t = uq_pad.reshape(MLA_Q_RANK, MLA_HEADS * HEAD_PAD).T
    wuqr_t = uq_rot_pad.reshape(MLA_Q_RANK, MLA_HEADS * HEAD_PAD).T
    ukv = w_ukv.reshape(MLA_KV_RANK, MLA_HEADS, MLA_NOPE + MLA_V)
    wuk = jnp.pad(ukv[..., :MLA_NOPE], ((0, 0), (0, 0), (0, HEAD_PAD - MLA_NOPE))).reshape(
        MLA_KV_RANK, MLA_HEADS * HEAD_PAD)
    wuv_t = ukv[..., MLA_NOPE:].reshape(MLA_KV_RANK, MLA_WIDTH).T
    bf = lambda a: a.astype(BF16)
    return dict(wa=bf(wa), wfq=bf(wfq_t), wfv=bf(wfv_t), wf=bf(wf_t), fb=fb, wuq=bf(wuq_t),
                wuqr=bf(wuqr_t), wuk=bf(wuk), wuv=bf(wuv_t))


def _input_projection(x2d, tabs, pw, gq, gkv, conv_w, conv_b, conv_ng, conv_nb, seq):
    n, d = x2d.shape
    tm = min(TM_IN, seq)
    cos_t, sin_t, cos_tt, sin_tt = tabs
    gidx = np.arange(CONV_CH) // (CONV_CH // CONV_GROUPS)
    gmat = jnp.asarray((gidx[:, None] == gidx[None, :]) / (CONV_CH // CONV_GROUPS), BF16)
    cw = jnp.pad(conv_w, ((0, 32 - CONV_WIDTH), (0, 0)))
    row = lambda a: a.reshape(1, -1)
    tok = lambda w: pl.BlockSpec((tm, w), lambda i: (i, 0))
    tok_t = lambda r: pl.BlockSpec((r, tm), lambda i: (0, i))
    consts = [pw["wa"], pw["wfq"], pw["wfv"], pw["wf"], pw["fb"], row(gq), pw["wuq"], pw["wuqr"],
              row(gkv), pw["wuk"], pw["wuv"], cw, row(conv_b), row(conv_ng), row(conv_nb), gmat]
    kern = functools.partial(_inproj_kernel, tiles_per_seq=seq // tm, tm=tm)
    return pl.pallas_call(
        kern,
        grid=(n // tm,),
        in_specs=[tok(d), tok(HEAD_PAD), tok(HEAD_PAD), tok_t(HEAD_PAD), tok_t(HEAD_PAD)]
        + [_const_spec(c.shape) for c in consts],
        out_specs=[pl.BlockSpec((tm // TQ, N_HEADS * HEAD_PAD, TQ), lambda i: (i, 0, 0)),
                   tok(N_HEADS * HEAD_PAD),
                   pl.BlockSpec((tm // TK, N_HEADS * V_ROWS, TK), lambda i: (i, 0, 0)),
                   tok(CONV_CH)],
        out_shape=[jax.ShapeDtypeStruct((n // TQ, N_HEADS * HEAD_PAD, TQ), BF16),
                   jax.ShapeDtypeStruct((n, N_HEADS * HEAD_PAD), BF16),
                   jax.ShapeDtypeStruct((n // TK, N_HEADS * V_ROWS, TK), BF16),
                   jax.ShapeDtypeStruct((n, CONV_CH), BF16)],
        scratch_shapes=[pltpu.VMEM((CONV_HALO + tm, CONV_CH), F32),
                        pltpu.VMEM((7, tm + CONV_HALO - 8, CONV_CH), F32),
                        pltpu.VMEM((tm, CONV_CH), F32),
                        pltpu.VMEM((_F_ROWS, 128), F32)],
        compiler_params=pltpu.CompilerParams(dimension_semantics=("arbitrary",),
                                             vmem_limit_bytes=VMEM_LIMIT),
        name="input_projection",
    )(x2d, cos_t, sin_t, cos_tt, sin_tt, *consts)


def _attn_kernel(qt_ref, k_ref, vt_ref, o_ref, *scratch, n_tiles):
    assert TQ == 2 * TK
    s_ref = (scratch[0:HPS], scratch[HPS:2 * HPS], scratch[2 * HPS:3 * HPS])
    p_ref = (scratch[3 * HPS:4 * HPS], scratch[4 * HPS:5 * HPS])
    acc_ref = scratch[5 * HPS:6 * HPS]
    diff_ref = scratch[6 * HPS]
    diff_ref[...] = (lax.broadcasted_iota(jnp.int32, (TK, TQ), 1)
                     - lax.broadcasted_iota(jnp.int32, (TK, TQ), 0))

    def tile_scores(tile, j, slot):
        row0 = pl.multiple_of(j * TK, TK)
        block_max = []
        for h in range(HPS):
            s = _dot(k_ref[pl.ds(row0, TK), h * HEAD_PAD:(h + 1) * HEAD_PAD],
                     qt_ref[tile, h * HEAD_PAD:(h + 1) * HEAD_PAD, :])
            s_ref[slot][h][...] = s
            block_max.append(jnp.max(s, axis=0, keepdims=True))
        return block_max

    def q_tile(i, bm0):
        scores = functools.partial(tile_scores, i)

        def softmax(slot, m, block_max):
            m_new = [jnp.maximum(m[h], block_max[h]) for h in range(HPS)]
            for h in range(HPS):
                p_ref[slot][h][...] = jnp.exp2(s_ref[slot][h][...] - m_new[h]).astype(BF16)
            return m_new, [jnp.exp2(m[h] - m_new[h]) for h in range(HPS)]

        def softmax_diag(s_slot, p_slot, m, dg):
            m_new, alpha = [], []
            for h in range(HPS):
                s = jnp.where(diff_ref[...] >= dg * TK, s_ref[s_slot][h][...], -jnp.inf)
                m_new.append(jnp.maximum(m[h], jnp.max(s, axis=0, keepdims=True)))
                p_ref[p_slot][h][...] = jnp.exp2(s - m_new[h]).astype(BF16)
                alpha.append(jnp.exp2(m[h] - m_new[h]))
            return m_new, alpha

        def values(j, slot, alpha):
            for h in range(HPS):
                acc_ref[h][...] = alpha[h] * acc_ref[h][...] + _dot(
                    vt_ref[j, h * V_ROWS:(h + 1) * V_ROWS, :], p_ref[slot][h][...])

        for h in range(HPS):
            acc_ref[h][...] = jnp.zeros_like(acc_ref[h])
            p_ref[1][h][...] = jnp.zeros_like(p_ref[1][h])
        m0 = [jnp.full((1, TQ), -1e30, F32)] * HPS
        a0 = [jnp.ones((1, TQ), F32)] * HPS

        def pair(u, state):
            m, alpha, bm_t = list(state[0:HPS]), list(state[HPS:2 * HPS]), list(state[2 * HPS:3 * HPS])
            t = 2 * u
            m, alpha_t = softmax(0, m, bm_t)
            values(jnp.maximum(t - 1, 0), 1, alpha)
            bm_t1 = scores(t + 1, 1)
            m, alpha_t1 = softmax(1, m, bm_t1)
            values(t, 0, alpha_t)
            bm_t2 = scores(t + 2, 0)
            return (*m, *alpha_t1, *bm_t2)

        state = lax.fori_loop(0, i // 2, lambda v, st: pair(2 * v + 1, pair(2 * v, st)), (*m0, *a0, *bm0))
        state = lax.fori_loop(i - i % 2, i, pair, state)
        m, alpha = list(state[0:HPS]), list(state[HPS:2 * HPS])
        d0 = 2 * i
        values(jnp.maximum(d0 - 1, 0), 1, alpha)
        nxt = jnp.minimum(i + 1, n_tiles - 1)
        m, alpha_d0 = softmax_diag(0, 0, m, 0)
        bm_next = tile_scores(nxt, 0, 0)
        m, alpha_d1 = softmax_diag(2, 1, m, 1)
        tile_scores(nxt, 2 * nxt + 1, 2)
        values(d0, 0, alpha_d0)
        values(d0 + 1, 1, alpha_d1)
        out_t = jnp.concatenate(
            [acc_ref[h][0:V_DIM, :] / acc_ref[h][V_DIM:V_DIM + 1, :] for h in range(HPS)], axis=0)
        o_ref[pl.ds(pl.multiple_of(i * TQ, TQ), TQ), :] = out_t.T.astype(o_ref.dtype)
        return tuple(bm_next)

    tile_scores(0, 1, 2)
    lax.fori_loop(0, n_tiles, q_tile, tuple(tile_scores(0, 0, 0)))


def _attention(q_t, k, v_t, batch, seq):
    n = k.shape[0]
    groups = N_HEADS // HPS
    return pl.pallas_call(
        functools.partial(_attn_kernel, n_tiles=seq // TQ),
        grid=(batch, groups),
        in_specs=[pl.BlockSpec((seq // TQ, HPS * HEAD_PAD, TQ), lambda b, p: (b, p, 0)),
                  pl.BlockSpec((seq, HPS * HEAD_PAD), lambda b, p: (b, p)),
                  pl.BlockSpec((seq // TK, HPS * V_ROWS, TK), lambda b, p: (b, p, 0))],
        out_specs=pl.BlockSpec((seq, HPS * V_DIM), lambda b, p: (b, p)),
        out_shape=jax.ShapeDtypeStruct((n, N_HEADS * V_DIM), BF16),
        scratch_shapes=[pltpu.VMEM((TK, TQ), F32)] * (3 * HPS) + [pltpu.VMEM((TK, TQ), BF16)] * (2 * HPS)
        + [pltpu.VMEM((V_ROWS, TQ), F32)] * HPS + [pltpu.VMEM((TK, TQ), jnp.int32)],
        compiler_params=pltpu.CompilerParams(dimension_semantics=("arbitrary", "arbitrary"),
                                             vmem_limit_bytes=VMEM_LIMIT),
        name="attention",
    )(q_t, k, v_t)


def _layer_norm(x, g, b):
    mu = jnp.mean(x, axis=-1, keepdims=True)
    d = x - mu
    var = jnp.mean(jnp.square(d), axis=-1, keepdims=True)
    return d * lax.rsqrt(var + NORM_EPS) * g + b


def _outproj_kernel(o_ref, hc_ref, x_ref, gm_ref, gf_ref, wo_ref, g1_ref, b1_ref, *rest, with_router):
    if with_router:
        rw_ref, x1_ref, route_ref, counts_ref, cnt_ref, upper_ref = rest
    else:
        (x1_ref,) = rest
    o = o_ref[...].astype(F32)
    mla = _rms(o[:, :MLA_WIDTH], gm_ref[...])
    fox = _rms(o[:, MLA_WIDTH:], gf_ref[...])
    mixed = jnp.concatenate([mla.astype(BF16), fox.astype(BF16), hc_ref[...]], axis=-1)
    y = _dot(mixed, wo_ref[...])
    x1 = _layer_norm(ALPHA * x_ref[...] + y, g1_ref[...], b1_ref[...])
    x1_ref[...] = x1
    if with_router:
        rw = rw_ref[...]
        x_hi = x1.astype(BF16)
        x_lo = (x1 - x_hi.astype(F32)).astype(BF16)
        w_hi = rw.astype(BF16)
        w_lo = (rw - w_hi.astype(F32)).astype(BF16)
        both = _dot(x_hi, jnp.concatenate([w_hi, w_lo], axis=1))
        logits = both[:, :128] + (_dot(x_lo, w_hi) + both[:, 128:])
        tm = logits.shape[0]
        lg = logits.T[0:N_EXPERTS, :]
        row = lax.broadcasted_iota(jnp.int32, lg.shape, 0)
        v1 = jnp.max(lg, axis=0, keepdims=True)
        i1 = jnp.min(jnp.where(lg == v1, row, N_EXPERTS), axis=0, keepdims=True)
        rest_l = jnp.where(row == i1, -jnp.inf, lg)
        v2 = jnp.max(rest_l, axis=0, keepdims=True)
        i2 = jnp.min(jnp.where(rest_l == v2, row, N_EXPERTS), axis=0, keepdims=True)
        e2 = jnp.exp(v2 - v1)
        den = 1.0 + e2

        @pl.when(pl.program_id(0) == 0)
        def _():
            cnt_ref[...] = jnp.zeros_like(cnt_ref)
            r_i = lax.broadcasted_iota(jnp.int32, (tm, tm), 0)
            c_i = lax.broadcasted_iota(jnp.int32, (tm, tm), 1)
            upper_ref[...] = jnp.where(r_i < c_i, 1.0, 0.0).astype(BF16)

        sel = jnp.where(row == i1, 1.0, jnp.where(row == i2, 1.0, 0.0))
        sel16 = jnp.concatenate([sel, jnp.zeros_like(sel)], axis=0).astype(BF16)
        before = cnt_ref[:, 0:1]
        rank = _dot(sel16, upper_ref[...])[0:N_EXPERTS, :] + before
        total = before + jnp.sum(sel, axis=1, keepdims=True)
        cnt_ref[...] = jnp.broadcast_to(total, cnt_ref.shape)
        counts_ref[...] = jnp.broadcast_to(total, counts_ref.shape)
        r1 = jnp.sum(jnp.where(row == i1, rank, 0.0), axis=0, keepdims=True)
        r2 = jnp.sum(jnp.where(row == i2, rank, 0.0), axis=0, keepdims=True)
        rows = (i1.astype(F32), i2.astype(F32), r1, r2, 1.0 / den, e2 / den)
        route = jnp.zeros(lg.shape, F32)
        for c, v in enumerate(rows):
            route = jnp.where(row == c, v, route)
        route_ref[...] = route


def _output_projection(o, hc, x2d, gm, gf, w_out, g1, b1, router_w=None):
    n, d = x2d.shape
    tm = min(TM_OUT, n)
    row = lambda a: a.reshape(1, -1)
    tok = lambda w: pl.BlockSpec((tm, w), lambda i: (i, 0))
    consts = [row(gm), row(gf), w_out.astype(BF16), row(g1), row(b1)]
    out_specs = [tok(d)]
    out_shape = [jax.ShapeDtypeStruct((n, d), F32)]
    scratch = []
    if router_w is not None:
        consts.append(jnp.pad(router_w, ((0, 0), (0, 128 - N_EXPERTS))))
        out_specs += [pl.BlockSpec((N_EXPERTS, tm), lambda i: (0, i)),
                      pl.BlockSpec((N_EXPERTS, 128), lambda i: (0, 0))]
        out_shape += [jax.ShapeDtypeStruct((N_EXPERTS, n), F32), jax.ShapeDtypeStruct((N_EXPERTS, 128), F32)]
        scratch = [pltpu.VMEM((N_EXPERTS, 128), F32), pltpu.VMEM((tm, tm), BF16)]
    return pl.pallas_call(
        functools.partial(_outproj_kernel, with_router=router_w is not None),
        grid=(n // tm,),
        in_specs=[tok(o.shape[1]), tok(CONV_CH), tok(d)] + [_const_spec(c.shape) for c in consts],
        out_specs=out_specs,
        out_shape=out_shape,
        scratch_shapes=scratch,
        compiler_params=pltpu.CompilerParams(dimension_semantics=("arbitrary",),
                                             vmem_limit_bytes=VMEM_LIMIT),
        name="output_projection",
    )(o, hc, x2d, *consts)


def _swiglu_tile(xb, w1, w3, w2):
    h1 = _dot(xb, w1)
    h3 = _dot(xb, w3)
    hid = (h1 * jax.nn.sigmoid(h1) * h3).astype(BF16)
    return _dot(hid, w2)


def _dense_ffn_kernel(x_ref, w1_ref, w3_ref, w2_ref, g_ref, b_ref, o_ref, *, f_chunk):
    x = x_ref[...]
    xb = x.astype(BF16)
    ff = None
    for c0 in range(0, w1_ref.shape[1], f_chunk):
        part = _swiglu_tile(xb, w1_ref[:, c0:c0 + f_chunk], w3_ref[:, c0:c0 + f_chunk],
                            w2_ref[c0:c0 + f_chunk, :])
        ff = part if ff is None else ff + part
    o_ref[...] = _layer_norm(ALPHA * x + ff, g_ref[...], b_ref[...])


def _dense_ffn(x2d, w1, w3, w2, g, b):
    n, d = x2d.shape
    tm = min(TM_FFN, n)
    f = w1.shape[1]
    f_chunk = f // 2 if (f // 2) % 128 == 0 else f
    row = lambda a: a.reshape(1, -1)
    consts = [w1.astype(BF16), w3.astype(BF16), w2.astype(BF16), row(g), row(b)]
    return pl.pallas_call(
        functools.partial(_dense_ffn_kernel, f_chunk=f_chunk),
        grid=(n // tm,),
        in_specs=[pl.BlockSpec((tm, d), lambda i: (i, 0))] + [_const_spec(c.shape) for c in consts],
        out_specs=pl.BlockSpec((tm, d), lambda i: (i, 0)),
        out_shape=jax.ShapeDtypeStruct((n, d), F32),
        compiler_params=pltpu.CompilerParams(dimension_semantics=("arbitrary",),
                                             vmem_limit_bytes=VMEM_LIMIT),
        name="dense_ffn",
    )(x2d, *consts)


def _to_row_tiles(ref, x):
    for c in range(ROW_TILE):
        ref[pl.ds(c, x.shape[0], stride=ROW_TILE), :] = x[:, c * 128:(c + 1) * 128]


def _from_row_tiles(ref, t):
    return jnp.concatenate([ref[pl.ds(c, t, stride=ROW_TILE), :] for c in range(ROW_TILE)], axis=-1)


def _row_tile(ref, r):
    return ref.at[pl.ds(pl.multiple_of(r * ROW_TILE, ROW_TILE), ROW_TILE)]


def _dispatch_kernel(d1_ref, d2_ref, se_ref, x_ref, xs_ref, xr, zbuf, sem, *, tm, tr):
    i = pl.program_id(0)

    @pl.when(i == 0)
    def _():
        zbuf[...] = jnp.zeros_like(zbuf)
        for e in range(N_EXPERTS):
            end = se_ref[e]
            start_e = se_ref[e - 1] if e else 0

            for first, live in ((end - tr, end > start_e),
                                (se_ref[N_EXPERTS - 1] + e * tr,
                                 (se_ref[N_EXPERTS - 1] + e * tr) * ROW_TILE < xs_ref.shape[0])):
                @pl.when(live)
                def _():
                    rows = pl.ds(pl.multiple_of(first * ROW_TILE, ROW_TILE), tr * ROW_TILE)
                    fill = pltpu.make_async_copy(zbuf, xs_ref.at[rows], sem.at[2])
                    fill.start()
                    fill.wait()

    slot = i % 2
    _to_row_tiles(xr.at[slot], x_ref[...])
    base = i * tm

    def start(r, c):
        src = _row_tile(xr.at[slot], r)
        pltpu.make_async_copy(src, _row_tile(xs_ref, d1_ref[base + r]), sem.at[slot]).start()
        pltpu.make_async_copy(src, _row_tile(xs_ref, d2_ref[base + r]), sem.at[slot]).start(priority=1)
        return c

    def wait_step(s):
        def wait(r, c):
            for _ in range(2):
                pltpu.make_async_copy(_row_tile(xr.at[s], 0), _row_tile(xs_ref, 0), sem.at[s]).wait()
            return c

        lax.fori_loop(0, tm, wait, 0, unroll=8)

    lax.fori_loop(0, tm, start, 0, unroll=8)

    @pl.when(i > 0)
    def _():
        wait_step(1 - slot)

    @pl.when(i == pl.num_programs(0) - 1)
    def _():
        wait_step(slot)


def _expert_kernel(te_ref, blk_ref, nu_ref, xs_ref, w1_ref, w3_ref, w2_ref, ys_ref, *, tr):
    del te_ref, blk_ref
    used = pl.program_id(0) < nu_ref[0]

    @pl.when(used)
    def _():
        xb = _from_row_tiles(xs_ref, tr).astype(BF16)
        _to_row_tiles(ys_ref, _swiglu_tile(xb, w1_ref[0], w3_ref[0], w2_ref[0]))

    @pl.when(jnp.logical_not(used))
    def _():
        ys_ref[...] = jnp.zeros_like(ys_ref)


def _combine_kernel(d1_ref, d2_ref, x_ref, route_ref, g_ref, b_ref, ys_ref, o_ref, ybuf, sem, *, tm):
    i = pl.program_id(0)
    n_steps = pl.num_programs(0)

    def issue(tile, slot):
        base = tile * tm

        def start(r, c):
            pltpu.make_async_copy(_row_tile(ys_ref, d1_ref[base + r]), _row_tile(ybuf.at[slot, 0], r),
                                  sem.at[slot]).start()
            pltpu.make_async_copy(_row_tile(ys_ref, d2_ref[base + r]), _row_tile(ybuf.at[slot, 1], r),
                                  sem.at[slot]).start(priority=1)
            return c

        lax.fori_loop(0, tm, start, 0, unroll=8)

    @pl.when(i == 0)
    def _():
        issue(0, 0)

    @pl.when(i + 1 < n_steps)
    def _():
        issue(i + 1, (i + 1) % 2)

    slot = i % 2

    def wait(r, c):
        for k in range(2):
            pltpu.make_async_copy(_row_tile(ys_ref, 0), _row_tile(ybuf.at[slot, k], 0), sem.at[slot]).wait()
        return c

    lax.fori_loop(0, tm, wait, 0, unroll=8)
    route = route_ref[...]
    gates = jnp.concatenate([route, jnp.zeros((128 - route.shape[0], tm), F32)], axis=0).T
    ff = (gates[:, 4:5] * _from_row_tiles(ybuf.at[slot, 0], tm)
          + gates[:, 5:6] * _from_row_tiles(ybuf.at[slot, 1], tm))
    o_ref[...] = _layer_norm(ALPHA * x_ref[...] + ff, g_ref[...], b_ref[...])


def _moe_ffn(x2d, route, counts, w1, w3, w2, g, b):
    n, d = x2d.shape
    n_exp, _, f = w1.shape
    tr = min(TR_MOE, n)
    tm = min(TM_MOE, n)
    n_pad = 2 * n + n_exp * tr
    n_tiles = n_pad // tr
    i32 = jnp.int32

    cnt = counts[:, 0].astype(i32)
    seg = (cnt + tr - 1) // tr * tr
    seg_end = jnp.cumsum(seg)
    seg_start = seg_end - seg
    e1, e2 = route[0].astype(i32), route[1].astype(i32)
    dest1 = seg_start[e1] + route[2].astype(i32)
    dest2 = seg_start[e2] + route[3].astype(i32)
    n_used = jnp.maximum(seg_end[-1] // tr, 1)
    tile = jnp.minimum(jnp.arange(n_tiles, dtype=i32), n_used - 1)
    tile_expert = jnp.minimum(jnp.sum(tile[:, None] * tr >= seg_end[None, :], axis=1), n_exp - 1).astype(i32)

    cparams = pltpu.CompilerParams(dimension_semantics=("arbitrary",), vmem_limit_bytes=VMEM_LIMIT)
    assert d == ROW_TILE * 128
    xs = pl.pallas_call(
        functools.partial(_dispatch_kernel, tm=tm, tr=tr),
        grid_spec=pltpu.PrefetchScalarGridSpec(
            num_scalar_prefetch=3, grid=(n // tm,),
            in_specs=[pl.BlockSpec((tm, d), lambda i, *_: (i, 0))],
            out_specs=pl.BlockSpec(memory_space=pl.ANY),
            scratch_shapes=[pltpu.VMEM((2, tm * ROW_TILE, 128), F32), pltpu.VMEM((tr * ROW_TILE, 128), F32),
                            pltpu.SemaphoreType.DMA((3,))]),
        out_shape=jax.ShapeDtypeStruct((n_pad * ROW_TILE, 128), F32),
        compiler_params=cparams,
        name="moe_dispatch",
    )(dest1, dest2, seg_end.astype(i32), x2d)

    ys = pl.pallas_call(
        functools.partial(_expert_kernel, tr=tr),
        grid_spec=pltpu.PrefetchScalarGridSpec(
            num_scalar_prefetch=3, grid=(n_tiles,),
            in_specs=[pl.BlockSpec((tr * ROW_TILE, 128), lambda i, te, blk, nu: (blk[i], 0)),
                      pl.BlockSpec((1, d, f), lambda i, te, blk, nu: (te[i], 0, 0)),
                      pl.BlockSpec((1, d, f), lambda i, te, blk, nu: (te[i], 0, 0)),
                      pl.BlockSpec((1, f, d), lambda i, te, blk, nu: (te[i], 0, 0))],
            out_specs=pl.BlockSpec((tr * ROW_TILE, 128), lambda i, te, blk, nu: (i, 0))),
        out_shape=jax.ShapeDtypeStruct((n_pad * ROW_TILE, 128), F32),
        compiler_params=cparams,
        name="moe_experts",
    )(tile_expert, tile, n_used.reshape(1), xs, w1.astype(BF16), w3.astype(BF16), w2.astype(BF16))

    row = lambda a: a.reshape(1, -1)
    return pl.pallas_call(
        functools.partial(_combine_kernel, tm=tm),
        grid_spec=pltpu.PrefetchScalarGridSpec(
            num_scalar_prefetch=2, grid=(n // tm,),
            in_specs=[pl.BlockSpec((tm, d), lambda i, *_: (i, 0)),
                      pl.BlockSpec((N_EXPERTS, tm), lambda i, *_: (0, i)),
                      pl.BlockSpec((1, d), lambda i, *_: (0, 0)),
                      pl.BlockSpec((1, d), lambda i, *_: (0, 0)),
                      pl.BlockSpec(memory_space=pl.ANY)],
            out_specs=pl.BlockSpec((tm, d), lambda i, *_: (i, 0)),
            scratch_shapes=[pltpu.VMEM((2, 2, tm * ROW_TILE, 128), F32), pltpu.SemaphoreType.DMA((2,))]),
        out_shape=jax.ShapeDtypeStruct((n, d), F32),
        compiler_params=cparams,
        name="moe_combine",
    )(dest1, dest2, x2d, route, row(g), row(b), ys)


def kernel(x, positions, w_in, mla_q_norm_g, w_uq, mla_kv_norm_g, w_ukv, fox_forget_b, conv_w, conv_b,
           conv_norm_g, conv_norm_b, mla_out_norm_g, fox_out_norm_g, w_out, ln1_g, ln1_b, dense_w1,
           dense_w3, dense_w2, router_w, expert_w1, expert_w3, expert_w2, ln2_g, ln2_b):
    batch, seq, d = x.shape
    assert d == D_MODEL and seq % TQ == 0 and seq % min(TM_IN, seq) == 0
    depth = w_in.shape[0]
    tabs = _rope_tables(positions)
    h = x.reshape(batch * seq, d)
    for layer in range(depth):
        pw = _prep_inproj_weights(w_in[layer], w_uq[layer], w_ukv[layer], fox_forget_b[layer])
        q_t, k, v_t, hc = _input_projection(
            h, tabs, pw, mla_q_norm_g[layer], mla_kv_norm_g[layer], conv_w[layer], conv_b[layer],
            conv_norm_g[layer], conv_norm_b[layer], seq)
        o = _attention(q_t, k, v_t, batch, seq)
        j = layer // 2
        if layer % 2 == 0:
            h = _output_projection(o, hc, h, mla_out_norm_g[layer], fox_out_norm_g[layer], w_out[layer],
                                   ln1_g[layer], ln1_b[layer])[0]
            h = _dense_ffn(h, dense_w1[j], dense_w3[j], dense_w2[j], ln2_g[layer], ln2_b[layer])
        else:
            h, route, counts = _output_projection(o, hc, h, mla_out_norm_g[layer], fox_out_norm_g[layer],
                                                  w_out[layer], ln1_g[layer], ln1_b[layer], router_w[j])
            h = _moe_ffn(h, route, counts, expert_w1[j], expert_w3[j], expert_w2[j], ln2_g[layer],
                         ln2_b[layer])
    return h.reshape(batch, seq, d)
```

```python
import functools
import math

import numpy as np
import jax
import jax.numpy as jnp
from jax import lax
from jax.experimental import pallas as pl
from jax.experimental.pallas import tpu as pltpu

F32 = jnp.float32
BF16 = jnp.bfloat16

D_MODEL = 1024
DEPTH = 4
MLA_HEADS = 8
MLA_NOPE = 64
MLA_ROPE = 32
MLA_V = 64
MLA_Q_RANK = 256
MLA_KV_RANK = 128
ROPE_THETA = 10000.0
FOX_HEADS = 4
FOX_DIM = 64
CONV_CH = 256
CONV_GROUPS = 4
CONV_WIDTH = 31
MLA_WIDTH = MLA_HEADS * MLA_V
FOX_WIDTH = FOX_HEADS * FOX_DIM
N_EXPERTS = 8
ALPHA = (2.0 * DEPTH) ** 0.25
NORM_EPS = 1e-5
LOG2E = math.log2(math.e)

HEAD_PAD = 128
N_HEADS = MLA_HEADS + FOX_HEADS
V_DIM = 64
V_ROWS = 80
CONV_HALO = 32
VMEM_LIMIT = 56 * 1024 * 1024

TQ = 512
TK = 256
HPS = 2
TM_IN = 512
TM_OUT = 512
TM_FFN = 1024
TR_MOE = 512
TM_MOE = 512
ROW_TILE = 8


def _nt_dot(a, b):
    return lax.dot_general(a, b, (((1,), (1,)), ((), ())), preferred_element_type=F32)


def _dot(a, b):
    return jnp.dot(a, b, preferred_element_type=F32)


def _split2_dot(a, m_bf16):
    hi = a.astype(BF16)
    lo = (a - hi.astype(F32)).astype(BF16)
    return _dot(hi, m_bf16) + _dot(lo, m_bf16)


def _split3(a):
    hi = a.astype(BF16).astype(F32)
    r1 = a - hi
    mid = r1.astype(BF16).astype(F32)
    lo = (r1 - mid).astype(BF16).astype(F32)
    return hi, mid, lo


def _const_spec(shape):
    nd = len(shape)
    return pl.BlockSpec(shape, lambda *_: (0,) * nd, pipeline_mode=pl.Buffered(1))


def _rope_kernel(pos_ref, invf_ref, c_ref, s_ref, ct_ref, st_ref):
    pos = pos_ref[...].astype(F32)
    ang = invf_ref[...] * pos
    cos = jnp.cos(ang)
    sin = jnp.sin(ang)
    tn = pos.shape[1]
    ct = jnp.concatenate([jnp.ones((MLA_NOPE, tn), F32), cos, cos, jnp.zeros((32, tn), F32)], axis=0)
    st = jnp.concatenate([jnp.zeros((MLA_NOPE, tn), F32), sin, sin, jnp.zeros((32, tn), F32)], axis=0)
    ct_ref[...] = ct
    st_ref[...] = st
    c_ref[...] = ct.T
    s_ref[...] = st.T


def _rope_tables(positions):
    n = positions.size
    tn = min(512, n)
    inv_freq = ROPE_THETA ** (-jnp.arange(0, MLA_ROPE, 2, dtype=F32) / MLA_ROPE)
    return pl.pallas_call(
        _rope_kernel,
        grid=(n // tn,),
        in_specs=[pl.BlockSpec((1, tn), lambda i: (0, i)),
                  pl.BlockSpec((MLA_ROPE // 2, 1), lambda i: (0, 0))],
        out_specs=[pl.BlockSpec((tn, HEAD_PAD), lambda i: (i, 0)),
                   pl.BlockSpec((tn, HEAD_PAD), lambda i: (i, 0)),
                   pl.BlockSpec((HEAD_PAD, tn), lambda i: (0, i)),
                   pl.BlockSpec((HEAD_PAD, tn), lambda i: (0, i))],
        out_shape=[jax.ShapeDtypeStruct((n, HEAD_PAD), F32),
                   jax.ShapeDtypeStruct((n, HEAD_PAD), F32),
                   jax.ShapeDtypeStruct((HEAD_PAD, n), F32),
                   jax.ShapeDtypeStruct((HEAD_PAD, n), F32)],
        name="rope_tables",
    )(positions.reshape(1, n), inv_freq.reshape(-1, 1))


_A_CQ = 0
_A_CKV = _A_CQ + MLA_Q_RANK
_A_KR = _A_CKV + MLA_KV_RANK
_A_KRR = _A_KR + HEAD_PAD
_A_FK = _A_KRR + HEAD_PAD
_A_CA = _A_FK + FOX_HEADS * HEAD_PAD
_A_CG = _A_CA + CONV_CH
_A_COLS = _A_CG + CONV_CH
_AUG_ROWS = 8
_F_ROWS = 16


def _rms(x, g):
    ms = jnp.mean(jnp.square(x), axis=-1, keepdims=True)
    return x * lax.rsqrt(ms + NORM_EPS) * g


def _inproj_kernel(x_ref, c_ref, s_ref, ct_ref, st_ref, wa_ref, wfq_ref, wfv_ref, wf_ref, fb_ref,
                   gq_ref, wuq_ref, wuqr_ref, gkv_ref, wuk_ref, wuv_ref,
                   cw_ref, cb_ref, cng_ref, cnb_ref, gmat_ref,
                   qt_ref, k_ref, vt_ref, hc_ref,
                   hbuf, hsh, cbuf, fcarry, *, tiles_per_seq, tm):
    i = pl.program_id(0)

    @pl.when(i % tiles_per_seq == 0)
    def _():
        hbuf[0:CONV_HALO, :] = jnp.zeros((CONV_HALO, CONV_CH), F32)
        fcarry[...] = jnp.zeros_like(fcarry)

    xb = x_ref[...].astype(BF16)
    p1 = _dot(xb, wa_ref[...])
    cos_t = c_ref[...]
    sin_t = s_ref[...]
    cos_tt = ct_ref[...]
    sin_tt = st_ref[...]

    cqn = _rms(p1[:, _A_CQ:_A_CQ + MLA_Q_RANK], gq_ref[...]).astype(BF16)
    q_t = _nt_dot(wuq_ref[...], cqn)
    q_rot_t = _nt_dot(wuqr_ref[...], cqn)
    mla_scale = (MLA_NOPE + MLA_ROPE) ** -0.5 * LOG2E
    for h in range(MLA_HEADS):
        rows = slice(h * HEAD_PAD, (h + 1) * HEAD_PAD)
        qh = (q_t[rows, :] * cos_tt + q_rot_t[rows, :] * sin_tt) * mla_scale
        for c in range(tm // TQ):
            qt_ref[c, rows, :] = qh[:, c * TQ:(c + 1) * TQ].astype(BF16)

    ckvn = _rms(p1[:, _A_CKV:_A_CKV + MLA_KV_RANK], gkv_ref[...]).astype(BF16)
    k_nope = _dot(ckvn, wuk_ref[...])
    k_rope = p1[:, _A_KR:_A_KR + HEAD_PAD] * cos_t + p1[:, _A_KRR:_A_KRR + HEAD_PAD] * sin_t
    for h in range(MLA_HEADS):
        cols = slice(h * HEAD_PAD, (h + 1) * HEAD_PAD)
        k_ref[:, cols] = (k_nope[:, cols] + k_rope).astype(BF16)
    v_t = _nt_dot(wuv_ref[...], ckvn)
    fv_t = _nt_dot(wfv_ref[...], xb)
    ones_blk = jnp.where(lax.broadcasted_iota(jnp.int32, (V_ROWS - V_DIM, tm), 0) == 0, 1.0, 0.0)
    for h in range(N_HEADS):
        src = v_t if h < MLA_HEADS else fv_t
        r0 = (h if h < MLA_HEADS else h - MLA_HEADS) * V_DIM
        vh = jnp.concatenate([src[r0:r0 + V_DIM, :], ones_blk], axis=0).astype(BF16)
        for c in range(tm // TK):
            vt_ref[c, h * V_ROWS:(h + 1) * V_ROWS, :] = vh[:, c * TK:(c + 1) * TK]

    z = _nt_dot(wf_ref[...], xb) + fb_ref[...]
    logf = (jnp.minimum(z, 0.0) - jnp.log1p(jnp.exp(-jnp.abs(z)))) * LOG2E
    r_i = lax.broadcasted_iota(jnp.int32, (tm, tm), 0)
    c_i = lax.broadcasted_iota(jnp.int32, (tm, tm), 1)
    upper = jnp.where(r_i <= c_i, 1.0, 0.0).astype(BF16)
    l_hi, l_mid, l_lo = _split3(logf)
    cum = (_dot(l_hi.astype(BF16), upper) + _dot(l_mid.astype(BF16), upper)
           + _dot(l_lo.astype(BF16), upper))
    f_cum = cum + fcarry[:, 0:1]
    fcarry[...] = jnp.broadcast_to(f_cum[:, tm - 1:tm], fcarry.shape)
    f_hi, f_mid, f_lo = _split3(f_cum)

    fq_t = _nt_dot(wfq_ref[...], xb)
    row8 = lax.broadcasted_iota(jnp.int32, (_AUG_ROWS, tm), 0)
    fox_scale = FOX_DIM ** -0.5 * LOG2E
    for h in range(FOX_HEADS):
        bh = lambda a: jnp.broadcast_to(a[h:h + 1, :], (_AUG_ROWS, tm))
        aug_q = jnp.where(row8 == 0, bh(f_hi), jnp.where(row8 == 1, bh(f_mid), jnp.where(
            row8 == 2, bh(f_lo), jnp.where(row8 < 6, 1.0, 0.0))))
        aug_k = jnp.where(row8 < 3, 1.0, jnp.where(row8 == 3, -bh(f_hi), jnp.where(
            row8 == 4, -bh(f_mid), jnp.where(row8 == 5, -bh(f_lo), 0.0))))
        pad = jnp.zeros((HEAD_PAD - FOX_DIM - _AUG_ROWS, tm), F32)
        qh = jnp.concatenate(
            [fq_t[h * HEAD_PAD:h * HEAD_PAD + FOX_DIM, :] * fox_scale, aug_q, pad], axis=0)
        rows = slice((MLA_HEADS + h) * HEAD_PAD, (MLA_HEADS + h + 1) * HEAD_PAD)
        for c in range(tm // TQ):
            qt_ref[c, rows, :] = qh[:, c * TQ:(c + 1) * TQ].astype(BF16)
        kaug_t = jnp.concatenate([jnp.zeros((FOX_DIM, tm), F32), aug_k, pad], axis=0)
        fk = p1[:, _A_FK + h * HEAD_PAD:_A_FK + (h + 1) * HEAD_PAD]
        k_ref[:, rows] = (fk + kaug_t.T).astype(BF16)

    a = p1[:, _A_CA:_A_CA + CONV_CH]
    g = p1[:, _A_CG:_A_CG + CONV_CH]
    hbuf[CONV_HALO:CONV_HALO + tm, :] = a * jax.nn.sigmoid(g)
    chunk = 64
    first = CONV_HALO - (CONV_WIDTH - 1)
    for r in range(1, 8):
        hsh[r - 1] = hbuf[r:r + tm + CONV_HALO - 8, :]
    for c0 in range(0, tm, chunk):
        acc = jnp.zeros((chunk, CONV_CH), F32)
        for o in range(first, first + CONV_WIDTH):
            r = o % 8
            row = c0 + o - r
            seg = hbuf[row:row + chunk, :] if r == 0 else hsh[r - 1, row:row + chunk, :]
            acc = acc + cw_ref[o - first:o - first + 1, :] * seg
        cbuf[c0:c0 + chunk, :] = acc
    hbuf[0:CONV_HALO, :] = hbuf[tm:tm + CONV_HALO, :]
    hv = cbuf[...] + cb_ref[...]
    gm = gmat_ref[...]
    mu = _split2_dot(hv, gm)
    d = hv - mu
    var = _split2_dot(d * d, gm)
    hn = d * lax.rsqrt(var + NORM_EPS) * cng_ref[...] + cnb_ref[...]
    hc_ref[...] = (hn * jax.nn.sigmoid(hn)).astype(BF16)


def _prep_inproj_weights(w_in, w_uq, w_ukv, fox_forget_b):
    o = np.cumsum((0, MLA_Q_RANK, MLA_KV_RANK, MLA_ROPE, FOX_WIDTH, FOX_WIDTH, FOX_WIDTH, FOX_HEADS,
                   2 * CONV_CH))
    w_cq, w_ckv, w_kr, w_fq, w_fk, w_fv, w_f, w_cv = (w_in[:, o[i]:o[i + 1]] for i in range(8))
    d = w_in.shape[0]
    half = MLA_ROPE // 2

    def rot_cols(w):
        return jnp.concatenate([-w[..., half:], w[..., :half]], axis=-1)

    def rope_block(w):
        return jnp.pad(w, ((0, 0), (MLA_NOPE, HEAD_PAD - MLA_NOPE - MLA_ROPE)))

    w_fk_pad = jnp.pad(w_fk.reshape(d, FOX_HEADS, FOX_DIM), ((0, 0), (0, 0), (0, HEAD_PAD - FOX_DIM)))
    wa = jnp.concatenate([w_cq, w_ckv, rope_block(w_kr), rope_block(rot_cols(w_kr)),
                          w_fk_pad.reshape(d, FOX_HEADS * HEAD_PAD), w_cv], axis=1)
    w_fq_pad = jnp.pad(w_fq.reshape(d, FOX_HEADS, FOX_DIM), ((0, 0), (0, 0), (0, HEAD_PAD - FOX_DIM)))
    wfq_t = w_fq_pad.reshape(d, FOX_HEADS * HEAD_PAD).T
    wfv_t = w_fv.T
    wf_t = jnp.pad(w_f, ((0, 0), (0, _F_ROWS - FOX_HEADS))).T
    fb = jnp.pad(fox_forget_b, (0, _F_ROWS - FOX_HEADS)).reshape(_F_ROWS, 1)

    uq = w_uq.reshape(MLA_Q_RANK, MLA_HEADS, MLA_NOPE + MLA_ROPE)
    uq_nope, uq_rope = uq[..., :MLA_NOPE], uq[..., MLA_NOPE:]
    tail = ((0, 0), (0, 0), (0, HEAD_PAD - MLA_NOPE - MLA_ROPE))
    uq_pad = jnp.pad(jnp.concatenate([uq_nope, uq_rope], axis=-1), tail)
    uq_rot_pad = jnp.pad(jnp.concatenate([jnp.zeros_like(uq_nope), rot_cols(uq_rope)], axis=-1), tail)
    wuq_t = uq_pad.reshape(MLA_Q_RANK, MLA_HEADS * HEAD_PAD).T
    wuqr_t = uq_rot_pad.reshape(MLA_Q_RANK, MLA_HEADS * HEAD_PAD).T
    ukv = w_ukv.reshape(MLA_KV_RANK, MLA_HEADS, MLA_NOPE + MLA_V)
    wuk = jnp.pad(ukv[..., :MLA_NOPE], ((0, 0), (0, 0), (0, HEAD_PAD - MLA_NOPE))).reshape(
        MLA_KV_RANK, MLA_HEADS * HEAD_PAD)
    wuv_t = ukv[..., MLA_NOPE:].reshape(MLA_KV_RANK, MLA_WIDTH).T
    bf = lambda a: a.astype(BF16)
    return dict(wa=bf(wa), wfq=bf(wfq_t), wfv=bf(wfv_t), wf=bf(wf_t), fb=fb, wuq=bf(wuq_t),
                wuqr=bf(wuqr_t), wuk=bf(wuk), wuv=bf(wuv_t))


def _input_projection(x2d, tabs, pw, gq, gkv, conv_w, conv_b, conv_ng, conv_nb, seq):
    n, d = x2d.shape
    tm = min(TM_IN, seq)
    cos_t, sin_t, cos_tt, sin_tt = tabs
    gidx = np.arange(CONV_CH) // (CONV_CH // CONV_GROUPS)
    gmat = jnp.asarray((gidx[:, None] == gidx[None, :]) / (CONV_CH // CONV_GROUPS), BF16)
    cw = jnp.pad(conv_w, ((0, 32 - CONV_WIDTH), (0, 0)))
    row = lambda a: a.reshape(1, -1)
    tok = lambda w: pl.BlockSpec((tm, w), lambda i: (i, 0))
    tok_t = lambda r: pl.BlockSpec((r, tm), lambda i: (0, i))
    consts = [pw["wa"], pw["wfq"], pw["wfv"], pw["wf"], pw["fb"], row(gq), pw["wuq"], pw["wuqr"],
              row(gkv), pw["wuk"], pw["wuv"], cw, row(conv_b), row(conv_ng), row(conv_nb), gmat]
    kern = functools.partial(_inproj_kernel, tiles_per_seq=seq // tm, tm=tm)
    return pl.pallas_call(
        kern,
        grid=(n // tm,),
        in_specs=[tok(d), tok(HEAD_PAD), tok(HEAD_PAD), tok_t(HEAD_PAD), tok_t(HEAD_PAD)]
        + [_const_spec(c.shape) for c in consts],
        out_specs=[pl.BlockSpec((tm // TQ, N_HEADS * HEAD_PAD, TQ), lambda i: (i, 0, 0)),
                   tok(N_HEADS * HEAD_PAD),
                   pl.BlockSpec((tm // TK, N_HEADS * V_ROWS, TK), lambda i: (i, 0, 0)),
                   tok(CONV_CH)],
        out_shape=[jax.ShapeDtypeStruct((n // TQ, N_HEADS * HEAD_PAD, TQ), BF16),
                   jax.ShapeDtypeStruct((n, N_HEADS * HEAD_PAD), BF16),
                   jax.ShapeDtypeStruct((n // TK, N_HEADS * V_ROWS, TK), BF16),
                   jax.ShapeDtypeStruct((n, CONV_CH), BF16)],
        scratch_shapes=[pltpu.VMEM((CONV_HALO + tm, CONV_CH), F32),
                        pltpu.VMEM((7, tm + CONV_HALO - 8, CONV_CH), F32),
                        pltpu.VMEM((tm, CONV_CH), F32),
                        pltpu.VMEM((_F_ROWS, 128), F32)],
        compiler_params=pltpu.CompilerParams(dimension_semantics=("arbitrary",),
                                             vmem_limit_bytes=VMEM_LIMIT),
        name="input_projection",
    )(x2d, cos_t, sin_t, cos_tt, sin_tt, *consts)


def _attn_kernel(qt_ref, k_ref, vt_ref, o_ref, *scratch, n_tiles):
    assert TQ == 2 * TK
    s_ref = (scratch[0:HPS], scratch[HPS:2 * HPS])
    sd_ref = scratch[2 * HPS:3 * HPS]
    p_ref = (scratch[3 * HPS:4 * HPS], scratch[4 * HPS:5 * HPS])
    pd_ref = scratch[5 * HPS:6 * HPS]
    acc_bufs = (scratch[6 * HPS:7 * HPS], scratch[7 * HPS:8 * HPS])
    diff_ref = scratch[8 * HPS]
    diff_ref[...] = (lax.broadcasted_iota(jnp.int32, (TK, TQ), 1)
                     - lax.broadcasted_iota(jnp.int32, (TK, TQ), 0))
    for h in range(HPS):
        p_ref[1][h][...] = jnp.zeros_like(p_ref[1][h])
        for par in range(2):
            acc_bufs[par][h][...] = jnp.ones_like(acc_bufs[par][h])

    def tile_scores(tile, j, slot):
        row0 = pl.multiple_of(j * TK, TK)
        block_max = []
        for h in range(HPS):
            s = _dot(k_ref[pl.ds(row0, TK), h * HEAD_PAD:(h + 1) * HEAD_PAD],
                     qt_ref[tile, h * HEAD_PAD:(h + 1) * HEAD_PAD, :])
            s_ref[slot][h][...] = s
            block_max.append(jnp.max(s, axis=0, keepdims=True))
        return block_max

    def last_diag_scores(tile):
        row0 = pl.multiple_of((2 * tile + 1) * TK, TK)
        for h in range(HPS):
            sd_ref[h][...] = _dot(k_ref[pl.ds(row0, TK), h * HEAD_PAD:(h + 1) * HEAD_PAD],
                                  qt_ref[tile, h * HEAD_PAD:(h + 1) * HEAD_PAD, TK:])

    def finalize(tile, par):
        out_t = jnp.concatenate([acc_bufs[par][h][0:V_DIM, :] / acc_bufs[par][h][V_DIM:V_DIM + 1, :]
                                 for h in range(HPS)], axis=0)
        o_ref[pl.ds(pl.multiple_of(tile * TQ, TQ), TQ), :] = out_t.T.astype(o_ref.dtype)

    def q_tile(i, par, bm0):
        acc_ref = acc_bufs[par]
        scores = functools.partial(tile_scores, i)

        def softmax(slot, m, block_max):
            m_new = [jnp.maximum(m[h], block_max[h]) for h in range(HPS)]
            for h in range(HPS):
                p_ref[slot][h][...] = jnp.exp2(s_ref[slot][h][...] - m_new[h]).astype(BF16)
            return m_new, [jnp.exp2(m[h] - m_new[h]) for h in range(HPS)]

        def values(j, slot, alpha, gate=None):
            for h in range(HPS):
                pv = _dot(vt_ref[j, h * V_ROWS:(h + 1) * V_ROWS, :], p_ref[slot][h][...])
                acc_ref[h][...] = alpha[h] * acc_ref[h][...] + (pv if gate is None else gate * pv)

        m0 = [jnp.full((1, TQ), -1e30, F32)] * HPS
        a0 = [jnp.zeros((1, TQ), F32)] * HPS

        def pair(u, state):
            m, alpha, bm_t = list(state[0:HPS]), list(state[HPS:2 * HPS]), list(state[2 * HPS:3 * HPS])
            t = 2 * u
            m, alpha_t = softmax(0, m, bm_t)
            values(jnp.maximum(t - 1, 0), 1, alpha, jnp.where(t > 0, 1.0, 0.0))
            bm_t1 = scores(t + 1, 1)
            m, alpha_t1 = softmax(1, m, bm_t1)
            values(t, 0, alpha_t)
            bm_t2 = scores(t + 2, 0)
            return (*m, *alpha_t1, *bm_t2)

        state = lax.fori_loop(0, i // 2, lambda v, st: pair(2 * v + 1, pair(2 * v, st)), (*m0, *a0, *bm0))
        state = lax.fori_loop(i - i % 2, i, pair, state)
        m, alpha = list(state[0:HPS]), list(state[HPS:2 * HPS])
        d0 = 2 * i
        values(jnp.maximum(d0 - 1, 0), 1, alpha, jnp.where(i > 0, 1.0, 0.0))
        finalize(jnp.where(i > 0, i - 1, n_tiles - 1), 1 - par)
        nxt = jnp.minimum(i + 1, n_tiles - 1)

        m_d0, alpha_d0 = [], []
        for h in range(HPS):
            s = s_ref[0][h][...]
            s = jnp.concatenate([jnp.where(diff_ref[:, 0:TK] >= 0, s[:, 0:TK], -jnp.inf), s[:, TK:]], axis=1)
            m_d0.append(jnp.maximum(m[h], jnp.max(s, axis=0, keepdims=True)))
            p_ref[0][h][...] = jnp.exp2(s - m_d0[h]).astype(BF16)
            alpha_d0.append(jnp.exp2(m[h] - m_d0[h]))
        bm_next = tile_scores(nxt, 0, 0)

        alpha_d1 = []
        for h in range(HPS):
            s = jnp.where(diff_ref[:, 0:TK] >= 0, sd_ref[h][...], -jnp.inf)
            m_old = m_d0[h][:, TK:]
            m_new = jnp.maximum(m_old, jnp.max(s, axis=0, keepdims=True))
            pd_ref[h][...] = jnp.exp2(s - m_new).astype(BF16)
            alpha_d1.append(jnp.exp2(m_old - m_new))
        last_diag_scores(nxt)

        values(d0, 0, alpha_d0)
        for h in range(HPS):
            acc_ref[h][:, TK:] = alpha_d1[h] * acc_ref[h][:, TK:] + _dot(
                vt_ref[d0 + 1, h * V_ROWS:(h + 1) * V_ROWS, :], pd_ref[h][...])
        return tuple(bm_next)

    last_diag_scores(0)
    bm = lax.fori_loop(0, n_tiles // 2, lambda a, st: q_tile(2 * a + 1, 1, q_tile(2 * a, 0, st)),
                       tuple(tile_scores(0, 0, 0)))
    if n_tiles % 2:
        q_tile(n_tiles - 1, 0, bm)
    finalize(n_tiles - 1, (n_tiles - 1) % 2)


def _attention(q_t, k, v_t, batch, seq):
    n = k.shape[0]
    groups = N_HEADS // HPS
    return pl.pallas_call(
        functools.partial(_attn_kernel, n_tiles=seq // TQ),
        grid=(batch, groups),
        in_specs=[pl.BlockSpec((seq // TQ, HPS * HEAD_PAD, TQ), lambda b, p: (b, p, 0)),
                  pl.BlockSpec((seq, HPS * HEAD_PAD), lambda b, p: (b, p)),
                  pl.BlockSpec((seq // TK, HPS * V_ROWS, TK), lambda b, p: (b, p, 0))],
        out_specs=pl.BlockSpec((seq, HPS * V_DIM), lambda b, p: (b, p)),
        out_shape=jax.ShapeDtypeStruct((n, N_HEADS * V_DIM), BF16),
        scratch_shapes=[pltpu.VMEM((TK, TQ), F32)] * (2 * HPS) + [pltpu.VMEM((TK, TK), F32)] * HPS
        + [pltpu.VMEM((TK, TQ), BF16)] * (2 * HPS) + [pltpu.VMEM((TK, TK), BF16)] * HPS
        + [pltpu.VMEM((V_ROWS, TQ), F32)] * (2 * HPS) + [pltpu.VMEM((TK, TQ), jnp.int32)],
        compiler_params=pltpu.CompilerParams(dimension_semantics=("arbitrary", "arbitrary"),
                                             vmem_limit_bytes=VMEM_LIMIT),
        name="attention",
    )(q_t, k, v_t)


def _layer_norm(x, g, b):
    mu = jnp.mean(x, axis=-1, keepdims=True)
    d = x - mu
    var = jnp.mean(jnp.square(d), axis=-1, keepdims=True)
    return d * lax.rsqrt(var + NORM_EPS) * g + b


def _outproj_kernel(o_ref, hc_ref, x_ref, gm_ref, gf_ref, wo_ref, g1_ref, b1_ref, *rest, with_router):
    if with_router:
        rw_ref, x1_ref, route_ref, counts_ref, cnt_ref, upper_ref = rest
    else:
        (x1_ref,) = rest
    o = o_ref[...].astype(F32)
    mla = _rms(o[:, :MLA_WIDTH], gm_ref[...])
    fox = _rms(o[:, MLA_WIDTH:], gf_ref[...])
    mixed = jnp.concatenate([mla.astype(BF16), fox.astype(BF16), hc_ref[...]], axis=-1)
    y = _dot(mixed, wo_ref[...])
    x1 = _layer_norm(ALPHA * x_ref[...] + y, g1_ref[...], b1_ref[...])
    x1_ref[...] = x1
    if with_router:
        rw = rw_ref[...]
        x_hi = x1.astype(BF16)
        x_lo = (x1 - x_hi.astype(F32)).astype(BF16)
        w_hi = rw.astype(BF16)
        w_lo = (rw - w_hi.astype(F32)).astype(BF16)
        both = _dot(x_hi, jnp.concatenate([w_hi, w_lo], axis=1))
        logits = both[:, :128] + (_dot(x_lo, w_hi) + both[:, 128:])
        tm = logits.shape[0]
        lg = logits.T[0:N_EXPERTS, :]
        row = lax.broadcasted_iota(jnp.int32, lg.shape, 0)
        v1 = jnp.max(lg, axis=0, keepdims=True)
        i1 = jnp.min(jnp.where(lg == v1, row, N_EXPERTS), axis=0, keepdims=True)
        rest_l = jnp.where(row == i1, -jnp.inf, lg)
        v2 = jnp.max(rest_l, axis=0, keepdims=True)
        i2 = jnp.min(jnp.where(rest_l == v2, row, N_EXPERTS), axis=0, keepdims=True)
        e2 = jnp.exp(v2 - v1)
        den = 1.0 + e2

        @pl.when(pl.program_id(0) == 0)
        def _():
            cnt_ref[...] = jnp.zeros_like(cnt_ref)
            r_i = lax.broadcasted_iota(jnp.int32, (tm, tm), 0)
            c_i = lax.broadcasted_iota(jnp.int32, (tm, tm), 1)
            upper_ref[...] = jnp.where(r_i < c_i, 1.0, 0.0).astype(BF16)

        sel = jnp.where(row == i1, 1.0, jnp.where(row == i2, 1.0, 0.0))
        sel16 = jnp.concatenate([sel, jnp.zeros_like(sel)], axis=0).astype(BF16)
        before = cnt_ref[:, 0:1]
        rank = _dot(sel16, upper_ref[...])[0:N_EXPERTS, :] + before
        total = before + jnp.sum(sel, axis=1, keepdims=True)
        cnt_ref[...] = jnp.broadcast_to(total, cnt_ref.shape)
        counts_ref[...] = jnp.broadcast_to(total, counts_ref.shape)
        r1 = jnp.sum(jnp.where(row == i1, rank, 0.0), axis=0, keepdims=True)
        r2 = jnp.sum(jnp.where(row == i2, rank, 0.0), axis=0, keepdims=True)
        rows = (i1.astype(F32), i2.astype(F32), r1, r2, 1.0 / den, e2 / den)
        route = jnp.zeros(lg.shape, F32)
        for c, v in enumerate(rows):
            route = jnp.where(row == c, v, route)
        route_ref[...] = route


def _output_projection(o, hc, x2d, gm, gf, w_out, g1, b1, router_w=None):
    n, d = x2d.shape
    tm = min(TM_OUT, n)
    row = lambda a: a.reshape(1, -1)
    tok = lambda w: pl.BlockSpec((tm, w), lambda i: (i, 0))
    consts = [row(gm), row(gf), w_out.astype(BF16), row(g1), row(b1)]
    out_specs = [tok(d)]
    out_shape = [jax.ShapeDtypeStruct((n, d), F32)]
    scratch = []
    if router_w is not None:
        consts.append(jnp.pad(router_w, ((0, 0), (0, 128 - N_EXPERTS))))
        out_specs += [pl.BlockSpec((N_EXPERTS, tm), lambda i: (0, i)),
                      pl.BlockSpec((N_EXPERTS, 128), lambda i: (0, 0))]
        out_shape += [jax.ShapeDtypeStruct((N_EXPERTS, n), F32), jax.ShapeDtypeStruct((N_EXPERTS, 128), F32)]
        scratch = [pltpu.VMEM((N_EXPERTS, 128), F32), pltpu.VMEM((tm, tm), BF16)]
    return pl.pallas_call(
        functools.partial(_outproj_kernel, with_router=router_w is not None),
        grid=(n // tm,),
        in_specs=[tok(o.shape[1]), tok(CONV_CH), tok(d)] + [_const_spec(c.shape) for c in consts],
        out_specs=out_specs,
        out_shape=out_shape,
        scratch_shapes=scratch,
        compiler_params=pltpu.CompilerParams(dimension_semantics=("arbitrary",),
                                             vmem_limit_bytes=VMEM_LIMIT),
        name="output_projection",
    )(o, hc, x2d, *consts)


def _swiglu_tile(xb, w1, w3, w2):
    h1 = _dot(xb, w1)
    h3 = _dot(xb, w3)
    hid = (h1 * jax.nn.sigmoid(h1) * h3).astype(BF16)
    return _dot(hid, w2)


def _dense_ffn_kernel(x_ref, w1_ref, w3_ref, w2_ref, g_ref, b_ref, o_ref, *, f_chunk):
    x = x_ref[...]
    xb = x.astype(BF16)
    ff = None
    for c0 in range(0, w1_ref.shape[1], f_chunk):
        part = _swiglu_tile(xb, w1_ref[:, c0:c0 + f_chunk], w3_ref[:, c0:c0 + f_chunk],
                            w2_ref[c0:c0 + f_chunk, :])
        ff = part if ff is None else ff + part
    o_ref[...] = _layer_norm(ALPHA * x + ff, g_ref[...], b_ref[...])


def _dense_ffn(x2d, w1, w3, w2, g, b):
    n, d = x2d.shape
    tm = min(TM_FFN, n)
    f = w1.shape[1]
    f_chunk = f // 2 if (f // 2) % 128 == 0 else f
    row = lambda a: a.reshape(1, -1)
    consts = [w1.astype(BF16), w3.astype(BF16), w2.astype(BF16), row(g), row(b)]
    return pl.pallas_call(
        functools.partial(_dense_ffn_kernel, f_chunk=f_chunk),
        grid=(n // tm,),
        in_specs=[pl.BlockSpec((tm, d), lambda i: (i, 0))] + [_const_spec(c.shape) for c in consts],
        out_specs=pl.BlockSpec((tm, d), lambda i: (i, 0)),
        out_shape=jax.ShapeDtypeStruct((n, d), F32),
        compiler_params=pltpu.CompilerParams(dimension_semantics=("arbitrary",),
                                             vmem_limit_bytes=VMEM_LIMIT),
        name="dense_ffn",
    )(x2d, *consts)


def _to_row_tiles(ref, x):
    for c in range(ROW_TILE):
        ref[pl.ds(c, x.shape[0], stride=ROW_TILE), :] = x[:, c * 128:(c + 1) * 128]


def _from_row_tiles(ref, t):
    return jnp.concatenate([ref[pl.ds(c, t, stride=ROW_TILE), :] for c in range(ROW_TILE)], axis=-1)


def _row_tile(ref, r):
    return ref.at[pl.ds(pl.multiple_of(r * ROW_TILE, ROW_TILE), ROW_TILE)]


def _dispatch_kernel(d1_ref, d2_ref, se_ref, x_ref, xs_ref, xr, zbuf, sem, *, tm, tr):
    i = pl.program_id(0)

    @pl.when(i == 0)
    def _():
        zbuf[...] = jnp.zeros_like(zbuf)
        for e in range(N_EXPERTS):
            end = se_ref[e]
            start_e = se_ref[e - 1] if e else 0

            for first, live in ((end - tr, end > start_e),
                                (se_ref[N_EXPERTS - 1] + e * tr,
                                 (se_ref[N_EXPERTS - 1] + e * tr) * ROW_TILE < xs_ref.shape[0])):
                @pl.when(live)
                def _():
                    rows = pl.ds(pl.multiple_of(first * ROW_TILE, ROW_TILE), tr * ROW_TILE)
                    fill = pltpu.make_async_copy(zbuf, xs_ref.at[rows], sem.at[2])
                    fill.start()
                    fill.wait()

    slot = i % 2
    _to_row_tiles(xr.at[slot], x_ref[...])
    base = i * tm

    def start(r, c):
        src = _row_tile(xr.at[slot], r)
        pltpu.make_async_copy(src, _row_tile(xs_ref, d1_ref[base + r]), sem.at[slot]).start()
        pltpu.make_async_copy(src, _row_tile(xs_ref, d2_ref[base + r]), sem.at[slot]).start(priority=1)
        return c

    def wait_step(s):
        def wait(r, c):
            for _ in range(2):
                pltpu.make_async_copy(_row_tile(xr.at[s], 0), _row_tile(xs_ref, 0), sem.at[s]).wait()
            return c

        lax.fori_loop(0, tm, wait, 0, unroll=8)

    lax.fori_loop(0, tm, start, 0, unroll=8)

    @pl.when(i > 0)
    def _():
        wait_step(1 - slot)

    @pl.when(i == pl.num_programs(0) - 1)
    def _():
        wait_step(slot)


def _expert_kernel(te_ref, blk_ref, nu_ref, xs_ref, w1_ref, w3_ref, w2_ref, ys_ref, *, tr):
    del te_ref, blk_ref
    used = pl.program_id(0) < nu_ref[0]

    @pl.when(used)
    def _():
        xb = _from_row_tiles(xs_ref, tr).astype(BF16)
        _to_row_tiles(ys_ref, _swiglu_tile(xb, w1_ref[0], w3_ref[0], w2_ref[0]))

    @pl.when(jnp.logical_not(used))
    def _():
        ys_ref[...] = jnp.zeros_like(ys_ref)


def _combine_kernel(d1_ref, d2_ref, x_ref, route_ref, g_ref, b_ref, ys_ref, o_ref, ybuf, sem, *, tm):
    i = pl.program_id(0)
    n_steps = pl.num_programs(0)

    def issue(tile, slot):
        base = tile * tm

        def start(r, c):
            pltpu.make_async_copy(_row_tile(ys_ref, d1_ref[base + r]), _row_tile(ybuf.at[slot, 0], r),
                                  sem.at[slot]).start()
            pltpu.make_async_copy(_row_tile(ys_ref, d2_ref[base + r]), _row_tile(ybuf.at[slot, 1], r),
                                  sem.at[slot]).start(priority=1)
            return c

        lax.fori_loop(0, tm, start, 0, unroll=8)

    @pl.when(i == 0)
    def _():
        issue(0, 0)

    @pl.when(i + 1 < n_steps)
    def _():
        issue(i + 1, (i + 1) % 2)

    slot = i % 2

    def wait(r, c):
        for k in range(2):
            pltpu.make_async_copy(_row_tile(ys_ref, 0), _row_tile(ybuf.at[slot, k], 0), sem.at[slot]).wait()
        return c

    lax.fori_loop(0, tm, wait, 0, unroll=8)
    route = route_ref[...]
    gates = jnp.concatenate([route, jnp.zeros((128 - route.shape[0], tm), F32)], axis=0).T
    ff = (gates[:, 4:5] * _from_row_tiles(ybuf.at[slot, 0], tm)
          + gates[:, 5:6] * _from_row_tiles(ybuf.at[slot, 1], tm))
    o_ref[...] = _layer_norm(ALPHA * x_ref[...] + ff, g_ref[...], b_ref[...])


def _moe_ffn(x2d, route, counts, w1, w3, w2, g, b):
    n, d = x2d.shape
    n_exp, _, f = w1.shape
    tr = min(TR_MOE, n)
    tm = min(TM_MOE, n)
    n_pad = 2 * n + n_exp * tr
    n_tiles = n_pad // tr
    i32 = jnp.int32

    cnt = counts[:, 0].astype(i32)
    seg = (cnt + tr - 1) // tr * tr
    seg_end = jnp.cumsum(seg)
    seg_start = seg_end - seg
    e1, e2 = route[0].astype(i32), route[1].astype(i32)
    dest1 = seg_start[e1] + route[2].astype(i32)
    dest2 = seg_start[e2] + route[3].astype(i32)
    n_used = jnp.maximum(seg_end[-1] // tr, 1)
    tile = jnp.minimum(jnp.arange(n_tiles, dtype=i32), n_used - 1)
    tile_expert = jnp.minimum(jnp.sum(tile[:, None] * tr >= seg_end[None, :], axis=1), n_exp - 1).astype(i32)

    cparams = pltpu.CompilerParams(dimension_semantics=("arbitrary",), vmem_limit_bytes=VMEM_LIMIT)
    assert d == ROW_TILE * 128
    xs = pl.pallas_call(
        functools.partial(_dispatch_kernel, tm=tm, tr=tr),
        grid_spec=pltpu.PrefetchScalarGridSpec(
            num_scalar_prefetch=3, grid=(n // tm,),
            in_specs=[pl.BlockSpec((tm, d), lambda i, *_: (i, 0))],
            out_specs=pl.BlockSpec(memory_space=pl.ANY),
            scratch_shapes=[pltpu.VMEM((2, tm * ROW_TILE, 128), F32), pltpu.VMEM((tr * ROW_TILE, 128), F32),
                            pltpu.SemaphoreType.DMA((3,))]),
        out_shape=jax.ShapeDtypeStruct((n_pad * ROW_TILE, 128), F32),
        compiler_params=cparams,
        name="moe_dispatch",
    )(dest1, dest2, seg_end.astype(i32), x2d)

    ys = pl.pallas_call(
        functools.partial(_expert_kernel, tr=tr),
        grid_spec=pltpu.PrefetchScalarGridSpec(
            num_scalar_prefetch=3, grid=(n_tiles,),
            in_specs=[pl.BlockSpec((tr * ROW_TILE, 128), lambda i, te, blk, nu: (blk[i], 0)),
                      pl.BlockSpec((1, d, f), lambda i, te, blk, nu: (te[i], 0, 0)),
                      pl.BlockSpec((1, d, f), lambda i, te, blk, nu: (te[i], 0, 0)),
                      pl.BlockSpec((1, f, d), lambda i, te, blk, nu: (te[i], 0, 0))],
            out_specs=pl.BlockSpec((tr * ROW_TILE, 128), lambda i, te, blk, nu: (i, 0))),
        out_shape=jax.ShapeDtypeStruct((n_pad * ROW_TILE, 128), F32),
        compiler_params=cparams,
        name="moe_experts",
    )(tile_expert, tile, n_used.reshape(1), xs, w1.astype(BF16), w3.astype(BF16), w2.astype(BF16))

    row = lambda a: a.reshape(1, -1)
    return pl.pallas_call(
        functools.partial(_combine_kernel, tm=tm),
        grid_spec=pltpu.PrefetchScalarGridSpec(
            num_scalar_prefetch=2, grid=(n // tm,),
            in_specs=[pl.BlockSpec((tm, d), lambda i, *_: (i, 0)),
                      pl.BlockSpec((N_EXPERTS, tm), lambda i, *_: (0, i)),
                      pl.BlockSpec((1, d), lambda i, *_: (0, 0)),
                      pl.BlockSpec((1, d), lambda i, *_: (0, 0)),
                      pl.BlockSpec(memory_space=pl.ANY)],
            out_specs=pl.BlockSpec((tm, d), lambda i, *_: (i, 0)),
            scratch_shapes=[pltpu.VMEM((2, 2, tm * ROW_TILE, 128), F32), pltpu.SemaphoreType.DMA((2,))]),
        out_shape=jax.ShapeDtypeStruct((n, d), F32),
        compiler_params=cparams,
        name="moe_combine",
    )(dest1, dest2, x2d, route, row(g), row(b), ys)


def kernel(x, positions, w_in, mla_q_norm_g, w_uq, mla_kv_norm_g, w_ukv, fox_forget_b, conv_w, conv_b,
           conv_norm_g, conv_norm_b, mla_out_norm_g, fox_out_norm_g, w_out, ln1_g, ln1_b, dense_w1,
           dense_w3, dense_w2, router_w, expert_w1, expert_w3, expert_w2, ln2_g, ln2_b):
    batch, seq, d = x.shape
    assert d == D_MODEL and seq % TQ == 0 and seq % min(TM_IN, seq) == 0
    depth = w_in.shape[0]
    tabs = _rope_tables(positions)
    h = x.reshape(batch * seq, d)
    for layer in range(depth):
        pw = _prep_inproj_weights(w_in[layer], w_uq[layer], w_ukv[layer], fox_forget_b[layer])
        q_t, k, v_t, hc = _input_projection(
            h, tabs, pw, mla_q_norm_g[layer], mla_kv_norm_g[layer], conv_w[layer], conv_b[layer],
            conv_norm_g[layer], conv_norm_b[layer], seq)
        o = _attention(q_t, k, v_t, batch, seq)
        j = layer // 2
        if layer % 2 == 0:
            h = _output_projection(o, hc, h, mla_out_norm_g[layer], fox_out_norm_g[layer], w_out[layer],
                                   ln1_g[layer], ln1_b[layer])[0]
            h = _dense_ffn(h, dense_w1[j], dense_w3[j], dense_w2[j], ln2_g[layer], ln2_b[layer])
        else:
            h, route, counts = _output_projection(o, hc, h, mla_out_norm_g[layer], fox_out_norm_g[layer],
                                                  w_out[layer], ln1_g[layer], ln1_b[layer], router_w[j])
            h = _moe_ffn(h, route, counts, expert_w1[j], expert_w3[j], expert_w2[j], ln2_g[layer],
                         ln2_b[layer])
    return h.reshape(batch, seq, d)
```

```python
import functools
import math

import numpy as np
import jax
import jax.numpy as jnp
from jax import lax
from jax.experimental import pallas as pl
from jax.experimental.pallas import tpu as pltpu

F32 = jnp.float32
BF16 = jnp.bfloat16

D_MODEL = 1024
DEPTH = 4
MLA_HEADS = 8
MLA_NOPE = 64
MLA_ROPE = 32
MLA_V = 64
MLA_Q_RANK = 256
MLA_KV_RANK = 128
ROPE_THETA = 10000.0
FOX_HEADS = 4
FOX_DIM = 64
CONV_CH = 256
CONV_GROUPS = 4
CONV_WIDTH = 31
MLA_WIDTH = MLA_HEADS * MLA_V
FOX_WIDTH = FOX_HEADS * FOX_DIM
N_EXPERTS = 8
ALPHA = (2.0 * DEPTH) ** 0.25
NORM_EPS = 1e-5
LOG2E = math.log2(math.e)

HEAD_PAD = 128
N_HEADS = MLA_HEADS + FOX_HEADS
V_DIM = 64
V_ROWS = 80
CONV_HALO = 32
VMEM_LIMIT = 56 * 1024 * 1024

TQ = 512
TK = 256
HPS = 2
TM_IN = 512
TM_OUT = 512
TM_FFN = 512
TR_MOE = 512
TM_MOE = 512
ROW_TILE = 8


def _nt_dot(a, b):
    return lax.dot_general(a, b, (((1,), (1,)), ((), ())), preferred_element_type=F32)


def _dot(a, b):
    return jnp.dot(a, b, preferred_element_type=F32)


def _split2_dot(a, m_bf16):
    hi = a.astype(BF16)
    lo = (a - hi.astype(F32)).astype(BF16)
    return _dot(hi, m_bf16) + _dot(lo, m_bf16)


def _split3(a):
    hi = a.astype(BF16).astype(F32)
    r1 = a - hi
    mid = r1.astype(BF16).astype(F32)
    lo = (r1 - mid).astype(BF16).astype(F32)
    return hi, mid, lo


def _const_spec(shape):
    nd = len(shape)
    return pl.BlockSpec(shape, lambda *_: (0,) * nd, pipeline_mode=pl.Buffered(1))


def _rope_kernel(pos_ref, invf_ref, c_ref, s_ref, ct_ref, st_ref):
    pos = pos_ref[...].astype(F32)
    ang = invf_ref[...] * pos
    cos = jnp.cos(ang)
    sin = jnp.sin(ang)
    tn = pos.shape[1]
    ct = jnp.concatenate([jnp.ones((MLA_NOPE, tn), F32), cos, cos, jnp.zeros((32, tn), F32)], axis=0)
    st = jnp.concatenate([jnp.zeros((MLA_NOPE, tn), F32), sin, sin, jnp.zeros((32, tn), F32)], axis=0)
    ct_ref[...] = ct
    st_ref[...] = st
    c_ref[...] = ct.T
    s_ref[...] = st.T


def _rope_tables(positions):
    n = positions.size
    tn = min(512, n)
    inv_freq = ROPE_THETA ** (-jnp.arange(0, MLA_ROPE, 2, dtype=F32) / MLA_ROPE)
    return pl.pallas_call(
        _rope_kernel,
        grid=(n // tn,),
        in_specs=[pl.BlockSpec((1, tn), lambda i: (0, i)),
                  pl.BlockSpec((MLA_ROPE // 2, 1), lambda i: (0, 0))],
        out_specs=[pl.BlockSpec((tn, HEAD_PAD), lambda i: (i, 0)),
                   pl.BlockSpec((tn, HEAD_PAD), lambda i: (i, 0)),
                   pl.BlockSpec((HEAD_PAD, tn), lambda i: (0, i)),
                   pl.BlockSpec((HEAD_PAD, tn), lambda i: (0, i))],
        out_shape=[jax.ShapeDtypeStruct((n, HEAD_PAD), F32),
                   jax.ShapeDtypeStruct((n, HEAD_PAD), F32),
                   jax.ShapeDtypeStruct((HEAD_PAD, n), F32),
                   jax.ShapeDtypeStruct((HEAD_PAD, n), F32)],
        name="rope_tables",
    )(positions.reshape(1, n), inv_freq.reshape(-1, 1))


_A_CQ = 0
_A_CKV = _A_CQ + MLA_Q_RANK
_A_KR = _A_CKV + MLA_KV_RANK
_A_KRR = _A_KR + HEAD_PAD
_A_FK = _A_KRR + HEAD_PAD
_A_CA = _A_FK + FOX_HEADS * HEAD_PAD
_A_CG = _A_CA + CONV_CH
_A_COLS = _A_CG + CONV_CH
_AUG_ROWS = 8
_F_ROWS = 16


def _rms(x, g):
    ms = jnp.mean(jnp.square(x), axis=-1, keepdims=True)
    return x * lax.rsqrt(ms + NORM_EPS) * g


def _inproj_kernel(x_ref, c_ref, s_ref, ct_ref, st_ref, wa_ref, wfq_ref, wfv_ref, wf_ref, fb_ref,
                   gq_ref, wuq_ref, wuqr_ref, gkv_ref, wuk_ref, wuv_ref,
                   cw_ref, cb_ref, cng_ref, cnb_ref, gmat_ref,
                   qt_ref, k_ref, vt_ref, hc_ref,
                   hbuf, hsh, cbuf, fcarry, *, tiles_per_seq, tm):
    i = pl.program_id(0)

    @pl.when(i % tiles_per_seq == 0)
    def _():
        hbuf[0:CONV_HALO, :] = jnp.zeros((CONV_HALO, CONV_CH), F32)
        fcarry[...] = jnp.zeros_like(fcarry)

    xb = x_ref[...].astype(BF16)
    p1 = _dot(xb, wa_ref[...])
    cos_t = c_ref[...]
    sin_t = s_ref[...]
    cos_tt = ct_ref[...]
    sin_tt = st_ref[...]

    cqn = _rms(p1[:, _A_CQ:_A_CQ + MLA_Q_RANK], gq_ref[...]).astype(BF16)
    q_t = _nt_dot(wuq_ref[...], cqn)
    q_rot_t = _nt_dot(wuqr_ref[...], cqn)
    mla_scale = (MLA_NOPE + MLA_ROPE) ** -0.5 * LOG2E
    for h in range(MLA_HEADS):
        rows = slice(h * HEAD_PAD, (h + 1) * HEAD_PAD)
        qh = (q_t[rows, :] * cos_tt + q_rot_t[rows, :] * sin_tt) * mla_scale
        for c in range(tm // TQ):
            qt_ref[c, rows, :] = qh[:, c * TQ:(c + 1) * TQ].astype(BF16)

    ckvn = _rms(p1[:, _A_CKV:_A_CKV + MLA_KV_RANK], gkv_ref[...]).astype(BF16)
    k_nope = _dot(ckvn, wuk_ref[...])
    k_rope = p1[:, _A_KR:_A_KR + HEAD_PAD] * cos_t + p1[:, _A_KRR:_A_KRR + HEAD_PAD] * sin_t
    for h in range(MLA_HEADS):
        cols = slice(h * HEAD_PAD, (h + 1) * HEAD_PAD)
        k_ref[:, cols] = (k_nope[:, cols] + k_rope).astype(BF16)
    v_t = _nt_dot(wuv_ref[...], ckvn)
    fv_t = _nt_dot(wfv_ref[...], xb)
    ones_blk = jnp.where(lax.broadcasted_iota(jnp.int32, (V_ROWS - V_DIM, tm), 0) == 0, 1.0, 0.0)
    for h in range(N_HEADS):
        src = v_t if h < MLA_HEADS else fv_t
        r0 = (h if h < MLA_HEADS else h - MLA_HEADS) * V_DIM
        vh = jnp.concatenate([src[r0:r0 + V_DIM, :], ones_blk], axis=0).astype(BF16)
        for c in range(tm // TK):
            vt_ref[c, h * V_ROWS:(h + 1) * V_ROWS, :] = vh[:, c * TK:(c + 1) * TK]

    z = _nt_dot(wf_ref[...], xb) + fb_ref[...]
    logf = (jnp.minimum(z, 0.0) - jnp.log1p(jnp.exp(-jnp.abs(z)))) * LOG2E
    r_i = lax.broadcasted_iota(jnp.int32, (tm, tm), 0)
    c_i = lax.broadcasted_iota(jnp.int32, (tm, tm), 1)
    upper = jnp.where(r_i <= c_i, 1.0, 0.0).astype(BF16)
    l_hi, l_mid, l_lo = _split3(logf)
    cum = (_dot(l_hi.astype(BF16), upper) + _dot(l_mid.astype(BF16), upper)
           + _dot(l_lo.astype(BF16), upper))
    f_cum = cum + fcarry[:, 0:1]
    fcarry[...] = jnp.broadcast_to(f_cum[:, tm - 1:tm], fcarry.shape)
    f_hi, f_mid, f_lo = _split3(f_cum)

    fq_t = _nt_dot(wfq_ref[...], xb)
    row8 = lax.broadcasted_iota(jnp.int32, (_AUG_ROWS, tm), 0)
    fox_scale = FOX_DIM ** -0.5 * LOG2E
    for h in range(FOX_HEADS):
        bh = lambda a: jnp.broadcast_to(a[h:h + 1, :], (_AUG_ROWS, tm))
        aug_q = jnp.where(row8 == 0, bh(f_hi), jnp.where(row8 == 1, bh(f_mid), jnp.where(
            row8 == 2, bh(f_lo), jnp.where(row8 < 6, 1.0, 0.0))))
        aug_k = jnp.where(row8 < 3, 1.0, jnp.where(row8 == 3, -bh(f_hi), jnp.where(
            row8 == 4, -bh(f_mid), jnp.where(row8 == 5, -bh(f_lo), 0.0))))
        pad = jnp.zeros((HEAD_PAD - FOX_DIM - _AUG_ROWS, tm), F32)
        qh = jnp.concatenate(
            [fq_t[h * HEAD_PAD:h * HEAD_PAD + FOX_DIM, :] * fox_scale, aug_q, pad], axis=0)
        rows = slice((MLA_HEADS + h) * HEAD_PAD, (MLA_HEADS + h + 1) * HEAD_PAD)
        for c in range(tm // TQ):
            qt_ref[c, rows, :] = qh[:, c * TQ:(c + 1) * TQ].astype(BF16)
        kaug_t = jnp.concatenate([jnp.zeros((FOX_DIM, tm), F32), aug_k, pad], axis=0)
        fk = p1[:, _A_FK + h * HEAD_PAD:_A_FK + (h + 1) * HEAD_PAD]
        k_ref[:, rows] = (fk + kaug_t.T).astype(BF16)

    a = p1[:, _A_CA:_A_CA + CONV_CH]
    g = p1[:, _A_CG:_A_CG + CONV_CH]
    hbuf[CONV_HALO:CONV_HALO + tm, :] = a * jax.nn.sigmoid(g)
    chunk = 64
    first = CONV_HALO - (CONV_WIDTH - 1)
    for r in range(1, 8):
        hsh[r - 1] = hbuf[r:r + tm + CONV_HALO - 8, :]
    for c0 in range(0, tm, chunk):
        acc = jnp.zeros((chunk, CONV_CH), F32)
        for o in range(first, first + CONV_WIDTH):
            r = o % 8
            row = c0 + o - r
            seg = hbuf[row:row + chunk, :] if r == 0 else hsh[r - 1, row:row + chunk, :]
            acc = acc + cw_ref[o - first:o - first + 1, :] * seg
        cbuf[c0:c0 + chunk, :] = acc
    hbuf[0:CONV_HALO, :] = hbuf[tm:tm + CONV_HALO, :]
    hv = cbuf[...] + cb_ref[...]
    gm = gmat_ref[...]
    mu = _split2_dot(hv, gm)
    d = hv - mu
    var = _split2_dot(d * d, gm)
    hn = d * lax.rsqrt(var + NORM_EPS) * cng_ref[...] + cnb_ref[...]
    hc_ref[...] = (hn * jax.nn.sigmoid(hn)).astype(BF16)


def _prep_inproj_weights(w_in, w_uq, w_ukv, fox_forget_b):
    o = np.cumsum((0, MLA_Q_RANK, MLA_KV_RANK, MLA_ROPE, FOX_WIDTH, FOX_WIDTH, FOX_WIDTH, FOX_HEADS,
                   2 * CONV_CH))
    w_cq, w_ckv, w_kr, w_fq, w_fk, w_fv, w_f, w_cv = (w_in[:, o[i]:o[i + 1]] for i in range(8))
    d = w_in.shape[0]
    half = MLA_ROPE // 2

    def rot_cols(w):
        return jnp.concatenate([-w[..., half:], w[..., :half]], axis=-1)

    def rope_block(w):
        return jnp.pad(w, ((0, 0), (MLA_NOPE, HEAD_PAD - MLA_NOPE - MLA_ROPE)))

    w_fk_pad = jnp.pad(w_fk.reshape(d, FOX_HEADS, FOX_DIM), ((0, 0), (0, 0), (0, HEAD_PAD - FOX_DIM)))
    wa = jnp.concatenate([w_cq, w_ckv, rope_block(w_kr), rope_block(rot_cols(w_kr)),
                          w_fk_pad.reshape(d, FOX_HEADS * HEAD_PAD), w_cv], axis=1)
    w_fq_pad = jnp.pad(w_fq.reshape(d, FOX_HEADS, FOX_DIM), ((0, 0), (0, 0), (0, HEAD_PAD - FOX_DIM)))
    wfq_t = w_fq_pad.reshape(d, FOX_HEADS * HEAD_PAD).T
    wfv_t = w_fv.T
    wf_t = jnp.pad(w_f, ((0, 0), (0, _F_ROWS - FOX_HEADS))).T
    fb = jnp.pad(fox_forget_b, (0, _F_ROWS - FOX_HEADS)).reshape(_F_ROWS, 1)

    uq = w_uq.reshape(MLA_Q_RANK, MLA_HEADS, MLA_NOPE + MLA_ROPE)
    uq_nope, uq_rope = uq[..., :MLA_NOPE], uq[..., MLA_NOPE:]
    tail = ((0, 0), (0, 0), (0, HEAD_PAD - MLA_NOPE - MLA_ROPE))
    uq_pad = jnp.pad(jnp.concatenate([uq_nope, uq_rope], axis=-1), tail)
    uq_rot_pad = jnp.pad(jnp.concatenate([jnp.zeros_like(uq_nope), rot_cols(uq_rope)], axis=-1), tail)
    wuq_t = uq_pad.reshape(MLA_Q_RANK, MLA_HEADS * HEAD_PAD).T
    wuqr_t = uq_rot_pad.reshape(MLA_Q_RANK, MLA_HEADS * HEAD_PAD).T
    ukv = w_ukv.reshape(MLA_KV_RANK, MLA_HEADS, MLA_NOPE + MLA_V)
    wuk = jnp.pad(ukv[..., :MLA_NOPE], ((0, 0), (0, 0), (0, HEAD_PAD - MLA_NOPE))).reshape(
        MLA_KV_RANK, MLA_HEADS * HEAD_PAD)
    wuv_t = ukv[..., MLA_NOPE:].reshape(MLA_KV_RANK, MLA_WIDTH).T
    bf = lambda a: a.astype(BF16)
    return dict(wa=bf(wa), wfq=bf(wfq_t), wfv=bf(wfv_t), wf=bf(wf_t), fb=fb, wuq=bf(wuq_t),
                wuqr=bf(wuqr_t), wuk=bf(wuk), wuv=bf(wuv_t))


def _input_projection(x2d, tabs, pw, gq, gkv, conv_w, conv_b, conv_ng, conv_nb, seq):
    n, d = x2d.shape
    tm = min(TM_IN, seq)
    cos_t, sin_t, cos_tt, sin_tt = tabs
    gidx = np.arange(CONV_CH) // (CONV_CH // CONV_GROUPS)
    gmat = jnp.asarray((gidx[:, None] == gidx[None, :]) / (CONV_CH // CONV_GROUPS), BF16)
    cw = jnp.pad(conv_w, ((0, 32 - CONV_WIDTH), (0, 0)))
    row = lambda a: a.reshape(1, -1)
    tok = lambda w: pl.BlockSpec((tm, w), lambda i: (i, 0))
    tok_t = lambda r: pl.BlockSpec((r, tm), lambda i: (0, i))
    consts = [pw["wa"], pw["wfq"], pw["wfv"], pw["wf"], pw["fb"], row(gq), pw["wuq"], pw["wuqr"],
              row(gkv), pw["wuk"], pw["wuv"], cw, row(conv_b), row(conv_ng), row(conv_nb), gmat]
    kern = functools.partial(_inproj_kernel, tiles_per_seq=seq // tm, tm=tm)
    return pl.pallas_call(
        kern,
        grid=(n // tm,),
        in_specs=[tok(d), tok(HEAD_PAD), tok(HEAD_PAD), tok_t(HEAD_PAD), tok_t(HEAD_PAD)]
        + [_const_spec(c.shape) for c in consts],
        out_specs=[pl.BlockSpec((tm // TQ, N_HEADS * HEAD_PAD, TQ), lambda i: (i, 0, 0)),
                   tok(N_HEADS * HEAD_PAD),
                   pl.BlockSpec((tm // TK, N_HEADS * V_ROWS, TK), lambda i: (i, 0, 0)),
                   tok(CONV_CH)],
        out_shape=[jax.ShapeDtypeStruct((n // TQ, N_HEADS * HEAD_PAD, TQ), BF16),
                   jax.ShapeDtypeStruct((n, N_HEADS * HEAD_PAD), BF16),
                   jax.ShapeDtypeStruct((n // TK, N_HEADS * V_ROWS, TK), BF16),
                   jax.ShapeDtypeStruct((n, CONV_CH), BF16)],
        scratch_shapes=[pltpu.VMEM((CONV_HALO + tm, CONV_CH), F32),
                        pltpu.VMEM((7, tm + CONV_HALO - 8, CONV_CH), F32),
                        pltpu.VMEM((tm, CONV_CH), F32),
                        pltpu.VMEM((_F_ROWS, 128), F32)],
        compiler_params=pltpu.CompilerParams(dimension_semantics=("arbitrary",),
                                             vmem_limit_bytes=VMEM_LIMIT),
        name="input_projection",
    )(x2d, cos_t, sin_t, cos_tt, sin_tt, *consts)


def _attn_kernel(qt_ref, k_ref, vt_ref, o_ref, *scratch, n_tiles):
    assert TQ == 2 * TK
    s_ref = (scratch[0:HPS], scratch[HPS:2 * HPS])
    sd_ref = scratch[2 * HPS:3 * HPS]
    p_ref = (scratch[3 * HPS:4 * HPS], scratch[4 * HPS:5 * HPS])
    pd_ref = scratch[5 * HPS:6 * HPS]
    acc_bufs = (scratch[6 * HPS:7 * HPS], scratch[7 * HPS:8 * HPS])
    diff_ref = scratch[8 * HPS]
    diff_ref[...] = (lax.broadcasted_iota(jnp.int32, (TK, TQ), 1)
                     - lax.broadcasted_iota(jnp.int32, (TK, TQ), 0))
    for h in range(HPS):
        p_ref[1][h][...] = jnp.zeros_like(p_ref[1][h])
        for par in range(2):
            acc_bufs[par][h][...] = jnp.ones_like(acc_bufs[par][h])

    def tile_scores(tile, j, slot):
        row0 = pl.multiple_of(j * TK, TK)
        block_max = []
        for h in range(HPS):
            s = _dot(k_ref[pl.ds(row0, TK), h * HEAD_PAD:(h + 1) * HEAD_PAD],
                     qt_ref[tile, h * HEAD_PAD:(h + 1) * HEAD_PAD, :])
            s_ref[slot][h][...] = s
            block_max.append(jnp.max(s, axis=0, keepdims=True))
        return block_max

    def last_diag_scores(tile):
        row0 = pl.multiple_of((2 * tile + 1) * TK, TK)
        for h in range(HPS):
            sd_ref[h][...] = _dot(k_ref[pl.ds(row0, TK), h * HEAD_PAD:(h + 1) * HEAD_PAD],
                                  qt_ref[tile, h * HEAD_PAD:(h + 1) * HEAD_PAD, TK:])

    def finalize(tile, par):
        out_t = jnp.concatenate([acc_bufs[par][h][0:V_DIM, :] / acc_bufs[par][h][V_DIM:V_DIM + 1, :]
                                 for h in range(HPS)], axis=0)
        o_ref[pl.ds(pl.multiple_of(tile * TQ, TQ), TQ), :] = out_t.T.astype(o_ref.dtype)

    def q_tile(i, par, bm0):
        acc_ref = acc_bufs[par]
        scores = functools.partial(tile_scores, i)

        def softmax(slot, m, block_max):
            m_new = [jnp.maximum(m[h], block_max[h]) for h in range(HPS)]
            for h in range(HPS):
                p_ref[slot][h][...] = jnp.exp2(s_ref[slot][h][...] - m_new[h]).astype(BF16)
            return m_new, [jnp.exp2(m[h] - m_new[h]) for h in range(HPS)]

        def values(j, slot, alpha, gate=None):
            for h in range(HPS):
                pv = _dot(vt_ref[j, h * V_ROWS:(h + 1) * V_ROWS, :], p_ref[slot][h][...])
                acc_ref[h][...] = alpha[h] * acc_ref[h][...] + (pv if gate is None else gate * pv)

        m0 = [jnp.full((1, TQ), -1e30, F32)] * HPS
        a0 = [jnp.zeros((1, TQ), F32)] * HPS

        def pair(u, state):
            m, alpha, bm_t = list(state[0:HPS]), list(state[HPS:2 * HPS]), list(state[2 * HPS:3 * HPS])
            t = 2 * u
            m, alpha_t = softmax(0, m, bm_t)
            values(jnp.maximum(t - 1, 0), 1, alpha, jnp.where(t > 0, 1.0, 0.0))
            bm_t1 = scores(t + 1, 1)
            m, alpha_t1 = softmax(1, m, bm_t1)
            values(t, 0, alpha_t)
            bm_t2 = scores(t + 2, 0)
            return (*m, *alpha_t1, *bm_t2)

        state = lax.fori_loop(0, i // 2, lambda v, st: pair(2 * v + 1, pair(2 * v, st)), (*m0, *a0, *bm0))
        state = lax.fori_loop(i - i % 2, i, pair, state)
        m, alpha = list(state[0:HPS]), list(state[HPS:2 * HPS])
        d0 = 2 * i
        values(jnp.maximum(d0 - 1, 0), 1, alpha, jnp.where(i > 0, 1.0, 0.0))
        finalize(jnp.where(i > 0, i - 1, n_tiles - 1), 1 - par)
        nxt = jnp.minimum(i + 1, n_tiles - 1)

        m_d0, alpha_d0 = [], []
        for h in range(HPS):
            s = s_ref[0][h][...]
            s = jnp.concatenate([jnp.where(diff_ref[:, 0:TK] >= 0, s[:, 0:TK], -jnp.inf), s[:, TK:]], axis=1)
            m_d0.append(jnp.maximum(m[h], jnp.max(s, axis=0, keepdims=True)))
            p_ref[0][h][...] = jnp.exp2(s - m_d0[h]).astype(BF16)
            alpha_d0.append(jnp.exp2(m[h] - m_d0[h]))
        bm_next = tile_scores(nxt, 0, 0)

        alpha_d1 = []
        for h in range(HPS):
            s = jnp.where(diff_ref[:, 0:TK] >= 0, sd_ref[h][...], -jnp.inf)
            m_old = m_d0[h][:, TK:]
            m_new = jnp.maximum(m_old, jnp.max(s, axis=0, keepdims=True))
            pd_ref[h][...] = jnp.exp2(s - m_new).astype(BF16)
            alpha_d1.append(jnp.exp2(m_old - m_new))
        last_diag_scores(nxt)

        values(d0, 0, alpha_d0)
        for h in range(HPS):
            acc_ref[h][:, TK:] = alpha_d1[h] * acc_ref[h][:, TK:] + _dot(
                vt_ref[d0 + 1, h * V_ROWS:(h + 1) * V_ROWS, :], pd_ref[h][...])
        return tuple(bm_next)

    last_diag_scores(0)
    bm = lax.fori_loop(0, n_tiles // 2, lambda a, st: q_tile(2 * a + 1, 1, q_tile(2 * a, 0, st)),
                       tuple(tile_scores(0, 0, 0)))
    if n_tiles % 2:
        q_tile(n_tiles - 1, 0, bm)
    finalize(n_tiles - 1, (n_tiles - 1) % 2)


def _attention(q_t, k, v_t, batch, seq):
    n = k.shape[0]
    groups = N_HEADS // HPS
    return pl.pallas_call(
        functools.partial(_attn_kernel, n_tiles=seq // TQ),
        grid=(batch, groups),
        in_specs=[pl.BlockSpec((seq // TQ, HPS * HEAD_PAD, TQ), lambda b, p: (b, p, 0)),
                  pl.BlockSpec((seq, HPS * HEAD_PAD), lambda b, p: (b, p)),
                  pl.BlockSpec((seq // TK, HPS * V_ROWS, TK), lambda b, p: (b, p, 0))],
        out_specs=pl.BlockSpec((seq, HPS * V_DIM), lambda b, p: (b, p)),
        out_shape=jax.ShapeDtypeStruct((n, N_HEADS * V_DIM), BF16),
        scratch_shapes=[pltpu.VMEM((TK, TQ), F32)] * (2 * HPS) + [pltpu.VMEM((TK, TK), F32)] * HPS
        + [pltpu.VMEM((TK, TQ), BF16)] * (2 * HPS) + [pltpu.VMEM((TK, TK), BF16)] * HPS
        + [pltpu.VMEM((V_ROWS, TQ), F32)] * (2 * HPS) + [pltpu.VMEM((TK, TQ), jnp.int32)],
        compiler_params=pltpu.CompilerParams(dimension_semantics=("arbitrary", "arbitrary"),
                                             vmem_limit_bytes=VMEM_LIMIT),
        name="attention",
    )(q_t, k, v_t)


def _layer_norm(x, g, b):
    mu = jnp.mean(x, axis=-1, keepdims=True)
    d = x - mu
    var = jnp.mean(jnp.square(d), axis=-1, keepdims=True)
    return d * lax.rsqrt(var + NORM_EPS) * g + b


def _mix_and_norm(o_ref, hc_ref, x_ref, gm_ref, gf_ref, wo_ref, g1_ref, b1_ref):
    o = o_ref[...].astype(F32)
    mla = _rms(o[:, :MLA_WIDTH], gm_ref[...])
    fox = _rms(o[:, MLA_WIDTH:], gf_ref[...])
    mixed = jnp.concatenate([mla.astype(BF16), fox.astype(BF16), hc_ref[...]], axis=-1)
    y = _dot(mixed, wo_ref[...])
    return _layer_norm(ALPHA * x_ref[...] + y, g1_ref[...], b1_ref[...])


def _outproj_kernel(o_ref, hc_ref, x_ref, gm_ref, gf_ref, wo_ref, g1_ref, b1_ref, *rest, with_router):
    if with_router:
        rw_ref, x1_ref, route_ref, counts_ref, cnt_ref, upper_ref = rest
    else:
        (x1_ref,) = rest
    x1 = _mix_and_norm(o_ref, hc_ref, x_ref, gm_ref, gf_ref, wo_ref, g1_ref, b1_ref)
    x1_ref[...] = x1
    if with_router:
        rw = rw_ref[...]
        x_hi = x1.astype(BF16)
        x_lo = (x1 - x_hi.astype(F32)).astype(BF16)
        w_hi = rw.astype(BF16)
        w_lo = (rw - w_hi.astype(F32)).astype(BF16)
        both = _dot(x_hi, jnp.concatenate([w_hi, w_lo], axis=1))
        logits = both[:, :128] + (_dot(x_lo, w_hi) + both[:, 128:])
        tm = logits.shape[0]
        lg = logits.T[0:N_EXPERTS, :]
        row = lax.broadcasted_iota(jnp.int32, lg.shape, 0)
        v1 = jnp.max(lg, axis=0, keepdims=True)
        i1 = jnp.min(jnp.where(lg == v1, row, N_EXPERTS), axis=0, keepdims=True)
        rest_l = jnp.where(row == i1, -jnp.inf, lg)
        v2 = jnp.max(rest_l, axis=0, keepdims=True)
        i2 = jnp.min(jnp.where(rest_l == v2, row, N_EXPERTS), axis=0, keepdims=True)
        e2 = jnp.exp(v2 - v1)
        den = 1.0 + e2

        @pl.when(pl.program_id(0) == 0)
        def _():
            cnt_ref[...] = jnp.zeros_like(cnt_ref)
            r_i = lax.broadcasted_iota(jnp.int32, (tm, tm), 0)
            c_i = lax.broadcasted_iota(jnp.int32, (tm, tm), 1)
            upper_ref[...] = jnp.where(r_i < c_i, 1.0, 0.0).astype(BF16)

        sel = jnp.where(row == i1, 1.0, jnp.where(row == i2, 1.0, 0.0))
        sel16 = jnp.concatenate([sel, jnp.zeros_like(sel)], axis=0).astype(BF16)
        before = cnt_ref[:, 0:1]
        rank = _dot(sel16, upper_ref[...])[0:N_EXPERTS, :] + before
        total = before + jnp.sum(sel, axis=1, keepdims=True)
        cnt_ref[...] = jnp.broadcast_to(total, cnt_ref.shape)
        counts_ref[...] = jnp.broadcast_to(total, counts_ref.shape)
        r1 = jnp.sum(jnp.where(row == i1, rank, 0.0), axis=0, keepdims=True)
        r2 = jnp.sum(jnp.where(row == i2, rank, 0.0), axis=0, keepdims=True)
        rows = (i1.astype(F32), i2.astype(F32), r1, r2, 1.0 / den, e2 / den)
        route = jnp.zeros(lg.shape, F32)
        for c, v in enumerate(rows):
            route = jnp.where(row == c, v, route)
        route_ref[...] = route


def _output_projection(o, hc, x2d, gm, gf, w_out, g1, b1, router_w=None):
    n, d = x2d.shape
    tm = min(TM_OUT, n)
    row = lambda a: a.reshape(1, -1)
    tok = lambda w: pl.BlockSpec((tm, w), lambda i: (i, 0))
    consts = [row(gm), row(gf), w_out.astype(BF16), row(g1), row(b1)]
    out_specs = [tok(d)]
    out_shape = [jax.ShapeDtypeStruct((n, d), F32)]
    scratch = []
    if router_w is not None:
        consts.append(jnp.pad(router_w, ((0, 0), (0, 128 - N_EXPERTS))))
        out_specs += [pl.BlockSpec((N_EXPERTS, tm), lambda i: (0, i)),
                      pl.BlockSpec((N_EXPERTS, 128), lambda i: (0, 0))]
        out_shape += [jax.ShapeDtypeStruct((N_EXPERTS, n), F32), jax.ShapeDtypeStruct((N_EXPERTS, 128), F32)]
        scratch = [pltpu.VMEM((N_EXPERTS, 128), F32), pltpu.VMEM((tm, tm), BF16)]
    return pl.pallas_call(
        functools.partial(_outproj_kernel, with_router=router_w is not None),
        grid=(n // tm,),
        in_specs=[tok(o.shape[1]), tok(CONV_CH), tok(d)] + [_const_spec(c.shape) for c in consts],
        out_specs=out_specs,
        out_shape=out_shape,
        scratch_shapes=scratch,
        compiler_params=pltpu.CompilerParams(dimension_semantics=("arbitrary",),
                                             vmem_limit_bytes=VMEM_LIMIT),
        name="output_projection",
    )(o, hc, x2d, *consts)


def _swiglu_tile(xb, w1, w3, w2):
    h1 = _dot(xb, w1)
    h3 = _dot(xb, w3)
    hid = (h1 * jax.nn.sigmoid(h1) * h3).astype(BF16)
    return _dot(hid, w2)


def _dense_layer_kernel(o_ref, hc_ref, x_ref, gm_ref, gf_ref, wo_ref, g1_ref, b1_ref,
                        w1_ref, w3_ref, w2_ref, g2_ref, b2_ref, out_ref, *, f_chunk):
    x1 = _mix_and_norm(o_ref, hc_ref, x_ref, gm_ref, gf_ref, wo_ref, g1_ref, b1_ref)
    xb = x1.astype(BF16)
    ff = None
    for c0 in range(0, w1_ref.shape[1], f_chunk):
        part = _swiglu_tile(xb, w1_ref[:, c0:c0 + f_chunk], w3_ref[:, c0:c0 + f_chunk],
                            w2_ref[c0:c0 + f_chunk, :])
        ff = part if ff is None else ff + part
    out_ref[...] = _layer_norm(ALPHA * x1 + ff, g2_ref[...], b2_ref[...])


def _dense_layer(o, hc, x2d, gm, gf, w_out, g1, b1, w1, w3, w2, g2, b2):
    n, d = x2d.shape
    tm = min(TM_FFN, n)
    f = w1.shape[1]
    f_chunk = f // 2 if (f // 2) % 128 == 0 else f
    row = lambda a: a.reshape(1, -1)
    tok = lambda w: pl.BlockSpec((tm, w), lambda i: (i, 0))
    consts = [row(gm), row(gf), w_out.astype(BF16), row(g1), row(b1),
              w1.astype(BF16), w3.astype(BF16), w2.astype(BF16), row(g2), row(b2)]
    return pl.pallas_call(
        functools.partial(_dense_layer_kernel, f_chunk=f_chunk),
        grid=(n // tm,),
        in_specs=[tok(o.shape[1]), tok(CONV_CH), tok(d)] + [_const_spec(c.shape) for c in consts],
        out_specs=tok(d),
        out_shape=jax.ShapeDtypeStruct((n, d), F32),
        compiler_params=pltpu.CompilerParams(dimension_semantics=("arbitrary",),
                                             vmem_limit_bytes=VMEM_LIMIT),
        name="dense_layer",
    )(o, hc, x2d, *consts)


def _to_row_tiles(ref, x):
    for c in range(ROW_TILE):
        ref[pl.ds(c, x.shape[0], stride=ROW_TILE), :] = x[:, c * 128:(c + 1) * 128]


def _from_row_tiles(ref, t):
    return jnp.concatenate([ref[pl.ds(c, t, stride=ROW_TILE), :] for c in range(ROW_TILE)], axis=-1)


def _row_tile(ref, r):
    return ref.at[pl.ds(pl.multiple_of(r * ROW_TILE, ROW_TILE), ROW_TILE)]


def _dispatch_kernel(d1_ref, d2_ref, se_ref, x_ref, xs_ref, xr, zbuf, sem, *, tm, tr):
    i = pl.program_id(0)

    @pl.when(i == 0)
    def _():
        zbuf[...] = jnp.zeros_like(zbuf)
        for e in range(N_EXPERTS):
            end = se_ref[e]
            start_e = se_ref[e - 1] if e else 0

            for first, live in ((end - tr, end > start_e),
                                (se_ref[N_EXPERTS - 1] + e * tr,
                                 (se_ref[N_EXPERTS - 1] + e * tr) * ROW_TILE < xs_ref.shape[0])):
                @pl.when(live)
                def _():
                    rows = pl.ds(pl.multiple_of(first * ROW_TILE, ROW_TILE), tr * ROW_TILE)
                    fill = pltpu.make_async_copy(zbuf, xs_ref.at[rows], sem.at[2])
                    fill.start()
                    fill.wait()

    slot = i % 2
    _to_row_tiles(xr.at[slot], x_ref[...])
    base = i * tm

    def start(r, c):
        src = _row_tile(xr.at[slot], r)
        pltpu.make_async_copy(src, _row_tile(xs_ref, d1_ref[base + r]), sem.at[slot]).start()
        pltpu.make_async_copy(src, _row_tile(xs_ref, d2_ref[base + r]), sem.at[slot]).start(priority=1)
        return c

    def wait_step(s):
        def wait(r, c):
            for _ in range(2):
                pltpu.make_async_copy(_row_tile(xr.at[s], 0), _row_tile(xs_ref, 0), sem.at[s]).wait()
            return c

        lax.fori_loop(0, tm, wait, 0, unroll=8)

    lax.fori_loop(0, tm, start, 0, unroll=8)

    @pl.when(i > 0)
    def _():
        wait_step(1 - slot)

    @pl.when(i == pl.num_programs(0) - 1)
    def _():
        wait_step(slot)


def _expert_kernel(te_ref, blk_ref, nu_ref, xs_ref, w1_ref, w3_ref, w2_ref, ys_ref, *, tr):
    del te_ref, blk_ref
    used = pl.program_id(0) < nu_ref[0]

    @pl.when(used)
    def _():
        xb = _from_row_tiles(xs_ref, tr).astype(BF16)
        _to_row_tiles(ys_ref, _swiglu_tile(xb, w1_ref[0], w3_ref[0], w2_ref[0]))

    @pl.when(jnp.logical_not(used))
    def _():
        ys_ref[...] = jnp.zeros_like(ys_ref)


def _combine_kernel(d1_ref, d2_ref, x_ref, route_ref, g_ref, b_ref, ys_ref, o_ref, ybuf, sem, *, tm):
    i = pl.program_id(0)
    n_steps = pl.num_programs(0)

    def issue(tile, slot):
        base = tile * tm

        def start(r, c):
            pltpu.make_async_copy(_row_tile(ys_ref, d1_ref[base + r]), _row_tile(ybuf.at[slot, 0], r),
                                  sem.at[slot]).start()
            pltpu.make_async_copy(_row_tile(ys_ref, d2_ref[base + r]), _row_tile(ybuf.at[slot, 1], r),
                                  sem.at[slot]).start(priority=1)
            return c

        lax.fori_loop(0, tm, start, 0, unroll=8)

    @pl.when(i == 0)
    def _():
        issue(0, 0)

    @pl.when(i + 1 < n_steps)
    def _():
        issue(i + 1, (i + 1) % 2)

    slot = i % 2

    def wait(r, c):
        for k in range(2):
            pltpu.make_async_copy(_row_tile(ys_ref, 0), _row_tile(ybuf.at[slot, k], 0), sem.at[slot]).wait()
        return c

    lax.fori_loop(0, tm, wait, 0, unroll=8)
    route = route_ref[...]
    gates = jnp.concatenate([route, jnp.zeros((128 - route.shape[0], tm), F32)], axis=0).T
    ff = (gates[:, 4:5] * _from_row_tiles(ybuf.at[slot, 0], tm)
          + gates[:, 5:6] * _from_row_tiles(ybuf.at[slot, 1], tm))
    o_ref[...] = _layer_norm(ALPHA * x_ref[...] + ff, g_ref[...], b_ref[...])


def _moe_ffn(x2d, route, counts, w1, w3, w2, g, b):
    n, d = x2d.shape
    n_exp, _, f = w1.shape
    tr = min(TR_MOE, n)
    tm = min(TM_MOE, n)
    n_pad = 2 * n + n_exp * tr
    n_tiles = n_pad // tr
    i32 = jnp.int32

    cnt = counts[:, 0].astype(i32)
    seg = (cnt + tr - 1) // tr * tr
    seg_end = jnp.cumsum(seg)
    seg_start = seg_end - seg
    e1, e2 = route[0].astype(i32), route[1].astype(i32)
    dest1 = seg_start[e1] + route[2].astype(i32)
    dest2 = seg_start[e2] + route[3].astype(i32)
    n_used = jnp.maximum(seg_end[-1] // tr, 1)
    tile = jnp.minimum(jnp.arange(n_tiles, dtype=i32), n_used - 1)
    tile_expert = jnp.minimum(jnp.sum(tile[:, None] * tr >= seg_end[None, :], axis=1), n_exp - 1).astype(i32)

    cparams = pltpu.CompilerParams(dimension_semantics=("arbitrary",), vmem_limit_bytes=VMEM_LIMIT)
    assert d == ROW_TILE * 128
    xs = pl.pallas_call(
        functools.partial(_dispatch_kernel, tm=tm, tr=tr),
        grid_spec=pltpu.PrefetchScalarGridSpec(
            num_scalar_prefetch=3, grid=(n // tm,),
            in_specs=[pl.BlockSpec((tm, d), lambda i, *_: (i, 0))],
            out_specs=pl.BlockSpec(memory_space=pl.ANY),
            scratch_shapes=[pltpu.VMEM((2, tm * ROW_TILE, 128), F32), pltpu.VMEM((tr * ROW_TILE, 128), F32),
                            pltpu.SemaphoreType.DMA((3,))]),
        out_shape=jax.ShapeDtypeStruct((n_pad * ROW_TILE, 128), F32),
        compiler_params=cparams,
        name="moe_dispatch",
    )(dest1, dest2, seg_end.astype(i32), x2d)

    ys = pl.pallas_call(
        functools.partial(_expert_kernel, tr=tr),
        grid_spec=pltpu.PrefetchScalarGridSpec(
            num_scalar_prefetch=3, grid=(n_tiles,),
            in_specs=[pl.BlockSpec((tr * ROW_TILE, 128), lambda i, te, blk, nu: (blk[i], 0)),
                      pl.BlockSpec((1, d, f), lambda i, te, blk, nu: (te[i], 0, 0)),
                      pl.BlockSpec((1, d, f), lambda i, te, blk, nu: (te[i], 0, 0)),
                      pl.BlockSpec((1, f, d), lambda i, te, blk, nu: (te[i], 0, 0))],
            out_specs=pl.BlockSpec((tr * ROW_TILE, 128), lambda i, te, blk, nu: (i, 0))),
        out_shape=jax.ShapeDtypeStruct((n_pad * ROW_TILE, 128), F32),
        compiler_params=cparams,
        name="moe_experts",
    )(tile_expert, tile, n_used.reshape(1), xs, w1.astype(BF16), w3.astype(BF16), w2.astype(BF16))

    row = lambda a: a.reshape(1, -1)
    return pl.pallas_call(
        functools.partial(_combine_kernel, tm=tm),
        grid_spec=pltpu.PrefetchScalarGridSpec(
            num_scalar_prefetch=2, grid=(n // tm,),
            in_specs=[pl.BlockSpec((tm, d), lambda i, *_: (i, 0)),
                      pl.BlockSpec((N_EXPERTS, tm), lambda i, *_: (0, i)),
                      pl.BlockSpec((1, d), lambda i, *_: (0, 0)),
                      pl.BlockSpec((1, d), lambda i, *_: (0, 0)),
                      pl.BlockSpec(memory_space=pl.ANY)],
            out_specs=pl.BlockSpec((tm, d), lambda i, *_: (i, 0)),
            scratch_shapes=[pltpu.VMEM((2, 2, tm * ROW_TILE, 128), F32), pltpu.SemaphoreType.DMA((2,))]),
        out_shape=jax.ShapeDtypeStruct((n, d), F32),
        compiler_params=cparams,
        name="moe_combine",
    )(dest1, dest2, x2d, route, row(g), row(b), ys)


def kernel(x, positions, w_in, mla_q_norm_g, w_uq, mla_kv_norm_g, w_ukv, fox_forget_b, conv_w, conv_b,
           conv_norm_g, conv_norm_b, mla_out_norm_g, fox_out_norm_g, w_out, ln1_g, ln1_b, dense_w1,
           dense_w3, dense_w2, router_w, expert_w1, expert_w3, expert_w2, ln2_g, ln2_b):
    batch, seq, d = x.shape
    assert d == D_MODEL and seq % TQ == 0 and seq % min(TM_IN, seq) == 0
    depth = w_in.shape[0]
    tabs = _rope_tables(positions)
    h = x.reshape(batch * seq, d)
    for layer in range(depth):
        pw = _prep_inproj_weights(w_in[layer], w_uq[layer], w_ukv[layer], fox_forget_b[layer])
        q_t, k, v_t, hc = _input_projection(
            h, tabs, pw, mla_q_norm_g[layer], mla_kv_norm_g[layer], conv_w[layer], conv_b[layer],
            conv_norm_g[layer], conv_norm_b[layer], seq)
        o = _attention(q_t, k, v_t, batch, seq)
        j = layer // 2
        if layer % 2 == 0:
            h = _dense_layer(o, hc, h, mla_out_norm_g[layer], fox_out_norm_g[layer], w_out[layer],
                             ln1_g[layer], ln1_b[layer], dense_w1[j], dense_w3[j], dense_w2[j],
                             ln2_g[layer], ln2_b[layer])
        else:
            h, route, counts = _output_projection(o, hc, h, mla_out_norm_g[layer], fox_out_norm_g[layer],
                                                  w_out[layer], ln1_g[layer], ln1_b[layer], router_w[j])
            h = _moe_ffn(h, route, counts, expert_w1[j], expert_w3[j], expert_w2[j], ln2_g[layer],
                         ln2_b[layer])
    return h.reshape(batch, seq, d)
```

```python
import functools
import math

import numpy as np
import jax
import jax.numpy as jnp
from jax import lax
from jax.experimental import pallas as pl
from jax.experimental.pallas import tpu as pltpu

F32 = jnp.float32
BF16 = jnp.bfloat16

D_MODEL = 1024
DEPTH = 4
MLA_HEADS = 8
MLA_NOPE = 64
MLA_ROPE = 32
MLA_V = 64
MLA_Q_RANK = 256
MLA_KV_RANK = 128
ROPE_THETA = 10000.0
FOX_HEADS = 4
FOX_DIM = 64
CONV_CH = 256
CONV_GROUPS = 4
CONV_WIDTH = 31
MLA_WIDTH = MLA_HEADS * MLA_V
FOX_WIDTH = FOX_HEADS * FOX_DIM
N_EXPERTS = 8
ALPHA = (2.0 * DEPTH) ** 0.25
NORM_EPS = 1e-5
LOG2E = math.log2(math.e)

HEAD_PAD = 128
N_HEADS = MLA_HEADS + FOX_HEADS
V_DIM = 64
V_ROWS = 80
CONV_HALO = 32
VMEM_LIMIT = 56 * 1024 * 1024

TQ = 512
TK = 256
HPS = 2
TM_IN = 512
TM_OUT = 512
TM_FFN = 512
TR_MOE = 512
TM_MOE = 512
ROW_TILE = 8


def _nt_dot(a, b):
    return lax.dot_general(a, b, (((1,), (1,)), ((), ())), preferred_element_type=F32)


def _dot(a, b):
    return jnp.dot(a, b, preferred_element_type=F32)


def _split2_dot(a, m_bf16):
    hi = a.astype(BF16)
    lo = (a - hi.astype(F32)).astype(BF16)
    return _dot(hi, m_bf16) + _dot(lo, m_bf16)


def _split3(a):
    hi = a.astype(BF16).astype(F32)
    r1 = a - hi
    mid = r1.astype(BF16).astype(F32)
    lo = (r1 - mid).astype(BF16).astype(F32)
    return hi, mid, lo


def _const_spec(shape):
    nd = len(shape)
    return pl.BlockSpec(shape, lambda *_: (0,) * nd, pipeline_mode=pl.Buffered(1))


def _rope_kernel(pos_ref, invf_ref, c_ref, s_ref, ct_ref, st_ref):
    pos = pos_ref[...].astype(F32)
    ang = invf_ref[...] * pos
    cos = jnp.cos(ang)
    sin = jnp.sin(ang)
    tn = pos.shape[1]
    ct = jnp.concatenate([jnp.ones((MLA_NOPE, tn), F32), cos, cos, jnp.zeros((32, tn), F32)], axis=0)
    st = jnp.concatenate([jnp.zeros((MLA_NOPE, tn), F32), sin, sin, jnp.zeros((32, tn), F32)], axis=0)
    ct_ref[...] = ct
    st_ref[...] = st
    c_ref[...] = ct.T
    s_ref[...] = st.T


def _rope_tables(positions):
    n = positions.size
    tn = min(512, n)
    inv_freq = ROPE_THETA ** (-jnp.arange(0, MLA_ROPE, 2, dtype=F32) / MLA_ROPE)
    return pl.pallas_call(
        _rope_kernel,
        grid=(n // tn,),
        in_specs=[pl.BlockSpec((1, tn), lambda i: (0, i)),
                  pl.BlockSpec((MLA_ROPE // 2, 1), lambda i: (0, 0))],
        out_specs=[pl.BlockSpec((tn, HEAD_PAD), lambda i: (i, 0)),
                   pl.BlockSpec((tn, HEAD_PAD), lambda i: (i, 0)),
                   pl.BlockSpec((HEAD_PAD, tn), lambda i: (0, i)),
                   pl.BlockSpec((HEAD_PAD, tn), lambda i: (0, i))],
        out_shape=[jax.ShapeDtypeStruct((n, HEAD_PAD), F32),
                   jax.ShapeDtypeStruct((n, HEAD_PAD), F32),
                   jax.ShapeDtypeStruct((HEAD_PAD, n), F32),
                   jax.ShapeDtypeStruct((HEAD_PAD, n), F32)],
        name="rope_tables",
    )(positions.reshape(1, n), inv_freq.reshape(-1, 1))


_A_CQ = 0
_A_CKV = _A_CQ + MLA_Q_RANK
_A_KR = _A_CKV + MLA_KV_RANK
_A_KRR = _A_KR + HEAD_PAD
_A_FK = _A_KRR + HEAD_PAD
_A_CA = _A_FK + FOX_HEADS * HEAD_PAD
_A_CG = _A_CA + CONV_CH
_A_COLS = _A_CG + CONV_CH
_AUG_ROWS = 8
_F_ROWS = 16


def _rms(x, g):
    ms = jnp.mean(jnp.square(x), axis=-1, keepdims=True)
    return x * lax.rsqrt(ms + NORM_EPS) * g


def _inproj_kernel(x_ref, c_ref, s_ref, ct_ref, st_ref, wa_ref, wfq_ref, wfv_ref, wf_ref, fb_ref,
                   gq_ref, wuq_ref, wuqr_ref, gkv_ref, wuk_ref, wuv_ref,
                   cw_ref, cb_ref, cng_ref, cnb_ref, gmat_ref,
                   qt_ref, k_ref, vt_ref, hc_ref,
                   hbuf, hsh, cbuf, fcarry, upper_ref, *, tiles_per_seq, tm):
    i = pl.program_id(0)

    @pl.when(i % tiles_per_seq == 0)
    def _():
        hbuf[0:CONV_HALO, :] = jnp.zeros((CONV_HALO, CONV_CH), F32)
        fcarry[...] = jnp.zeros_like(fcarry)
        r_i = lax.broadcasted_iota(jnp.int32, (tm, tm), 0)
        c_i = lax.broadcasted_iota(jnp.int32, (tm, tm), 1)
        upper_ref[...] = jnp.where(r_i <= c_i, 1.0, 0.0).astype(BF16)

    xb = x_ref[...].astype(BF16)
    p1 = _dot(xb, wa_ref[...])
    cos_t = c_ref[...]
    sin_t = s_ref[...]
    cos_tt = ct_ref[...]
    sin_tt = st_ref[...]

    cqn = _rms(p1[:, _A_CQ:_A_CQ + MLA_Q_RANK], gq_ref[...]).astype(BF16)
    q_t = _nt_dot(wuq_ref[...], cqn)
    q_rot_t = _nt_dot(wuqr_ref[...], cqn)
    mla_scale = (MLA_NOPE + MLA_ROPE) ** -0.5 * LOG2E
    for h in range(MLA_HEADS):
        rows = slice(h * HEAD_PAD, (h + 1) * HEAD_PAD)
        qh = (q_t[rows, :] * cos_tt + q_rot_t[rows, :] * sin_tt) * mla_scale
        for c in range(tm // TQ):
            qt_ref[c, rows, :] = qh[:, c * TQ:(c + 1) * TQ].astype(BF16)

    ckvn = _rms(p1[:, _A_CKV:_A_CKV + MLA_KV_RANK], gkv_ref[...]).astype(BF16)
    k_nope = _dot(ckvn, wuk_ref[...])
    k_rope = p1[:, _A_KR:_A_KR + HEAD_PAD] * cos_t + p1[:, _A_KRR:_A_KRR + HEAD_PAD] * sin_t
    for h in range(MLA_HEADS):
        cols = slice(h * HEAD_PAD, (h + 1) * HEAD_PAD)
        k_ref[:, cols] = (k_nope[:, cols] + k_rope).astype(BF16)
    v_t = _nt_dot(wuv_ref[...], ckvn)
    fv_t = _nt_dot(wfv_ref[...], xb)
    ones_blk = jnp.where(lax.broadcasted_iota(jnp.int32, (V_ROWS - V_DIM, tm), 0) == 0, 1.0, 0.0)
    for h in range(N_HEADS):
        src = v_t if h < MLA_HEADS else fv_t
        r0 = (h if h < MLA_HEADS else h - MLA_HEADS) * V_DIM
        vh = jnp.concatenate([src[r0:r0 + V_DIM, :], ones_blk], axis=0).astype(BF16)
        for c in range(tm // TK):
            vt_ref[c, h * V_ROWS:(h + 1) * V_ROWS, :] = vh[:, c * TK:(c + 1) * TK]

    z = _nt_dot(wf_ref[...], xb) + fb_ref[...]
    logf = (jnp.minimum(z, 0.0) - jnp.log1p(jnp.exp(-jnp.abs(z)))) * LOG2E
    limbs = jnp.concatenate(_split3(logf), axis=0).astype(BF16)
    sums = _dot(limbs, upper_ref[...])
    cum = (sums[0:_F_ROWS] + sums[_F_ROWS:2 * _F_ROWS]) + sums[2 * _F_ROWS:3 * _F_ROWS]
    f_cum = cum + fcarry[:, 0:1]
    fcarry[...] = jnp.broadcast_to(f_cum[:, tm - 1:tm], fcarry.shape)
    f_hi, f_mid, f_lo = _split3(f_cum)

    fq_t = _nt_dot(wfq_ref[...], xb)
    row8 = lax.broadcasted_iota(jnp.int32, (_AUG_ROWS, tm), 0)
    fox_scale = FOX_DIM ** -0.5 * LOG2E
    for h in range(FOX_HEADS):
        bh = lambda a: jnp.broadcast_to(a[h:h + 1, :], (_AUG_ROWS, tm))
        aug_q = jnp.where(row8 == 0, bh(f_hi), jnp.where(row8 == 1, bh(f_mid), jnp.where(
            row8 == 2, bh(f_lo), jnp.where(row8 < 6, 1.0, 0.0))))
        aug_k = jnp.where(row8 < 3, 1.0, jnp.where(row8 == 3, -bh(f_hi), jnp.where(
            row8 == 4, -bh(f_mid), jnp.where(row8 == 5, -bh(f_lo), 0.0))))
        pad = jnp.zeros((HEAD_PAD - FOX_DIM - _AUG_ROWS, tm), F32)
        qh = jnp.concatenate(
            [fq_t[h * HEAD_PAD:h * HEAD_PAD + FOX_DIM, :] * fox_scale, aug_q, pad], axis=0)
        rows = slice((MLA_HEADS + h) * HEAD_PAD, (MLA_HEADS + h + 1) * HEAD_PAD)
        for c in range(tm // TQ):
            qt_ref[c, rows, :] = qh[:, c * TQ:(c + 1) * TQ].astype(BF16)
        kaug_t = jnp.concatenate([jnp.zeros((FOX_DIM, tm), F32), aug_k, pad], axis=0)
        fk = p1[:, _A_FK + h * HEAD_PAD:_A_FK + (h + 1) * HEAD_PAD]
        k_ref[:, rows] = (fk + kaug_t.T).astype(BF16)

    a = p1[:, _A_CA:_A_CA + CONV_CH]
    g = p1[:, _A_CG:_A_CG + CONV_CH]
    hbuf[CONV_HALO:CONV_HALO + tm, :] = a * jax.nn.sigmoid(g)
    chunk = 64
    first = CONV_HALO - (CONV_WIDTH - 1)
    for r in range(1, 8):
        hsh[r - 1] = hbuf[r:r + tm + CONV_HALO - 8, :]
    for c0 in range(0, tm, chunk):
        acc = jnp.zeros((chunk, CONV_CH), F32)
        for o in range(first, first + CONV_WIDTH):
            r = o % 8
            row = c0 + o - r
            seg = hbuf[row:row + chunk, :] if r == 0 else hsh[r - 1, row:row + chunk, :]
            acc = acc + cw_ref[o - first:o - first + 1, :] * seg
        cbuf[c0:c0 + chunk, :] = acc
    hbuf[0:CONV_HALO, :] = hbuf[tm:tm + CONV_HALO, :]
    hv = cbuf[...] + cb_ref[...]
    gm = gmat_ref[...]
    mu = _split2_dot(hv, gm)
    d = hv - mu
    var = _split2_dot(d * d, gm)
    hn = d * lax.rsqrt(var + NORM_EPS) * cng_ref[...] + cnb_ref[...]
    hc_ref[...] = (hn * jax.nn.sigmoid(hn)).astype(BF16)


def _prep_inproj_weights(w_in, w_uq, w_ukv, fox_forget_b):
    o = np.cumsum((0, MLA_Q_RANK, MLA_KV_RANK, MLA_ROPE, FOX_WIDTH, FOX_WIDTH, FOX_WIDTH, FOX_HEADS,
                   2 * CONV_CH))
    w_cq, w_ckv, w_kr, w_fq, w_fk, w_fv, w_f, w_cv = (w_in[:, o[i]:o[i + 1]] for i in range(8))
    d = w_in.shape[0]
    half = MLA_ROPE // 2

    def rot_cols(w):
        return jnp.concatenate([-w[..., half:], w[..., :half]], axis=-1)

    def rope_block(w):
        return jnp.pad(w, ((0, 0), (MLA_NOPE, HEAD_PAD - MLA_NOPE - MLA_ROPE)))

    w_fk_pad = jnp.pad(w_fk.reshape(d, FOX_HEADS, FOX_DIM), ((0, 0), (0, 0), (0, HEAD_PAD - FOX_DIM)))
    wa = jnp.concatenate([w_cq, w_ckv, rope_block(w_kr), rope_block(rot_cols(w_kr)),
                          w_fk_pad.reshape(d, FOX_HEADS * HEAD_PAD), w_cv], axis=1)
    w_fq_pad = jnp.pad(w_fq.reshape(d, FOX_HEADS, FOX_DIM), ((0, 0), (0, 0), (0, HEAD_PAD - FOX_DIM)))
    wfq_t = w_fq_pad.reshape(d, FOX_HEADS * HEAD_PAD).T
    wfv_t = w_fv.T
    wf_t = jnp.pad(w_f, ((0, 0), (0, _F_ROWS - FOX_HEADS))).T
    fb = jnp.pad(fox_forget_b, (0, _F_ROWS - FOX_HEADS)).reshape(_F_ROWS, 1)

    uq = w_uq.reshape(MLA_Q_RANK, MLA_HEADS, MLA_NOPE + MLA_ROPE)
    uq_nope, uq_rope = uq[..., :MLA_NOPE], uq[..., MLA_NOPE:]
    tail = ((0, 0), (0, 0), (0, HEAD_PAD - MLA_NOPE - MLA_ROPE))
    uq_pad = jnp.pad(jnp.concatenate([uq_nope, uq_rope], axis=-1), tail)
    uq_rot_pad = jnp.pad(jnp.concatenate([jnp.zeros_like(uq_nope), rot_cols(uq_rope)], axis=-1), tail)
    wuq_t = uq_pad.reshape(MLA_Q_RANK, MLA_HEADS * HEAD_PAD).T
    wuqr_t = uq_rot_pad.reshape(MLA_Q_RANK, MLA_HEADS * HEAD_PAD).T
    ukv = w_ukv.reshape(MLA_KV_RANK, MLA_HEADS, MLA_NOPE + MLA_V)
    wuk = jnp.pad(ukv[..., :MLA_NOPE], ((0, 0), (0, 0), (0, HEAD_PAD - MLA_NOPE))).reshape(
        MLA_KV_RANK, MLA_HEADS * HEAD_PAD)
    wuv_t = ukv[..., MLA_NOPE:].reshape(MLA_KV_RANK, MLA_WIDTH).T
    bf = lambda a: a.astype(BF16)
    return dict(wa=bf(wa), wfq=bf(wfq_t), wfv=bf(wfv_t), wf=bf(wf_t), fb=fb, wuq=bf(wuq_t),
                wuqr=bf(wuqr_t), wuk=bf(wuk), wuv=bf(wuv_t))


def _input_projection(x2d, tabs, pw, gq, gkv, conv_w, conv_b, conv_ng, conv_nb, seq):
    n, d = x2d.shape
    tm = min(TM_IN, seq)
    cos_t, sin_t, cos_tt, sin_tt = tabs
    gidx = np.arange(CONV_CH) // (CONV_CH // CONV_GROUPS)
    gmat = jnp.asarray((gidx[:, None] == gidx[None, :]) / (CONV_CH // CONV_GROUPS), BF16)
    cw = jnp.pad(conv_w, ((0, 32 - CONV_WIDTH), (0, 0)))
    row = lambda a: a.reshape(1, -1)
    tok = lambda w: pl.BlockSpec((tm, w), lambda i: (i, 0))
    tok_t = lambda r: pl.BlockSpec((r, tm), lambda i: (0, i))
    consts = [pw["wa"], pw["wfq"], pw["wfv"], pw["wf"], pw["fb"], row(gq), pw["wuq"], pw["wuqr"],
              row(gkv), pw["wuk"], pw["wuv"], cw, row(conv_b), row(conv_ng), row(conv_nb), gmat]
    kern = functools.partial(_inproj_kernel, tiles_per_seq=seq // tm, tm=tm)
    return pl.pallas_call(
        kern,
        grid=(n // tm,),
        in_specs=[tok(d), tok(HEAD_PAD), tok(HEAD_PAD), tok_t(HEAD_PAD), tok_t(HEAD_PAD)]
        + [_const_spec(c.shape) for c in consts],
        out_specs=[pl.BlockSpec((tm // TQ, N_HEADS * HEAD_PAD, TQ), lambda i: (i, 0, 0)),
                   tok(N_HEADS * HEAD_PAD),
                   pl.BlockSpec((tm // TK, N_HEADS * V_ROWS, TK), lambda i: (i, 0, 0)),
                   tok(CONV_CH)],
        out_shape=[jax.ShapeDtypeStruct((n // TQ, N_HEADS * HEAD_PAD, TQ), BF16),
                   jax.ShapeDtypeStruct((n, N_HEADS * HEAD_PAD), BF16),
                   jax.ShapeDtypeStruct((n // TK, N_HEADS * V_ROWS, TK), BF16),
                   jax.ShapeDtypeStruct((n, CONV_CH), BF16)],
        scratch_shapes=[pltpu.VMEM((CONV_HALO + tm, CONV_CH), F32),
                        pltpu.VMEM((7, tm + CONV_HALO - 8, CONV_CH), F32),
                        pltpu.VMEM((tm, CONV_CH), F32),
                        pltpu.VMEM((_F_ROWS, 128), F32),
                        pltpu.VMEM((tm, tm), BF16)],
        compiler_params=pltpu.CompilerParams(dimension_semantics=("arbitrary",),
                                             vmem_limit_bytes=VMEM_LIMIT),
        name="input_projection",
    )(x2d, cos_t, sin_t, cos_tt, sin_tt, *consts)


def _attn_kernel(qt_ref, k_ref, vt_ref, o_ref, *scratch, n_tiles):
    assert TQ == 2 * TK
    s_ref = (scratch[0:HPS], scratch[HPS:2 * HPS])
    sd_ref = scratch[2 * HPS:3 * HPS]
    p_ref = (scratch[3 * HPS:4 * HPS], scratch[4 * HPS:5 * HPS])
    pd_ref = scratch[5 * HPS:6 * HPS]
    acc_bufs = (scratch[6 * HPS:7 * HPS], scratch[7 * HPS:8 * HPS])
    diff_ref = scratch[8 * HPS]
    diff_ref[...] = (lax.broadcasted_iota(jnp.int32, (TK, TQ), 1)
                     - lax.broadcasted_iota(jnp.int32, (TK, TQ), 0))
    for h in range(HPS):
        p_ref[1][h][...] = jnp.zeros_like(p_ref[1][h])
        for par in range(2):
            acc_bufs[par][h][...] = jnp.ones_like(acc_bufs[par][h])

    def tile_scores(tile, j, slot):
        row0 = pl.multiple_of(j * TK, TK)
        block_max = []
        for h in range(HPS):
            s = _dot(k_ref[pl.ds(row0, TK), h * HEAD_PAD:(h + 1) * HEAD_PAD],
                     qt_ref[tile, h * HEAD_PAD:(h + 1) * HEAD_PAD, :])
            s_ref[slot][h][...] = s
            block_max.append(jnp.max(s, axis=0, keepdims=True))
        return block_max

    def last_diag_scores(tile):
        row0 = pl.multiple_of((2 * tile + 1) * TK, TK)
        for h in range(HPS):
            sd_ref[h][...] = _dot(k_ref[pl.ds(row0, TK), h * HEAD_PAD:(h + 1) * HEAD_PAD],
                                  qt_ref[tile, h * HEAD_PAD:(h + 1) * HEAD_PAD, TK:])

    def finalize(tile, par):
        out_t = jnp.concatenate([acc_bufs[par][h][0:V_DIM, :] / acc_bufs[par][h][V_DIM:V_DIM + 1, :]
                                 for h in range(HPS)], axis=0)
        o_ref[pl.ds(pl.multiple_of(tile * TQ, TQ), TQ), :] = out_t.T.astype(o_ref.dtype)

    def q_tile(i, par, bm0):
        acc_ref = acc_bufs[par]
        scores = functools.partial(tile_scores, i)

        def softmax(slot, m, block_max):
            m_new = [jnp.maximum(m[h], block_max[h]) for h in range(HPS)]
            for h in range(HPS):
                p_ref[slot][h][...] = jnp.exp2(s_ref[slot][h][...] - m_new[h]).astype(BF16)
            return m_new, [jnp.exp2(m[h] - m_new[h]) for h in range(HPS)]

        def values(j, slot, alpha, gate=None):
            for h in range(HPS):
                pv = _dot(vt_ref[j, h * V_ROWS:(h + 1) * V_ROWS, :], p_ref[slot][h][...])
                acc_ref[h][...] = alpha[h] * acc_ref[h][...] + (pv if gate is None else gate * pv)

        m0 = [jnp.full((1, TQ), -1e30, F32)] * HPS
        a0 = [jnp.zeros((1, TQ), F32)] * HPS

        def pair(u, state):
            m, alpha, bm_t = list(state[0:HPS]), list(state[HPS:2 * HPS]), list(state[2 * HPS:3 * HPS])
            t = 2 * u
            m, alpha_t = softmax(0, m, bm_t)
            values(jnp.maximum(t - 1, 0), 1, alpha, jnp.where(t > 0, 1.0, 0.0))
            bm_t1 = scores(t + 1, 1)
            m, alpha_t1 = softmax(1, m, bm_t1)
            values(t, 0, alpha_t)
            bm_t2 = scores(t + 2, 0)
            return (*m, *alpha_t1, *bm_t2)

        state = lax.fori_loop(0, i // 2, lambda v, st: pair(2 * v + 1, pair(2 * v, st)), (*m0, *a0, *bm0))
        state = lax.fori_loop(i - i % 2, i, pair, state)
        m, alpha = list(state[0:HPS]), list(state[HPS:2 * HPS])
        d0 = 2 * i
        values(jnp.maximum(d0 - 1, 0), 1, alpha, jnp.where(i > 0, 1.0, 0.0))
        finalize(jnp.where(i > 0, i - 1, n_tiles - 1), 1 - par)
        nxt = jnp.minimum(i + 1, n_tiles - 1)

        m_d0, alpha_d0 = [], []
        for h in range(HPS):
            s = s_ref[0][h][...]
            s = jnp.concatenate([jnp.where(diff_ref[:, 0:TK] >= 0, s[:, 0:TK], -jnp.inf), s[:, TK:]], axis=1)
            m_d0.append(jnp.maximum(m[h], jnp.max(s, axis=0, keepdims=True)))
            p_ref[0][h][...] = jnp.exp2(s - m_d0[h]).astype(BF16)
            alpha_d0.append(jnp.exp2(m[h] - m_d0[h]))
        bm_next = tile_scores(nxt, 0, 0)

        alpha_d1 = []
        for h in range(HPS):
            s = jnp.where(diff_ref[:, 0:TK] >= 0, sd_ref[h][...], -jnp.inf)
            m_old = m_d0[h][:, TK:]
            m_new = jnp.maximum(m_old, jnp.max(s, axis=0, keepdims=True))
            pd_ref[h][...] = jnp.exp2(s - m_new).astype(BF16)
            alpha_d1.append(jnp.exp2(m_old - m_new))
        last_diag_scores(nxt)

        values(d0, 0, alpha_d0)
        for h in range(HPS):
            acc_ref[h][:, TK:] = alpha_d1[h] * acc_ref[h][:, TK:] + _dot(
                vt_ref[d0 + 1, h * V_ROWS:(h + 1) * V_ROWS, :], pd_ref[h][...])
        return tuple(bm_next)

    last_diag_scores(0)
    bm = lax.fori_loop(0, n_tiles // 2, lambda a, st: q_tile(2 * a + 1, 1, q_tile(2 * a, 0, st)),
                       tuple(tile_scores(0, 0, 0)))
    if n_tiles % 2:
        q_tile(n_tiles - 1, 0, bm)
    finalize(n_tiles - 1, (n_tiles - 1) % 2)


def _attention(q_t, k, v_t, batch, seq):
    n = k.shape[0]
    groups = N_HEADS // HPS
    return pl.pallas_call(
        functools.partial(_attn_kernel, n_tiles=seq // TQ),
        grid=(batch, groups),
        in_specs=[pl.BlockSpec((seq // TQ, HPS * HEAD_PAD, TQ), lambda b, p: (b, p, 0)),
                  pl.BlockSpec((seq, HPS * HEAD_PAD), lambda b, p: (b, p)),
                  pl.BlockSpec((seq // TK, HPS * V_ROWS, TK), lambda b, p: (b, p, 0))],
        out_specs=pl.BlockSpec((seq, HPS * V_DIM), lambda b, p: (b, p)),
        out_shape=jax.ShapeDtypeStruct((n, N_HEADS * V_DIM), BF16),
        scratch_shapes=[pltpu.VMEM((TK, TQ), F32)] * (2 * HPS) + [pltpu.VMEM((TK, TK), F32)] * HPS
        + [pltpu.VMEM((TK, TQ), BF16)] * (2 * HPS) + [pltpu.VMEM((TK, TK), BF16)] * HPS
        + [pltpu.VMEM((V_ROWS, TQ), F32)] * (2 * HPS) + [pltpu.VMEM((TK, TQ), jnp.int32)],
        compiler_params=pltpu.CompilerParams(dimension_semantics=("arbitrary", "arbitrary"),
                                             vmem_limit_bytes=VMEM_LIMIT),
        name="attention",
    )(q_t, k, v_t)


def _layer_norm(x, g, b):
    mu = jnp.mean(x, axis=-1, keepdims=True)
    d = x - mu
    var = jnp.mean(jnp.square(d), axis=-1, keepdims=True)
    return d * lax.rsqrt(var + NORM_EPS) * g + b


def _mix_and_norm(o_ref, hc_ref, x_ref, gm_ref, gf_ref, wo_ref, g1_ref, b1_ref):
    o = o_ref[...].astype(F32)
    mla = _rms(o[:, :MLA_WIDTH], gm_ref[...])
    fox = _rms(o[:, MLA_WIDTH:], gf_ref[...])
    mixed = jnp.concatenate([mla.astype(BF16), fox.astype(BF16), hc_ref[...]], axis=-1)
    y = _dot(mixed, wo_ref[...])
    return _layer_norm(ALPHA * x_ref[...] + y, g1_ref[...], b1_ref[...])


def _outproj_kernel(o_ref, hc_ref, x_ref, gm_ref, gf_ref, wo_ref, g1_ref, b1_ref, *rest, with_router):
    if with_router:
        rw_ref, x1_ref, route_ref, counts_ref, cnt_ref, upper_ref = rest
    else:
        (x1_ref,) = rest
    x1 = _mix_and_norm(o_ref, hc_ref, x_ref, gm_ref, gf_ref, wo_ref, g1_ref, b1_ref)
    x1_ref[...] = x1
    if with_router:
        rw = rw_ref[...]
        x_hi = x1.astype(BF16)
        x_lo = (x1 - x_hi.astype(F32)).astype(BF16)
        w_hi = rw.astype(BF16)
        w_lo = (rw - w_hi.astype(F32)).astype(BF16)
        both = _dot(x_hi, jnp.concatenate([w_hi, w_lo], axis=1))
        logits = both[:, :128] + (_dot(x_lo, w_hi) + both[:, 128:])
        tm = logits.shape[0]
        lg = logits.T[0:N_EXPERTS, :]
        row = lax.broadcasted_iota(jnp.int32, lg.shape, 0)
        v1 = jnp.max(lg, axis=0, keepdims=True)
        i1 = jnp.min(jnp.where(lg == v1, row, N_EXPERTS), axis=0, keepdims=True)
        rest_l = jnp.where(row == i1, -jnp.inf, lg)
        v2 = jnp.max(rest_l, axis=0, keepdims=True)
        i2 = jnp.min(jnp.where(rest_l == v2, row, N_EXPERTS), axis=0, keepdims=True)
        e2 = jnp.exp(v2 - v1)
        den = 1.0 + e2

        @pl.when(pl.program_id(0) == 0)
        def _():
            cnt_ref[...] = jnp.zeros_like(cnt_ref)
            r_i = lax.broadcasted_iota(jnp.int32, (tm, tm), 0)
            c_i = lax.broadcasted_iota(jnp.int32, (tm, tm), 1)
            upper_ref[...] = jnp.where(r_i < c_i, 1.0, 0.0).astype(BF16)

        sel = jnp.where(row == i1, 1.0, jnp.where(row == i2, 1.0, 0.0))
        sel16 = jnp.concatenate([sel, jnp.zeros_like(sel)], axis=0).astype(BF16)
        before = cnt_ref[:, 0:1]
        rank = _dot(sel16, upper_ref[...])[0:N_EXPERTS, :] + before
        total = before + jnp.sum(sel, axis=1, keepdims=True)
        cnt_ref[...] = jnp.broadcast_to(total, cnt_ref.shape)
        counts_ref[...] = jnp.broadcast_to(total, counts_ref.shape)
        r1 = jnp.sum(jnp.where(row == i1, rank, 0.0), axis=0, keepdims=True)
        r2 = jnp.sum(jnp.where(row == i2, rank, 0.0), axis=0, keepdims=True)
        rows = (i1.astype(F32), i2.astype(F32), r1, r2, 1.0 / den, e2 / den)
        route = jnp.zeros(lg.shape, F32)
        for c, v in enumerate(rows):
            route = jnp.where(row == c, v, route)
        route_ref[...] = route


def _output_projection(o, hc, x2d, gm, gf, w_out, g1, b1, router_w=None):
    n, d = x2d.shape
    tm = min(TM_OUT, n)
    row = lambda a: a.reshape(1, -1)
    tok = lambda w: pl.BlockSpec((tm, w), lambda i: (i, 0))
    consts = [row(gm), row(gf), w_out.astype(BF16), row(g1), row(b1)]
    out_specs = [tok(d)]
    out_shape = [jax.ShapeDtypeStruct((n, d), F32)]
    scratch = []
    if router_w is not None:
        consts.append(jnp.pad(router_w, ((0, 0), (0, 128 - N_EXPERTS))))
        out_specs += [pl.BlockSpec((N_EXPERTS, tm), lambda i: (0, i)),
                      pl.BlockSpec((N_EXPERTS, 128), lambda i: (0, 0))]
        out_shape += [jax.ShapeDtypeStruct((N_EXPERTS, n), F32), jax.ShapeDtypeStruct((N_EXPERTS, 128), F32)]
        scratch = [pltpu.VMEM((N_EXPERTS, 128), F32), pltpu.VMEM((tm, tm), BF16)]
    return pl.pallas_call(
        functools.partial(_outproj_kernel, with_router=router_w is not None),
        grid=(n // tm,),
        in_specs=[tok(o.shape[1]), tok(CONV_CH), tok(d)] + [_const_spec(c.shape) for c in consts],
        out_specs=out_specs,
        out_shape=out_shape,
        scratch_shapes=scratch,
        compiler_params=pltpu.CompilerParams(dimension_semantics=("arbitrary",),
                                             vmem_limit_bytes=VMEM_LIMIT),
        name="output_projection",
    )(o, hc, x2d, *consts)


def _swiglu_tile(xb, w1, w3, w2):
    h1 = _dot(xb, w1)
    h3 = _dot(xb, w3)
    hid = (h1 * jax.nn.sigmoid(h1) * h3).astype(BF16)
    return _dot(hid, w2)


def _dense_layer_kernel(o_ref, hc_ref, x_ref, gm_ref, gf_ref, wo_ref, g1_ref, b1_ref,
                        w1_ref, w3_ref, w2_ref, g2_ref, b2_ref, out_ref, *, f_chunk):
    x1 = _mix_and_norm(o_ref, hc_ref, x_ref, gm_ref, gf_ref, wo_ref, g1_ref, b1_ref)
    xb = x1.astype(BF16)
    ff = None
    for c0 in range(0, w1_ref.shape[1], f_chunk):
        part = _swiglu_tile(xb, w1_ref[:, c0:c0 + f_chunk], w3_ref[:, c0:c0 + f_chunk],
                            w2_ref[c0:c0 + f_chunk, :])
        ff = part if ff is None else ff + part
    out_ref[...] = _layer_norm(ALPHA * x1 + ff, g2_ref[...], b2_ref[...])


def _dense_layer(o, hc, x2d, gm, gf, w_out, g1, b1, w1, w3, w2, g2, b2):
    n, d = x2d.shape
    tm = min(TM_FFN, n)
    f = w1.shape[1]
    f_chunk = f // 2 if (f // 2) % 128 == 0 else f
    row = lambda a: a.reshape(1, -1)
    tok = lambda w: pl.BlockSpec((tm, w), lambda i: (i, 0))
    consts = [row(gm), row(gf), w_out.astype(BF16), row(g1), row(b1),
              w1.astype(BF16), w3.astype(BF16), w2.astype(BF16), row(g2), row(b2)]
    return pl.pallas_call(
        functools.partial(_dense_layer_kernel, f_chunk=f_chunk),
        grid=(n // tm,),
        in_specs=[tok(o.shape[1]), tok(CONV_CH), tok(d)] + [_const_spec(c.shape) for c in consts],
        out_specs=tok(d),
        out_shape=jax.ShapeDtypeStruct((n, d), F32),
        compiler_params=pltpu.CompilerParams(dimension_semantics=("arbitrary",),
                                             vmem_limit_bytes=VMEM_LIMIT),
        name="dense_layer",
    )(o, hc, x2d, *consts)


def _to_row_tiles(ref, x):
    for c in range(ROW_TILE):
        ref[pl.ds(c, x.shape[0], stride=ROW_TILE), :] = x[:, c * 128:(c + 1) * 128]


def _from_row_tiles(ref, t):
    return jnp.concatenate([ref[pl.ds(c, t, stride=ROW_TILE), :] for c in range(ROW_TILE)], axis=-1)


def _row_tile(ref, r):
    return ref.at[pl.ds(pl.multiple_of(r * ROW_TILE, ROW_TILE), ROW_TILE)]


def _dispatch_kernel(d1_ref, d2_ref, se_ref, x_ref, xs_ref, xr, zbuf, sem, *, tm, tr):
    i = pl.program_id(0)

    @pl.when(i == 0)
    def _():
        zbuf[...] = jnp.zeros_like(zbuf)
        for e in range(N_EXPERTS):
            end = se_ref[e]
            start_e = se_ref[e - 1] if e else 0

            for first, live in ((end - tr, end > start_e),
                                (se_ref[N_EXPERTS - 1] + e * tr,
                                 (se_ref[N_EXPERTS - 1] + e * tr) * ROW_TILE < xs_ref.shape[0])):
                @pl.when(live)
                def _():
                    rows = pl.ds(pl.multiple_of(first * ROW_TILE, ROW_TILE), tr * ROW_TILE)
                    fill = pltpu.make_async_copy(zbuf, xs_ref.at[rows], sem.at[2])
                    fill.start()
                    fill.wait()

    slot = i % 2
    _to_row_tiles(xr.at[slot], x_ref[...])
    base = i * tm

    def start(r, c):
        src = _row_tile(xr.at[slot], r)
        pltpu.make_async_copy(src, _row_tile(xs_ref, d1_ref[base + r]), sem.at[slot]).start()
        pltpu.make_async_copy(src, _row_tile(xs_ref, d2_ref[base + r]), sem.at[slot]).start(priority=1)
        return c

    def wait_step(s):
        def wait(r, c):
            for _ in range(2):
                pltpu.make_async_copy(_row_tile(xr.at[s], 0), _row_tile(xs_ref, 0), sem.at[s]).wait()
            return c

        lax.fori_loop(0, tm, wait, 0, unroll=8)

    lax.fori_loop(0, tm, start, 0, unroll=8)

    @pl.when(i > 0)
    def _():
        wait_step(1 - slot)

    @pl.when(i == pl.num_programs(0) - 1)
    def _():
        wait_step(slot)


def _expert_kernel(te_ref, blk_ref, nu_ref, xs_ref, w1_ref, w3_ref, w2_ref, ys_ref, *, tr):
    del te_ref, blk_ref
    used = pl.program_id(0) < nu_ref[0]

    @pl.when(used)
    def _():
        xb = _from_row_tiles(xs_ref, tr).astype(BF16)
        _to_row_tiles(ys_ref, _swiglu_tile(xb, w1_ref[0], w3_ref[0], w2_ref[0]))

    @pl.when(jnp.logical_not(used))
    def _():
        ys_ref[...] = jnp.zeros_like(ys_ref)


def _combine_kernel(d1_ref, d2_ref, x_ref, route_ref, g_ref, b_ref, ys_ref, o_ref, ybuf, sem, *, tm):
    i = pl.program_id(0)
    n_steps = pl.num_programs(0)

    def issue(tile, slot):
        base = tile * tm

        def start(r, c):
            pltpu.make_async_copy(_row_tile(ys_ref, d1_ref[base + r]), _row_tile(ybuf.at[slot, 0], r),
                                  sem.at[slot]).start()
            pltpu.make_async_copy(_row_tile(ys_ref, d2_ref[base + r]), _row_tile(ybuf.at[slot, 1], r),
                                  sem.at[slot]).start(priority=1)
            return c

        lax.fori_loop(0, tm, start, 0, unroll=8)

    @pl.when(i == 0)
    def _():
        issue(0, 0)

    @pl.when(i + 1 < n_steps)
    def _():
        issue(i + 1, (i + 1) % 2)

    slot = i % 2

    def wait(r, c):
        for k in range(2):
            pltpu.make_async_copy(_row_tile(ys_ref, 0), _row_tile(ybuf.at[slot, k], 0), sem.at[slot]).wait()
        return c

    lax.fori_loop(0, tm, wait, 0, unroll=8)
    route = route_ref[...]
    gates = jnp.concatenate([route, jnp.zeros((128 - route.shape[0], tm), F32)], axis=0).T
    ff = (gates[:, 4:5] * _from_row_tiles(ybuf.at[slot, 0], tm)
          + gates[:, 5:6] * _from_row_tiles(ybuf.at[slot, 1], tm))
    o_ref[...] = _layer_norm(ALPHA * x_ref[...] + ff, g_ref[...], b_ref[...])


def _moe_ffn(x2d, route, counts, w1, w3, w2, g, b):
    n, d = x2d.shape
    n_exp, _, f = w1.shape
    tr = min(TR_MOE, n)
    tm = min(TM_MOE, n)
    n_pad = 2 * n + n_exp * tr
    n_tiles = n_pad // tr
    i32 = jnp.int32

    cnt = counts[:, 0].astype(i32)
    seg = (cnt + tr - 1) // tr * tr
    seg_end = jnp.cumsum(seg)
    seg_start = seg_end - seg
    e1, e2 = route[0].astype(i32), route[1].astype(i32)
    dest1 = seg_start[e1] + route[2].astype(i32)
    dest2 = seg_start[e2] + route[3].astype(i32)
    n_used = jnp.maximum(seg_end[-1] // tr, 1)
    tile = jnp.minimum(jnp.arange(n_tiles, dtype=i32), n_used - 1)
    tile_expert = jnp.minimum(jnp.sum(tile[:, None] * tr >= seg_end[None, :], axis=1), n_exp - 1).astype(i32)

    cparams = pltpu.CompilerParams(dimension_semantics=("arbitrary",), vmem_limit_bytes=VMEM_LIMIT)
    assert d == ROW_TILE * 128
    xs = pl.pallas_call(
        functools.partial(_dispatch_kernel, tm=tm, tr=tr),
        grid_spec=pltpu.PrefetchScalarGridSpec(
            num_scalar_prefetch=3, grid=(n // tm,),
            in_specs=[pl.BlockSpec((tm, d), lambda i, *_: (i, 0))],
            out_specs=pl.BlockSpec(memory_space=pl.ANY),
            scratch_shapes=[pltpu.VMEM((2, tm * ROW_TILE, 128), F32), pltpu.VMEM((tr * ROW_TILE, 128), F32),
                            pltpu.SemaphoreType.DMA((3,))]),
        out_shape=jax.ShapeDtypeStruct((n_pad * ROW_TILE, 128), F32),
        compiler_params=cparams,
        name="moe_dispatch",
    )(dest1, dest2, seg_end.astype(i32), x2d)

    ys = pl.pallas_call(
        functools.partial(_expert_kernel, tr=tr),
        grid_spec=pltpu.PrefetchScalarGridSpec(
            num_scalar_prefetch=3, grid=(n_tiles,),
            in_specs=[pl.BlockSpec((tr * ROW_TILE, 128), lambda i, te, blk, nu: (blk[i], 0)),
                      pl.BlockSpec((1, d, f), lambda i, te, blk, nu: (te[i], 0, 0)),
                      pl.BlockSpec((1, d, f), lambda i, te, blk, nu: (te[i], 0, 0)),
                      pl.BlockSpec((1, f, d), lambda i, te, blk, nu: (te[i], 0, 0))],
            out_specs=pl.BlockSpec((tr * ROW_TILE, 128), lambda i, te, blk, nu: (i, 0))),
        out_shape=jax.ShapeDtypeStruct((n_pad * ROW_TILE, 128), F32),
        compiler_params=cparams,
        name="moe_experts",
    )(tile_expert, tile, n_used.reshape(1), xs, w1.astype(BF16), w3.astype(BF16), w2.astype(BF16))

    row = lambda a: a.reshape(1, -1)
    return pl.pallas_call(
        functools.partial(_combine_kernel, tm=tm),
        grid_spec=pltpu.PrefetchScalarGridSpec(
            num_scalar_prefetch=2, grid=(n // tm,),
            in_specs=[pl.BlockSpec((tm, d), lambda i, *_: (i, 0)),
                      pl.BlockSpec((N_EXPERTS, tm), lambda i, *_: (0, i)),
                      pl.BlockSpec((1, d), lambda i, *_: (0, 0)),
                      pl.BlockSpec((1, d), lambda i, *_: (0, 0)),
                      pl.BlockSpec(memory_space=pl.ANY)],
            out_specs=pl.BlockSpec((tm, d), lambda i, *_: (i, 0)),
            scratch_shapes=[pltpu.VMEM((2, 2, tm * ROW_TILE, 128), F32), pltpu.SemaphoreType.DMA((2,))]),
        out_shape=jax.ShapeDtypeStruct((n, d), F32),
        compiler_params=cparams,
        name="moe_combine",
    )(dest1, dest2, x2d, route, row(g), row(b), ys)


def kernel(x, positions, w_in, mla_q_norm_g, w_uq, mla_kv_norm_g, w_ukv, fox_forget_b, conv_w, conv_b,
           conv_norm_g, conv_norm_b, mla_out_norm_g, fox_out_norm_g, w_out, ln1_g, ln1_b, dense_w1,
           dense_w3, dense_w2, router_w, expert_w1, expert_w3, expert_w2, ln2_g, ln2_b):
    batch, seq, d = x.shape
    assert d == D_MODEL and seq % TQ == 0 and seq % min(TM_IN, seq) == 0
    depth = w_in.shape[0]
    tabs = _rope_tables(positions)
    h = x.reshape(batch * seq, d)
    pw_all = jax.vmap(_prep_inproj_weights)(w_in, w_uq, w_ukv, fox_forget_b)
    w_out, dense_w1, dense_w3, dense_w2, expert_w1, expert_w3, expert_w2 = (
        w.astype(BF16) for w in (w_out, dense_w1, dense_w3, dense_w2, expert_w1, expert_w3, expert_w2))
    for layer in range(depth):
        pw = {name: w[layer] for name, w in pw_all.items()}
        q_t, k, v_t, hc = _input_projection(
            h, tabs, pw, mla_q_norm_g[layer], mla_kv_norm_g[layer], conv_w[layer], conv_b[layer],
            conv_norm_g[layer], conv_norm_b[layer], seq)
        o = _attention(q_t, k, v_t, batch, seq)
        j = layer // 2
        if layer % 2 == 0:
            h = _dense_layer(o, hc, h, mla_out_norm_g[layer], fox_out_norm_g[layer], w_out[layer],
                             ln1_g[layer], ln1_b[layer], dense_w1[j], dense_w3[j], dense_w2[j],
                             ln2_g[layer], ln2_b[layer])
        else:
            h, route, counts = _output_projection(o, hc, h, mla_out_norm_g[layer], fox_out_norm_g[layer],
                                                  w_out[layer], ln1_g[layer], ln1_b[layer], router_w[j])
            h = _moe_ffn(h, route, counts, expert_w1[j], expert_w3[j], expert_w2[j], ln2_g[layer],
                         ln2_b[layer])
    return h.reshape(batch, seq, d)
```

```python
import functools
import math

import numpy as np
import jax
import jax.numpy as jnp
from jax import lax
from jax.experimental import pallas as pl
from jax.experimental.pallas import tpu as pltpu

F32 = jnp.float32
BF16 = jnp.bfloat16

D_MODEL = 1024
DEPTH = 4
MLA_HEADS = 8
MLA_NOPE = 64
MLA_ROPE = 32
MLA_V = 64
MLA_Q_RANK = 256
MLA_KV_RANK = 128
ROPE_THETA = 10000.0
FOX_HEADS = 4
FOX_DIM = 64
CONV_CH = 256
CONV_GROUPS = 4
CONV_WIDTH = 31
MLA_WIDTH = MLA_HEADS * MLA_V
FOX_WIDTH = FOX_HEADS * FOX_DIM
N_EXPERTS = 8
ALPHA = (2.0 * DEPTH) ** 0.25
NORM_EPS = 1e-5
LOG2E = math.log2(math.e)

LANES = 128
HEAD_PAD = LANES
N_HEADS = MLA_HEADS + FOX_HEADS
V_DIM = 64
V_ROWS = 80
CONV_HALO = 32
VMEM_LIMIT = 56 * 1024 * 1024

TQ = 512
TK = 256
HPS = 2
TM_IN = 512
TM_OUT = 512
TM_FFN = 512
TR_MOE = 512
TM_MOE = 512
ROW_TILE = 8


def _nt_dot(a, b):
    return lax.dot_general(a, b, (((1,), (1,)), ((), ())), preferred_element_type=F32)


def _dot(a, b):
    return jnp.dot(a, b, preferred_element_type=F32)


def _split2_dot(a, m_bf16):
    hi = a.astype(BF16)
    lo = (a - hi.astype(F32)).astype(BF16)
    return _dot(hi, m_bf16) + _dot(lo, m_bf16)


def _split3(a):
    hi = a.astype(BF16).astype(F32)
    r1 = a - hi
    mid = r1.astype(BF16).astype(F32)
    lo = (r1 - mid).astype(BF16).astype(F32)
    return hi, mid, lo


def _const_spec(shape):
    nd = len(shape)
    return pl.BlockSpec(shape, lambda *_: (0,) * nd, pipeline_mode=pl.Buffered(1))


def _rope_kernel(pos_ref, invf_ref, c_ref, s_ref, ct_ref, st_ref):
    pos = pos_ref[...].astype(F32)
    ang = invf_ref[...] * pos
    cos = jnp.cos(ang)
    sin = jnp.sin(ang)
    tn = pos.shape[1]
    ct = jnp.concatenate([jnp.ones((MLA_NOPE, tn), F32), cos, cos, jnp.zeros((32, tn), F32)], axis=0)
    st = jnp.concatenate([jnp.zeros((MLA_NOPE, tn), F32), sin, sin, jnp.zeros((32, tn), F32)], axis=0)
    ct_ref[...] = ct
    st_ref[...] = st
    c_ref[...] = ct.T
    s_ref[...] = st.T


def _rope_tables(positions):
    n = positions.size
    tn = min(512, n)
    inv_freq = ROPE_THETA ** (-jnp.arange(0, MLA_ROPE, 2, dtype=F32) / MLA_ROPE)
    return pl.pallas_call(
        _rope_kernel,
        grid=(n // tn,),
        in_specs=[pl.BlockSpec((1, tn), lambda i: (0, i)),
                  pl.BlockSpec((MLA_ROPE // 2, 1), lambda i: (0, 0))],
        out_specs=[pl.BlockSpec((tn, HEAD_PAD), lambda i: (i, 0)),
                   pl.BlockSpec((tn, HEAD_PAD), lambda i: (i, 0)),
                   pl.BlockSpec((HEAD_PAD, tn), lambda i: (0, i)),
                   pl.BlockSpec((HEAD_PAD, tn), lambda i: (0, i))],
        out_shape=[jax.ShapeDtypeStruct((n, HEAD_PAD), F32),
                   jax.ShapeDtypeStruct((n, HEAD_PAD), F32),
                   jax.ShapeDtypeStruct((HEAD_PAD, n), F32),
                   jax.ShapeDtypeStruct((HEAD_PAD, n), F32)],
        name="rope_tables",
    )(positions.reshape(1, n), inv_freq.reshape(-1, 1))


_A_CQ = 0
_A_CKV = _A_CQ + MLA_Q_RANK
_A_KR = _A_CKV + MLA_KV_RANK
_A_KRR = _A_KR + HEAD_PAD
_A_FK = _A_KRR + HEAD_PAD
_A_CA = _A_FK + FOX_HEADS * HEAD_PAD
_A_CG = _A_CA + CONV_CH
_A_COLS = _A_CG + CONV_CH
_AUG_ROWS = 8
_F_ROWS = 16


def _rms(x, g):
    ms = jnp.mean(jnp.square(x), axis=-1, keepdims=True)
    return x * lax.rsqrt(ms + NORM_EPS) * g


def _inproj_kernel(x_ref, c_ref, s_ref, ct_ref, st_ref, wa_ref, wfq_ref, wfv_ref, wf_ref, fb_ref,
                   gq_ref, wuq_ref, wuqr_ref, gkv_ref, wuk_ref, wuv_ref,
                   cw_ref, cb_ref, cng_ref, cnb_ref, gmat_ref,
                   qt_ref, k_ref, vt_ref, hc_ref,
                   hbuf, hsh, cbuf, fcarry, upper_ref, *, tiles_per_seq, tm):
    i = pl.program_id(0)

    @pl.when(i % tiles_per_seq == 0)
    def _():
        hbuf[0:CONV_HALO, :] = jnp.zeros((CONV_HALO, CONV_CH), F32)
        fcarry[...] = jnp.zeros_like(fcarry)
        r_i = lax.broadcasted_iota(jnp.int32, (tm, tm), 0)
        c_i = lax.broadcasted_iota(jnp.int32, (tm, tm), 1)
        upper_ref[...] = jnp.where(r_i <= c_i, 1.0, 0.0).astype(BF16)

    xb = x_ref[...].astype(BF16)
    p1 = _dot(xb, wa_ref[...])
    cos_t = c_ref[...]
    sin_t = s_ref[...]
    cos_tt = ct_ref[...]
    sin_tt = st_ref[...]

    cqn = _rms(p1[:, _A_CQ:_A_CQ + MLA_Q_RANK], gq_ref[...]).astype(BF16)
    q_t = _nt_dot(wuq_ref[...], cqn)
    q_rot_t = _nt_dot(wuqr_ref[...], cqn)
    mla_scale = (MLA_NOPE + MLA_ROPE) ** -0.5 * LOG2E
    for h in range(MLA_HEADS):
        rows = slice(h * HEAD_PAD, (h + 1) * HEAD_PAD)
        qh = (q_t[rows, :] * cos_tt + q_rot_t[rows, :] * sin_tt) * mla_scale
        for c in range(tm // TQ):
            qt_ref[c, rows, :] = qh[:, c * TQ:(c + 1) * TQ].astype(BF16)

    ckvn = _rms(p1[:, _A_CKV:_A_CKV + MLA_KV_RANK], gkv_ref[...]).astype(BF16)
    k_nope = _dot(ckvn, wuk_ref[...])
    k_rope = p1[:, _A_KR:_A_KR + HEAD_PAD] * cos_t + p1[:, _A_KRR:_A_KRR + HEAD_PAD] * sin_t
    for h in range(MLA_HEADS):
        cols = slice(h * HEAD_PAD, (h + 1) * HEAD_PAD)
        k_ref[:, cols] = (k_nope[:, cols] + k_rope).astype(BF16)
    v_t = _nt_dot(wuv_ref[...], ckvn)
    fv_t = _nt_dot(wfv_ref[...], xb)
    ones_blk = jnp.where(lax.broadcasted_iota(jnp.int32, (V_ROWS - V_DIM, tm), 0) == 0, 1.0, 0.0)
    for h in range(N_HEADS):
        src = v_t if h < MLA_HEADS else fv_t
        r0 = (h if h < MLA_HEADS else h - MLA_HEADS) * V_DIM
        vh = jnp.concatenate([src[r0:r0 + V_DIM, :], ones_blk], axis=0).astype(BF16)
        for c in range(tm // TK):
            vt_ref[c, h * V_ROWS:(h + 1) * V_ROWS, :] = vh[:, c * TK:(c + 1) * TK]

    z = _nt_dot(wf_ref[...], xb) + fb_ref[...]
    logf = (jnp.minimum(z, 0.0) - jnp.log1p(jnp.exp(-jnp.abs(z)))) * LOG2E
    limbs = jnp.concatenate(_split3(logf), axis=0).astype(BF16)
    sums = _dot(limbs, upper_ref[...])
    cum = (sums[0:_F_ROWS] + sums[_F_ROWS:2 * _F_ROWS]) + sums[2 * _F_ROWS:3 * _F_ROWS]
    f_cum = cum + fcarry[:, 0:1]
    fcarry[...] = jnp.broadcast_to(f_cum[:, tm - 1:tm], fcarry.shape)
    f_hi, f_mid, f_lo = _split3(f_cum)

    fq_t = _nt_dot(wfq_ref[...], xb)
    row8 = lax.broadcasted_iota(jnp.int32, (_AUG_ROWS, tm), 0)
    fox_scale = FOX_DIM ** -0.5 * LOG2E
    for h in range(FOX_HEADS):
        bh = lambda a: jnp.broadcast_to(a[h:h + 1, :], (_AUG_ROWS, tm))
        aug_q = jnp.where(row8 == 0, bh(f_hi), jnp.where(row8 == 1, bh(f_mid), jnp.where(
            row8 == 2, bh(f_lo), jnp.where(row8 < 6, 1.0, 0.0))))
        aug_k = jnp.where(row8 < 3, 1.0, jnp.where(row8 == 3, -bh(f_hi), jnp.where(
            row8 == 4, -bh(f_mid), jnp.where(row8 == 5, -bh(f_lo), 0.0))))
        pad = jnp.zeros((HEAD_PAD - FOX_DIM - _AUG_ROWS, tm), F32)
        qh = jnp.concatenate(
            [fq_t[h * HEAD_PAD:h * HEAD_PAD + FOX_DIM, :] * fox_scale, aug_q, pad], axis=0)
        rows = slice((MLA_HEADS + h) * HEAD_PAD, (MLA_HEADS + h + 1) * HEAD_PAD)
        for c in range(tm // TQ):
            qt_ref[c, rows, :] = qh[:, c * TQ:(c + 1) * TQ].astype(BF16)
        kaug_t = jnp.concatenate([jnp.zeros((FOX_DIM, tm), F32), aug_k, pad], axis=0)
        fk = p1[:, _A_FK + h * HEAD_PAD:_A_FK + (h + 1) * HEAD_PAD]
        k_ref[:, rows] = (fk + kaug_t.T).astype(BF16)

    a = p1[:, _A_CA:_A_CA + CONV_CH]
    g = p1[:, _A_CG:_A_CG + CONV_CH]
    hbuf[CONV_HALO:CONV_HALO + tm, :] = a * jax.nn.sigmoid(g)
    chunk = 64
    first = CONV_HALO - (CONV_WIDTH - 1)
    for r in range(1, 8):
        hsh[r - 1] = hbuf[r:r + tm + CONV_HALO - 8, :]
    for c0 in range(0, tm, chunk):
        acc = jnp.zeros((chunk, CONV_CH), F32)
        for o in range(first, first + CONV_WIDTH):
            r = o % 8
            row = c0 + o - r
            seg = hbuf[row:row + chunk, :] if r == 0 else hsh[r - 1, row:row + chunk, :]
            acc = acc + cw_ref[o - first:o - first + 1, :] * seg
        cbuf[c0:c0 + chunk, :] = acc
    hbuf[0:CONV_HALO, :] = hbuf[tm:tm + CONV_HALO, :]
    hv = cbuf[...] + cb_ref[...]
    gm = gmat_ref[...]
    mu = _split2_dot(hv, gm)
    d = hv - mu
    var = _split2_dot(d * d, gm)
    hn = d * lax.rsqrt(var + NORM_EPS) * cng_ref[...] + cnb_ref[...]
    hc_ref[...] = (hn * jax.nn.sigmoid(hn)).astype(BF16)


def _prep_inproj_weights(w_in, w_uq, w_ukv, fox_forget_b):
    o = np.cumsum((0, MLA_Q_RANK, MLA_KV_RANK, MLA_ROPE, FOX_WIDTH, FOX_WIDTH, FOX_WIDTH, FOX_HEADS,
                   2 * CONV_CH))
    w_cq, w_ckv, w_kr, w_fq, w_fk, w_fv, w_f, w_cv = (w_in[:, o[i]:o[i + 1]] for i in range(8))
    d = w_in.shape[0]
    half = MLA_ROPE // 2

    def rot_cols(w):
        return jnp.concatenate([-w[..., half:], w[..., :half]], axis=-1)

    def rope_block(w):
        return jnp.pad(w, ((0, 0), (MLA_NOPE, HEAD_PAD - MLA_NOPE - MLA_ROPE)))

    w_fk_pad = jnp.pad(w_fk.reshape(d, FOX_HEADS, FOX_DIM), ((0, 0), (0, 0), (0, HEAD_PAD - FOX_DIM)))
    wa = jnp.concatenate([w_cq, w_ckv, rope_block(w_kr), rope_block(rot_cols(w_kr)),
                          w_fk_pad.reshape(d, FOX_HEADS * HEAD_PAD), w_cv], axis=1)
    w_fq_pad = jnp.pad(w_fq.reshape(d, FOX_HEADS, FOX_DIM), ((0, 0), (0, 0), (0, HEAD_PAD - FOX_DIM)))
    wfq_t = w_fq_pad.reshape(d, FOX_HEADS * HEAD_PAD).T
    wfv_t = w_fv.T
    wf_t = jnp.pad(w_f, ((0, 0), (0, _F_ROWS - FOX_HEADS))).T
    fb = jnp.pad(fox_forget_b, (0, _F_ROWS - FOX_HEADS)).reshape(_F_ROWS, 1)

    uq = w_uq.reshape(MLA_Q_RANK, MLA_HEADS, MLA_NOPE + MLA_ROPE)
    uq_nope, uq_rope = uq[..., :MLA_NOPE], uq[..., MLA_NOPE:]
    tail = ((0, 0), (0, 0), (0, HEAD_PAD - MLA_NOPE - MLA_ROPE))
    uq_pad = jnp.pad(jnp.concatenate([uq_nope, uq_rope], axis=-1), tail)
    uq_rot_pad = jnp.pad(jnp.concatenate([jnp.zeros_like(uq_nope), rot_cols(uq_rope)], axis=-1), tail)
    wuq_t = uq_pad.reshape(MLA_Q_RANK, MLA_HEADS * HEAD_PAD).T
    wuqr_t = uq_rot_pad.reshape(MLA_Q_RANK, MLA_HEADS * HEAD_PAD).T
    ukv = w_ukv.reshape(MLA_KV_RANK, MLA_HEADS, MLA_NOPE + MLA_V)
    wuk = jnp.pad(ukv[..., :MLA_NOPE], ((0, 0), (0, 0), (0, HEAD_PAD - MLA_NOPE))).reshape(
        MLA_KV_RANK, MLA_HEADS * HEAD_PAD)
    wuv_t = ukv[..., MLA_NOPE:].reshape(MLA_KV_RANK, MLA_WIDTH).T
    bf = lambda a: a.astype(BF16)
    return dict(wa=bf(wa), wfq=bf(wfq_t), wfv=bf(wfv_t), wf=bf(wf_t), fb=fb, wuq=bf(wuq_t),
                wuqr=bf(wuqr_t), wuk=bf(wuk), wuv=bf(wuv_t))


def _input_projection(x2d, tabs, pw, gq, gkv, conv_w, conv_b, conv_ng, conv_nb, seq):
    n, d = x2d.shape
    tm = min(TM_IN, seq)
    cos_t, sin_t, cos_tt, sin_tt = tabs
    gidx = np.arange(CONV_CH) // (CONV_CH // CONV_GROUPS)
    gmat = jnp.asarray((gidx[:, None] == gidx[None, :]) / (CONV_CH // CONV_GROUPS), BF16)
    cw = jnp.pad(conv_w, ((0, 32 - CONV_WIDTH), (0, 0)))
    row = lambda a: a.reshape(1, -1)
    tok = lambda w: pl.BlockSpec((tm, w), lambda i: (i, 0))
    tok_t = lambda r: pl.BlockSpec((r, tm), lambda i: (0, i))
    consts = [pw["wa"], pw["wfq"], pw["wfv"], pw["wf"], pw["fb"], row(gq), pw["wuq"], pw["wuqr"],
              row(gkv), pw["wuk"], pw["wuv"], cw, row(conv_b), row(conv_ng), row(conv_nb), gmat]
    kern = functools.partial(_inproj_kernel, tiles_per_seq=seq // tm, tm=tm)
    return pl.pallas_call(
        kern,
        grid=(n // tm,),
        in_specs=[tok(d), tok(HEAD_PAD), tok(HEAD_PAD), tok_t(HEAD_PAD), tok_t(HEAD_PAD)]
        + [_const_spec(c.shape) for c in consts],
        out_specs=[pl.BlockSpec((tm // TQ, N_HEADS * HEAD_PAD, TQ), lambda i: (i, 0, 0)),
                   tok(N_HEADS * HEAD_PAD),
                   pl.BlockSpec((tm // TK, N_HEADS * V_ROWS, TK), lambda i: (i, 0, 0)),
                   tok(CONV_CH)],
        out_shape=[jax.ShapeDtypeStruct((n // TQ, N_HEADS * HEAD_PAD, TQ), BF16),
                   jax.ShapeDtypeStruct((n, N_HEADS * HEAD_PAD), BF16),
                   jax.ShapeDtypeStruct((n // TK, N_HEADS * V_ROWS, TK), BF16),
                   jax.ShapeDtypeStruct((n, CONV_CH), BF16)],
        scratch_shapes=[pltpu.VMEM((CONV_HALO + tm, CONV_CH), F32),
                        pltpu.VMEM((7, tm + CONV_HALO - 8, CONV_CH), F32),
                        pltpu.VMEM((tm, CONV_CH), F32),
                        pltpu.VMEM((_F_ROWS, 128), F32),
                        pltpu.VMEM((tm, tm), BF16)],
        compiler_params=pltpu.CompilerParams(dimension_semantics=("arbitrary",),
                                             vmem_limit_bytes=VMEM_LIMIT),
        name="input_projection",
    )(x2d, cos_t, sin_t, cos_tt, sin_tt, *consts)


def _attn_kernel(qt_ref, k_ref, vt_ref, o_ref, *scratch, n_tiles):
    assert TQ == 2 * TK
    s_ref = (scratch[0:HPS], scratch[HPS:2 * HPS])
    sd_ref = scratch[2 * HPS:3 * HPS]
    p_ref = (scratch[3 * HPS:4 * HPS], scratch[4 * HPS:5 * HPS])
    pd_ref = scratch[5 * HPS:6 * HPS]
    acc_bufs = (scratch[6 * HPS:7 * HPS], scratch[7 * HPS:8 * HPS])
    diff_ref = scratch[8 * HPS]
    diff_ref[...] = (lax.broadcasted_iota(jnp.int32, (TK, TQ), 1)
                     - lax.broadcasted_iota(jnp.int32, (TK, TQ), 0))
    for h in range(HPS):
        p_ref[1][h][...] = jnp.zeros_like(p_ref[1][h])
        for par in range(2):
            acc_bufs[par][h][...] = jnp.ones_like(acc_bufs[par][h])

    def tile_scores(tile, j, slot):
        row0 = pl.multiple_of(j * TK, TK)
        block_max = []
        for h in range(HPS):
            s = _dot(k_ref[pl.ds(row0, TK), h * HEAD_PAD:(h + 1) * HEAD_PAD],
                     qt_ref[tile, h * HEAD_PAD:(h + 1) * HEAD_PAD, :])
            s_ref[slot][h][...] = s
            block_max.append(jnp.max(s, axis=0, keepdims=True))
        return block_max

    def last_diag_scores(tile):
        row0 = pl.multiple_of((2 * tile + 1) * TK, TK)
        for h in range(HPS):
            sd_ref[h][...] = _dot(k_ref[pl.ds(row0, TK), h * HEAD_PAD:(h + 1) * HEAD_PAD],
                                  qt_ref[tile, h * HEAD_PAD:(h + 1) * HEAD_PAD, TK:])

    def finalize(tile, par):
        out_t = jnp.concatenate([acc_bufs[par][h][0:V_DIM, :] / acc_bufs[par][h][V_DIM:V_DIM + 1, :]
                                 for h in range(HPS)], axis=0)
        o_ref[pl.ds(pl.multiple_of(tile * TQ, TQ), TQ), :] = out_t.T.astype(o_ref.dtype)

    def q_tile(i, par, bm0):
        acc_ref = acc_bufs[par]
        scores = functools.partial(tile_scores, i)

        def softmax(slot, m, block_max):
            m_new = [jnp.maximum(m[h], block_max[h]) for h in range(HPS)]
            for h in range(HPS):
                p_ref[slot][h][...] = jnp.exp2(s_ref[slot][h][...] - m_new[h]).astype(BF16)
            return m_new, [jnp.exp2(m[h] - m_new[h]) for h in range(HPS)]

        def values(j, slot, alpha, gate=None):
            for h in range(HPS):
                pv = _dot(vt_ref[j, h * V_ROWS:(h + 1) * V_ROWS, :], p_ref[slot][h][...])
                acc_ref[h][...] = alpha[h] * acc_ref[h][...] + (pv if gate is None else gate * pv)

        m0 = [jnp.full((1, TQ), -1e30, F32)] * HPS
        a0 = [jnp.zeros((1, TQ), F32)] * HPS

        def pair(u, state):
            m, alpha, bm_t = list(state[0:HPS]), list(state[HPS:2 * HPS]), list(state[2 * HPS:3 * HPS])
            t = 2 * u
            m, alpha_t = softmax(0, m, bm_t)
            values(jnp.maximum(t - 1, 0), 1, alpha, jnp.where(t > 0, 1.0, 0.0))
            bm_t1 = scores(t + 1, 1)
            m, alpha_t1 = softmax(1, m, bm_t1)
            values(t, 0, alpha_t)
            bm_t2 = scores(t + 2, 0)
            return (*m, *alpha_t1, *bm_t2)

        state = lax.fori_loop(0, i // 2, lambda v, st: pair(2 * v + 1, pair(2 * v, st)), (*m0, *a0, *bm0))
        state = lax.fori_loop(i - i % 2, i, pair, state)
        m, alpha = list(state[0:HPS]), list(state[HPS:2 * HPS])
        d0 = 2 * i
        values(jnp.maximum(d0 - 1, 0), 1, alpha, jnp.where(i > 0, 1.0, 0.0))
        finalize(jnp.where(i > 0, i - 1, n_tiles - 1), 1 - par)
        nxt = jnp.minimum(i + 1, n_tiles - 1)

        m_d0, alpha_d0 = [], []
        for h in range(HPS):
            s = s_ref[0][h][...]
            s = jnp.concatenate([jnp.where(diff_ref[:, 0:TK] >= 0, s[:, 0:TK], -jnp.inf), s[:, TK:]], axis=1)
            m_d0.append(jnp.maximum(m[h], jnp.max(s, axis=0, keepdims=True)))
            p_ref[0][h][...] = jnp.exp2(s - m_d0[h]).astype(BF16)
            alpha_d0.append(jnp.exp2(m[h] - m_d0[h]))
        bm_next = tile_scores(nxt, 0, 0)

        alpha_d1 = []
        for h in range(HPS):
            s = jnp.where(diff_ref[:, 0:TK] >= 0, sd_ref[h][...], -jnp.inf)
            m_old = m_d0[h][:, TK:]
            m_new = jnp.maximum(m_old, jnp.max(s, axis=0, keepdims=True))
            pd_ref[h][...] = jnp.exp2(s - m_new).astype(BF16)
            alpha_d1.append(jnp.exp2(m_old - m_new))
        last_diag_scores(nxt)

        values(d0, 0, alpha_d0)
        for h in range(HPS):
            acc_ref[h][:, TK:] = alpha_d1[h] * acc_ref[h][:, TK:] + _dot(
                vt_ref[d0 + 1, h * V_ROWS:(h + 1) * V_ROWS, :], pd_ref[h][...])
        return tuple(bm_next)

    last_diag_scores(0)
    bm = lax.fori_loop(0, n_tiles // 2, lambda a, st: q_tile(2 * a + 1, 1, q_tile(2 * a, 0, st)),
                       tuple(tile_scores(0, 0, 0)))
    if n_tiles % 2:
        q_tile(n_tiles - 1, 0, bm)
    finalize(n_tiles - 1, (n_tiles - 1) % 2)


def _attention(q_t, k, v_t, batch, seq):
    n = k.shape[0]
    groups = N_HEADS // HPS
    return pl.pallas_call(
        functools.partial(_attn_kernel, n_tiles=seq // TQ),
        grid=(batch, groups),
        in_specs=[pl.BlockSpec((seq // TQ, HPS * HEAD_PAD, TQ), lambda b, p: (b, p, 0)),
                  pl.BlockSpec((seq, HPS * HEAD_PAD), lambda b, p: (b, p)),
                  pl.BlockSpec((seq // TK, HPS * V_ROWS, TK), lambda b, p: (b, p, 0))],
        out_specs=pl.BlockSpec((seq, HPS * V_DIM), lambda b, p: (b, p)),
        out_shape=jax.ShapeDtypeStruct((n, N_HEADS * V_DIM), BF16),
        scratch_shapes=[pltpu.VMEM((TK, TQ), F32)] * (2 * HPS) + [pltpu.VMEM((TK, TK), F32)] * HPS
        + [pltpu.VMEM((TK, TQ), BF16)] * (2 * HPS) + [pltpu.VMEM((TK, TK), BF16)] * HPS
        + [pltpu.VMEM((V_ROWS, TQ), F32)] * (2 * HPS) + [pltpu.VMEM((TK, TQ), jnp.int32)],
        compiler_params=pltpu.CompilerParams(dimension_semantics=("arbitrary", "arbitrary"),
                                             vmem_limit_bytes=VMEM_LIMIT),
        name="attention",
    )(q_t, k, v_t)


def _layer_norm(x, g, b):
    mu = jnp.mean(x, axis=-1, keepdims=True)
    d = x - mu
    var = jnp.mean(jnp.square(d), axis=-1, keepdims=True)
    return d * lax.rsqrt(var + NORM_EPS) * g + b


def _mix_and_norm(o_ref, hc_ref, x_ref, gm_ref, gf_ref, wo_ref, g1_ref, b1_ref):
    o = o_ref[...].astype(F32)
    mla = _rms(o[:, :MLA_WIDTH], gm_ref[...])
    fox = _rms(o[:, MLA_WIDTH:], gf_ref[...])
    mixed = jnp.concatenate([mla.astype(BF16), fox.astype(BF16), hc_ref[...]], axis=-1)
    y = _dot(mixed, wo_ref[...])
    return _layer_norm(ALPHA * x_ref[...] + y, g1_ref[...], b1_ref[...])


def _router_layer_kernel(o_ref, hc_ref, x_ref, gm_ref, gf_ref, wo_ref, g1_ref, b1_ref, rw_ref,
                         x1_ref, route_ref, counts_ref, cnt_ref, upper_ref):
    tm = x_ref.shape[0]

    @pl.when(pl.program_id(0) == 0)
    def _():
        cnt_ref[...] = jnp.zeros_like(cnt_ref)
        r_i = lax.broadcasted_iota(jnp.int32, (tm, tm), 0)
        c_i = lax.broadcasted_iota(jnp.int32, (tm, tm), 1)
        upper_ref[...] = jnp.where(r_i < c_i, 1.0, 0.0).astype(BF16)

    x1 = _mix_and_norm(o_ref, hc_ref, x_ref, gm_ref, gf_ref, wo_ref, g1_ref, b1_ref)
    x1_ref[...] = x1
    rw = rw_ref[...]
    x_hi = x1.astype(BF16)
    x_lo = (x1 - x_hi.astype(F32)).astype(BF16)
    w_hi = rw.astype(BF16)
    w_lo = (rw - w_hi.astype(F32)).astype(BF16)
    both = _dot(x_hi, jnp.concatenate([w_hi, w_lo], axis=1))
    logits = both[:, :LANES] + (_dot(x_lo, w_hi) + both[:, LANES:])
    lg = logits.T[0:N_EXPERTS, :]
    row = lax.broadcasted_iota(jnp.int32, lg.shape, 0)
    v1 = jnp.max(lg, axis=0, keepdims=True)
    i1 = jnp.min(jnp.where(lg == v1, row, N_EXPERTS), axis=0, keepdims=True)
    rest_l = jnp.where(row == i1, -jnp.inf, lg)
    v2 = jnp.max(rest_l, axis=0, keepdims=True)
    i2 = jnp.min(jnp.where(rest_l == v2, row, N_EXPERTS), axis=0, keepdims=True)
    e2 = jnp.exp(v2 - v1)
    den = 1.0 + e2
    sel = jnp.where(row == i1, 1.0, jnp.where(row == i2, 1.0, 0.0))
    sel16 = jnp.concatenate([sel, jnp.zeros_like(sel)], axis=0).astype(BF16)
    before = cnt_ref[:, 0:1]
    rank = _dot(sel16, upper_ref[...])[0:N_EXPERTS, :] + before
    total = before + jnp.sum(sel, axis=1, keepdims=True)
    cnt_ref[...] = jnp.broadcast_to(total, cnt_ref.shape)
    counts_ref[...] = jnp.broadcast_to(total, counts_ref.shape)
    r1 = jnp.sum(jnp.where(row == i1, rank, 0.0), axis=0, keepdims=True)
    r2 = jnp.sum(jnp.where(row == i2, rank, 0.0), axis=0, keepdims=True)
    rows = (i1.astype(F32), i2.astype(F32), r1, r2, 1.0 / den, e2 / den)
    route = jnp.zeros(lg.shape, F32)
    for c, v in enumerate(rows):
        route = jnp.where(row == c, v, route)
    route_ref[...] = route


def _router_layer(o, hc, x2d, gm, gf, w_out, g1, b1, router_w):
    n, d = x2d.shape
    tm = min(TM_OUT, n)
    row = lambda a: a.reshape(1, -1)
    tok = lambda w: pl.BlockSpec((tm, w), lambda i: (i, 0))
    consts = [row(gm), row(gf), w_out.astype(BF16), row(g1), row(b1),
              jnp.pad(router_w, ((0, 0), (0, LANES - N_EXPERTS)))]
    return pl.pallas_call(
        _router_layer_kernel,
        grid=(n // tm,),
        in_specs=[tok(o.shape[1]), tok(CONV_CH), tok(d)] + [_const_spec(c.shape) for c in consts],
        out_specs=[tok(d), pl.BlockSpec((N_EXPERTS, tm), lambda i: (0, i)),
                   pl.BlockSpec((N_EXPERTS, LANES), lambda i: (0, 0))],
        out_shape=[jax.ShapeDtypeStruct((n, d), F32), jax.ShapeDtypeStruct((N_EXPERTS, n), F32),
                   jax.ShapeDtypeStruct((N_EXPERTS, LANES), F32)],
        scratch_shapes=[pltpu.VMEM((N_EXPERTS, LANES), F32), pltpu.VMEM((tm, tm), BF16)],
        compiler_params=pltpu.CompilerParams(dimension_semantics=("arbitrary",),
                                             vmem_limit_bytes=VMEM_LIMIT),
        name="router_layer",
    )(o, hc, x2d, *consts)


def _swiglu_tile(xb, w1, w3, w2):
    h1 = _dot(xb, w1)
    h3 = _dot(xb, w3)
    hid = (h1 * jax.nn.sigmoid(h1) * h3).astype(BF16)
    return _dot(hid, w2)


def _dense_layer_kernel(o_ref, hc_ref, x_ref, gm_ref, gf_ref, wo_ref, g1_ref, b1_ref,
                        w1_ref, w3_ref, w2_ref, g2_ref, b2_ref, out_ref, *, f_chunk):
    x1 = _mix_and_norm(o_ref, hc_ref, x_ref, gm_ref, gf_ref, wo_ref, g1_ref, b1_ref)
    xb = x1.astype(BF16)
    ff = None
    for c0 in range(0, w1_ref.shape[1], f_chunk):
        part = _swiglu_tile(xb, w1_ref[:, c0:c0 + f_chunk], w3_ref[:, c0:c0 + f_chunk],
                            w2_ref[c0:c0 + f_chunk, :])
        ff = part if ff is None else ff + part
    out_ref[...] = _layer_norm(ALPHA * x1 + ff, g2_ref[...], b2_ref[...])


def _dense_layer(o, hc, x2d, gm, gf, w_out, g1, b1, w1, w3, w2, g2, b2):
    n, d = x2d.shape
    tm = min(TM_FFN, n)
    f = w1.shape[1]
    f_chunk = f // 2 if (f // 2) % 128 == 0 else f
    row = lambda a: a.reshape(1, -1)
    tok = lambda w: pl.BlockSpec((tm, w), lambda i: (i, 0))
    consts = [row(gm), row(gf), w_out.astype(BF16), row(g1), row(b1),
              w1.astype(BF16), w3.astype(BF16), w2.astype(BF16), row(g2), row(b2)]
    return pl.pallas_call(
        functools.partial(_dense_layer_kernel, f_chunk=f_chunk),
        grid=(n // tm,),
        in_specs=[tok(o.shape[1]), tok(CONV_CH), tok(d)] + [_const_spec(c.shape) for c in consts],
        out_specs=tok(d),
        out_shape=jax.ShapeDtypeStruct((n, d), F32),
        compiler_params=pltpu.CompilerParams(dimension_semantics=("arbitrary",),
                                             vmem_limit_bytes=VMEM_LIMIT),
        name="dense_layer",
    )(o, hc, x2d, *consts)


def _to_row_tiles(ref, x):
    for c in range(ROW_TILE):
        ref[pl.ds(c, x.shape[0], stride=ROW_TILE), :] = x[:, c * 128:(c + 1) * 128]


def _from_row_tiles(ref, t):
    return jnp.concatenate([ref[pl.ds(c, t, stride=ROW_TILE), :] for c in range(ROW_TILE)], axis=-1)


def _row_tile(ref, r):
    return ref.at[pl.ds(pl.multiple_of(r * ROW_TILE, ROW_TILE), ROW_TILE)]


def _dispatch_kernel(d1_ref, d2_ref, se_ref, x_ref, xs_ref, xr, zbuf, sem, *, tm, tr):
    i = pl.program_id(0)

    @pl.when(i == 0)
    def _():
        zbuf[...] = jnp.zeros_like(zbuf)
        for e in range(N_EXPERTS):
            end = se_ref[e]
            start_e = se_ref[e - 1] if e else 0

            for first, live in ((end - tr, end > start_e),
                                (se_ref[N_EXPERTS - 1] + e * tr,
                                 (se_ref[N_EXPERTS - 1] + e * tr) * ROW_TILE < xs_ref.shape[0])):
                @pl.when(live)
                def _():
                    rows = pl.ds(pl.multiple_of(first * ROW_TILE, ROW_TILE), tr * ROW_TILE)
                    fill = pltpu.make_async_copy(zbuf, xs_ref.at[rows], sem.at[2])
                    fill.start()
                    fill.wait()

    slot = i % 2
    _to_row_tiles(xr.at[slot], x_ref[...])
    base = i * tm

    def start(r, c):
        src = _row_tile(xr.at[slot], r)
        pltpu.make_async_copy(src, _row_tile(xs_ref, d1_ref[base + r]), sem.at[slot]).start()
        pltpu.make_async_copy(src, _row_tile(xs_ref, d2_ref[base + r]), sem.at[slot]).start(priority=1)
        return c

    def wait_step(s):
        def wait(r, c):
            for _ in range(2):
                pltpu.make_async_copy(_row_tile(xr.at[s], 0), _row_tile(xs_ref, 0), sem.at[s]).wait()
            return c

        lax.fori_loop(0, tm, wait, 0, unroll=8)

    lax.fori_loop(0, tm, start, 0, unroll=8)

    @pl.when(i > 0)
    def _():
        wait_step(1 - slot)

    @pl.when(i == pl.num_programs(0) - 1)
    def _():
        wait_step(slot)


def _expert_kernel(te_ref, blk_ref, nu_ref, xs_ref, w1_ref, w3_ref, w2_ref, ys_ref, *, tr):
    del te_ref, blk_ref
    used = pl.program_id(0) < nu_ref[0]

    @pl.when(used)
    def _():
        xb = _from_row_tiles(xs_ref, tr).astype(BF16)
        _to_row_tiles(ys_ref, _swiglu_tile(xb, w1_ref[0], w3_ref[0], w2_ref[0]))

    @pl.when(jnp.logical_not(used))
    def _():
        ys_ref[...] = jnp.zeros_like(ys_ref)


def _combine_kernel(d1_ref, d2_ref, x_ref, route_ref, g_ref, b_ref, ys_ref, o_ref, ybuf, sem, *, tm):
    i = pl.program_id(0)
    n_steps = pl.num_programs(0)

    def issue(tile, slot):
        base = tile * tm

        def start(r, c):
            pltpu.make_async_copy(_row_tile(ys_ref, d1_ref[base + r]), _row_tile(ybuf.at[slot, 0], r),
                                  sem.at[slot]).start()
            pltpu.make_async_copy(_row_tile(ys_ref, d2_ref[base + r]), _row_tile(ybuf.at[slot, 1], r),
                                  sem.at[slot]).start(priority=1)
            return c

        lax.fori_loop(0, tm, start, 0, unroll=8)

    @pl.when(i == 0)
    def _():
        issue(0, 0)

    @pl.when(i + 1 < n_steps)
    def _():
        issue(i + 1, (i + 1) % 2)

    slot = i % 2

    def wait(r, c):
        for k in range(2):
            pltpu.make_async_copy(_row_tile(ys_ref, 0), _row_tile(ybuf.at[slot, k], 0), sem.at[slot]).wait()
        return c

    lax.fori_loop(0, tm, wait, 0, unroll=8)
    route = route_ref[...]
    gates = jnp.concatenate([route, jnp.zeros((128 - route.shape[0], tm), F32)], axis=0).T
    ff = (gates[:, 4:5] * _from_row_tiles(ybuf.at[slot, 0], tm)
          + gates[:, 5:6] * _from_row_tiles(ybuf.at[slot, 1], tm))
    o_ref[...] = _layer_norm(ALPHA * x_ref[...] + ff, g_ref[...], b_ref[...])


def _moe_ffn(x2d, route, counts, w1, w3, w2, g, b):
    n, d = x2d.shape
    n_exp, _, f = w1.shape
    tr = min(TR_MOE, n)
    tm = min(TM_MOE, n)
    n_pad = 2 * n + n_exp * tr
    n_tiles = n_pad // tr
    i32 = jnp.int32

    cnt = counts[:, 0].astype(i32)
    seg = (cnt + tr - 1) // tr * tr
    seg_end = jnp.cumsum(seg)
    seg_start = seg_end - seg
    e1, e2 = route[0].astype(i32), route[1].astype(i32)
    dest1 = seg_start[e1] + route[2].astype(i32)
    dest2 = seg_start[e2] + route[3].astype(i32)
    n_used = jnp.maximum(seg_end[-1] // tr, 1)
    tile = jnp.minimum(jnp.arange(n_tiles, dtype=i32), n_used - 1)
    tile_expert = jnp.minimum(jnp.sum(tile[:, None] * tr >= seg_end[None, :], axis=1), n_exp - 1).astype(i32)

    cparams = pltpu.CompilerParams(dimension_semantics=("arbitrary",), vmem_limit_bytes=VMEM_LIMIT)
    assert d == ROW_TILE * 128
    xs = pl.pallas_call(
        functools.partial(_dispatch_kernel, tm=tm, tr=tr),
        grid_spec=pltpu.PrefetchScalarGridSpec(
            num_scalar_prefetch=3, grid=(n // tm,),
            in_specs=[pl.BlockSpec((tm, d), lambda i, *_: (i, 0))],
            out_specs=pl.BlockSpec(memory_space=pl.ANY),
            scratch_shapes=[pltpu.VMEM((2, tm * ROW_TILE, 128), F32), pltpu.VMEM((tr * ROW_TILE, 128), F32),
                            pltpu.SemaphoreType.DMA((3,))]),
        out_shape=jax.ShapeDtypeStruct((n_pad * ROW_TILE, 128), F32),
        compiler_params=cparams,
        name="moe_dispatch",
    )(dest1, dest2, seg_end.astype(i32), x2d)

    ys = pl.pallas_call(
        functools.partial(_expert_kernel, tr=tr),
        grid_spec=pltpu.PrefetchScalarGridSpec(
            num_scalar_prefetch=3, grid=(n_tiles,),
            in_specs=[pl.BlockSpec((tr * ROW_TILE, 128), lambda i, te, blk, nu: (blk[i], 0)),
                      pl.BlockSpec((1, d, f), lambda i, te, blk, nu: (te[i], 0, 0)),
                      pl.BlockSpec((1, d, f), lambda i, te, blk, nu: (te[i], 0, 0)),
                      pl.BlockSpec((1, f, d), lambda i, te, blk, nu: (te[i], 0, 0))],
            out_specs=pl.BlockSpec((tr * ROW_TILE, 128), lambda i, te, blk, nu: (i, 0))),
        out_shape=jax.ShapeDtypeStruct((n_pad * ROW_TILE, 128), F32),
        compiler_params=cparams,
        name="moe_experts",
    )(tile_expert, tile, n_used.reshape(1), xs, w1.astype(BF16), w3.astype(BF16), w2.astype(BF16))

    row = lambda a: a.reshape(1, -1)
    return pl.pallas_call(
        functools.partial(_combine_kernel, tm=tm),
        grid_spec=pltpu.PrefetchScalarGridSpec(
            num_scalar_prefetch=2, grid=(n // tm,),
            in_specs=[pl.BlockSpec((tm, d), lambda i, *_: (i, 0)),
                      pl.BlockSpec((N_EXPERTS, tm), lambda i, *_: (0, i)),
                      pl.BlockSpec((1, d), lambda i, *_: (0, 0)),
                      pl.BlockSpec((1, d), lambda i, *_: (0, 0)),
                      pl.BlockSpec(memory_space=pl.ANY)],
            out_specs=pl.BlockSpec((tm, d), lambda i, *_: (i, 0)),
            scratch_shapes=[pltpu.VMEM((2, 2, tm * ROW_TILE, 128), F32), pltpu.SemaphoreType.DMA((2,))]),
        out_shape=jax.ShapeDtypeStruct((n, d), F32),
        compiler_params=cparams,
        name="moe_combine",
    )(dest1, dest2, x2d, route, row(g), row(b), ys)


def kernel(x, positions, w_in, mla_q_norm_g, w_uq, mla_kv_norm_g, w_ukv, fox_forget_b, conv_w, conv_b,
           conv_norm_g, conv_norm_b, mla_out_norm_g, fox_out_norm_g, w_out, ln1_g, ln1_b, dense_w1,
           dense_w3, dense_w2, router_w, expert_w1, expert_w3, expert_w2, ln2_g, ln2_b):
    batch, seq, d = x.shape
    assert d == D_MODEL and seq % TQ == 0 and seq % min(TM_IN, seq) == 0
    depth = w_in.shape[0]
    tabs = _rope_tables(positions)
    h = x.reshape(batch * seq, d)
    pw_all = jax.vmap(_prep_inproj_weights)(w_in, w_uq, w_ukv, fox_forget_b)
    w_out, dense_w1, dense_w3, dense_w2, expert_w1, expert_w3, expert_w2 = (
        w.astype(BF16) for w in (w_out, dense_w1, dense_w3, dense_w2, expert_w1, expert_w3, expert_w2))
    for layer in range(depth):
        pw = {name: w[layer] for name, w in pw_all.items()}
        q_t, k, v_t, hc = _input_projection(
            h, tabs, pw, mla_q_norm_g[layer], mla_kv_norm_g[layer], conv_w[layer], conv_b[layer],
            conv_norm_g[layer], conv_norm_b[layer], seq)
        o = _attention(q_t, k, v_t, batch, seq)
        j = layer // 2
        if layer % 2 == 0:
            h = _dense_layer(o, hc, h, mla_out_norm_g[layer], fox_out_norm_g[layer], w_out[layer],
                             ln1_g[layer], ln1_b[layer], dense_w1[j], dense_w3[j], dense_w2[j],
                             ln2_g[layer], ln2_b[layer])
        else:
            h, route, counts = _router_layer(o, hc, h, mla_out_norm_g[layer], fox_out_norm_g[layer], w_out[layer],
                                             ln1_g[layer], ln1_b[layer], router_w[j])
            h = _moe_ffn(h, route, counts, expert_w1[j], expert_w3[j], expert_w2[j], ln2_g[layer],
                         ln2_b[layer])
    return h.reshape(batch, seq, d)
```

```python
import functools
import math

import numpy as np
import jax
import jax.numpy as jnp
from jax import lax
from jax.experimental import pallas as pl
from jax.experimental.pallas import tpu as pltpu

F32 = jnp.float32
BF16 = jnp.bfloat16

D_MODEL = 1024
DEPTH = 4
MLA_HEADS = 8
MLA_NOPE = 64
MLA_ROPE = 32
MLA_V = 64
MLA_Q_RANK = 256
MLA_KV_RANK = 128
ROPE_THETA = 10000.0
FOX_HEADS = 4
FOX_DIM = 64
CONV_CH = 256
CONV_GROUPS = 4
CONV_WIDTH = 31
MLA_WIDTH = MLA_HEADS * MLA_V
FOX_WIDTH = FOX_HEADS * FOX_DIM
N_EXPERTS = 8
ALPHA = (2.0 * DEPTH) ** 0.25
NORM_EPS = 1e-5
LOG2E = math.log2(math.e)

LANES = 128
HEAD_PAD = LANES
N_HEADS = MLA_HEADS + FOX_HEADS
V_DIM = 64
V_ROWS = 80
CONV_HALO = 32
VMEM_LIMIT = 56 * 1024 * 1024

TQ = 512
TK = 256
HPS = 2
TM_IN = 512
TM_OUT = 512
TM_FFN = 512
TR_MOE = 512
TM_MOE = 512
ROW_TILE = 8
CONV_CHUNK = 64
M_INIT = -1e30


def _nt_dot(a, b):
    return lax.dot_general(a, b, (((1,), (1,)), ((), ())), preferred_element_type=F32)


def _dot(a, b):
    return jnp.dot(a, b, preferred_element_type=F32)


def _split2_dot(a, m_bf16):
    hi = a.astype(BF16)
    lo = (a - hi.astype(F32)).astype(BF16)
    return _dot(hi, m_bf16) + _dot(lo, m_bf16)


def _split3(a):
    hi = a.astype(BF16).astype(F32)
    r1 = a - hi
    mid = r1.astype(BF16).astype(F32)
    lo = (r1 - mid).astype(BF16).astype(F32)
    return hi, mid, lo


def _const_spec(shape):
    nd = len(shape)
    return pl.BlockSpec(shape, lambda *_: (0,) * nd, pipeline_mode=pl.Buffered(1))


def _rope_kernel(pos_ref, invf_ref, c_ref, s_ref, ct_ref, st_ref):
    pos = pos_ref[...].astype(F32)
    ang = invf_ref[...] * pos
    cos = jnp.cos(ang)
    sin = jnp.sin(ang)
    tn = pos.shape[1]
    ct = jnp.concatenate([jnp.ones((MLA_NOPE, tn), F32), cos, cos, jnp.zeros((HEAD_PAD - MLA_NOPE - MLA_ROPE, tn), F32)], axis=0)
    st = jnp.concatenate([jnp.zeros((MLA_NOPE, tn), F32), sin, sin, jnp.zeros((HEAD_PAD - MLA_NOPE - MLA_ROPE, tn), F32)], axis=0)
    ct_ref[...] = ct
    st_ref[...] = st
    c_ref[...] = ct.T
    s_ref[...] = st.T


def _rope_tables(positions):
    n = positions.size
    tn = min(512, n)
    inv_freq = ROPE_THETA ** (-jnp.arange(0, MLA_ROPE, 2, dtype=F32) / MLA_ROPE)
    return pl.pallas_call(
        _rope_kernel,
        grid=(n // tn,),
        in_specs=[pl.BlockSpec((1, tn), lambda i: (0, i)),
                  pl.BlockSpec((MLA_ROPE // 2, 1), lambda i: (0, 0))],
        out_specs=[pl.BlockSpec((tn, HEAD_PAD), lambda i: (i, 0)),
                   pl.BlockSpec((tn, HEAD_PAD), lambda i: (i, 0)),
                   pl.BlockSpec((HEAD_PAD, tn), lambda i: (0, i)),
                   pl.BlockSpec((HEAD_PAD, tn), lambda i: (0, i))],
        out_shape=[jax.ShapeDtypeStruct((n, HEAD_PAD), F32),
                   jax.ShapeDtypeStruct((n, HEAD_PAD), F32),
                   jax.ShapeDtypeStruct((HEAD_PAD, n), F32),
                   jax.ShapeDtypeStruct((HEAD_PAD, n), F32)],
        name="rope_tables",
    )(positions.reshape(1, n), inv_freq.reshape(-1, 1))


_A_CQ = 0
_A_CKV = _A_CQ + MLA_Q_RANK
_A_KR = _A_CKV + MLA_KV_RANK
_A_KRR = _A_KR + HEAD_PAD
_A_FK = _A_KRR + HEAD_PAD
_A_CA = _A_FK + FOX_HEADS * HEAD_PAD
_A_CG = _A_CA + CONV_CH
_A_COLS = _A_CG + CONV_CH
_AUG_ROWS = 8
_F_ROWS = 16


def _rms(x, g):
    ms = jnp.mean(jnp.square(x), axis=-1, keepdims=True)
    return x * lax.rsqrt(ms + NORM_EPS) * g


def _inproj_kernel(x_ref, c_ref, s_ref, ct_ref, st_ref, wa_ref, wfq_ref, wfv_ref, wf_ref, fb_ref,
                   gq_ref, wuq_ref, wuqr_ref, gkv_ref, wuk_ref, wuv_ref,
                   cw_ref, cb_ref, cng_ref, cnb_ref, gmat_ref,
                   qt_ref, k_ref, vt_ref, hc_ref,
                   hbuf, hsh, cbuf, fcarry, upper_ref, *, tiles_per_seq, tm):
    i = pl.program_id(0)

    @pl.when(i % tiles_per_seq == 0)
    def _():
        hbuf[0:CONV_HALO, :] = jnp.zeros((CONV_HALO, CONV_CH), F32)
        fcarry[...] = jnp.zeros_like(fcarry)
        r_i = lax.broadcasted_iota(jnp.int32, (tm, tm), 0)
        c_i = lax.broadcasted_iota(jnp.int32, (tm, tm), 1)
        upper_ref[...] = jnp.where(r_i <= c_i, 1.0, 0.0).astype(BF16)

    xb = x_ref[...].astype(BF16)
    p1 = _dot(xb, wa_ref[...])
    cos_t = c_ref[...]
    sin_t = s_ref[...]
    cos_tt = ct_ref[...]
    sin_tt = st_ref[...]

    cqn = _rms(p1[:, _A_CQ:_A_CQ + MLA_Q_RANK], gq_ref[...]).astype(BF16)
    q_t = _nt_dot(wuq_ref[...], cqn)
    q_rot_t = _nt_dot(wuqr_ref[...], cqn)
    mla_scale = (MLA_NOPE + MLA_ROPE) ** -0.5 * LOG2E
    for h in range(MLA_HEADS):
        rows = slice(h * HEAD_PAD, (h + 1) * HEAD_PAD)
        qh = (q_t[rows, :] * cos_tt + q_rot_t[rows, :] * sin_tt) * mla_scale
        for c in range(tm // TQ):
            qt_ref[c, rows, :] = qh[:, c * TQ:(c + 1) * TQ].astype(BF16)

    ckvn = _rms(p1[:, _A_CKV:_A_CKV + MLA_KV_RANK], gkv_ref[...]).astype(BF16)
    k_nope = _dot(ckvn, wuk_ref[...])
    k_rope = p1[:, _A_KR:_A_KR + HEAD_PAD] * cos_t + p1[:, _A_KRR:_A_KRR + HEAD_PAD] * sin_t
    for h in range(MLA_HEADS):
        cols = slice(h * HEAD_PAD, (h + 1) * HEAD_PAD)
        k_ref[:, cols] = (k_nope[:, cols] + k_rope).astype(BF16)
    v_t = _nt_dot(wuv_ref[...], ckvn)
    fv_t = _nt_dot(wfv_ref[...], xb)
    ones_blk = jnp.where(lax.broadcasted_iota(jnp.int32, (V_ROWS - V_DIM, tm), 0) == 0, 1.0, 0.0)
    for h in range(N_HEADS):
        src = v_t if h < MLA_HEADS else fv_t
        r0 = (h if h < MLA_HEADS else h - MLA_HEADS) * V_DIM
        vh = jnp.concatenate([src[r0:r0 + V_DIM, :], ones_blk], axis=0).astype(BF16)
        for c in range(tm // TK):
            vt_ref[c, h * V_ROWS:(h + 1) * V_ROWS, :] = vh[:, c * TK:(c + 1) * TK]

    z = _nt_dot(wf_ref[...], xb) + fb_ref[...]
    logf = (jnp.minimum(z, 0.0) - jnp.log1p(jnp.exp(-jnp.abs(z)))) * LOG2E
    limbs = jnp.concatenate(_split3(logf), axis=0).astype(BF16)
    sums = _dot(limbs, upper_ref[...])
    cum = (sums[0:_F_ROWS] + sums[_F_ROWS:2 * _F_ROWS]) + sums[2 * _F_ROWS:3 * _F_ROWS]
    f_cum = cum + fcarry[:, 0:1]
    fcarry[...] = jnp.broadcast_to(f_cum[:, tm - 1:tm], fcarry.shape)
    f_hi, f_mid, f_lo = _split3(f_cum)

    fq_t = _nt_dot(wfq_ref[...], xb)
    row8 = lax.broadcasted_iota(jnp.int32, (_AUG_ROWS, tm), 0)
    fox_scale = FOX_DIM ** -0.5 * LOG2E
    for h in range(FOX_HEADS):
        bh = lambda a: jnp.broadcast_to(a[h:h + 1, :], (_AUG_ROWS, tm))
        aug_q = jnp.where(row8 == 0, bh(f_hi), jnp.where(row8 == 1, bh(f_mid), jnp.where(
            row8 == 2, bh(f_lo), jnp.where(row8 < 6, 1.0, 0.0))))
        aug_k = jnp.where(row8 < 3, 1.0, jnp.where(row8 == 3, -bh(f_hi), jnp.where(
            row8 == 4, -bh(f_mid), jnp.where(row8 == 5, -bh(f_lo), 0.0))))
        pad = jnp.zeros((HEAD_PAD - FOX_DIM - _AUG_ROWS, tm), F32)
        qh = jnp.concatenate(
            [fq_t[h * HEAD_PAD:h * HEAD_PAD + FOX_DIM, :] * fox_scale, aug_q, pad], axis=0)
        rows = slice((MLA_HEADS + h) * HEAD_PAD, (MLA_HEADS + h + 1) * HEAD_PAD)
        for c in range(tm // TQ):
            qt_ref[c, rows, :] = qh[:, c * TQ:(c + 1) * TQ].astype(BF16)
        kaug_t = jnp.concatenate([jnp.zeros((FOX_DIM, tm), F32), aug_k, pad], axis=0)
        fk = p1[:, _A_FK + h * HEAD_PAD:_A_FK + (h + 1) * HEAD_PAD]
        k_ref[:, rows] = (fk + kaug_t.T).astype(BF16)

    a = p1[:, _A_CA:_A_CA + CONV_CH]
    g = p1[:, _A_CG:_A_CG + CONV_CH]
    hbuf[CONV_HALO:CONV_HALO + tm, :] = a * jax.nn.sigmoid(g)
    chunk = CONV_CHUNK
    first = CONV_HALO - (CONV_WIDTH - 1)
    for r in range(1, 8):
        hsh[r - 1] = hbuf[r:r + tm + CONV_HALO - 8, :]
    for c0 in range(0, tm, chunk):
        acc = jnp.zeros((chunk, CONV_CH), F32)
        for o in range(first, first + CONV_WIDTH):
            r = o % 8
            row = c0 + o - r
            seg = hbuf[row:row + chunk, :] if r == 0 else hsh[r - 1, row:row + chunk, :]
            acc = acc + cw_ref[o - first:o - first + 1, :] * seg
        cbuf[c0:c0 + chunk, :] = acc
    hbuf[0:CONV_HALO, :] = hbuf[tm:tm + CONV_HALO, :]
    hv = cbuf[...] + cb_ref[...]
    gm = gmat_ref[...]
    mu = _split2_dot(hv, gm)
    d = hv - mu
    var = _split2_dot(d * d, gm)
    hn = d * lax.rsqrt(var + NORM_EPS) * cng_ref[...] + cnb_ref[...]
    hc_ref[...] = (hn * jax.nn.sigmoid(hn)).astype(BF16)


def _prep_inproj_weights(w_in, w_uq, w_ukv, fox_forget_b):
    o = np.cumsum((0, MLA_Q_RANK, MLA_KV_RANK, MLA_ROPE, FOX_WIDTH, FOX_WIDTH, FOX_WIDTH, FOX_HEADS,
                   2 * CONV_CH))
    w_cq, w_ckv, w_kr, w_fq, w_fk, w_fv, w_f, w_cv = (w_in[:, o[i]:o[i + 1]] for i in range(8))
    d = w_in.shape[0]
    half = MLA_ROPE // 2

    def rot_cols(w):
        return jnp.concatenate([-w[..., half:], w[..., :half]], axis=-1)

    def rope_block(w):
        return jnp.pad(w, ((0, 0), (MLA_NOPE, HEAD_PAD - MLA_NOPE - MLA_ROPE)))

    w_fk_pad = jnp.pad(w_fk.reshape(d, FOX_HEADS, FOX_DIM), ((0, 0), (0, 0), (0, HEAD_PAD - FOX_DIM)))
    wa = jnp.concatenate([w_cq, w_ckv, rope_block(w_kr), rope_block(rot_cols(w_kr)),
                          w_fk_pad.reshape(d, FOX_HEADS * HEAD_PAD), w_cv], axis=1)
    w_fq_pad = jnp.pad(w_fq.reshape(d, FOX_HEADS, FOX_DIM), ((0, 0), (0, 0), (0, HEAD_PAD - FOX_DIM)))
    wfq_t = w_fq_pad.reshape(d, FOX_HEADS * HEAD_PAD).T
    wfv_t = w_fv.T
    wf_t = jnp.pad(w_f, ((0, 0), (0, _F_ROWS - FOX_HEADS))).T
    fb = jnp.pad(fox_forget_b, (0, _F_ROWS - FOX_HEADS)).reshape(_F_ROWS, 1)

    uq = w_uq.reshape(MLA_Q_RANK, MLA_HEADS, MLA_NOPE + MLA_ROPE)
    uq_nope, uq_rope = uq[..., :MLA_NOPE], uq[..., MLA_NOPE:]
    tail = ((0, 0), (0, 0), (0, HEAD_PAD - MLA_NOPE - MLA_ROPE))
    uq_pad = jnp.pad(jnp.concatenate([uq_nope, uq_rope], axis=-1), tail)
    uq_rot_pad = jnp.pad(jnp.concatenate([jnp.zeros_like(uq_nope), rot_cols(uq_rope)], axis=-1), tail)
    wuq_t = uq_pad.reshape(MLA_Q_RANK, MLA_HEADS * HEAD_PAD).T
    wuqr_t = uq_rot_pad.reshape(MLA_Q_RANK, MLA_HEADS * HEAD_PAD).T
    ukv = w_ukv.reshape(MLA_KV_RANK, MLA_HEADS, MLA_NOPE + MLA_V)
    wuk = jnp.pad(ukv[..., :MLA_NOPE], ((0, 0), (0, 0), (0, HEAD_PAD - MLA_NOPE))).reshape(
        MLA_KV_RANK, MLA_HEADS * HEAD_PAD)
    wuv_t = ukv[..., MLA_NOPE:].reshape(MLA_KV_RANK, MLA_WIDTH).T
    bf = lambda a: a.astype(BF16)
    return dict(wa=bf(wa), wfq=bf(wfq_t), wfv=bf(wfv_t), wf=bf(wf_t), fb=fb, wuq=bf(wuq_t),
                wuqr=bf(wuqr_t), wuk=bf(wuk), wuv=bf(wuv_t))


def _input_projection(x2d, tabs, pw, gq, gkv, conv_w, conv_b, conv_ng, conv_nb, seq):
    n, d = x2d.shape
    tm = min(TM_IN, seq)
    cos_t, sin_t, cos_tt, sin_tt = tabs
    gidx = np.arange(CONV_CH) // (CONV_CH // CONV_GROUPS)
    gmat = jnp.asarray((gidx[:, None] == gidx[None, :]) / (CONV_CH // CONV_GROUPS), BF16)
    cw = jnp.pad(conv_w, ((0, CONV_HALO - CONV_WIDTH), (0, 0)))
    row = lambda a: a.reshape(1, -1)
    tok = lambda w: pl.BlockSpec((tm, w), lambda i: (i, 0))
    tok_t = lambda r: pl.BlockSpec((r, tm), lambda i: (0, i))
    consts = [pw["wa"], pw["wfq"], pw["wfv"], pw["wf"], pw["fb"], row(gq), pw["wuq"], pw["wuqr"],
              row(gkv), pw["wuk"], pw["wuv"], cw, row(conv_b), row(conv_ng), row(conv_nb), gmat]
    kern = functools.partial(_inproj_kernel, tiles_per_seq=seq // tm, tm=tm)
    return pl.pallas_call(
        kern,
        grid=(n // tm,),
        in_specs=[tok(d), tok(HEAD_PAD), tok(HEAD_PAD), tok_t(HEAD_PAD), tok_t(HEAD_PAD)]
        + [_const_spec(c.shape) for c in consts],
        out_specs=[pl.BlockSpec((tm // TQ, N_HEADS * HEAD_PAD, TQ), lambda i: (i, 0, 0)),
                   tok(N_HEADS * HEAD_PAD),
                   pl.BlockSpec((tm // TK, N_HEADS * V_ROWS, TK), lambda i: (i, 0, 0)),
                   tok(CONV_CH)],
        out_shape=[jax.ShapeDtypeStruct((n // TQ, N_HEADS * HEAD_PAD, TQ), BF16),
                   jax.ShapeDtypeStruct((n, N_HEADS * HEAD_PAD), BF16),
                   jax.ShapeDtypeStruct((n // TK, N_HEADS * V_ROWS, TK), BF16),
                   jax.ShapeDtypeStruct((n, CONV_CH), BF16)],
        scratch_shapes=[pltpu.VMEM((CONV_HALO + tm, CONV_CH), F32),
                        pltpu.VMEM((7, tm + CONV_HALO - 8, CONV_CH), F32),
                        pltpu.VMEM((tm, CONV_CH), F32),
                        pltpu.VMEM((_F_ROWS, LANES), F32),
                        pltpu.VMEM((tm, tm), BF16)],
        compiler_params=pltpu.CompilerParams(dimension_semantics=("arbitrary",),
                                             vmem_limit_bytes=VMEM_LIMIT),
        name="input_projection",
    )(x2d, cos_t, sin_t, cos_tt, sin_tt, *consts)


def _attn_kernel(qt_ref, k_ref, vt_ref, o_ref, *scratch, n_tiles):
    assert TQ == 2 * TK
    s_ref = (scratch[0:HPS], scratch[HPS:2 * HPS])
    sd_ref = scratch[2 * HPS:3 * HPS]
    p_ref = (scratch[3 * HPS:4 * HPS], scratch[4 * HPS:5 * HPS])
    pd_ref = scratch[5 * HPS:6 * HPS]
    acc_bufs = (scratch[6 * HPS:7 * HPS], scratch[7 * HPS:8 * HPS])
    diff_ref = scratch[8 * HPS]
    diff_ref[...] = (lax.broadcasted_iota(jnp.int32, (TK, TQ), 1)
                     - lax.broadcasted_iota(jnp.int32, (TK, TQ), 0))
    for h in range(HPS):
        p_ref[1][h][...] = jnp.zeros_like(p_ref[1][h])
        for par in range(2):
            acc_bufs[par][h][...] = jnp.ones_like(acc_bufs[par][h])

    def tile_scores(tile, j, slot):
        row0 = pl.multiple_of(j * TK, TK)
        block_max = []
        for h in range(HPS):
            s = _dot(k_ref[pl.ds(row0, TK), h * HEAD_PAD:(h + 1) * HEAD_PAD],
                     qt_ref[tile, h * HEAD_PAD:(h + 1) * HEAD_PAD, :])
            s_ref[slot][h][...] = s
            block_max.append(jnp.max(s, axis=0, keepdims=True))
        return block_max

    def last_diag_scores(tile):
        row0 = pl.multiple_of((2 * tile + 1) * TK, TK)
        for h in range(HPS):
            sd_ref[h][...] = _dot(k_ref[pl.ds(row0, TK), h * HEAD_PAD:(h + 1) * HEAD_PAD],
                                  qt_ref[tile, h * HEAD_PAD:(h + 1) * HEAD_PAD, TK:])

    def finalize(tile, par):
        out_t = jnp.concatenate([acc_bufs[par][h][0:V_DIM, :] / acc_bufs[par][h][V_DIM:V_DIM + 1, :]
                                 for h in range(HPS)], axis=0)
        o_ref[pl.ds(pl.multiple_of(tile * TQ, TQ), TQ), :] = out_t.T.astype(o_ref.dtype)

    def q_tile(i, par, bm0):
        acc_ref = acc_bufs[par]
        scores = functools.partial(tile_scores, i)

        def softmax(slot, m, block_max):
            m_new = [jnp.maximum(m[h], block_max[h]) for h in range(HPS)]
            for h in range(HPS):
                p_ref[slot][h][...] = jnp.exp2(s_ref[slot][h][...] - m_new[h]).astype(BF16)
            return m_new, [jnp.exp2(m[h] - m_new[h]) for h in range(HPS)]

        def values(j, slot, alpha, gate=None):
            for h in range(HPS):
                pv = _dot(vt_ref[j, h * V_ROWS:(h + 1) * V_ROWS, :], p_ref[slot][h][...])
                acc_ref[h][...] = alpha[h] * acc_ref[h][...] + (pv if gate is None else gate * pv)

        m0 = [jnp.full((1, TQ), M_INIT, F32)] * HPS
        a0 = [jnp.zeros((1, TQ), F32)] * HPS

        def pair(u, state):
            m, alpha, bm_t = list(state[0:HPS]), list(state[HPS:2 * HPS]), list(state[2 * HPS:3 * HPS])
            t = 2 * u
            m, alpha_t = softmax(0, m, bm_t)
            values(jnp.maximum(t - 1, 0), 1, alpha, jnp.where(t > 0, 1.0, 0.0))
            bm_t1 = scores(t + 1, 1)
            m, alpha_t1 = softmax(1, m, bm_t1)
            values(t, 0, alpha_t)
            bm_t2 = scores(t + 2, 0)
            return (*m, *alpha_t1, *bm_t2)

        state = lax.fori_loop(0, i // 2, lambda v, st: pair(2 * v + 1, pair(2 * v, st)), (*m0, *a0, *bm0))
        state = lax.fori_loop(i - i % 2, i, pair, state)
        m, alpha = list(state[0:HPS]), list(state[HPS:2 * HPS])
        d0 = 2 * i
        values(jnp.maximum(d0 - 1, 0), 1, alpha, jnp.where(i > 0, 1.0, 0.0))
        finalize(jnp.where(i > 0, i - 1, n_tiles - 1), 1 - par)
        nxt = jnp.minimum(i + 1, n_tiles - 1)

        m_d0, alpha_d0 = [], []
        for h in range(HPS):
            s = s_ref[0][h][...]
            s = jnp.concatenate([jnp.where(diff_ref[:, 0:TK] >= 0, s[:, 0:TK], -jnp.inf), s[:, TK:]], axis=1)
            m_d0.append(jnp.maximum(m[h], jnp.max(s, axis=0, keepdims=True)))
            p_ref[0][h][...] = jnp.exp2(s - m_d0[h]).astype(BF16)
            alpha_d0.append(jnp.exp2(m[h] - m_d0[h]))
        bm_next = tile_scores(nxt, 0, 0)

        alpha_d1 = []
        for h in range(HPS):
            s = jnp.where(diff_ref[:, 0:TK] >= 0, sd_ref[h][...], -jnp.inf)
            m_old = m_d0[h][:, TK:]
            m_new = jnp.maximum(m_old, jnp.max(s, axis=0, keepdims=True))
            pd_ref[h][...] = jnp.exp2(s - m_new).astype(BF16)
            alpha_d1.append(jnp.exp2(m_old - m_new))
        last_diag_scores(nxt)

        values(d0, 0, alpha_d0)
        for h in range(HPS):
            acc_ref[h][:, TK:] = alpha_d1[h] * acc_ref[h][:, TK:] + _dot(
                vt_ref[d0 + 1, h * V_ROWS:(h + 1) * V_ROWS, :], pd_ref[h][...])
        return tuple(bm_next)

    last_diag_scores(0)
    bm = lax.fori_loop(0, n_tiles // 2, lambda a, st: q_tile(2 * a + 1, 1, q_tile(2 * a, 0, st)),
                       tuple(tile_scores(0, 0, 0)))
    if n_tiles % 2:
        q_tile(n_tiles - 1, 0, bm)
    finalize(n_tiles - 1, (n_tiles - 1) % 2)


def _attention(q_t, k, v_t, batch, seq):
    n = k.shape[0]
    groups = N_HEADS // HPS
    return pl.pallas_call(
        functools.partial(_attn_kernel, n_tiles=seq // TQ),
        grid=(batch, groups),
        in_specs=[pl.BlockSpec((seq // TQ, HPS * HEAD_PAD, TQ), lambda b, p: (b, p, 0)),
                  pl.BlockSpec((seq, HPS * HEAD_PAD), lambda b, p: (b, p)),
                  pl.BlockSpec((seq // TK, HPS * V_ROWS, TK), lambda b, p: (b, p, 0))],
        out_specs=pl.BlockSpec((seq, HPS * V_DIM), lambda b, p: (b, p)),
        out_shape=jax.ShapeDtypeStruct((n, N_HEADS * V_DIM), BF16),
        scratch_shapes=[pltpu.VMEM((TK, TQ), F32)] * (2 * HPS) + [pltpu.VMEM((TK, TK), F32)] * HPS
        + [pltpu.VMEM((TK, TQ), BF16)] * (2 * HPS) + [pltpu.VMEM((TK, TK), BF16)] * HPS
        + [pltpu.VMEM((V_ROWS, TQ), F32)] * (2 * HPS) + [pltpu.VMEM((TK, TQ), jnp.int32)],
        compiler_params=pltpu.CompilerParams(dimension_semantics=("arbitrary", "arbitrary"),
                                             vmem_limit_bytes=VMEM_LIMIT),
        name="attention",
    )(q_t, k, v_t)


def _layer_norm(x, g, b):
    mu = jnp.mean(x, axis=-1, keepdims=True)
    d = x - mu
    var = jnp.mean(jnp.square(d), axis=-1, keepdims=True)
    return d * lax.rsqrt(var + NORM_EPS) * g + b


def _mix_and_norm(o_ref, hc_ref, x_ref, gm_ref, gf_ref, wo_ref, g1_ref, b1_ref):
    o = o_ref[...].astype(F32)
    mla = _rms(o[:, :MLA_WIDTH], gm_ref[...])
    fox = _rms(o[:, MLA_WIDTH:], gf_ref[...])
    mixed = jnp.concatenate([mla.astype(BF16), fox.astype(BF16), hc_ref[...]], axis=-1)
    y = _dot(mixed, wo_ref[...])
    return _layer_norm(ALPHA * x_ref[...] + y, g1_ref[...], b1_ref[...])


def _router_layer_kernel(o_ref, hc_ref, x_ref, gm_ref, gf_ref, wo_ref, g1_ref, b1_ref, rw_ref,
                         x1_ref, route_ref, counts_ref, cnt_ref, upper_ref):
    x1 = _mix_and_norm(o_ref, hc_ref, x_ref, gm_ref, gf_ref, wo_ref, g1_ref, b1_ref)
    x1_ref[...] = x1
    rw = rw_ref[...]
    x_hi = x1.astype(BF16)
    x_lo = (x1 - x_hi.astype(F32)).astype(BF16)
    w_hi = rw.astype(BF16)
    w_lo = (rw - w_hi.astype(F32)).astype(BF16)
    both = _dot(x_hi, jnp.concatenate([w_hi, w_lo], axis=1))
    logits = both[:, :LANES] + (_dot(x_lo, w_hi) + both[:, LANES:])
    tm = logits.shape[0]
    lg = logits.T[0:N_EXPERTS, :]
    row = lax.broadcasted_iota(jnp.int32, lg.shape, 0)
    v1 = jnp.max(lg, axis=0, keepdims=True)
    i1 = jnp.min(jnp.where(lg == v1, row, N_EXPERTS), axis=0, keepdims=True)
    rest_l = jnp.where(row == i1, -jnp.inf, lg)
    v2 = jnp.max(rest_l, axis=0, keepdims=True)
    i2 = jnp.min(jnp.where(rest_l == v2, row, N_EXPERTS), axis=0, keepdims=True)
    e2 = jnp.exp(v2 - v1)
    den = 1.0 + e2

    @pl.when(pl.program_id(0) == 0)
    def _():
        cnt_ref[...] = jnp.zeros_like(cnt_ref)
        r_i = lax.broadcasted_iota(jnp.int32, (tm, tm), 0)
        c_i = lax.broadcasted_iota(jnp.int32, (tm, tm), 1)
        upper_ref[...] = jnp.where(r_i < c_i, 1.0, 0.0).astype(BF16)

    sel = jnp.where(row == i1, 1.0, jnp.where(row == i2, 1.0, 0.0))
    sel16 = jnp.concatenate([sel, jnp.zeros_like(sel)], axis=0).astype(BF16)
    before = cnt_ref[:, 0:1]
    rank = _dot(sel16, upper_ref[...])[0:N_EXPERTS, :] + before
    total = before + jnp.sum(sel, axis=1, keepdims=True)
    cnt_ref[...] = jnp.broadcast_to(total, cnt_ref.shape)
    counts_ref[...] = jnp.broadcast_to(total, counts_ref.shape)
    r1 = jnp.sum(jnp.where(row == i1, rank, 0.0), axis=0, keepdims=True)
    r2 = jnp.sum(jnp.where(row == i2, rank, 0.0), axis=0, keepdims=True)
    rows = (i1.astype(F32), i2.astype(F32), r1, r2, 1.0 / den, e2 / den)
    route = jnp.zeros(lg.shape, F32)
    for c, v in enumerate(rows):
        route = jnp.where(row == c, v, route)
    route_ref[...] = route


def _router_layer(o, hc, x2d, gm, gf, w_out, g1, b1, router_w):
    n, d = x2d.shape
    tm = min(TM_OUT, n)
    row = lambda a: a.reshape(1, -1)
    tok = lambda w: pl.BlockSpec((tm, w), lambda i: (i, 0))
    consts = [row(gm), row(gf), w_out.astype(BF16), row(g1), row(b1),
              jnp.pad(router_w, ((0, 0), (0, LANES - N_EXPERTS)))]
    return pl.pallas_call(
        _router_layer_kernel,
        grid=(n // tm,),
        in_specs=[tok(o.shape[1]), tok(CONV_CH), tok(d)] + [_const_spec(c.shape) for c in consts],
        out_specs=[tok(d), pl.BlockSpec((N_EXPERTS, tm), lambda i: (0, i)),
                   pl.BlockSpec((N_EXPERTS, LANES), lambda i: (0, 0))],
        out_shape=[jax.ShapeDtypeStruct((n, d), F32), jax.ShapeDtypeStruct((N_EXPERTS, n), F32),
                   jax.ShapeDtypeStruct((N_EXPERTS, LANES), F32)],
        scratch_shapes=[pltpu.VMEM((N_EXPERTS, LANES), F32), pltpu.VMEM((tm, tm), BF16)],
        compiler_params=pltpu.CompilerParams(dimension_semantics=("arbitrary",),
                                             vmem_limit_bytes=VMEM_LIMIT),
        name="router_layer",
    )(o, hc, x2d, *consts)


def _swiglu_tile(xb, w1, w3, w2):
    h1 = _dot(xb, w1)
    h3 = _dot(xb, w3)
    hid = (h1 * jax.nn.sigmoid(h1) * h3).astype(BF16)
    return _dot(hid, w2)


def _dense_layer_kernel(o_ref, hc_ref, x_ref, gm_ref, gf_ref, wo_ref, g1_ref, b1_ref,
                        w1_ref, w3_ref, w2_ref, g2_ref, b2_ref, out_ref, *, f_chunk):
    x1 = _mix_and_norm(o_ref, hc_ref, x_ref, gm_ref, gf_ref, wo_ref, g1_ref, b1_ref)
    xb = x1.astype(BF16)
    ff = None
    for c0 in range(0, w1_ref.shape[1], f_chunk):
        part = _swiglu_tile(xb, w1_ref[:, c0:c0 + f_chunk], w3_ref[:, c0:c0 + f_chunk],
                            w2_ref[c0:c0 + f_chunk, :])
        ff = part if ff is None else ff + part
    out_ref[...] = _layer_norm(ALPHA * x1 + ff, g2_ref[...], b2_ref[...])


def _dense_layer(o, hc, x2d, gm, gf, w_out, g1, b1, w1, w3, w2, g2, b2):
    n, d = x2d.shape
    tm = min(TM_FFN, n)
    f = w1.shape[1]
    f_chunk = f // 2 if (f // 2) % LANES == 0 else f
    row = lambda a: a.reshape(1, -1)
    tok = lambda w: pl.BlockSpec((tm, w), lambda i: (i, 0))
    consts = [row(gm), row(gf), w_out.astype(BF16), row(g1), row(b1),
              w1.astype(BF16), w3.astype(BF16), w2.astype(BF16), row(g2), row(b2)]
    return pl.pallas_call(
        functools.partial(_dense_layer_kernel, f_chunk=f_chunk),
        grid=(n // tm,),
        in_specs=[tok(o.shape[1]), tok(CONV_CH), tok(d)] + [_const_spec(c.shape) for c in consts],
        out_specs=tok(d),
        out_shape=jax.ShapeDtypeStruct((n, d), F32),
        compiler_params=pltpu.CompilerParams(dimension_semantics=("arbitrary",),
                                             vmem_limit_bytes=VMEM_LIMIT),
        name="dense_layer",
    )(o, hc, x2d, *consts)


def _to_row_tiles(ref, x):
    for c in range(ROW_TILE):
        ref[pl.ds(c, x.shape[0], stride=ROW_TILE), :] = x[:, c * LANES:(c + 1) * LANES]


def _from_row_tiles(ref, t):
    return jnp.concatenate([ref[pl.ds(c, t, stride=ROW_TILE), :] for c in range(ROW_TILE)], axis=-1)


def _row_tile(ref, r):
    return ref.at[pl.ds(pl.multiple_of(r * ROW_TILE, ROW_TILE), ROW_TILE)]


def _dispatch_kernel(d1_ref, d2_ref, se_ref, x_ref, xs_ref, xr, zbuf, sem, *, tm, tr):
    i = pl.program_id(0)

    @pl.when(i == 0)
    def _():
        zbuf[...] = jnp.zeros_like(zbuf)
        for e in range(N_EXPERTS):
            end = se_ref[e]
            start_e = se_ref[e - 1] if e else 0

            for first, live in ((end - tr, end > start_e),
                                (se_ref[N_EXPERTS - 1] + e * tr,
                                 (se_ref[N_EXPERTS - 1] + e * tr) * ROW_TILE < xs_ref.shape[0])):
                @pl.when(live)
                def _():
                    rows = pl.ds(pl.multiple_of(first * ROW_TILE, ROW_TILE), tr * ROW_TILE)
                    fill = pltpu.make_async_copy(zbuf, xs_ref.at[rows], sem.at[2])
                    fill.start()
                    fill.wait()

    slot = i % 2
    _to_row_tiles(xr.at[slot], x_ref[...])
    base = i * tm

    def start(r, c):
        src = _row_tile(xr.at[slot], r)
        pltpu.make_async_copy(src, _row_tile(xs_ref, d1_ref[base + r]), sem.at[slot]).start()
        pltpu.make_async_copy(src, _row_tile(xs_ref, d2_ref[base + r]), sem.at[slot]).start(priority=1)
        return c

    def wait_step(s):
        def wait(r, c):
            for _ in range(2):
                pltpu.make_async_copy(_row_tile(xr.at[s], 0), _row_tile(xs_ref, 0), sem.at[s]).wait()
            return c

        lax.fori_loop(0, tm, wait, 0, unroll=8)

    lax.fori_loop(0, tm, start, 0, unroll=8)

    @pl.when(i > 0)
    def _():
        wait_step(1 - slot)

    @pl.when(i == pl.num_programs(0) - 1)
    def _():
        wait_step(slot)


def _expert_kernel(te_ref, blk_ref, nu_ref, xs_ref, w1_ref, w3_ref, w2_ref, ys_ref, *, tr):
    del te_ref, blk_ref
    used = pl.program_id(0) < nu_ref[0]

    @pl.when(used)
    def _():
        xb = _from_row_tiles(xs_ref, tr).astype(BF16)
        _to_row_tiles(ys_ref, _swiglu_tile(xb, w1_ref[0], w3_ref[0], w2_ref[0]))

    @pl.when(jnp.logical_not(used))
    def _():
        ys_ref[...] = jnp.zeros_like(ys_ref)


def _combine_kernel(d1_ref, d2_ref, x_ref, route_ref, g_ref, b_ref, ys_ref, o_ref, ybuf, sem, *, tm):
    i = pl.program_id(0)
    n_steps = pl.num_programs(0)

    def issue(tile, slot):
        base = tile * tm

        def start(r, c):
            pltpu.make_async_copy(_row_tile(ys_ref, d1_ref[base + r]), _row_tile(ybuf.at[slot, 0], r),
                                  sem.at[slot]).start()
            pltpu.make_async_copy(_row_tile(ys_ref, d2_ref[base + r]), _row_tile(ybuf.at[slot, 1], r),
                                  sem.at[slot]).start(priority=1)
            return c

        lax.fori_loop(0, tm, start, 0, unroll=8)

    @pl.when(i == 0)
    def _():
        issue(0, 0)

    @pl.when(i + 1 < n_steps)
    def _():
        issue(i + 1, (i + 1) % 2)

    slot = i % 2

    def wait(r, c):
        for k in range(2):
            pltpu.make_async_copy(_row_tile(ys_ref, 0), _row_tile(ybuf.at[slot, k], 0), sem.at[slot]).wait()
        return c

    lax.fori_loop(0, tm, wait, 0, unroll=8)
    route = route_ref[...]
    gates = jnp.concatenate([route, jnp.zeros((LANES - route.shape[0], tm), F32)], axis=0).T
    ff = (gates[:, 4:5] * _from_row_tiles(ybuf.at[slot, 0], tm)
          + gates[:, 5:6] * _from_row_tiles(ybuf.at[slot, 1], tm))
    o_ref[...] = _layer_norm(ALPHA * x_ref[...] + ff, g_ref[...], b_ref[...])


def _moe_ffn(x2d, route, counts, w1, w3, w2, g, b):
    n, d = x2d.shape
    n_exp, _, f = w1.shape
    tr = min(TR_MOE, n)
    tm = min(TM_MOE, n)
    n_pad = 2 * n + n_exp * tr
    n_tiles = n_pad // tr
    i32 = jnp.int32

    cnt = counts[:, 0].astype(i32)
    seg = (cnt + tr - 1) // tr * tr
    seg_end = jnp.cumsum(seg)
    seg_start = seg_end - seg
    e1, e2 = route[0].astype(i32), route[1].astype(i32)
    dest1 = seg_start[e1] + route[2].astype(i32)
    dest2 = seg_start[e2] + route[3].astype(i32)
    n_used = jnp.maximum(seg_end[-1] // tr, 1)
    tile = jnp.minimum(jnp.arange(n_tiles, dtype=i32), n_used - 1)
    tile_expert = jnp.minimum(jnp.sum(tile[:, None] * tr >= seg_end[None, :], axis=1), n_exp - 1).astype(i32)

    cparams = pltpu.CompilerParams(dimension_semantics=("arbitrary",), vmem_limit_bytes=VMEM_LIMIT)
    assert d == ROW_TILE * LANES
    xs = pl.pallas_call(
        functools.partial(_dispatch_kernel, tm=tm, tr=tr),
        grid_spec=pltpu.PrefetchScalarGridSpec(
            num_scalar_prefetch=3, grid=(n // tm,),
            in_specs=[pl.BlockSpec((tm, d), lambda i, *_: (i, 0))],
            out_specs=pl.BlockSpec(memory_space=pl.ANY),
            scratch_shapes=[pltpu.VMEM((2, tm * ROW_TILE, LANES), F32), pltpu.VMEM((tr * ROW_TILE, LANES), F32),
                            pltpu.SemaphoreType.DMA((3,))]),
        out_shape=jax.ShapeDtypeStruct((n_pad * ROW_TILE, LANES), F32),
        compiler_params=cparams,
        name="moe_dispatch",
    )(dest1, dest2, seg_end.astype(i32), x2d)

    ys = pl.pallas_call(
        functools.partial(_expert_kernel, tr=tr),
        grid_spec=pltpu.PrefetchScalarGridSpec(
            num_scalar_prefetch=3, grid=(n_tiles,),
            in_specs=[pl.BlockSpec((tr * ROW_TILE, LANES), lambda i, te, blk, nu: (blk[i], 0)),
                      pl.BlockSpec((1, d, f), lambda i, te, blk, nu: (te[i], 0, 0)),
                      pl.BlockSpec((1, d, f), lambda i, te, blk, nu: (te[i], 0, 0)),
                      pl.BlockSpec((1, f, d), lambda i, te, blk, nu: (te[i], 0, 0))],
            out_specs=pl.BlockSpec((tr * ROW_TILE, LANES), lambda i, te, blk, nu: (i, 0))),
        out_shape=jax.ShapeDtypeStruct((n_pad * ROW_TILE, LANES), F32),
        compiler_params=cparams,
        name="moe_experts",
    )(tile_expert, tile, n_used.reshape(1), xs, w1.astype(BF16), w3.astype(BF16), w2.astype(BF16))

    row = lambda a: a.reshape(1, -1)
    return pl.pallas_call(
        functools.partial(_combine_kernel, tm=tm),
        grid_spec=pltpu.PrefetchScalarGridSpec(
            num_scalar_prefetch=2, grid=(n // tm,),
            in_specs=[pl.BlockSpec((tm, d), lambda i, *_: (i, 0)),
                      pl.BlockSpec((N_EXPERTS, tm), lambda i, *_: (0, i)),
                      pl.BlockSpec((1, d), lambda i, *_: (0, 0)),
                      pl.BlockSpec((1, d), lambda i, *_: (0, 0)),
                      pl.BlockSpec(memory_space=pl.ANY)],
            out_specs=pl.BlockSpec((tm, d), lambda i, *_: (i, 0)),
            scratch_shapes=[pltpu.VMEM((2, 2, tm * ROW_TILE, LANES), F32), pltpu.SemaphoreType.DMA((2,))]),
        out_shape=jax.ShapeDtypeStruct((n, d), F32),
        compiler_params=cparams,
        name="moe_combine",
    )(dest1, dest2, x2d, route, row(g), row(b), ys)


def kernel(x, positions, w_in, mla_q_norm_g, w_uq, mla_kv_norm_g, w_ukv, fox_forget_b, conv_w, conv_b,
           conv_norm_g, conv_norm_b, mla_out_norm_g, fox_out_norm_g, w_out, ln1_g, ln1_b, dense_w1,
           dense_w3, dense_w2, router_w, expert_w1, expert_w3, expert_w2, ln2_g, ln2_b):
    batch, seq, d = x.shape
    assert d == D_MODEL and seq % TQ == 0 and seq % min(TM_IN, seq) == 0
    depth = w_in.shape[0]
    tabs = _rope_tables(positions)
    h = x.reshape(batch * seq, d)
    pw_all = jax.vmap(_prep_inproj_weights)(w_in, w_uq, w_ukv, fox_forget_b)
    w_out, dense_w1, dense_w3, dense_w2, expert_w1, expert_w3, expert_w2 = (
        w.astype(BF16) for w in (w_out, dense_w1, dense_w3, dense_w2, expert_w1, expert_w3, expert_w2))
    for layer in range(depth):
        pw = {name: w[layer] for name, w in pw_all.items()}
        q_t, k, v_t, hc = _input_projection(
            h, tabs, pw, mla_q_norm_g[layer], mla_kv_norm_g[layer], conv_w[layer], conv_b[layer],
            conv_norm_g[layer], conv_norm_b[layer], seq)
        o = _attention(q_t, k, v_t, batch, seq)
        j = layer // 2
        if layer % 2 == 0:
            h = _dense_layer(o, hc, h, mla_out_norm_g[layer], fox_out_norm_g[layer], w_out[layer],
                             ln1_g[layer], ln1_b[layer], dense_w1[j], dense_w3[j], dense_w2[j],
                             ln2_g[layer], ln2_b[layer])
        else:
            h, route, counts = _router_layer(o, hc, h, mla_out_norm_g[layer], fox_out_norm_g[layer], w_out[layer],
                                             ln1_g[layer], ln1_b[layer], router_w[j])
            h = _moe_ffn(h, route, counts, expert_w1[j], expert_w3[j], expert_w2[j], ln2_g[layer],
                         ln2_b[layer])
    return h.reshape(batch, seq, d)
```

```python
import functools
import math

import numpy as np
import jax
import jax.numpy as jnp
from jax import lax
from jax.experimental import pallas as pl
from jax.experimental.pallas import tpu as pltpu

F32 = jnp.float32
BF16 = jnp.bfloat16

D_MODEL = 1024
DEPTH = 4
MLA_HEADS = 8
MLA_NOPE = 64
MLA_ROPE = 32
MLA_V = 64
MLA_Q_RANK = 256
MLA_KV_RANK = 128
ROPE_THETA = 10000.0
FOX_HEADS = 4
FOX_DIM = 64
CONV_CH = 256
CONV_GROUPS = 4
CONV_WIDTH = 31
MLA_WIDTH = MLA_HEADS * MLA_V
FOX_WIDTH = FOX_HEADS * FOX_DIM
N_EXPERTS = 8
ALPHA = (2.0 * DEPTH) ** 0.25
NORM_EPS = 1e-5
LOG2E = math.log2(math.e)

LANES = 128
HEAD_PAD = LANES
N_HEADS = MLA_HEADS + FOX_HEADS
V_DIM = 64
V_ROWS = 80
CONV_HALO = 32
VMEM_LIMIT = 56 * 1024 * 1024

TQ = 512
TK = 256
HPS = 2
TN_ROPE = 4096
TM_IN = 1024
TM_OUT = 512
TM_FFN = 512
TR_MOE = 512
TM_MOE = 512
ROW_TILE = 8
CONV_CHUNK = 64
M_INIT = -1e30


def _nt_dot(a, b):
    return lax.dot_general(a, b, (((1,), (1,)), ((), ())), preferred_element_type=F32)


def _dot(a, b):
    return jnp.dot(a, b, preferred_element_type=F32)


def _split2_dot(a, m_bf16):
    hi = a.astype(BF16)
    lo = (a - hi.astype(F32)).astype(BF16)
    return _dot(hi, m_bf16) + _dot(lo, m_bf16)


def _split3(a):
    hi = a.astype(BF16).astype(F32)
    r1 = a - hi
    mid = r1.astype(BF16).astype(F32)
    lo = (r1 - mid).astype(BF16).astype(F32)
    return hi, mid, lo


def _const_spec(shape):
    nd = len(shape)
    return pl.BlockSpec(shape, lambda *_: (0,) * nd, pipeline_mode=pl.Buffered(1))


def _rope_kernel(pos_ref, invf_ref, c_ref, s_ref, ct_ref, st_ref):
    pos = pos_ref[...].astype(F32)
    ang = invf_ref[...] * pos
    cos = jnp.cos(ang)
    sin = jnp.sin(ang)
    tn = pos.shape[1]
    ct = jnp.concatenate([jnp.ones((MLA_NOPE, tn), F32), cos, cos, jnp.zeros((HEAD_PAD - MLA_NOPE - MLA_ROPE, tn), F32)], axis=0)
    st = jnp.concatenate([jnp.zeros((MLA_NOPE, tn), F32), sin, sin, jnp.zeros((HEAD_PAD - MLA_NOPE - MLA_ROPE, tn), F32)], axis=0)
    ct_ref[...] = ct
    st_ref[...] = st
    c_ref[...] = ct.T
    s_ref[...] = st.T


def _rope_tables(positions):
    n = positions.size
    tn = min(TN_ROPE, n)
    inv_freq = ROPE_THETA ** (-jnp.arange(0, MLA_ROPE, 2, dtype=F32) / MLA_ROPE)
    return pl.pallas_call(
        _rope_kernel,
        grid=(n // tn,),
        in_specs=[pl.BlockSpec((1, tn), lambda i: (0, i)),
                  pl.BlockSpec((MLA_ROPE // 2, 1), lambda i: (0, 0))],
        out_specs=[pl.BlockSpec((tn, HEAD_PAD), lambda i: (i, 0)),
                   pl.BlockSpec((tn, HEAD_PAD), lambda i: (i, 0)),
                   pl.BlockSpec((HEAD_PAD, tn), lambda i: (0, i)),
                   pl.BlockSpec((HEAD_PAD, tn), lambda i: (0, i))],
        out_shape=[jax.ShapeDtypeStruct((n, HEAD_PAD), F32),
                   jax.ShapeDtypeStruct((n, HEAD_PAD), F32),
                   jax.ShapeDtypeStruct((HEAD_PAD, n), F32),
                   jax.ShapeDtypeStruct((HEAD_PAD, n), F32)],
        name="rope_tables",
    )(positions.reshape(1, n), inv_freq.reshape(-1, 1))


_A_CQ = 0
_A_CKV = _A_CQ + MLA_Q_RANK
_A_KR = _A_CKV + MLA_KV_RANK
_A_KRR = _A_KR + HEAD_PAD
_A_FK = _A_KRR + HEAD_PAD
_A_CA = _A_FK + FOX_HEADS * HEAD_PAD
_A_CG = _A_CA + CONV_CH
_A_COLS = _A_CG + CONV_CH
_AUG_ROWS = 8
_F_ROWS = 16


def _rms(x, g):
    ms = jnp.mean(jnp.square(x), axis=-1, keepdims=True)
    return x * lax.rsqrt(ms + NORM_EPS) * g


def _inproj_kernel(x_ref, c_ref, s_ref, ct_ref, st_ref, wa_ref, wfq_ref, wfv_ref, wf_ref, fb_ref,
                   gq_ref, wuq_ref, wuqr_ref, gkv_ref, wuk_ref, wuv_ref,
                   cw_ref, cb_ref, cng_ref, cnb_ref, gmat_ref,
                   qt_ref, k_ref, vt_ref, hc_ref,
                   hbuf, hsh, cbuf, fcarry, upper_ref, *, tiles_per_seq, tm):
    i = pl.program_id(0)

    @pl.when(i % tiles_per_seq == 0)
    def _():
        hbuf[0:CONV_HALO, :] = jnp.zeros((CONV_HALO, CONV_CH), F32)
        fcarry[...] = jnp.zeros_like(fcarry)
        r_i = lax.broadcasted_iota(jnp.int32, (tm, tm), 0)
        c_i = lax.broadcasted_iota(jnp.int32, (tm, tm), 1)
        upper_ref[...] = jnp.where(r_i <= c_i, 1.0, 0.0).astype(BF16)

    xb = x_ref[...].astype(BF16)
    p1 = _dot(xb, wa_ref[...])
    cos_t = c_ref[...]
    sin_t = s_ref[...]
    cos_tt = ct_ref[...]
    sin_tt = st_ref[...]

    cqn = _rms(p1[:, _A_CQ:_A_CQ + MLA_Q_RANK], gq_ref[...]).astype(BF16)
    q_t = _nt_dot(wuq_ref[...], cqn)
    q_rot_t = _nt_dot(wuqr_ref[...], cqn)
    mla_scale = (MLA_NOPE + MLA_ROPE) ** -0.5 * LOG2E
    for h in range(MLA_HEADS):
        rows = slice(h * HEAD_PAD, (h + 1) * HEAD_PAD)
        qh = (q_t[rows, :] * cos_tt + q_rot_t[rows, :] * sin_tt) * mla_scale
        for c in range(tm // TQ):
            qt_ref[c, rows, :] = qh[:, c * TQ:(c + 1) * TQ].astype(BF16)

    ckvn = _rms(p1[:, _A_CKV:_A_CKV + MLA_KV_RANK], gkv_ref[...]).astype(BF16)
    k_nope = _dot(ckvn, wuk_ref[...])
    k_rope = p1[:, _A_KR:_A_KR + HEAD_PAD] * cos_t + p1[:, _A_KRR:_A_KRR + HEAD_PAD] * sin_t
    for h in range(MLA_HEADS):
        cols = slice(h * HEAD_PAD, (h + 1) * HEAD_PAD)
        k_ref[:, cols] = (k_nope[:, cols] + k_rope).astype(BF16)
    v_t = _nt_dot(wuv_ref[...], ckvn)
    fv_t = _nt_dot(wfv_ref[...], xb)
    ones_blk = jnp.where(lax.broadcasted_iota(jnp.int32, (V_ROWS - V_DIM, tm), 0) == 0, 1.0, 0.0)
    for h in range(N_HEADS):
        src = v_t if h < MLA_HEADS else fv_t
        r0 = (h if h < MLA_HEADS else h - MLA_HEADS) * V_DIM
        vh = jnp.concatenate([src[r0:r0 + V_DIM, :], ones_blk], axis=0).astype(BF16)
        for c in range(tm // TK):
            vt_ref[c, h * V_ROWS:(h + 1) * V_ROWS, :] = vh[:, c * TK:(c + 1) * TK]

    z = _nt_dot(wf_ref[...], xb) + fb_ref[...]
    logf = (jnp.minimum(z, 0.0) - jnp.log1p(jnp.exp(-jnp.abs(z)))) * LOG2E
    limbs = jnp.concatenate(_split3(logf), axis=0).astype(BF16)
    sums = _dot(limbs, upper_ref[...])
    cum = (sums[0:_F_ROWS] + sums[_F_ROWS:2 * _F_ROWS]) + sums[2 * _F_ROWS:3 * _F_ROWS]
    f_cum = cum + fcarry[:, 0:1]
    fcarry[...] = jnp.broadcast_to(f_cum[:, tm - 1:tm], fcarry.shape)
    f_hi, f_mid, f_lo = _split3(f_cum)

    fq_t = _nt_dot(wfq_ref[...], xb)
    row8 = lax.broadcasted_iota(jnp.int32, (_AUG_ROWS, tm), 0)
    fox_scale = FOX_DIM ** -0.5 * LOG2E
    for h in range(FOX_HEADS):
        bh = lambda a: jnp.broadcast_to(a[h:h + 1, :], (_AUG_ROWS, tm))
        aug_q = jnp.where(row8 == 0, bh(f_hi), jnp.where(row8 == 1, bh(f_mid), jnp.where(
            row8 == 2, bh(f_lo), jnp.where(row8 < 6, 1.0, 0.0))))
        aug_k = jnp.where(row8 < 3, 1.0, jnp.where(row8 == 3, -bh(f_hi), jnp.where(
            row8 == 4, -bh(f_mid), jnp.where(row8 == 5, -bh(f_lo), 0.0))))
        pad = jnp.zeros((HEAD_PAD - FOX_DIM - _AUG_ROWS, tm), F32)
        qh = jnp.concatenate(
            [fq_t[h * HEAD_PAD:h * HEAD_PAD + FOX_DIM, :] * fox_scale, aug_q, pad], axis=0)
        rows = slice((MLA_HEADS + h) * HEAD_PAD, (MLA_HEADS + h + 1) * HEAD_PAD)
        for c in range(tm // TQ):
            qt_ref[c, rows, :] = qh[:, c * TQ:(c + 1) * TQ].astype(BF16)
        kaug_t = jnp.concatenate([jnp.zeros((FOX_DIM, tm), F32), aug_k, pad], axis=0)
        fk = p1[:, _A_FK + h * HEAD_PAD:_A_FK + (h + 1) * HEAD_PAD]
        k_ref[:, rows] = (fk + kaug_t.T).astype(BF16)

    a = p1[:, _A_CA:_A_CA + CONV_CH]
    g = p1[:, _A_CG:_A_CG + CONV_CH]
    hbuf[CONV_HALO:CONV_HALO + tm, :] = a * jax.nn.sigmoid(g)
    chunk = CONV_CHUNK
    first = CONV_HALO - (CONV_WIDTH - 1)
    for r in range(1, 8):
        hsh[r - 1] = hbuf[r:r + tm + CONV_HALO - 8, :]
    for c0 in range(0, tm, chunk):
        acc = jnp.zeros((chunk, CONV_CH), F32)
        for o in range(first, first + CONV_WIDTH):
            r = o % 8
            row = c0 + o - r
            seg = hbuf[row:row + chunk, :] if r == 0 else hsh[r - 1, row:row + chunk, :]
            acc = acc + cw_ref[o - first:o - first + 1, :] * seg
        cbuf[c0:c0 + chunk, :] = acc
    hbuf[0:CONV_HALO, :] = hbuf[tm:tm + CONV_HALO, :]
    hv = cbuf[...] + cb_ref[...]
    gm = gmat_ref[...]
    mu = _split2_dot(hv, gm)
    d = hv - mu
    var = _split2_dot(d * d, gm)
    hn = d * lax.rsqrt(var + NORM_EPS) * cng_ref[...] + cnb_ref[...]
    hc_ref[...] = (hn * jax.nn.sigmoid(hn)).astype(BF16)


def _prep_inproj_weights(w_in, w_uq, w_ukv, fox_forget_b):
    o = np.cumsum((0, MLA_Q_RANK, MLA_KV_RANK, MLA_ROPE, FOX_WIDTH, FOX_WIDTH, FOX_WIDTH, FOX_HEADS,
                   2 * CONV_CH))
    w_cq, w_ckv, w_kr, w_fq, w_fk, w_fv, w_f, w_cv = (w_in[:, o[i]:o[i + 1]] for i in range(8))
    d = w_in.shape[0]
    half = MLA_ROPE // 2

    def rot_cols(w):
        return jnp.concatenate([-w[..., half:], w[..., :half]], axis=-1)

    def rope_block(w):
        return jnp.pad(w, ((0, 0), (MLA_NOPE, HEAD_PAD - MLA_NOPE - MLA_ROPE)))

    w_fk_pad = jnp.pad(w_fk.reshape(d, FOX_HEADS, FOX_DIM), ((0, 0), (0, 0), (0, HEAD_PAD - FOX_DIM)))
    wa = jnp.concatenate([w_cq, w_ckv, rope_block(w_kr), rope_block(rot_cols(w_kr)),
                          w_fk_pad.reshape(d, FOX_HEADS * HEAD_PAD), w_cv], axis=1)
    w_fq_pad = jnp.pad(w_fq.reshape(d, FOX_HEADS, FOX_DIM), ((0, 0), (0, 0), (0, HEAD_PAD - FOX_DIM)))
    wfq_t = w_fq_pad.reshape(d, FOX_HEADS * HEAD_PAD).T
    wfv_t = w_fv.T
    wf_t = jnp.pad(w_f, ((0, 0), (0, _F_ROWS - FOX_HEADS))).T
    fb = jnp.pad(fox_forget_b, (0, _F_ROWS - FOX_HEADS)).reshape(_F_ROWS, 1)

    uq = w_uq.reshape(MLA_Q_RANK, MLA_HEADS, MLA_NOPE + MLA_ROPE)
    uq_nope, uq_rope = uq[..., :MLA_NOPE], uq[..., MLA_NOPE:]
    tail = ((0, 0), (0, 0), (0, HEAD_PAD - MLA_NOPE - MLA_ROPE))
    uq_pad = jnp.pad(jnp.concatenate([uq_nope, uq_rope], axis=-1), tail)
    uq_rot_pad = jnp.pad(jnp.concatenate([jnp.zeros_like(uq_nope), rot_cols(uq_rope)], axis=-1), tail)
    wuq_t = uq_pad.reshape(MLA_Q_RANK, MLA_HEADS * HEAD_PAD).T
    wuqr_t = uq_rot_pad.reshape(MLA_Q_RANK, MLA_HEADS * HEAD_PAD).T
    ukv = w_ukv.reshape(MLA_KV_RANK, MLA_HEADS, MLA_NOPE + MLA_V)
    wuk = jnp.pad(ukv[..., :MLA_NOPE], ((0, 0), (0, 0), (0, HEAD_PAD - MLA_NOPE))).reshape(
        MLA_KV_RANK, MLA_HEADS * HEAD_PAD)
    wuv_t = ukv[..., MLA_NOPE:].reshape(MLA_KV_RANK, MLA_WIDTH).T
    bf = lambda a: a.astype(BF16)
    return dict(wa=bf(wa), wfq=bf(wfq_t), wfv=bf(wfv_t), wf=bf(wf_t), fb=fb, wuq=bf(wuq_t),
                wuqr=bf(wuqr_t), wuk=bf(wuk), wuv=bf(wuv_t))


def _input_projection(x2d, tabs, pw, gq, gkv, conv_w, conv_b, conv_ng, conv_nb, seq):
    n, d = x2d.shape
    tm = min(TM_IN, seq)
    cos_t, sin_t, cos_tt, sin_tt = tabs
    gidx = np.arange(CONV_CH) // (CONV_CH // CONV_GROUPS)
    gmat = jnp.asarray((gidx[:, None] == gidx[None, :]) / (CONV_CH // CONV_GROUPS), BF16)
    cw = jnp.pad(conv_w, ((0, CONV_HALO - CONV_WIDTH), (0, 0)))
    row = lambda a: a.reshape(1, -1)
    tok = lambda w: pl.BlockSpec((tm, w), lambda i: (i, 0))
    tok_t = lambda r: pl.BlockSpec((r, tm), lambda i: (0, i))
    consts = [pw["wa"], pw["wfq"], pw["wfv"], pw["wf"], pw["fb"], row(gq), pw["wuq"], pw["wuqr"],
              row(gkv), pw["wuk"], pw["wuv"], cw, row(conv_b), row(conv_ng), row(conv_nb), gmat]
    kern = functools.partial(_inproj_kernel, tiles_per_seq=seq // tm, tm=tm)
    return pl.pallas_call(
        kern,
        grid=(n // tm,),
        in_specs=[tok(d), tok(HEAD_PAD), tok(HEAD_PAD), tok_t(HEAD_PAD), tok_t(HEAD_PAD)]
        + [_const_spec(c.shape) for c in consts],
        out_specs=[pl.BlockSpec((tm // TQ, N_HEADS * HEAD_PAD, TQ), lambda i: (i, 0, 0)),
                   tok(N_HEADS * HEAD_PAD),
                   pl.BlockSpec((tm // TK, N_HEADS * V_ROWS, TK), lambda i: (i, 0, 0)),
                   tok(CONV_CH)],
        out_shape=[jax.ShapeDtypeStruct((n // TQ, N_HEADS * HEAD_PAD, TQ), BF16),
                   jax.ShapeDtypeStruct((n, N_HEADS * HEAD_PAD), BF16),
                   jax.ShapeDtypeStruct((n // TK, N_HEADS * V_ROWS, TK), BF16),
                   jax.ShapeDtypeStruct((n, CONV_CH), BF16)],
        scratch_shapes=[pltpu.VMEM((CONV_HALO + tm, CONV_CH), F32),
                        pltpu.VMEM((7, tm + CONV_HALO - 8, CONV_CH), F32),
                        pltpu.VMEM((tm, CONV_CH), F32),
                        pltpu.VMEM((_F_ROWS, LANES), F32),
                        pltpu.VMEM((tm, tm), BF16)],
        compiler_params=pltpu.CompilerParams(dimension_semantics=("arbitrary",),
                                             vmem_limit_bytes=VMEM_LIMIT),
        name="input_projection",
    )(x2d, cos_t, sin_t, cos_tt, sin_tt, *consts)


def _attn_kernel(qt_ref, k_ref, vt_ref, o_ref, *scratch, n_tiles):
    assert TQ == 2 * TK
    s_ref = (scratch[0:HPS], scratch[HPS:2 * HPS])
    sd_ref = scratch[2 * HPS:3 * HPS]
    p_ref = (scratch[3 * HPS:4 * HPS], scratch[4 * HPS:5 * HPS])
    pd_ref = scratch[5 * HPS:6 * HPS]
    acc_bufs = (scratch[6 * HPS:7 * HPS], scratch[7 * HPS:8 * HPS])
    diff_ref = scratch[8 * HPS]
    diff_ref[...] = (lax.broadcasted_iota(jnp.int32, (TK, TQ), 1)
                     - lax.broadcasted_iota(jnp.int32, (TK, TQ), 0))
    for h in range(HPS):
        p_ref[1][h][...] = jnp.zeros_like(p_ref[1][h])
        for par in range(2):
            acc_bufs[par][h][...] = jnp.ones_like(acc_bufs[par][h])

    def tile_scores(tile, j, slot):
        row0 = pl.multiple_of(j * TK, TK)
        block_max = []
        for h in range(HPS):
            s = _dot(k_ref[pl.ds(row0, TK), h * HEAD_PAD:(h + 1) * HEAD_PAD],
                     qt_ref[tile, h * HEAD_PAD:(h + 1) * HEAD_PAD, :])
            s_ref[slot][h][...] = s
            block_max.append(jnp.max(s, axis=0, keepdims=True))
        return block_max

    def last_diag_scores(tile):
        row0 = pl.multiple_of((2 * tile + 1) * TK, TK)
        for h in range(HPS):
            sd_ref[h][...] = _dot(k_ref[pl.ds(row0, TK), h * HEAD_PAD:(h + 1) * HEAD_PAD],
                                  qt_ref[tile, h * HEAD_PAD:(h + 1) * HEAD_PAD, TK:])

    def finalize(tile, par):
        out_t = jnp.concatenate([acc_bufs[par][h][0:V_DIM, :] / acc_bufs[par][h][V_DIM:V_DIM + 1, :]
                                 for h in range(HPS)], axis=0)
        o_ref[pl.ds(pl.multiple_of(tile * TQ, TQ), TQ), :] = out_t.T.astype(o_ref.dtype)

    def q_tile(i, par, bm0):
        acc_ref = acc_bufs[par]
        scores = functools.partial(tile_scores, i)

        def softmax(slot, m, block_max):
            m_new = [jnp.maximum(m[h], block_max[h]) for h in range(HPS)]
            for h in range(HPS):
                p_ref[slot][h][...] = jnp.exp2(s_ref[slot][h][...] - m_new[h]).astype(BF16)
            return m_new, [jnp.exp2(m[h] - m_new[h]) for h in range(HPS)]

        def values(j, slot, alpha, gate=None):
            for h in range(HPS):
                pv = _dot(vt_ref[j, h * V_ROWS:(h + 1) * V_ROWS, :], p_ref[slot][h][...])
                acc_ref[h][...] = alpha[h] * acc_ref[h][...] + (pv if gate is None else gate * pv)

        m0 = [jnp.full((1, TQ), M_INIT, F32)] * HPS
        a0 = [jnp.zeros((1, TQ), F32)] * HPS

        def pair(u, state):
            m, alpha, bm_t = list(state[0:HPS]), list(state[HPS:2 * HPS]), list(state[2 * HPS:3 * HPS])
            t = 2 * u
            m, alpha_t = softmax(0, m, bm_t)
            values(jnp.maximum(t - 1, 0), 1, alpha, jnp.where(t > 0, 1.0, 0.0))
            bm_t1 = scores(t + 1, 1)
            m, alpha_t1 = softmax(1, m, bm_t1)
            values(t, 0, alpha_t)
            bm_t2 = scores(t + 2, 0)
            return (*m, *alpha_t1, *bm_t2)

        state = lax.fori_loop(0, i // 2, lambda v, st: pair(2 * v + 1, pair(2 * v, st)), (*m0, *a0, *bm0))
        state = lax.fori_loop(i - i % 2, i, pair, state)
        m, alpha = list(state[0:HPS]), list(state[HPS:2 * HPS])
        d0 = 2 * i
        values(jnp.maximum(d0 - 1, 0), 1, alpha, jnp.where(i > 0, 1.0, 0.0))
        finalize(jnp.where(i > 0, i - 1, n_tiles - 1), 1 - par)
        nxt = jnp.minimum(i + 1, n_tiles - 1)

        m_d0, alpha_d0 = [], []
        for h in range(HPS):
            s = s_ref[0][h][...]
            s = jnp.concatenate([jnp.where(diff_ref[:, 0:TK] >= 0, s[:, 0:TK], -jnp.inf), s[:, TK:]], axis=1)
            m_d0.append(jnp.maximum(m[h], jnp.max(s, axis=0, keepdims=True)))
            p_ref[0][h][...] = jnp.exp2(s - m_d0[h]).astype(BF16)
            alpha_d0.append(jnp.exp2(m[h] - m_d0[h]))
        bm_next = tile_scores(nxt, 0, 0)

        alpha_d1 = []
        for h in range(HPS):
            s = jnp.where(diff_ref[:, 0:TK] >= 0, sd_ref[h][...], -jnp.inf)
            m_old = m_d0[h][:, TK:]
            m_new = jnp.maximum(m_old, jnp.max(s, axis=0, keepdims=True))
            pd_ref[h][...] = jnp.exp2(s - m_new).astype(BF16)
            alpha_d1.append(jnp.exp2(m_old - m_new))
        last_diag_scores(nxt)

        values(d0, 0, alpha_d0)
        for h in range(HPS):
            acc_ref[h][:, TK:] = alpha_d1[h] * acc_ref[h][:, TK:] + _dot(
                vt_ref[d0 + 1, h * V_ROWS:(h + 1) * V_ROWS, :], pd_ref[h][...])
        return tuple(bm_next)

    last_diag_scores(0)
    bm = lax.fori_loop(0, n_tiles // 2, lambda a, st: q_tile(2 * a + 1, 1, q_tile(2 * a, 0, st)),
                       tuple(tile_scores(0, 0, 0)))
    if n_tiles % 2:
        q_tile(n_tiles - 1, 0, bm)
    finalize(n_tiles - 1, (n_tiles - 1) % 2)


def _attention(q_t, k, v_t, batch, seq):
    n = k.shape[0]
    groups = N_HEADS // HPS
    return pl.pallas_call(
        functools.partial(_attn_kernel, n_tiles=seq // TQ),
        grid=(batch, groups),
        in_specs=[pl.BlockSpec((seq // TQ, HPS * HEAD_PAD, TQ), lambda b, p: (b, p, 0)),
                  pl.BlockSpec((seq, HPS * HEAD_PAD), lambda b, p: (b, p)),
                  pl.BlockSpec((seq // TK, HPS * V_ROWS, TK), lambda b, p: (b, p, 0))],
        out_specs=pl.BlockSpec((seq, HPS * V_DIM), lambda b, p: (b, p)),
        out_shape=jax.ShapeDtypeStruct((n, N_HEADS * V_DIM), BF16),
        scratch_shapes=[pltpu.VMEM((TK, TQ), F32)] * (2 * HPS) + [pltpu.VMEM((TK, TK), F32)] * HPS
        + [pltpu.VMEM((TK, TQ), BF16)] * (2 * HPS) + [pltpu.VMEM((TK, TK), BF16)] * HPS
        + [pltpu.VMEM((V_ROWS, TQ), F32)] * (2 * HPS) + [pltpu.VMEM((TK, TQ), jnp.int32)],
        compiler_params=pltpu.CompilerParams(dimension_semantics=("arbitrary", "arbitrary"),
                                             vmem_limit_bytes=VMEM_LIMIT),
        name="attention",
    )(q_t, k, v_t)


def _layer_norm(x, g, b):
    mu = jnp.mean(x, axis=-1, keepdims=True)
    d = x - mu
    var = jnp.mean(jnp.square(d), axis=-1, keepdims=True)
    return d * lax.rsqrt(var + NORM_EPS) * g + b


def _mix_and_norm(o_ref, hc_ref, x_ref, gm_ref, gf_ref, wo_ref, g1_ref, b1_ref):
    o = o_ref[...].astype(F32)
    mla = _rms(o[:, :MLA_WIDTH], gm_ref[...])
    fox = _rms(o[:, MLA_WIDTH:], gf_ref[...])
    mixed = jnp.concatenate([mla.astype(BF16), fox.astype(BF16), hc_ref[...]], axis=-1)
    y = _dot(mixed, wo_ref[...])
    return _layer_norm(ALPHA * x_ref[...] + y, g1_ref[...], b1_ref[...])


def _router_layer_kernel(o_ref, hc_ref, x_ref, gm_ref, gf_ref, wo_ref, g1_ref, b1_ref, rw_ref,
                         x1_ref, route_ref, counts_ref, cnt_ref, upper_ref):
    x1 = _mix_and_norm(o_ref, hc_ref, x_ref, gm_ref, gf_ref, wo_ref, g1_ref, b1_ref)
    x1_ref[...] = x1
    rw = rw_ref[...]
    x_hi = x1.astype(BF16)
    x_lo = (x1 - x_hi.astype(F32)).astype(BF16)
    w_hi = rw.astype(BF16)
    w_lo = (rw - w_hi.astype(F32)).astype(BF16)
    both = _dot(x_hi, jnp.concatenate([w_hi, w_lo], axis=1))
    logits = both[:, :LANES] + (_dot(x_lo, w_hi) + both[:, LANES:])
    tm = logits.shape[0]
    lg = logits.T[0:N_EXPERTS, :]
    row = lax.broadcasted_iota(jnp.int32, lg.shape, 0)
    v1 = jnp.max(lg, axis=0, keepdims=True)
    i1 = jnp.min(jnp.where(lg == v1, row, N_EXPERTS), axis=0, keepdims=True)
    rest_l = jnp.where(row == i1, -jnp.inf, lg)
    v2 = jnp.max(rest_l, axis=0, keepdims=True)
    i2 = jnp.min(jnp.where(rest_l == v2, row, N_EXPERTS), axis=0, keepdims=True)
    e2 = jnp.exp(v2 - v1)
    den = 1.0 + e2

    @pl.when(pl.program_id(0) == 0)
    def _():
        cnt_ref[...] = jnp.zeros_like(cnt_ref)
        r_i = lax.broadcasted_iota(jnp.int32, (tm, tm), 0)
        c_i = lax.broadcasted_iota(jnp.int32, (tm, tm), 1)
        upper_ref[...] = jnp.where(r_i < c_i, 1.0, 0.0).astype(BF16)

    sel = jnp.where(row == i1, 1.0, jnp.where(row == i2, 1.0, 0.0))
    sel16 = jnp.concatenate([sel, jnp.zeros_like(sel)], axis=0).astype(BF16)
    before = cnt_ref[:, 0:1]
    rank = _dot(sel16, upper_ref[...])[0:N_EXPERTS, :] + before
    total = before + jnp.sum(sel, axis=1, keepdims=True)
    cnt_ref[...] = jnp.broadcast_to(total, cnt_ref.shape)
    counts_ref[...] = jnp.broadcast_to(total, counts_ref.shape)
    r1 = jnp.sum(jnp.where(row == i1, rank, 0.0), axis=0, keepdims=True)
    r2 = jnp.sum(jnp.where(row == i2, rank, 0.0), axis=0, keepdims=True)
    rows = (i1.astype(F32), i2.astype(F32), r1, r2, 1.0 / den, e2 / den)
    route = jnp.zeros(lg.shape, F32)
    for c, v in enumerate(rows):
        route = jnp.where(row == c, v, route)
    route_ref[...] = route


def _router_layer(o, hc, x2d, gm, gf, w_out, g1, b1, router_w):
    n, d = x2d.shape
    tm = min(TM_OUT, n)
    row = lambda a: a.reshape(1, -1)
    tok = lambda w: pl.BlockSpec((tm, w), lambda i: (i, 0))
    consts = [row(gm), row(gf), w_out.astype(BF16), row(g1), row(b1),
              jnp.pad(router_w, ((0, 0), (0, LANES - N_EXPERTS)))]
    return pl.pallas_call(
        _router_layer_kernel,
        grid=(n // tm,),
        in_specs=[tok(o.shape[1]), tok(CONV_CH), tok(d)] + [_const_spec(c.shape) for c in consts],
        out_specs=[tok(d), pl.BlockSpec((N_EXPERTS, tm), lambda i: (0, i)),
                   pl.BlockSpec((N_EXPERTS, LANES), lambda i: (0, 0))],
        out_shape=[jax.ShapeDtypeStruct((n, d), F32), jax.ShapeDtypeStruct((N_EXPERTS, n), F32),
                   jax.ShapeDtypeStruct((N_EXPERTS, LANES), F32)],
        scratch_shapes=[pltpu.VMEM((N_EXPERTS, LANES), F32), pltpu.VMEM((tm, tm), BF16)],
        compiler_params=pltpu.CompilerParams(dimension_semantics=("arbitrary",),
                                             vmem_limit_bytes=VMEM_LIMIT),
        name="router_layer",
    )(o, hc, x2d, *consts)


def _swiglu_tile(xb, w1, w3, w2):
    h1 = _dot(xb, w1)
    h3 = _dot(xb, w3)
    hid = (h1 * jax.nn.sigmoid(h1) * h3).astype(BF16)
    return _dot(hid, w2)


def _dense_layer_kernel(o_ref, hc_ref, x_ref, gm_ref, gf_ref, wo_ref, g1_ref, b1_ref,
                        w1_ref, w3_ref, w2_ref, g2_ref, b2_ref, out_ref, *, f_chunk):
    x1 = _mix_and_norm(o_ref, hc_ref, x_ref, gm_ref, gf_ref, wo_ref, g1_ref, b1_ref)
    xb = x1.astype(BF16)
    ff = None
    for c0 in range(0, w1_ref.shape[1], f_chunk):
        part = _swiglu_tile(xb, w1_ref[:, c0:c0 + f_chunk], w3_ref[:, c0:c0 + f_chunk],
                            w2_ref[c0:c0 + f_chunk, :])
        ff = part if ff is None else ff + part
    out_ref[...] = _layer_norm(ALPHA * x1 + ff, g2_ref[...], b2_ref[...])


def _dense_layer(o, hc, x2d, gm, gf, w_out, g1, b1, w1, w3, w2, g2, b2):
    n, d = x2d.shape
    tm = min(TM_FFN, n)
    f = w1.shape[1]
    f_chunk = f // 2 if (f // 2) % LANES == 0 else f
    row = lambda a: a.reshape(1, -1)
    tok = lambda w: pl.BlockSpec((tm, w), lambda i: (i, 0))
    consts = [row(gm), row(gf), w_out.astype(BF16), row(g1), row(b1),
              w1.astype(BF16), w3.astype(BF16), w2.astype(BF16), row(g2), row(b2)]
    return pl.pallas_call(
        functools.partial(_dense_layer_kernel, f_chunk=f_chunk),
        grid=(n // tm,),
        in_specs=[tok(o.shape[1]), tok(CONV_CH), tok(d)] + [_const_spec(c.shape) for c in consts],
        out_specs=tok(d),
        out_shape=jax.ShapeDtypeStruct((n, d), F32),
        compiler_params=pltpu.CompilerParams(dimension_semantics=("arbitrary",),
                                             vmem_limit_bytes=VMEM_LIMIT),
        name="dense_layer",
    )(o, hc, x2d, *consts)


def _to_row_tiles(ref, x):
    for c in range(ROW_TILE):
        ref[pl.ds(c, x.shape[0], stride=ROW_TILE), :] = x[:, c * LANES:(c + 1) * LANES]


def _from_row_tiles(ref, t):
    return jnp.concatenate([ref[pl.ds(c, t, stride=ROW_TILE), :] for c in range(ROW_TILE)], axis=-1)


def _row_tile(ref, r):
    return ref.at[pl.ds(pl.multiple_of(r * ROW_TILE, ROW_TILE), ROW_TILE)]


def _dispatch_kernel(d1_ref, d2_ref, se_ref, x_ref, xs_ref, xr, zbuf, sem, *, tm, tr):
    i = pl.program_id(0)

    @pl.when(i == 0)
    def _():
        zbuf[...] = jnp.zeros_like(zbuf)
        for e in range(N_EXPERTS):
            end = se_ref[e]
            start_e = se_ref[e - 1] if e else 0

            for first, live in ((end - tr, end > start_e),
                                (se_ref[N_EXPERTS - 1] + e * tr,
                                 (se_ref[N_EXPERTS - 1] + e * tr) * ROW_TILE < xs_ref.shape[0])):
                @pl.when(live)
                def _():
                    rows = pl.ds(pl.multiple_of(first * ROW_TILE, ROW_TILE), tr * ROW_TILE)
                    fill = pltpu.make_async_copy(zbuf, xs_ref.at[rows], sem.at[2])
                    fill.start()
                    fill.wait()

    slot = i % 2
    _to_row_tiles(xr.at[slot], x_ref[...])
    base = i * tm

    def start(r, c):
        src = _row_tile(xr.at[slot], r)
        pltpu.make_async_copy(src, _row_tile(xs_ref, d1_ref[base + r]), sem.at[slot]).start()
        pltpu.make_async_copy(src, _row_tile(xs_ref, d2_ref[base + r]), sem.at[slot]).start(priority=1)
        return c

    def wait_step(s):
        def wait(r, c):
            for _ in range(2):
                pltpu.make_async_copy(_row_tile(xr.at[s], 0), _row_tile(xs_ref, 0), sem.at[s]).wait()
            return c

        lax.fori_loop(0, tm, wait, 0, unroll=8)

    lax.fori_loop(0, tm, start, 0, unroll=8)

    @pl.when(i > 0)
    def _():
        wait_step(1 - slot)

    @pl.when(i == pl.num_programs(0) - 1)
    def _():
        wait_step(slot)


def _expert_kernel(te_ref, blk_ref, nu_ref, xs_ref, w1_ref, w3_ref, w2_ref, ys_ref, *, tr):
    del te_ref, blk_ref
    used = pl.program_id(0) < nu_ref[0]

    @pl.when(used)
    def _():
        xb = _from_row_tiles(xs_ref, tr).astype(BF16)
        _to_row_tiles(ys_ref, _swiglu_tile(xb, w1_ref[0], w3_ref[0], w2_ref[0]))

    @pl.when(jnp.logical_not(used))
    def _():
        ys_ref[...] = jnp.zeros_like(ys_ref)


def _combine_kernel(d1_ref, d2_ref, x_ref, route_ref, g_ref, b_ref, ys_ref, o_ref, ybuf, sem, *, tm):
    i = pl.program_id(0)
    n_steps = pl.num_programs(0)

    def issue(tile, slot):
        base = tile * tm

        def start(r, c):
            pltpu.make_async_copy(_row_tile(ys_ref, d1_ref[base + r]), _row_tile(ybuf.at[slot, 0], r),
                                  sem.at[slot]).start()
            pltpu.make_async_copy(_row_tile(ys_ref, d2_ref[base + r]), _row_tile(ybuf.at[slot, 1], r),
                                  sem.at[slot]).start(priority=1)
            return c

        lax.fori_loop(0, tm, start, 0, unroll=8)

    @pl.when(i == 0)
    def _():
        issue(0, 0)

    @pl.when(i + 1 < n_steps)
    def _():
        issue(i + 1, (i + 1) % 2)

    slot = i % 2

    def wait(r, c):
        for k in range(2):
            pltpu.make_async_copy(_row_tile(ys_ref, 0), _row_tile(ybuf.at[slot, k], 0), sem.at[slot]).wait()
        return c

    lax.fori_loop(0, tm, wait, 0, unroll=8)
    route = route_ref[...]
    gates = jnp.concatenate([route, jnp.zeros((LANES - route.shape[0], tm), F32)], axis=0).T
    ff = (gates[:, 4:5] * _from_row_tiles(ybuf.at[slot, 0], tm)
          + gates[:, 5:6] * _from_row_tiles(ybuf.at[slot, 1], tm))
    o_ref[...] = _layer_norm(ALPHA * x_ref[...] + ff, g_ref[...], b_ref[...])


def _moe_ffn(x2d, route, counts, w1, w3, w2, g, b):
    n, d = x2d.shape
    n_exp, _, f = w1.shape
    tr = min(TR_MOE, n)
    tm = min(TM_MOE, n)
    n_pad = 2 * n + n_exp * tr
    n_tiles = n_pad // tr
    i32 = jnp.int32

    cnt = counts[:, 0].astype(i32)
    seg = (cnt + tr - 1) // tr * tr
    seg_end = jnp.cumsum(seg)
    seg_start = seg_end - seg
    e1, e2 = route[0].astype(i32), route[1].astype(i32)
    dest1 = seg_start[e1] + route[2].astype(i32)
    dest2 = seg_start[e2] + route[3].astype(i32)
    n_used = jnp.maximum(seg_end[-1] // tr, 1)
    tile = jnp.minimum(jnp.arange(n_tiles, dtype=i32), n_used - 1)
    tile_expert = jnp.minimum(jnp.sum(tile[:, None] * tr >= seg_end[None, :], axis=1), n_exp - 1).astype(i32)

    cparams = pltpu.CompilerParams(dimension_semantics=("arbitrary",), vmem_limit_bytes=VMEM_LIMIT)
    assert d == ROW_TILE * LANES
    xs = pl.pallas_call(
        functools.partial(_dispatch_kernel, tm=tm, tr=tr),
        grid_spec=pltpu.PrefetchScalarGridSpec(
            num_scalar_prefetch=3, grid=(n // tm,),
            in_specs=[pl.BlockSpec((tm, d), lambda i, *_: (i, 0))],
            out_specs=pl.BlockSpec(memory_space=pl.ANY),
            scratch_shapes=[pltpu.VMEM((2, tm * ROW_TILE, LANES), F32), pltpu.VMEM((tr * ROW_TILE, LANES), F32),
                            pltpu.SemaphoreType.DMA((3,))]),
        out_shape=jax.ShapeDtypeStruct((n_pad * ROW_TILE, LANES), F32),
        compiler_params=cparams,
        name="moe_dispatch",
    )(dest1, dest2, seg_end.astype(i32), x2d)

    ys = pl.pallas_call(
        functools.partial(_expert_kernel, tr=tr),
        grid_spec=pltpu.PrefetchScalarGridSpec(
            num_scalar_prefetch=3, grid=(n_tiles,),
            in_specs=[pl.BlockSpec((tr * ROW_TILE, LANES), lambda i, te, blk, nu: (blk[i], 0)),
                      pl.BlockSpec((1, d, f), lambda i, te, blk, nu: (te[i], 0, 0)),
                      pl.BlockSpec((1, d, f), lambda i, te, blk, nu: (te[i], 0, 0)),
                      pl.BlockSpec((1, f, d), lambda i, te, blk, nu: (te[i], 0, 0))],
            out_specs=pl.BlockSpec((tr * ROW_TILE, LANES), lambda i, te, blk, nu: (i, 0))),
        out_shape=jax.ShapeDtypeStruct((n_pad * ROW_TILE, LANES), F32),
        compiler_params=cparams,
        name="moe_experts",
    )(tile_expert, tile, n_used.reshape(1), xs, w1.astype(BF16), w3.astype(BF16), w2.astype(BF16))

    row = lambda a: a.reshape(1, -1)
    return pl.pallas_call(
        functools.partial(_combine_kernel, tm=tm),
        grid_spec=pltpu.PrefetchScalarGridSpec(
            num_scalar_prefetch=2, grid=(n // tm,),
            in_specs=[pl.BlockSpec((tm, d), lambda i, *_: (i, 0)),
                      pl.BlockSpec((N_EXPERTS, tm), lambda i, *_: (0, i)),
                      pl.BlockSpec((1, d), lambda i, *_: (0, 0)),
                      pl.BlockSpec((1, d), lambda i, *_: (0, 0)),
                      pl.BlockSpec(memory_space=pl.ANY)],
            out_specs=pl.BlockSpec((tm, d), lambda i, *_: (i, 0)),
            scratch_shapes=[pltpu.VMEM((2, 2, tm * ROW_TILE, LANES), F32), pltpu.SemaphoreType.DMA((2,))]),
        out_shape=jax.ShapeDtypeStruct((n, d), F32),
        compiler_params=cparams,
        name="moe_combine",
    )(dest1, dest2, x2d, route, row(g), row(b), ys)


def kernel(x, positions, w_in, mla_q_norm_g, w_uq, mla_kv_norm_g, w_ukv, fox_forget_b, conv_w, conv_b,
           conv_norm_g, conv_norm_b, mla_out_norm_g, fox_out_norm_g, w_out, ln1_g, ln1_b, dense_w1,
           dense_w3, dense_w2, router_w, expert_w1, expert_w3, expert_w2, ln2_g, ln2_b):
    batch, seq, d = x.shape
    assert d == D_MODEL and seq % TQ == 0 and seq % min(TM_IN, seq) == 0
    depth = w_in.shape[0]
    tabs = _rope_tables(positions)
    h = x.reshape(batch * seq, d)
    pw_all = jax.vmap(_prep_inproj_weights)(w_in, w_uq, w_ukv, fox_forget_b)
    w_out, dense_w1, dense_w3, dense_w2, expert_w1, expert_w3, expert_w2 = (
        w.astype(BF16) for w in (w_out, dense_w1, dense_w3, dense_w2, expert_w1, expert_w3, expert_w2))
    for layer in range(depth):
        pw = {name: w[layer] for name, w in pw_all.items()}
        q_t, k, v_t, hc = _input_projection(
            h, tabs, pw, mla_q_norm_g[layer], mla_kv_norm_g[layer], conv_w[layer], conv_b[layer],
            conv_norm_g[layer], conv_norm_b[layer], seq)
        o = _attention(q_t, k, v_t, batch, seq)
        j = layer // 2
        if layer % 2 == 0:
            h = _dense_layer(o, hc, h, mla_out_norm_g[layer], fox_out_norm_g[layer], w_out[layer],
                             ln1_g[layer], ln1_b[layer], dense_w1[j], dense_w3[j], dense_w2[j],
                             ln2_g[layer], ln2_b[layer])
        else:
            h, route, counts = _router_layer(o, hc, h, mla_out_norm_g[layer], fox_out_norm_g[layer], w_out[layer],
                                             ln1_g[layer], ln1_b[layer], router_w[j])
            h = _moe_ffn(h, route, counts, expert_w1[j], expert_w3[j], expert_w2[j], ln2_g[layer],
                         ln2_b[layer])
    return h.reshape(batch, seq, d)
```

```python
import functools
import math

import numpy as np
import jax
import jax.numpy as jnp
from jax import lax
from jax.experimental import pallas as pl
from jax.experimental.pallas import tpu as pltpu

F32 = jnp.float32
BF16 = jnp.bfloat16

D_MODEL = 1024
DEPTH = 4
MLA_HEADS = 8
MLA_NOPE = 64
MLA_ROPE = 32
MLA_V = 64
MLA_Q_RANK = 256
MLA_KV_RANK = 128
ROPE_THETA = 10000.0
FOX_HEADS = 4
FOX_DIM = 64
CONV_CH = 256
CONV_GROUPS = 4
CONV_WIDTH = 31
MLA_WIDTH = MLA_HEADS * MLA_V
FOX_WIDTH = FOX_HEADS * FOX_DIM
N_EXPERTS = 8
ALPHA = (2.0 * DEPTH) ** 0.25
NORM_EPS = 1e-5
LOG2E = math.log2(math.e)

LANES = 128
HEAD_PAD = LANES
N_HEADS = MLA_HEADS + FOX_HEADS
V_DIM = 64
V_ROWS = 80
CONV_HALO = 32
VMEM_LIMIT = 56 * 1024 * 1024

TQ = 512
TK = 256
HPS = 2
TN_ROPE = 4096
TM_IN = 1024
TM_OUT = 1024
TM_FFN = 512
TR_MOE = 512
TM_MOE = 1024
ROW_TILE = 8
CONV_CHUNK = 64
M_INIT = -1e30


def _nt_dot(a, b):
    return lax.dot_general(a, b, (((1,), (1,)), ((), ())), preferred_element_type=F32)


def _dot(a, b):
    return jnp.dot(a, b, preferred_element_type=F32)


def _split2_dot(a, m_bf16):
    hi = a.astype(BF16)
    lo = (a - hi.astype(F32)).astype(BF16)
    return _dot(hi, m_bf16) + _dot(lo, m_bf16)


def _split3(a):
    hi = a.astype(BF16).astype(F32)
    r1 = a - hi
    mid = r1.astype(BF16).astype(F32)
    lo = (r1 - mid).astype(BF16).astype(F32)
    return hi, mid, lo


def _const_spec(shape):
    nd = len(shape)
    return pl.BlockSpec(shape, lambda *_: (0,) * nd, pipeline_mode=pl.Buffered(1))


def _rope_kernel(pos_ref, invf_ref, c_ref, s_ref, ct_ref, st_ref):
    pos = pos_ref[...].astype(F32)
    ang = invf_ref[...] * pos
    cos = jnp.cos(ang)
    sin = jnp.sin(ang)
    tn = pos.shape[1]
    ct = jnp.concatenate([jnp.ones((MLA_NOPE, tn), F32), cos, cos, jnp.zeros((HEAD_PAD - MLA_NOPE - MLA_ROPE, tn), F32)], axis=0)
    st = jnp.concatenate([jnp.zeros((MLA_NOPE, tn), F32), sin, sin, jnp.zeros((HEAD_PAD - MLA_NOPE - MLA_ROPE, tn), F32)], axis=0)
    ct_ref[...] = ct
    st_ref[...] = st
    c_ref[...] = ct.T
    s_ref[...] = st.T


def _rope_tables(positions):
    n = positions.size
    tn = min(TN_ROPE, n)
    inv_freq = ROPE_THETA ** (-jnp.arange(0, MLA_ROPE, 2, dtype=F32) / MLA_ROPE)
    return pl.pallas_call(
        _rope_kernel,
        grid=(n // tn,),
        in_specs=[pl.BlockSpec((1, tn), lambda i: (0, i)),
                  pl.BlockSpec((MLA_ROPE // 2, 1), lambda i: (0, 0))],
        out_specs=[pl.BlockSpec((tn, HEAD_PAD), lambda i: (i, 0)),
                   pl.BlockSpec((tn, HEAD_PAD), lambda i: (i, 0)),
                   pl.BlockSpec((HEAD_PAD, tn), lambda i: (0, i)),
                   pl.BlockSpec((HEAD_PAD, tn), lambda i: (0, i))],
        out_shape=[jax.ShapeDtypeStruct((n, HEAD_PAD), F32),
                   jax.ShapeDtypeStruct((n, HEAD_PAD), F32),
                   jax.ShapeDtypeStruct((HEAD_PAD, n), F32),
                   jax.ShapeDtypeStruct((HEAD_PAD, n), F32)],
        name="rope_tables",
    )(positions.reshape(1, n), inv_freq.reshape(-1, 1))


_A_CQ = 0
_A_CKV = _A_CQ + MLA_Q_RANK
_A_KR = _A_CKV + MLA_KV_RANK
_A_KRR = _A_KR + HEAD_PAD
_A_FK = _A_KRR + HEAD_PAD
_A_CA = _A_FK + FOX_HEADS * HEAD_PAD
_A_CG = _A_CA + CONV_CH
_A_COLS = _A_CG + CONV_CH
_AUG_ROWS = 8
_F_ROWS = 16


def _rms(x, g):
    ms = jnp.mean(jnp.square(x), axis=-1, keepdims=True)
    return x * lax.rsqrt(ms + NORM_EPS) * g


def _inproj_kernel(x_ref, c_ref, s_ref, ct_ref, st_ref, wa_ref, wfq_ref, wfv_ref, wf_ref, fb_ref,
                   gq_ref, wuq_ref, wuqr_ref, gkv_ref, wuk_ref, wuv_ref,
                   cw_ref, cb_ref, cng_ref, cnb_ref, gmat_ref,
                   qt_ref, k_ref, vt_ref, hc_ref,
                   hbuf, hsh, cbuf, fcarry, upper_ref, *, tiles_per_seq, tm):
    i = pl.program_id(0)

    @pl.when(i % tiles_per_seq == 0)
    def _():
        hbuf[0:CONV_HALO, :] = jnp.zeros((CONV_HALO, CONV_CH), F32)
        fcarry[...] = jnp.zeros_like(fcarry)
        r_i = lax.broadcasted_iota(jnp.int32, (tm, tm), 0)
        c_i = lax.broadcasted_iota(jnp.int32, (tm, tm), 1)
        upper_ref[...] = jnp.where(r_i <= c_i, 1.0, 0.0).astype(BF16)

    xb = x_ref[...].astype(BF16)
    p1 = _dot(xb, wa_ref[...])
    cos_t = c_ref[...]
    sin_t = s_ref[...]
    cos_tt = ct_ref[...]
    sin_tt = st_ref[...]

    cqn = _rms(p1[:, _A_CQ:_A_CQ + MLA_Q_RANK], gq_ref[...]).astype(BF16)
    q_t = _nt_dot(wuq_ref[...], cqn)
    q_rot_t = _nt_dot(wuqr_ref[...], cqn)
    mla_scale = (MLA_NOPE + MLA_ROPE) ** -0.5 * LOG2E
    for h in range(MLA_HEADS):
        rows = slice(h * HEAD_PAD, (h + 1) * HEAD_PAD)
        qh = (q_t[rows, :] * cos_tt + q_rot_t[rows, :] * sin_tt) * mla_scale
        for c in range(tm // TQ):
            qt_ref[c, rows, :] = qh[:, c * TQ:(c + 1) * TQ].astype(BF16)

    ckvn = _rms(p1[:, _A_CKV:_A_CKV + MLA_KV_RANK], gkv_ref[...]).astype(BF16)
    k_nope = _dot(ckvn, wuk_ref[...])
    k_rope = p1[:, _A_KR:_A_KR + HEAD_PAD] * cos_t + p1[:, _A_KRR:_A_KRR + HEAD_PAD] * sin_t
    for h in range(MLA_HEADS):
        cols = slice(h * HEAD_PAD, (h + 1) * HEAD_PAD)
        k_ref[:, cols] = (k_nope[:, cols] + k_rope).astype(BF16)
    v_t = _nt_dot(wuv_ref[...], ckvn)
    fv_t = _nt_dot(wfv_ref[...], xb)
    ones_blk = jnp.where(lax.broadcasted_iota(jnp.int32, (V_ROWS - V_DIM, tm), 0) == 0, 1.0, 0.0)
    for h in range(N_HEADS):
        src = v_t if h < MLA_HEADS else fv_t
        r0 = (h if h < MLA_HEADS else h - MLA_HEADS) * V_DIM
        vh = jnp.concatenate([src[r0:r0 + V_DIM, :], ones_blk], axis=0).astype(BF16)
        for c in range(tm // TK):
            vt_ref[c, h * V_ROWS:(h + 1) * V_ROWS, :] = vh[:, c * TK:(c + 1) * TK]

    z = _nt_dot(wf_ref[...], xb) + fb_ref[...]
    logf = (jnp.minimum(z, 0.0) - jnp.log1p(jnp.exp(-jnp.abs(z)))) * LOG2E
    limbs = jnp.concatenate(_split3(logf), axis=0).astype(BF16)
    sums = _dot(limbs, upper_ref[...])
    cum = (sums[0:_F_ROWS] + sums[_F_ROWS:2 * _F_ROWS]) + sums[2 * _F_ROWS:3 * _F_ROWS]
    f_cum = cum + fcarry[:, 0:1]
    fcarry[...] = jnp.broadcast_to(f_cum[:, tm - 1:tm], fcarry.shape)
    f_hi, f_mid, f_lo = _split3(f_cum)

    fq_t = _nt_dot(wfq_ref[...], xb)
    row8 = lax.broadcasted_iota(jnp.int32, (_AUG_ROWS, tm), 0)
    fox_scale = FOX_DIM ** -0.5 * LOG2E
    for h in range(FOX_HEADS):
        bh = lambda a: jnp.broadcast_to(a[h:h + 1, :], (_AUG_ROWS, tm))
        aug_q = jnp.where(row8 == 0, bh(f_hi), jnp.where(row8 == 1, bh(f_mid), jnp.where(
            row8 == 2, bh(f_lo), jnp.where(row8 < 6, 1.0, 0.0))))
        aug_k = jnp.where(row8 < 3, 1.0, jnp.where(row8 == 3, -bh(f_hi), jnp.where(
            row8 == 4, -bh(f_mid), jnp.where(row8 == 5, -bh(f_lo), 0.0))))
        pad = jnp.zeros((HEAD_PAD - FOX_DIM - _AUG_ROWS, tm), F32)
        qh = jnp.concatenate(
            [fq_t[h * HEAD_PAD:h * HEAD_PAD + FOX_DIM, :] * fox_scale, aug_q, pad], axis=0)
        rows = slice((MLA_HEADS + h) * HEAD_PAD, (MLA_HEADS + h + 1) * HEAD_PAD)
        for c in range(tm // TQ):
            qt_ref[c, rows, :] = qh[:, c * TQ:(c + 1) * TQ].astype(BF16)
        kaug_t = jnp.concatenate([jnp.zeros((FOX_DIM, tm), F32), aug_k, pad], axis=0)
        fk = p1[:, _A_FK + h * HEAD_PAD:_A_FK + (h + 1) * HEAD_PAD]
        k_ref[:, rows] = (fk + kaug_t.T).astype(BF16)

    a = p1[:, _A_CA:_A_CA + CONV_CH]
    g = p1[:, _A_CG:_A_CG + CONV_CH]
    hbuf[CONV_HALO:CONV_HALO + tm, :] = a * jax.nn.sigmoid(g)
    chunk = CONV_CHUNK
    first = CONV_HALO - (CONV_WIDTH - 1)
    for r in range(1, 8):
        hsh[r - 1] = hbuf[r:r + tm + CONV_HALO - 8, :]
    for c0 in range(0, tm, chunk):
        acc = jnp.zeros((chunk, CONV_CH), F32)
        for o in range(first, first + CONV_WIDTH):
            r = o % 8
            row = c0 + o - r
            seg = hbuf[row:row + chunk, :] if r == 0 else hsh[r - 1, row:row + chunk, :]
            acc = acc + cw_ref[o - first:o - first + 1, :] * seg
        cbuf[c0:c0 + chunk, :] = acc
    hbuf[0:CONV_HALO, :] = hbuf[tm:tm + CONV_HALO, :]
    hv = cbuf[...] + cb_ref[...]
    gm = gmat_ref[...]
    mu = _split2_dot(hv, gm)
    d = hv - mu
    var = _split2_dot(d * d, gm)
    hn = d * lax.rsqrt(var + NORM_EPS) * cng_ref[...] + cnb_ref[...]
    hc_ref[...] = (hn * jax.nn.sigmoid(hn)).astype(BF16)


def _prep_inproj_weights(w_in, w_uq, w_ukv, fox_forget_b):
    o = np.cumsum((0, MLA_Q_RANK, MLA_KV_RANK, MLA_ROPE, FOX_WIDTH, FOX_WIDTH, FOX_WIDTH, FOX_HEADS,
                   2 * CONV_CH))
    w_cq, w_ckv, w_kr, w_fq, w_fk, w_fv, w_f, w_cv = (w_in[:, o[i]:o[i + 1]] for i in range(8))
    d = w_in.shape[0]
    half = MLA_ROPE // 2

    def rot_cols(w):
        return jnp.concatenate([-w[..., half:], w[..., :half]], axis=-1)

    def rope_block(w):
        return jnp.pad(w, ((0, 0), (MLA_NOPE, HEAD_PAD - MLA_NOPE - MLA_ROPE)))

    w_fk_pad = jnp.pad(w_fk.reshape(d, FOX_HEADS, FOX_DIM), ((0, 0), (0, 0), (0, HEAD_PAD - FOX_DIM)))
    wa = jnp.concatenate([w_cq, w_ckv, rope_block(w_kr), rope_block(rot_cols(w_kr)),
                          w_fk_pad.reshape(d, FOX_HEADS * HEAD_PAD), w_cv], axis=1)
    w_fq_pad = jnp.pad(w_fq.reshape(d, FOX_HEADS, FOX_DIM), ((0, 0), (0, 0), (0, HEAD_PAD - FOX_DIM)))
    wfq_t = w_fq_pad.reshape(d, FOX_HEADS * HEAD_PAD).T
    wfv_t = w_fv.T
    wf_t = jnp.pad(w_f, ((0, 0), (0, _F_ROWS - FOX_HEADS))).T
    fb = jnp.pad(fox_forget_b, (0, _F_ROWS - FOX_HEADS)).reshape(_F_ROWS, 1)

    uq = w_uq.reshape(MLA_Q_RANK, MLA_HEADS, MLA_NOPE + MLA_ROPE)
    uq_nope, uq_rope = uq[..., :MLA_NOPE], uq[..., MLA_NOPE:]
    tail = ((0, 0), (0, 0), (0, HEAD_PAD - MLA_NOPE - MLA_ROPE))
    uq_pad = jnp.pad(jnp.concatenate([uq_nope, uq_rope], axis=-1), tail)
    uq_rot_pad = jnp.pad(jnp.concatenate([jnp.zeros_like(uq_nope), rot_cols(uq_rope)], axis=-1), tail)
    wuq_t = uq_pad.reshape(MLA_Q_RANK, MLA_HEADS * HEAD_PAD).T
    wuqr_t = uq_rot_pad.reshape(MLA_Q_RANK, MLA_HEADS * HEAD_PAD).T
    ukv = w_ukv.reshape(MLA_KV_RANK, MLA_HEADS, MLA_NOPE + MLA_V)
    wuk = jnp.pad(ukv[..., :MLA_NOPE], ((0, 0), (0, 0), (0, HEAD_PAD - MLA_NOPE))).reshape(
        MLA_KV_RANK, MLA_HEADS * HEAD_PAD)
    wuv_t = ukv[..., MLA_NOPE:].reshape(MLA_KV_RANK, MLA_WIDTH).T
    bf = lambda a: a.astype(BF16)
    return dict(wa=bf(wa), wfq=bf(wfq_t), wfv=bf(wfv_t), wf=bf(wf_t), fb=fb, wuq=bf(wuq_t),
                wuqr=bf(wuqr_t), wuk=bf(wuk), wuv=bf(wuv_t))


def _input_projection(x2d, tabs, pw, gq, gkv, conv_w, conv_b, conv_ng, conv_nb, seq):
    n, d = x2d.shape
    tm = min(TM_IN, seq)
    cos_t, sin_t, cos_tt, sin_tt = tabs
    gidx = np.arange(CONV_CH) // (CONV_CH // CONV_GROUPS)
    gmat = jnp.asarray((gidx[:, None] == gidx[None, :]) / (CONV_CH // CONV_GROUPS), BF16)
    cw = jnp.pad(conv_w, ((0, CONV_HALO - CONV_WIDTH), (0, 0)))
    row = lambda a: a.reshape(1, -1)
    tok = lambda w: pl.BlockSpec((tm, w), lambda i: (i, 0))
    tok_t = lambda r: pl.BlockSpec((r, tm), lambda i: (0, i))
    consts = [pw["wa"], pw["wfq"], pw["wfv"], pw["wf"], pw["fb"], row(gq), pw["wuq"], pw["wuqr"],
              row(gkv), pw["wuk"], pw["wuv"], cw, row(conv_b), row(conv_ng), row(conv_nb), gmat]
    kern = functools.partial(_inproj_kernel, tiles_per_seq=seq // tm, tm=tm)
    return pl.pallas_call(
        kern,
        grid=(n // tm,),
        in_specs=[tok(d), tok(HEAD_PAD), tok(HEAD_PAD), tok_t(HEAD_PAD), tok_t(HEAD_PAD)]
        + [_const_spec(c.shape) for c in consts],
        out_specs=[pl.BlockSpec((tm // TQ, N_HEADS * HEAD_PAD, TQ), lambda i: (i, 0, 0)),
                   tok(N_HEADS * HEAD_PAD),
                   pl.BlockSpec((tm // TK, N_HEADS * V_ROWS, TK), lambda i: (i, 0, 0)),
                   tok(CONV_CH)],
        out_shape=[jax.ShapeDtypeStruct((n // TQ, N_HEADS * HEAD_PAD, TQ), BF16),
                   jax.ShapeDtypeStruct((n, N_HEADS * HEAD_PAD), BF16),
                   jax.ShapeDtypeStruct((n // TK, N_HEADS * V_ROWS, TK), BF16),
                   jax.ShapeDtypeStruct((n, CONV_CH), BF16)],
        scratch_shapes=[pltpu.VMEM((CONV_HALO + tm, CONV_CH), F32),
                        pltpu.VMEM((7, tm + CONV_HALO - 8, CONV_CH), F32),
                        pltpu.VMEM((tm, CONV_CH), F32),
                        pltpu.VMEM((_F_ROWS, LANES), F32),
                        pltpu.VMEM((tm, tm), BF16)],
        compiler_params=pltpu.CompilerParams(dimension_semantics=("arbitrary",),
                                             vmem_limit_bytes=VMEM_LIMIT),
        name="input_projection",
    )(x2d, cos_t, sin_t, cos_tt, sin_tt, *consts)


def _attn_kernel(qt_ref, k_ref, vt_ref, o_ref, *scratch, n_tiles):
    assert TQ == 2 * TK
    s_ref = (scratch[0:HPS], scratch[HPS:2 * HPS])
    sd_ref = scratch[2 * HPS:3 * HPS]
    p_ref = (scratch[3 * HPS:4 * HPS], scratch[4 * HPS:5 * HPS])
    pd_ref = scratch[5 * HPS:6 * HPS]
    acc_bufs = (scratch[6 * HPS:7 * HPS], scratch[7 * HPS:8 * HPS])
    diff_ref = scratch[8 * HPS]
    diff_ref[...] = (lax.broadcasted_iota(jnp.int32, (TK, TQ), 1)
                     - lax.broadcasted_iota(jnp.int32, (TK, TQ), 0))
    for h in range(HPS):
        p_ref[1][h][...] = jnp.zeros_like(p_ref[1][h])
        for par in range(2):
            acc_bufs[par][h][...] = jnp.ones_like(acc_bufs[par][h])

    def tile_scores(tile, j, slot):
        row0 = pl.multiple_of(j * TK, TK)
        block_max = []
        for h in range(HPS):
            s = _dot(k_ref[pl.ds(row0, TK), h * HEAD_PAD:(h + 1) * HEAD_PAD],
                     qt_ref[tile, h * HEAD_PAD:(h + 1) * HEAD_PAD, :])
            s_ref[slot][h][...] = s
            block_max.append(jnp.max(s, axis=0, keepdims=True))
        return block_max

    def last_diag_scores(tile):
        row0 = pl.multiple_of((2 * tile + 1) * TK, TK)
        for h in range(HPS):
            sd_ref[h][...] = _dot(k_ref[pl.ds(row0, TK), h * HEAD_PAD:(h + 1) * HEAD_PAD],
                                  qt_ref[tile, h * HEAD_PAD:(h + 1) * HEAD_PAD, TK:])

    def finalize(tile, par):
        out_t = jnp.concatenate([acc_bufs[par][h][0:V_DIM, :] / acc_bufs[par][h][V_DIM:V_DIM + 1, :]
                                 for h in range(HPS)], axis=0)
        o_ref[pl.ds(pl.multiple_of(tile * TQ, TQ), TQ), :] = out_t.T.astype(o_ref.dtype)

    def q_tile(i, par, bm0):
        acc_ref = acc_bufs[par]
        scores = functools.partial(tile_scores, i)

        def softmax(slot, m, block_max):
            m_new = [jnp.maximum(m[h], block_max[h]) for h in range(HPS)]
            for h in range(HPS):
                p_ref[slot][h][...] = jnp.exp2(s_ref[slot][h][...] - m_new[h]).astype(BF16)
            return m_new, [jnp.exp2(m[h] - m_new[h]) for h in range(HPS)]

        def values(j, slot, alpha, gate=None):
            for h in range(HPS):
                pv = _dot(vt_ref[j, h * V_ROWS:(h + 1) * V_ROWS, :], p_ref[slot][h][...])
                acc_ref[h][...] = alpha[h] * acc_ref[h][...] + (pv if gate is None else gate * pv)

        m0 = [jnp.full((1, TQ), M_INIT, F32)] * HPS
        a0 = [jnp.zeros((1, TQ), F32)] * HPS

        def pair(u, state):
            m, alpha, bm_t = list(state[0:HPS]), list(state[HPS:2 * HPS]), list(state[2 * HPS:3 * HPS])
            t = 2 * u
            m, alpha_t = softmax(0, m, bm_t)
            values(jnp.maximum(t - 1, 0), 1, alpha, jnp.where(t > 0, 1.0, 0.0))
            bm_t1 = scores(t + 1, 1)
            m, alpha_t1 = softmax(1, m, bm_t1)
            values(t, 0, alpha_t)
            bm_t2 = scores(t + 2, 0)
            return (*m, *alpha_t1, *bm_t2)

        state = lax.fori_loop(0, i // 2, lambda v, st: pair(2 * v + 1, pair(2 * v, st)), (*m0, *a0, *bm0))
        state = lax.fori_loop(i - i % 2, i, pair, state)
        m, alpha = list(state[0:HPS]), list(state[HPS:2 * HPS])
        d0 = 2 * i
        values(jnp.maximum(d0 - 1, 0), 1, alpha, jnp.where(i > 0, 1.0, 0.0))
        finalize(jnp.where(i > 0, i - 1, n_tiles - 1), 1 - par)
        nxt = jnp.minimum(i + 1, n_tiles - 1)

        m_d0, alpha_d0 = [], []
        for h in range(HPS):
            s = s_ref[0][h][...]
            s = jnp.concatenate([jnp.where(diff_ref[:, 0:TK] >= 0, s[:, 0:TK], -jnp.inf), s[:, TK:]], axis=1)
            m_d0.append(jnp.maximum(m[h], jnp.max(s, axis=0, keepdims=True)))
            p_ref[0][h][...] = jnp.exp2(s - m_d0[h]).astype(BF16)
            alpha_d0.append(jnp.exp2(m[h] - m_d0[h]))
        bm_next = tile_scores(nxt, 0, 0)

        alpha_d1 = []
        for h in range(HPS):
            s = jnp.where(diff_ref[:, 0:TK] >= 0, sd_ref[h][...], -jnp.inf)
            m_old = m_d0[h][:, TK:]
            m_new = jnp.maximum(m_old, jnp.max(s, axis=0, keepdims=True))
            pd_ref[h][...] = jnp.exp2(s - m_new).astype(BF16)
            alpha_d1.append(jnp.exp2(m_old - m_new))
        last_diag_scores(nxt)

        values(d0, 0, alpha_d0)
        for h in range(HPS):
            acc_ref[h][:, TK:] = alpha_d1[h] * acc_ref[h][:, TK:] + _dot(
                vt_ref[d0 + 1, h * V_ROWS:(h + 1) * V_ROWS, :], pd_ref[h][...])
        return tuple(bm_next)

    last_diag_scores(0)
    bm = lax.fori_loop(0, n_tiles // 2, lambda a, st: q_tile(2 * a + 1, 1, q_tile(2 * a, 0, st)),
                       tuple(tile_scores(0, 0, 0)))
    if n_tiles % 2:
        q_tile(n_tiles - 1, 0, bm)
    finalize(n_tiles - 1, (n_tiles - 1) % 2)


def _attention(q_t, k, v_t, batch, seq):
    n = k.shape[0]
    groups = N_HEADS // HPS
    return pl.pallas_call(
        functools.partial(_attn_kernel, n_tiles=seq // TQ),
        grid=(batch, groups),
        in_specs=[pl.BlockSpec((seq // TQ, HPS * HEAD_PAD, TQ), lambda b, p: (b, p, 0)),
                  pl.BlockSpec((seq, HPS * HEAD_PAD), lambda b, p: (b, p)),
                  pl.BlockSpec((seq // TK, HPS * V_ROWS, TK), lambda b, p: (b, p, 0))],
        out_specs=pl.BlockSpec((seq, HPS * V_DIM), lambda b, p: (b, p)),
        out_shape=jax.ShapeDtypeStruct((n, N_HEADS * V_DIM), BF16),
        scratch_shapes=[pltpu.VMEM((TK, TQ), F32)] * (2 * HPS) + [pltpu.VMEM((TK, TK), F32)] * HPS
        + [pltpu.VMEM((TK, TQ), BF16)] * (2 * HPS) + [pltpu.VMEM((TK, TK), BF16)] * HPS
        + [pltpu.VMEM((V_ROWS, TQ), F32)] * (2 * HPS) + [pltpu.VMEM((TK, TQ), jnp.int32)],
        compiler_params=pltpu.CompilerParams(dimension_semantics=("arbitrary", "arbitrary"),
                                             vmem_limit_bytes=VMEM_LIMIT),
        name="attention",
    )(q_t, k, v_t)


def _layer_norm(x, g, b):
    mu = jnp.mean(x, axis=-1, keepdims=True)
    d = x - mu
    var = jnp.mean(jnp.square(d), axis=-1, keepdims=True)
    return d * lax.rsqrt(var + NORM_EPS) * g + b


def _mix_and_norm(o_ref, hc_ref, x_ref, gm_ref, gf_ref, wo_ref, g1_ref, b1_ref):
    o = o_ref[...].astype(F32)
    mla = _rms(o[:, :MLA_WIDTH], gm_ref[...])
    fox = _rms(o[:, MLA_WIDTH:], gf_ref[...])
    mixed = jnp.concatenate([mla.astype(BF16), fox.astype(BF16), hc_ref[...]], axis=-1)
    y = _dot(mixed, wo_ref[...])
    return _layer_norm(ALPHA * x_ref[...] + y, g1_ref[...], b1_ref[...])


def _router_layer_kernel(o_ref, hc_ref, x_ref, gm_ref, gf_ref, wo_ref, g1_ref, b1_ref, rw_ref,
                         x1_ref, route_ref, counts_ref, cnt_ref, upper_ref):
    x1 = _mix_and_norm(o_ref, hc_ref, x_ref, gm_ref, gf_ref, wo_ref, g1_ref, b1_ref)
    x1_ref[...] = x1
    rw = rw_ref[...]
    x_hi = x1.astype(BF16)
    x_lo = (x1 - x_hi.astype(F32)).astype(BF16)
    w_hi = rw.astype(BF16)
    w_lo = (rw - w_hi.astype(F32)).astype(BF16)
    both = _dot(x_hi, jnp.concatenate([w_hi, w_lo], axis=1))
    logits = both[:, :LANES] + (_dot(x_lo, w_hi) + both[:, LANES:])
    tm = logits.shape[0]
    lg = logits.T[0:N_EXPERTS, :]
    row = lax.broadcasted_iota(jnp.int32, lg.shape, 0)
    v1 = jnp.max(lg, axis=0, keepdims=True)
    i1 = jnp.min(jnp.where(lg == v1, row, N_EXPERTS), axis=0, keepdims=True)
    rest_l = jnp.where(row == i1, -jnp.inf, lg)
    v2 = jnp.max(rest_l, axis=0, keepdims=True)
    i2 = jnp.min(jnp.where(rest_l == v2, row, N_EXPERTS), axis=0, keepdims=True)
    e2 = jnp.exp(v2 - v1)
    den = 1.0 + e2

    @pl.when(pl.program_id(0) == 0)
    def _():
        cnt_ref[...] = jnp.zeros_like(cnt_ref)
        r_i = lax.broadcasted_iota(jnp.int32, (tm, tm), 0)
        c_i = lax.broadcasted_iota(jnp.int32, (tm, tm), 1)
        upper_ref[...] = jnp.where(r_i < c_i, 1.0, 0.0).astype(BF16)

    sel = jnp.where(row == i1, 1.0, jnp.where(row == i2, 1.0, 0.0))
    sel16 = jnp.concatenate([sel, jnp.zeros_like(sel)], axis=0).astype(BF16)
    before = cnt_ref[:, 0:1]
    rank = _dot(sel16, upper_ref[...])[0:N_EXPERTS, :] + before
    total = before + jnp.sum(sel, axis=1, keepdims=True)
    cnt_ref[...] = jnp.broadcast_to(total, cnt_ref.shape)
    counts_ref[...] = jnp.broadcast_to(total, counts_ref.shape)
    r1 = jnp.sum(jnp.where(row == i1, rank, 0.0), axis=0, keepdims=True)
    r2 = jnp.sum(jnp.where(row == i2, rank, 0.0), axis=0, keepdims=True)
    rows = (i1.astype(F32), i2.astype(F32), r1, r2, 1.0 / den, e2 / den)
    route = jnp.zeros(lg.shape, F32)
    for c, v in enumerate(rows):
        route = jnp.where(row == c, v, route)
    route_ref[...] = route


def _router_layer(o, hc, x2d, gm, gf, w_out, g1, b1, router_w):
    n, d = x2d.shape
    tm = min(TM_OUT, n)
    row = lambda a: a.reshape(1, -1)
    tok = lambda w: pl.BlockSpec((tm, w), lambda i: (i, 0))
    consts = [row(gm), row(gf), w_out.astype(BF16), row(g1), row(b1),
              jnp.pad(router_w, ((0, 0), (0, LANES - N_EXPERTS)))]
    return pl.pallas_call(
        _router_layer_kernel,
        grid=(n // tm,),
        in_specs=[tok(o.shape[1]), tok(CONV_CH), tok(d)] + [_const_spec(c.shape) for c in consts],
        out_specs=[tok(d), pl.BlockSpec((N_EXPERTS, tm), lambda i: (0, i)),
                   pl.BlockSpec((N_EXPERTS, LANES), lambda i: (0, 0))],
        out_shape=[jax.ShapeDtypeStruct((n, d), F32), jax.ShapeDtypeStruct((N_EXPERTS, n), F32),
                   jax.ShapeDtypeStruct((N_EXPERTS, LANES), F32)],
        scratch_shapes=[pltpu.VMEM((N_EXPERTS, LANES), F32), pltpu.VMEM((tm, tm), BF16)],
        compiler_params=pltpu.CompilerParams(dimension_semantics=("arbitrary",),
                                             vmem_limit_bytes=VMEM_LIMIT),
        name="router_layer",
    )(o, hc, x2d, *consts)


def _swiglu_tile(xb, w1, w3, w2):
    h1 = _dot(xb, w1)
    h3 = _dot(xb, w3)
    hid = (h1 * jax.nn.sigmoid(h1) * h3).astype(BF16)
    return _dot(hid, w2)


def _dense_layer_kernel(o_ref, hc_ref, x_ref, gm_ref, gf_ref, wo_ref, g1_ref, b1_ref,
                        w1_ref, w3_ref, w2_ref, g2_ref, b2_ref, out_ref, *, f_chunk):
    x1 = _mix_and_norm(o_ref, hc_ref, x_ref, gm_ref, gf_ref, wo_ref, g1_ref, b1_ref)
    xb = x1.astype(BF16)
    ff = None
    for c0 in range(0, w1_ref.shape[1], f_chunk):
        part = _swiglu_tile(xb, w1_ref[:, c0:c0 + f_chunk], w3_ref[:, c0:c0 + f_chunk],
                            w2_ref[c0:c0 + f_chunk, :])
        ff = part if ff is None else ff + part
    out_ref[...] = _layer_norm(ALPHA * x1 + ff, g2_ref[...], b2_ref[...])


def _dense_layer(o, hc, x2d, gm, gf, w_out, g1, b1, w1, w3, w2, g2, b2):
    n, d = x2d.shape
    tm = min(TM_FFN, n)
    f = w1.shape[1]
    f_chunk = f // 2 if (f // 2) % LANES == 0 else f
    row = lambda a: a.reshape(1, -1)
    tok = lambda w: pl.BlockSpec((tm, w), lambda i: (i, 0))
    consts = [row(gm), row(gf), w_out.astype(BF16), row(g1), row(b1),
              w1.astype(BF16), w3.astype(BF16), w2.astype(BF16), row(g2), row(b2)]
    return pl.pallas_call(
        functools.partial(_dense_layer_kernel, f_chunk=f_chunk),
        grid=(n // tm,),
        in_specs=[tok(o.shape[1]), tok(CONV_CH), tok(d)] + [_const_spec(c.shape) for c in consts],
        out_specs=tok(d),
        out_shape=jax.ShapeDtypeStruct((n, d), F32),
        compiler_params=pltpu.CompilerParams(dimension_semantics=("arbitrary",),
                                             vmem_limit_bytes=VMEM_LIMIT),
        name="dense_layer",
    )(o, hc, x2d, *consts)


def _to_row_tiles(ref, x):
    for c in range(ROW_TILE):
        ref[pl.ds(c, x.shape[0], stride=ROW_TILE), :] = x[:, c * LANES:(c + 1) * LANES]


def _from_row_tiles(ref, t):
    return jnp.concatenate([ref[pl.ds(c, t, stride=ROW_TILE), :] for c in range(ROW_TILE)], axis=-1)


def _row_tile(ref, r):
    return ref.at[pl.ds(pl.multiple_of(r * ROW_TILE, ROW_TILE), ROW_TILE)]


def _dispatch_kernel(d1_ref, d2_ref, se_ref, x_ref, xs_ref, xr, zbuf, sem, *, tm, tr):
    i = pl.program_id(0)

    @pl.when(i == 0)
    def _():
        zbuf[...] = jnp.zeros_like(zbuf)
        for e in range(N_EXPERTS):
            end = se_ref[e]
            start_e = se_ref[e - 1] if e else 0

            for first, live in ((end - tr, end > start_e),
                                (se_ref[N_EXPERTS - 1] + e * tr,
                                 (se_ref[N_EXPERTS - 1] + e * tr) * ROW_TILE < xs_ref.shape[0])):
                @pl.when(live)
                def _():
                    rows = pl.ds(pl.multiple_of(first * ROW_TILE, ROW_TILE), tr * ROW_TILE)
                    fill = pltpu.make_async_copy(zbuf, xs_ref.at[rows], sem.at[2])
                    fill.start()
                    fill.wait()

    slot = i % 2
    _to_row_tiles(xr.at[slot], x_ref[...])
    base = i * tm

    def start(r, c):
        src = _row_tile(xr.at[slot], r)
        pltpu.make_async_copy(src, _row_tile(xs_ref, d1_ref[base + r]), sem.at[slot]).start()
        pltpu.make_async_copy(src, _row_tile(xs_ref, d2_ref[base + r]), sem.at[slot]).start(priority=1)
        return c

    def wait_step(s):
        def wait(r, c):
            for _ in range(2):
                pltpu.make_async_copy(_row_tile(xr.at[s], 0), _row_tile(xs_ref, 0), sem.at[s]).wait()
            return c

        lax.fori_loop(0, tm, wait, 0, unroll=8)

    lax.fori_loop(0, tm, start, 0, unroll=8)

    @pl.when(i > 0)
    def _():
        wait_step(1 - slot)

    @pl.when(i == pl.num_programs(0) - 1)
    def _():
        wait_step(slot)


def _expert_kernel(te_ref, blk_ref, nu_ref, xs_ref, w1_ref, w3_ref, w2_ref, ys_ref, *, tr):
    del te_ref, blk_ref
    used = pl.program_id(0) < nu_ref[0]

    @pl.when(used)
    def _():
        xb = _from_row_tiles(xs_ref, tr).astype(BF16)
        _to_row_tiles(ys_ref, _swiglu_tile(xb, w1_ref[0], w3_ref[0], w2_ref[0]))

    @pl.when(jnp.logical_not(used))
    def _():
        ys_ref[...] = jnp.zeros_like(ys_ref)


def _combine_kernel(d1_ref, d2_ref, x_ref, route_ref, g_ref, b_ref, ys_ref, o_ref, ybuf, sem, *, tm):
    i = pl.program_id(0)
    n_steps = pl.num_programs(0)

    def issue(tile, slot):
        base = tile * tm

        def start(r, c):
            pltpu.make_async_copy(_row_tile(ys_ref, d1_ref[base + r]), _row_tile(ybuf.at[slot, 0], r),
                                  sem.at[slot]).start()
            pltpu.make_async_copy(_row_tile(ys_ref, d2_ref[base + r]), _row_tile(ybuf.at[slot, 1], r),
                                  sem.at[slot]).start(priority=1)
            return c

        lax.fori_loop(0, tm, start, 0, unroll=8)

    @pl.when(i == 0)
    def _():
        issue(0, 0)

    @pl.when(i + 1 < n_steps)
    def _():
        issue(i + 1, (i + 1) % 2)

    slot = i % 2

    def wait(r, c):
        for k in range(2):
            pltpu.make_async_copy(_row_tile(ys_ref, 0), _row_tile(ybuf.at[slot, k], 0), sem.at[slot]).wait()
        return c

    lax.fori_loop(0, tm, wait, 0, unroll=8)
    route = route_ref[...]
    gates = jnp.concatenate([route, jnp.zeros((LANES - route.shape[0], tm), F32)], axis=0).T
    ff = (gates[:, 4:5] * _from_row_tiles(ybuf.at[slot, 0], tm)
          + gates[:, 5:6] * _from_row_tiles(ybuf.at[slot, 1], tm))
    o_ref[...] = _layer_norm(ALPHA * x_ref[...] + ff, g_ref[...], b_ref[...])


def _moe_ffn(x2d, route, counts, w1, w3, w2, g, b):
    n, d = x2d.shape
    n_exp, _, f = w1.shape
    tr = min(TR_MOE, n)
    tm = min(TM_MOE, n)
    n_pad = 2 * n + n_exp * tr
    n_tiles = n_pad // tr
    i32 = jnp.int32

    cnt = counts[:, 0].astype(i32)
    seg = (cnt + tr - 1) // tr * tr
    seg_end = jnp.cumsum(seg)
    seg_start = seg_end - seg
    e1, e2 = route[0].astype(i32), route[1].astype(i32)
    dest1 = seg_start[e1] + route[2].astype(i32)
    dest2 = seg_start[e2] + route[3].astype(i32)
    n_used = jnp.maximum(seg_end[-1] // tr, 1)
    tile = jnp.minimum(jnp.arange(n_tiles, dtype=i32), n_used - 1)
    tile_expert = jnp.minimum(jnp.sum(tile[:, None] * tr >= seg_end[None, :], axis=1), n_exp - 1).astype(i32)

    cparams = pltpu.CompilerParams(dimension_semantics=("arbitrary",), vmem_limit_bytes=VMEM_LIMIT)
    assert d == ROW_TILE * LANES
    xs = pl.pallas_call(
        functools.partial(_dispatch_kernel, tm=tm, tr=tr),
        grid_spec=pltpu.PrefetchScalarGridSpec(
            num_scalar_prefetch=3, grid=(n // tm,),
            in_specs=[pl.BlockSpec((tm, d), lambda i, *_: (i, 0))],
            out_specs=pl.BlockSpec(memory_space=pl.ANY),
            scratch_shapes=[pltpu.VMEM((2, tm * ROW_TILE, LANES), F32), pltpu.VMEM((tr * ROW_TILE, LANES), F32),
                            pltpu.SemaphoreType.DMA((3,))]),
        out_shape=jax.ShapeDtypeStruct((n_pad * ROW_TILE, LANES), F32),
        compiler_params=cparams,
        name="moe_dispatch",
    )(dest1, dest2, seg_end.astype(i32), x2d)

    ys = pl.pallas_call(
        functools.partial(_expert_kernel, tr=tr),
        grid_spec=pltpu.PrefetchScalarGridSpec(
            num_scalar_prefetch=3, grid=(n_tiles,),
            in_specs=[pl.BlockSpec((tr * ROW_TILE, LANES), lambda i, te, blk, nu: (blk[i], 0)),
                      pl.BlockSpec((1, d, f), lambda i, te, blk, nu: (te[i], 0, 0)),
                      pl.BlockSpec((1, d, f), lambda i, te, blk, nu: (te[i], 0, 0)),
                      pl.BlockSpec((1, f, d), lambda i, te, blk, nu: (te[i], 0, 0))],
            out_specs=pl.BlockSpec((tr * ROW_TILE, LANES), lambda i, te, blk, nu: (i, 0))),
        out_shape=jax.ShapeDtypeStruct((n_pad * ROW_TILE, LANES), F32),
        compiler_params=cparams,
        name="moe_experts",
    )(tile_expert, tile, n_used.reshape(1), xs, w1.astype(BF16), w3.astype(BF16), w2.astype(BF16))

    row = lambda a: a.reshape(1, -1)
    return pl.pallas_call(
        functools.partial(_combine_kernel, tm=tm),
        grid_spec=pltpu.PrefetchScalarGridSpec(
            num_scalar_prefetch=2, grid=(n // tm,),
            in_specs=[pl.BlockSpec((tm, d), lambda i, *_: (i, 0)),
                      pl.BlockSpec((N_EXPERTS, tm), lambda i, *_: (0, i)),
                      pl.BlockSpec((1, d), lambda i, *_: (0, 0)),
                      pl.BlockSpec((1, d), lambda i, *_: (0, 0)),
                      pl.BlockSpec(memory_space=pl.ANY)],
            out_specs=pl.BlockSpec((tm, d), lambda i, *_: (i, 0)),
            scratch_shapes=[pltpu.VMEM((2, 2, tm * ROW_TILE, LANES), F32), pltpu.SemaphoreType.DMA((2,))]),
        out_shape=jax.ShapeDtypeStruct((n, d), F32),
        compiler_params=cparams,
        name="moe_combine",
    )(dest1, dest2, x2d, route, row(g), row(b), ys)


def kernel(x, positions, w_in, mla_q_norm_g, w_uq, mla_kv_norm_g, w_ukv, fox_forget_b, conv_w, conv_b,
           conv_norm_g, conv_norm_b, mla_out_norm_g, fox_out_norm_g, w_out, ln1_g, ln1_b, dense_w1,
           dense_w3, dense_w2, router_w, expert_w1, expert_w3, expert_w2, ln2_g, ln2_b):
    batch, seq, d = x.shape
    assert d == D_MODEL and seq % TQ == 0 and seq % min(TM_IN, seq) == 0
    depth = w_in.shape[0]
    tabs = _rope_tables(positions)
    h = x.reshape(batch * seq, d)
    pw_all = jax.vmap(_prep_inproj_weights)(w_in, w_uq, w_ukv, fox_forget_b)
    w_out, dense_w1, dense_w3, dense_w2, expert_w1, expert_w3, expert_w2 = (
        w.astype(BF16) for w in (w_out, dense_w1, dense_w3, dense_w2, expert_w1, expert_w3, expert_w2))
    for layer in range(depth):
        pw = {name: w[layer] for name, w in pw_all.items()}
        q_t, k, v_t, hc = _input_projection(
            h, tabs, pw, mla_q_norm_g[layer], mla_kv_norm_g[layer], conv_w[layer], conv_b[layer],
            conv_norm_g[layer], conv_norm_b[layer], seq)
        o = _attention(q_t, k, v_t, batch, seq)
        j = layer // 2
        if layer % 2 == 0:
            h = _dense_layer(o, hc, h, mla_out_norm_g[layer], fox_out_norm_g[layer], w_out[layer],
                             ln1_g[layer], ln1_b[layer], dense_w1[j], dense_w3[j], dense_w2[j],
                             ln2_g[layer], ln2_b[layer])
        else:
            h, route, counts = _router_layer(o, hc, h, mla_out_norm_g[layer], fox_out_norm_g[layer], w_out[layer],
                                             ln1_g[layer], ln1_b[layer], router_w[j])
            h = _moe_ffn(h, route, counts, expert_w1[j], expert_w3[j], expert_w2[j], ln2_g[layer],
                         ln2_b[layer])
    return h.reshape(batch, seq, d)
```

```python
import functools
import math

import numpy as np
import jax
import jax.numpy as jnp
from jax import lax
from jax.experimental import pallas as pl
from jax.experimental.pallas import tpu as pltpu

F32 = jnp.float32
BF16 = jnp.bfloat16

D_MODEL = 1024
DEPTH = 4
MLA_HEADS = 8
MLA_NOPE = 64
MLA_ROPE = 32
MLA_V = 64
MLA_Q_RANK = 256
MLA_KV_RANK = 128
ROPE_THETA = 10000.0
FOX_HEADS = 4
FOX_DIM = 64
CONV_CH = 256
CONV_GROUPS = 4
CONV_WIDTH = 31
MLA_WIDTH = MLA_HEADS * MLA_V
FOX_WIDTH = FOX_HEADS * FOX_DIM
N_EXPERTS = 8
ALPHA = (2.0 * DEPTH) ** 0.25
NORM_EPS = 1e-5
LOG2E = math.log2(math.e)

LANES = 128
HEAD_PAD = LANES
N_HEADS = MLA_HEADS + FOX_HEADS
V_DIM = 64
V_ROWS = 80
CONV_HALO = 32
VMEM_LIMIT = 56 * 1024 * 1024

TQ = 512
TK = 256
HPS = 2
TN_ROPE = 4096
TM_IN = 1024
TM_OUT = 512
TM_FFN = 512
TR_MOE = 512
TM_MOE = 512
ROW_TILE = 8
CONV_CHUNK = 64
M_INIT = -1e30


def _nt_dot(a, b):
    return lax.dot_general(a, b, (((1,), (1,)), ((), ())), preferred_element_type=F32)


def _dot(a, b):
    return jnp.dot(a, b, preferred_element_type=F32)


def _split2_dot(a, m_bf16):
    hi = a.astype(BF16)
    lo = (a - hi.astype(F32)).astype(BF16)
    return _dot(hi, m_bf16) + _dot(lo, m_bf16)


def _split3(a):
    hi = a.astype(BF16).astype(F32)
    r1 = a - hi
    mid = r1.astype(BF16).astype(F32)
    lo = (r1 - mid).astype(BF16).astype(F32)
    return hi, mid, lo


def _const_spec(shape):
    nd = len(shape)
    return pl.BlockSpec(shape, lambda *_: (0,) * nd, pipeline_mode=pl.Buffered(1))


def _rope_kernel(pos_ref, invf_ref, c_ref, s_ref, ct_ref, st_ref):
    pos = pos_ref[...].astype(F32)
    ang = invf_ref[...] * pos
    cos = jnp.cos(ang)
    sin = jnp.sin(ang)
    tn = pos.shape[1]
    ct = jnp.concatenate([jnp.ones((MLA_NOPE, tn), F32), cos, cos, jnp.zeros((HEAD_PAD - MLA_NOPE - MLA_ROPE, tn), F32)], axis=0)
    st = jnp.concatenate([jnp.zeros((MLA_NOPE, tn), F32), sin, sin, jnp.zeros((HEAD_PAD - MLA_NOPE - MLA_ROPE, tn), F32)], axis=0)
    ct_ref[...] = ct
    st_ref[...] = st
    c_ref[...] = ct.T
    s_ref[...] = st.T


def _rope_tables(positions):
    n = positions.size
    tn = min(TN_ROPE, n)
    inv_freq = ROPE_THETA ** (-jnp.arange(0, MLA_ROPE, 2, dtype=F32) / MLA_ROPE)
    return pl.pallas_call(
        _rope_kernel,
        grid=(n // tn,),
        in_specs=[pl.BlockSpec((1, tn), lambda i: (0, i)),
                  pl.BlockSpec((MLA_ROPE // 2, 1), lambda i: (0, 0))],
        out_specs=[pl.BlockSpec((tn, HEAD_PAD), lambda i: (i, 0)),
                   pl.BlockSpec((tn, HEAD_PAD), lambda i: (i, 0)),
                   pl.BlockSpec((HEAD_PAD, tn), lambda i: (0, i)),
                   pl.BlockSpec((HEAD_PAD, tn), lambda i: (0, i))],
        out_shape=[jax.ShapeDtypeStruct((n, HEAD_PAD), F32),
                   jax.ShapeDtypeStruct((n, HEAD_PAD), F32),
                   jax.ShapeDtypeStruct((HEAD_PAD, n), F32),
                   jax.ShapeDtypeStruct((HEAD_PAD, n), F32)],
        name="rope_tables",
    )(positions.reshape(1, n), inv_freq.reshape(-1, 1))


_A_CQ = 0
_A_CKV = _A_CQ + MLA_Q_RANK
_A_KR = _A_CKV + MLA_KV_RANK
_A_KRR = _A_KR + HEAD_PAD
_A_FK = _A_KRR + HEAD_PAD
_A_CA = _A_FK + FOX_HEADS * HEAD_PAD
_A_CG = _A_CA + CONV_CH
_A_COLS = _A_CG + CONV_CH
_AUG_ROWS = 8
_F_ROWS = 16


def _rms(x, g):
    ms = jnp.mean(jnp.square(x), axis=-1, keepdims=True)
    return x * lax.rsqrt(ms + NORM_EPS) * g


def _inproj_kernel(x_ref, c_ref, s_ref, ct_ref, st_ref, wa_ref, wfq_ref, wfv_ref, wf_ref, fb_ref,
                   gq_ref, wuq_ref, wuqr_ref, gkv_ref, wuk_ref, wuv_ref,
                   cw_ref, cb_ref, cng_ref, cnb_ref, gmat_ref,
                   qt_ref, k_ref, vt_ref, hc_ref,
                   hbuf, hsh, cbuf, fcarry, upper_ref, *, tiles_per_seq, tm):
    i = pl.program_id(0)

    @pl.when(i % tiles_per_seq == 0)
    def _():
        hbuf[0:CONV_HALO, :] = jnp.zeros((CONV_HALO, CONV_CH), F32)
        fcarry[...] = jnp.zeros_like(fcarry)
        r_i = lax.broadcasted_iota(jnp.int32, (tm, tm), 0)
        c_i = lax.broadcasted_iota(jnp.int32, (tm, tm), 1)
        upper_ref[...] = jnp.where(r_i <= c_i, 1.0, 0.0).astype(BF16)

    xb = x_ref[...].astype(BF16)
    p1 = _dot(xb, wa_ref[...])
    cos_t = c_ref[...]
    sin_t = s_ref[...]
    cos_tt = ct_ref[...]
    sin_tt = st_ref[...]

    cqn = _rms(p1[:, _A_CQ:_A_CQ + MLA_Q_RANK], gq_ref[...]).astype(BF16)
    q_t = _nt_dot(wuq_ref[...], cqn)
    q_rot_t = _nt_dot(wuqr_ref[...], cqn)
    mla_scale = (MLA_NOPE + MLA_ROPE) ** -0.5 * LOG2E
    for h in range(MLA_HEADS):
        rows = slice(h * HEAD_PAD, (h + 1) * HEAD_PAD)
        qh = (q_t[rows, :] * cos_tt + q_rot_t[rows, :] * sin_tt) * mla_scale
        for c in range(tm // TQ):
            qt_ref[c, rows, :] = qh[:, c * TQ:(c + 1) * TQ].astype(BF16)

    ckvn = _rms(p1[:, _A_CKV:_A_CKV + MLA_KV_RANK], gkv_ref[...]).astype(BF16)
    k_nope = _dot(ckvn, wuk_ref[...])
    k_rope = p1[:, _A_KR:_A_KR + HEAD_PAD] * cos_t + p1[:, _A_KRR:_A_KRR + HEAD_PAD] * sin_t
    for h in range(MLA_HEADS):
        cols = slice(h * HEAD_PAD, (h + 1) * HEAD_PAD)
        k_ref[:, cols] = (k_nope[:, cols] + k_rope).astype(BF16)
    v_t = _nt_dot(wuv_ref[...], ckvn)
    fv_t = _nt_dot(wfv_ref[...], xb)
    ones_blk = jnp.where(lax.broadcasted_iota(jnp.int32, (V_ROWS - V_DIM, tm), 0) == 0, 1.0, 0.0)
    for h in range(N_HEADS):
        src = v_t if h < MLA_HEADS else fv_t
        r0 = (h if h < MLA_HEADS else h - MLA_HEADS) * V_DIM
        vh = jnp.concatenate([src[r0:r0 + V_DIM, :], ones_blk], axis=0).astype(BF16)
        for c in range(tm // TK):
            vt_ref[c, h * V_ROWS:(h + 1) * V_ROWS, :] = vh[:, c * TK:(c + 1) * TK]

    z = _nt_dot(wf_ref[...], xb) + fb_ref[...]
    logf = (jnp.minimum(z, 0.0) - jnp.log1p(jnp.exp(-jnp.abs(z)))) * LOG2E
    limbs = jnp.concatenate(_split3(logf), axis=0).astype(BF16)
    sums = _dot(limbs, upper_ref[...])
    cum = (sums[0:_F_ROWS] + sums[_F_ROWS:2 * _F_ROWS]) + sums[2 * _F_ROWS:3 * _F_ROWS]
    f_cum = cum + fcarry[:, 0:1]
    fcarry[...] = jnp.broadcast_to(f_cum[:, tm - 1:tm], fcarry.shape)
    f_hi, f_mid, f_lo = _split3(f_cum)

    fq_t = _nt_dot(wfq_ref[...], xb)
    row8 = lax.broadcasted_iota(jnp.int32, (_AUG_ROWS, tm), 0)
    fox_scale = FOX_DIM ** -0.5 * LOG2E
    for h in range(FOX_HEADS):
        bh = lambda a: jnp.broadcast_to(a[h:h + 1, :], (_AUG_ROWS, tm))
        aug_q = jnp.where(row8 == 0, bh(f_hi), jnp.where(row8 == 1, bh(f_mid), jnp.where(
            row8 == 2, bh(f_lo), jnp.where(row8 < 6, 1.0, 0.0))))
        aug_k = jnp.where(row8 < 3, 1.0, jnp.where(row8 == 3, -bh(f_hi), jnp.where(
            row8 == 4, -bh(f_mid), jnp.where(row8 == 5, -bh(f_lo), 0.0))))
        pad = jnp.zeros((HEAD_PAD - FOX_DIM - _AUG_ROWS, tm), F32)
        qh = jnp.concatenate(
            [fq_t[h * HEAD_PAD:h * HEAD_PAD + FOX_DIM, :] * fox_scale, aug_q, pad], axis=0)
        rows = slice((MLA_HEADS + h) * HEAD_PAD, (MLA_HEADS + h + 1) * HEAD_PAD)
        for c in range(tm // TQ):
            qt_ref[c, rows, :] = qh[:, c * TQ:(c + 1) * TQ].astype(BF16)
        kaug_t = jnp.concatenate([jnp.zeros((FOX_DIM, tm), F32), aug_k, pad], axis=0)
        fk = p1[:, _A_FK + h * HEAD_PAD:_A_FK + (h + 1) * HEAD_PAD]
        k_ref[:, rows] = (fk + kaug_t.T).astype(BF16)

    a = p1[:, _A_CA:_A_CA + CONV_CH]
    g = p1[:, _A_CG:_A_CG + CONV_CH]
    hbuf[CONV_HALO:CONV_HALO + tm, :] = a * jax.nn.sigmoid(g)
    chunk = CONV_CHUNK
    first = CONV_HALO - (CONV_WIDTH - 1)
    for r in range(1, 8):
        hsh[r - 1] = hbuf[r:r + tm + CONV_HALO - 8, :]
    for c0 in range(0, tm, chunk):
        acc = jnp.zeros((chunk, CONV_CH), F32)
        for o in range(first, first + CONV_WIDTH):
            r = o % 8
            row = c0 + o - r
            seg = hbuf[row:row + chunk, :] if r == 0 else hsh[r - 1, row:row + chunk, :]
            acc = acc + cw_ref[o - first:o - first + 1, :] * seg
        cbuf[c0:c0 + chunk, :] = acc
    hbuf[0:CONV_HALO, :] = hbuf[tm:tm + CONV_HALO, :]
    hv = cbuf[...] + cb_ref[...]
    gm = gmat_ref[...]
    mu = _split2_dot(hv, gm)
    d = hv - mu
    var = _split2_dot(d * d, gm)
    hn = d * lax.rsqrt(var + NORM_EPS) * cng_ref[...] + cnb_ref[...]
    hc_ref[...] = (hn * jax.nn.sigmoid(hn)).astype(BF16)


def _prep_inproj_weights(w_in, w_uq, w_ukv, fox_forget_b):
    o = np.cumsum((0, MLA_Q_RANK, MLA_KV_RANK, MLA_ROPE, FOX_WIDTH, FOX_WIDTH, FOX_WIDTH, FOX_HEADS,
                   2 * CONV_CH))
    w_cq, w_ckv, w_kr, w_fq, w_fk, w_fv, w_f, w_cv = (w_in[:, o[i]:o[i + 1]] for i in range(8))
    d = w_in.shape[0]
    half = MLA_ROPE // 2

    def rot_cols(w):
        return jnp.concatenate([-w[..., half:], w[..., :half]], axis=-1)

    def rope_block(w):
        return jnp.pad(w, ((0, 0), (MLA_NOPE, HEAD_PAD - MLA_NOPE - MLA_ROPE)))

    w_fk_pad = jnp.pad(w_fk.reshape(d, FOX_HEADS, FOX_DIM), ((0, 0), (0, 0), (0, HEAD_PAD - FOX_DIM)))
    wa = jnp.concatenate([w_cq, w_ckv, rope_block(w_kr), rope_block(rot_cols(w_kr)),
                          w_fk_pad.reshape(d, FOX_HEADS * HEAD_PAD), w_cv], axis=1)
    w_fq_pad = jnp.pad(w_fq.reshape(d, FOX_HEADS, FOX_DIM), ((0, 0), (0, 0), (0, HEAD_PAD - FOX_DIM)))
    wfq_t = w_fq_pad.reshape(d, FOX_HEADS * HEAD_PAD).T
    wfv_t = w_fv.T
    wf_t = jnp.pad(w_f, ((0, 0), (0, _F_ROWS - FOX_HEADS))).T
    fb = jnp.pad(fox_forget_b, (0, _F_ROWS - FOX_HEADS)).reshape(_F_ROWS, 1)

    uq = w_uq.reshape(MLA_Q_RANK, MLA_HEADS, MLA_NOPE + MLA_ROPE)
    uq_nope, uq_rope = uq[..., :MLA_NOPE], uq[..., MLA_NOPE:]
    tail = ((0, 0), (0, 0), (0, HEAD_PAD - MLA_NOPE - MLA_ROPE))
    uq_pad = jnp.pad(jnp.concatenate([uq_nope, uq_rope], axis=-1), tail)
    uq_rot_pad = jnp.pad(jnp.concatenate([jnp.zeros_like(uq_nope), rot_cols(uq_rope)], axis=-1), tail)
    wuq_t = uq_pad.reshape(MLA_Q_RANK, MLA_HEADS * HEAD_PAD).T
    wuqr_t = uq_rot_pad.reshape(MLA_Q_RANK, MLA_HEADS * HEAD_PAD).T
    ukv = w_ukv.reshape(MLA_KV_RANK, MLA_HEADS, MLA_NOPE + MLA_V)
    wuk = jnp.pad(ukv[..., :MLA_NOPE], ((0, 0), (0, 0), (0, HEAD_PAD - MLA_NOPE))).reshape(
        MLA_KV_RANK, MLA_HEADS * HEAD_PAD)
    wuv_t = ukv[..., MLA_NOPE:].reshape(MLA_KV_RANK, MLA_WIDTH).T
    bf = lambda a: a.astype(BF16)
    return dict(wa=bf(wa), wfq=bf(wfq_t), wfv=bf(wfv_t), wf=bf(wf_t), fb=fb, wuq=bf(wuq_t),
                wuqr=bf(wuqr_t), wuk=bf(wuk), wuv=bf(wuv_t))


def _input_projection(x2d, tabs, pw, gq, gkv, conv_w, conv_b, conv_ng, conv_nb, seq):
    n, d = x2d.shape
    tm = min(TM_IN, seq)
    cos_t, sin_t, cos_tt, sin_tt = tabs
    gidx = np.arange(CONV_CH) // (CONV_CH // CONV_GROUPS)
    gmat = jnp.asarray((gidx[:, None] == gidx[None, :]) / (CONV_CH // CONV_GROUPS), BF16)
    cw = jnp.pad(conv_w, ((0, CONV_HALO - CONV_WIDTH), (0, 0)))
    row = lambda a: a.reshape(1, -1)
    tok = lambda w: pl.BlockSpec((tm, w), lambda i: (i, 0))
    tok_t = lambda r: pl.BlockSpec((r, tm), lambda i: (0, i))
    consts = [pw["wa"], pw["wfq"], pw["wfv"], pw["wf"], pw["fb"], row(gq), pw["wuq"], pw["wuqr"],
              row(gkv), pw["wuk"], pw["wuv"], cw, row(conv_b), row(conv_ng), row(conv_nb), gmat]
    kern = functools.partial(_inproj_kernel, tiles_per_seq=seq // tm, tm=tm)
    return pl.pallas_call(
        kern,
        grid=(n // tm,),
        in_specs=[tok(d), tok(HEAD_PAD), tok(HEAD_PAD), tok_t(HEAD_PAD), tok_t(HEAD_PAD)]
        + [_const_spec(c.shape) for c in consts],
        out_specs=[pl.BlockSpec((tm // TQ, N_HEADS * HEAD_PAD, TQ), lambda i: (i, 0, 0)),
                   tok(N_HEADS * HEAD_PAD),
                   pl.BlockSpec((tm // TK, N_HEADS * V_ROWS, TK), lambda i: (i, 0, 0)),
                   tok(CONV_CH)],
        out_shape=[jax.ShapeDtypeStruct((n // TQ, N_HEADS * HEAD_PAD, TQ), BF16),
                   jax.ShapeDtypeStruct((n, N_HEADS * HEAD_PAD), BF16),
                   jax.ShapeDtypeStruct((n // TK, N_HEADS * V_ROWS, TK), BF16),
                   jax.ShapeDtypeStruct((n, CONV_CH), BF16)],
        scratch_shapes=[pltpu.VMEM((CONV_HALO + tm, CONV_CH), F32),
                        pltpu.VMEM((7, tm + CONV_HALO - 8, CONV_CH), F32),
                        pltpu.VMEM((tm, CONV_CH), F32),
                        pltpu.VMEM((_F_ROWS, LANES), F32),
                        pltpu.VMEM((tm, tm), BF16)],
        compiler_params=pltpu.CompilerParams(dimension_semantics=("arbitrary",),
                                             vmem_limit_bytes=VMEM_LIMIT),
        name="input_projection",
    )(x2d, cos_t, sin_t, cos_tt, sin_tt, *consts)


def _attn_kernel(qt_ref, k_ref, vt_ref, o_ref, *scratch, n_tiles):
    assert TQ == 2 * TK
    s_ref = (scratch[0:HPS], scratch[HPS:2 * HPS])
    sd_ref = scratch[2 * HPS:3 * HPS]
    p_ref = (scratch[3 * HPS:4 * HPS], scratch[4 * HPS:5 * HPS])
    pd_ref = scratch[5 * HPS:6 * HPS]
    acc_bufs = (scratch[6 * HPS:7 * HPS], scratch[7 * HPS:8 * HPS])
    diff_ref = scratch[8 * HPS]
    diff_ref[...] = (lax.broadcasted_iota(jnp.int32, (TK, TQ), 1)
                     - lax.broadcasted_iota(jnp.int32, (TK, TQ), 0))
    for h in range(HPS):
        p_ref[1][h][...] = jnp.zeros_like(p_ref[1][h])
        for par in range(2):
            acc_bufs[par][h][...] = jnp.ones_like(acc_bufs[par][h])

    def tile_scores(tile, j, slot):
        row0 = pl.multiple_of(j * TK, TK)
        block_max = []
        for h in range(HPS):
            s = _dot(k_ref[pl.ds(row0, TK), h * HEAD_PAD:(h + 1) * HEAD_PAD],
                     qt_ref[tile, h * HEAD_PAD:(h + 1) * HEAD_PAD, :])
            s_ref[slot][h][...] = s
            block_max.append(jnp.max(s, axis=0, keepdims=True))
        return block_max

    def last_diag_scores(tile):
        row0 = pl.multiple_of((2 * tile + 1) * TK, TK)
        for h in range(HPS):
            sd_ref[h][...] = _dot(k_ref[pl.ds(row0, TK), h * HEAD_PAD:(h + 1) * HEAD_PAD],
                                  qt_ref[tile, h * HEAD_PAD:(h + 1) * HEAD_PAD, TK:])

    def finalize(tile, par):
        out_t = jnp.concatenate([acc_bufs[par][h][0:V_DIM, :] / acc_bufs[par][h][V_DIM:V_DIM + 1, :]
                                 for h in range(HPS)], axis=0)
        o_ref[pl.ds(pl.multiple_of(tile * TQ, TQ), TQ), :] = out_t.T.astype(o_ref.dtype)

    def q_tile(i, par, bm0):
        acc_ref = acc_bufs[par]
        scores = functools.partial(tile_scores, i)

        def softmax(slot, m, block_max):
            m_new = [jnp.maximum(m[h], block_max[h]) for h in range(HPS)]
            for h in range(HPS):
                p_ref[slot][h][...] = jnp.exp2(s_ref[slot][h][...] - m_new[h]).astype(BF16)
            return m_new, [jnp.exp2(m[h] - m_new[h]) for h in range(HPS)]

        def values(j, slot, alpha, gate=None):
            for h in range(HPS):
                pv = _dot(vt_ref[j, h * V_ROWS:(h + 1) * V_ROWS, :], p_ref[slot][h][...])
                acc_ref[h][...] = alpha[h] * acc_ref[h][...] + (pv if gate is None else gate * pv)

        m0 = [jnp.full((1, TQ), M_INIT, F32)] * HPS
        a0 = [jnp.zeros((1, TQ), F32)] * HPS

        def pair(u, state):
            m, alpha, bm_t = list(state[0:HPS]), list(state[HPS:2 * HPS]), list(state[2 * HPS:3 * HPS])
            t = 2 * u
            m, alpha_t = softmax(0, m, bm_t)
            values(jnp.maximum(t - 1, 0), 1, alpha, jnp.where(t > 0, 1.0, 0.0))
            bm_t1 = scores(t + 1, 1)
            m, alpha_t1 = softmax(1, m, bm_t1)
            values(t, 0, alpha_t)
            bm_t2 = scores(t + 2, 0)
            return (*m, *alpha_t1, *bm_t2)

        state = lax.fori_loop(0, i // 2, lambda v, st: pair(2 * v + 1, pair(2 * v, st)), (*m0, *a0, *bm0))
        state = lax.fori_loop(i - i % 2, i, pair, state)
        m, alpha = list(state[0:HPS]), list(state[HPS:2 * HPS])
        d0 = 2 * i
        values(jnp.maximum(d0 - 1, 0), 1, alpha, jnp.where(i > 0, 1.0, 0.0))
        finalize(jnp.where(i > 0, i - 1, n_tiles - 1), 1 - par)
        nxt = jnp.minimum(i + 1, n_tiles - 1)

        m_d0, alpha_d0 = [], []
        for h in range(HPS):
            s = s_ref[0][h][...]
            s = jnp.concatenate([jnp.where(diff_ref[:, 0:TK] >= 0, s[:, 0:TK], -jnp.inf), s[:, TK:]], axis=1)
            m_d0.append(jnp.maximum(m[h], jnp.max(s, axis=0, keepdims=True)))
            p_ref[0][h][...] = jnp.exp2(s - m_d0[h]).astype(BF16)
            alpha_d0.append(jnp.exp2(m[h] - m_d0[h]))
        bm_next = tile_scores(nxt, 0, 0)

        alpha_d1 = []
        for h in range(HPS):
            s = jnp.where(diff_ref[:, 0:TK] >= 0, sd_ref[h][...], -jnp.inf)
            m_old = m_d0[h][:, TK:]
            m_new = jnp.maximum(m_old, jnp.max(s, axis=0, keepdims=True))
            pd_ref[h][...] = jnp.exp2(s - m_new).astype(BF16)
            alpha_d1.append(jnp.exp2(m_old - m_new))
        last_diag_scores(nxt)

        values(d0, 0, alpha_d0)
        for h in range(HPS):
            acc_ref[h][:, TK:] = alpha_d1[h] * acc_ref[h][:, TK:] + _dot(
                vt_ref[d0 + 1, h * V_ROWS:(h + 1) * V_ROWS, :], pd_ref[h][...])
        return tuple(bm_next)

    last_diag_scores(0)
    bm = lax.fori_loop(0, n_tiles // 2, lambda a, st: q_tile(2 * a + 1, 1, q_tile(2 * a, 0, st)),
                       tuple(tile_scores(0, 0, 0)))
    if n_tiles % 2:
        q_tile(n_tiles - 1, 0, bm)
    finalize(n_tiles - 1, (n_tiles - 1) % 2)


def _attention(q_t, k, v_t, batch, seq):
    n = k.shape[0]
    groups = N_HEADS // HPS
    return pl.pallas_call(
        functools.partial(_attn_kernel, n_tiles=seq // TQ),
        grid=(batch, groups),
        in_specs=[pl.BlockSpec((seq // TQ, HPS * HEAD_PAD, TQ), lambda b, p: (b, p, 0)),
                  pl.BlockSpec((seq, HPS * HEAD_PAD), lambda b, p: (b, p)),
                  pl.BlockSpec((seq // TK, HPS * V_ROWS, TK), lambda b, p: (b, p, 0))],
        out_specs=pl.BlockSpec((seq, HPS * V_DIM), lambda b, p: (b, p)),
        out_shape=jax.ShapeDtypeStruct((n, N_HEADS * V_DIM), BF16),
        scratch_shapes=[pltpu.VMEM((TK, TQ), F32)] * (2 * HPS) + [pltpu.VMEM((TK, TK), F32)] * HPS
        + [pltpu.VMEM((TK, TQ), BF16)] * (2 * HPS) + [pltpu.VMEM((TK, TK), BF16)] * HPS
        + [pltpu.VMEM((V_ROWS, TQ), F32)] * (2 * HPS) + [pltpu.VMEM((TK, TQ), jnp.int32)],
        compiler_params=pltpu.CompilerParams(dimension_semantics=("arbitrary", "arbitrary"),
                                             vmem_limit_bytes=VMEM_LIMIT),
        name="attention",
    )(q_t, k, v_t)


def _layer_norm(x, g, b):
    mu = jnp.mean(x, axis=-1, keepdims=True)
    d = x - mu
    var = jnp.mean(jnp.square(d), axis=-1, keepdims=True)
    return d * lax.rsqrt(var + NORM_EPS) * g + b


def _mix_and_norm(o_ref, hc_ref, x_ref, gm_ref, gf_ref, wo_ref, g1_ref, b1_ref):
    o = o_ref[...].astype(F32)
    mla = _rms(o[:, :MLA_WIDTH], gm_ref[...])
    fox = _rms(o[:, MLA_WIDTH:], gf_ref[...])
    mixed = jnp.concatenate([mla.astype(BF16), fox.astype(BF16), hc_ref[...]], axis=-1)
    y = _dot(mixed, wo_ref[...])
    return _layer_norm(ALPHA * x_ref[...] + y, g1_ref[...], b1_ref[...])


def _router_layer_kernel(o_ref, hc_ref, x_ref, gm_ref, gf_ref, wo_ref, g1_ref, b1_ref, rw_ref,
                         x1_ref, route_ref, counts_ref, cnt_ref, upper_ref):
    x1 = _mix_and_norm(o_ref, hc_ref, x_ref, gm_ref, gf_ref, wo_ref, g1_ref, b1_ref)
    x1_ref[...] = x1
    rw = rw_ref[...]
    x_hi = x1.astype(BF16)
    x_lo = (x1 - x_hi.astype(F32)).astype(BF16)
    w_hi = rw.astype(BF16)
    w_lo = (rw - w_hi.astype(F32)).astype(BF16)
    both = _dot(x_hi, jnp.concatenate([w_hi, w_lo], axis=1))
    logits = both[:, :LANES] + (_dot(x_lo, w_hi) + both[:, LANES:])
    tm = logits.shape[0]
    lg = logits.T[0:N_EXPERTS, :]
    row = lax.broadcasted_iota(jnp.int32, lg.shape, 0)
    v1 = jnp.max(lg, axis=0, keepdims=True)
    i1 = jnp.min(jnp.where(lg == v1, row, N_EXPERTS), axis=0, keepdims=True)
    rest_l = jnp.where(row == i1, -jnp.inf, lg)
    v2 = jnp.max(rest_l, axis=0, keepdims=True)
    i2 = jnp.min(jnp.where(rest_l == v2, row, N_EXPERTS), axis=0, keepdims=True)
    e2 = jnp.exp(v2 - v1)
    den = 1.0 + e2

    @pl.when(pl.program_id(0) == 0)
    def _():
        cnt_ref[...] = jnp.zeros_like(cnt_ref)
        r_i = lax.broadcasted_iota(jnp.int32, (tm, tm), 0)
        c_i = lax.broadcasted_iota(jnp.int32, (tm, tm), 1)
        upper_ref[...] = jnp.where(r_i < c_i, 1.0, 0.0).astype(BF16)

    sel = jnp.where(row == i1, 1.0, jnp.where(row == i2, 1.0, 0.0))
    sel16 = jnp.concatenate([sel, jnp.zeros_like(sel)], axis=0).astype(BF16)
    before = cnt_ref[:, 0:1]
    rank = _dot(sel16, upper_ref[...])[0:N_EXPERTS, :] + before
    total = before + jnp.sum(sel, axis=1, keepdims=True)
    cnt_ref[...] = jnp.broadcast_to(total, cnt_ref.shape)
    counts_ref[...] = jnp.broadcast_to(total, counts_ref.shape)
    r1 = jnp.sum(jnp.where(row == i1, rank, 0.0), axis=0, keepdims=True)
    r2 = jnp.sum(jnp.where(row == i2, rank, 0.0), axis=0, keepdims=True)
    rows = (i1.astype(F32), i2.astype(F32), r1, r2, 1.0 / den, e2 / den)
    route = jnp.zeros(lg.shape, F32)
    for c, v in enumerate(rows):
        route = jnp.where(row == c, v, route)
    route_ref[...] = route


def _router_layer(o, hc, x2d, gm, gf, w_out, g1, b1, router_w):
    n, d = x2d.shape
    tm = min(TM_OUT, n)
    row = lambda a: a.reshape(1, -1)
    tok = lambda w: pl.BlockSpec((tm, w), lambda i: (i, 0))
    consts = [row(gm), row(gf), w_out.astype(BF16), row(g1), row(b1),
              jnp.pad(router_w, ((0, 0), (0, LANES - N_EXPERTS)))]
    return pl.pallas_call(
        _router_layer_kernel,
        grid=(n // tm,),
        in_specs=[tok(o.shape[1]), tok(CONV_CH), tok(d)] + [_const_spec(c.shape) for c in consts],
        out_specs=[tok(d), pl.BlockSpec((N_EXPERTS, tm), lambda i: (0, i)),
                   pl.BlockSpec((N_EXPERTS, LANES), lambda i: (0, 0))],
        out_shape=[jax.ShapeDtypeStruct((n, d), F32), jax.ShapeDtypeStruct((N_EXPERTS, n), F32),
                   jax.ShapeDtypeStruct((N_EXPERTS, LANES), F32)],
        scratch_shapes=[pltpu.VMEM((N_EXPERTS, LANES), F32), pltpu.VMEM((tm, tm), BF16)],
        compiler_params=pltpu.CompilerParams(dimension_semantics=("arbitrary",),
                                             vmem_limit_bytes=VMEM_LIMIT),
        name="router_layer",
    )(o, hc, x2d, *consts)


def _swiglu_tile(xb, w1, w3, w2):
    h1 = _dot(xb, w1)
    h3 = _dot(xb, w3)
    hid = (h1 * jax.nn.sigmoid(h1) * h3).astype(BF16)
    return _dot(hid, w2)


def _dense_layer_kernel(o_ref, hc_ref, x_ref, gm_ref, gf_ref, wo_ref, g1_ref, b1_ref,
                        w1_ref, w3_ref, w2_ref, g2_ref, b2_ref, out_ref, *, f_chunk):
    x1 = _mix_and_norm(o_ref, hc_ref, x_ref, gm_ref, gf_ref, wo_ref, g1_ref, b1_ref)
    xb = x1.astype(BF16)
    ff = None
    for c0 in range(0, w1_ref.shape[1], f_chunk):
        part = _swiglu_tile(xb, w1_ref[:, c0:c0 + f_chunk], w3_ref[:, c0:c0 + f_chunk],
                            w2_ref[c0:c0 + f_chunk, :])
        ff = part if ff is None else ff + part
    out_ref[...] = _layer_norm(ALPHA * x1 + ff, g2_ref[...], b2_ref[...])


def _dense_layer(o, hc, x2d, gm, gf, w_out, g1, b1, w1, w3, w2, g2, b2):
    n, d = x2d.shape
    tm = min(TM_FFN, n)
    f = w1.shape[1]
    f_chunk = f // 2 if (f // 2) % LANES == 0 else f
    row = lambda a: a.reshape(1, -1)
    tok = lambda w: pl.BlockSpec((tm, w), lambda i: (i, 0))
    consts = [row(gm), row(gf), w_out.astype(BF16), row(g1), row(b1),
              w1.astype(BF16), w3.astype(BF16), w2.astype(BF16), row(g2), row(b2)]
    return pl.pallas_call(
        functools.partial(_dense_layer_kernel, f_chunk=f_chunk),
        grid=(n // tm,),
        in_specs=[tok(o.shape[1]), tok(CONV_CH), tok(d)] + [_const_spec(c.shape) for c in consts],
        out_specs=tok(d),
        out_shape=jax.ShapeDtypeStruct((n, d), F32),
        compiler_params=pltpu.CompilerParams(dimension_semantics=("arbitrary",),
                                             vmem_limit_bytes=VMEM_LIMIT),
        name="dense_layer",
    )(o, hc, x2d, *consts)


def _to_row_tiles(ref, x):
    for c in range(ROW_TILE):
        ref[pl.ds(c, x.shape[0], stride=ROW_TILE), :] = x[:, c * LANES:(c + 1) * LANES]


def _from_row_tiles(ref, t):
    return jnp.concatenate([ref[pl.ds(c, t, stride=ROW_TILE), :] for c in range(ROW_TILE)], axis=-1)


def _row_tile(ref, r):
    return ref.at[pl.ds(pl.multiple_of(r * ROW_TILE, ROW_TILE), ROW_TILE)]


def _dispatch_kernel(d1_ref, d2_ref, se_ref, x_ref, xs_ref, xr, zbuf, sem, *, tm, tr):
    i = pl.program_id(0)

    @pl.when(i == 0)
    def _():
        zbuf[...] = jnp.zeros_like(zbuf)
        for e in range(N_EXPERTS):
            end = se_ref[e]
            start_e = se_ref[e - 1] if e else 0

            for first, live in ((end - tr, end > start_e),
                                (se_ref[N_EXPERTS - 1] + e * tr,
                                 (se_ref[N_EXPERTS - 1] + e * tr) * ROW_TILE < xs_ref.shape[0])):
                @pl.when(live)
                def _():
                    rows = pl.ds(pl.multiple_of(first * ROW_TILE, ROW_TILE), tr * ROW_TILE)
                    fill = pltpu.make_async_copy(zbuf, xs_ref.at[rows], sem.at[2])
                    fill.start()
                    fill.wait()

    slot = i % 2
    _to_row_tiles(xr.at[slot], x_ref[...])
    base = i * tm

    def start(r, c):
        src = _row_tile(xr.at[slot], r)
        pltpu.make_async_copy(src, _row_tile(xs_ref, d1_ref[base + r]), sem.at[slot]).start()
        pltpu.make_async_copy(src, _row_tile(xs_ref, d2_ref[base + r]), sem.at[slot]).start(priority=1)
        return c

    def wait_step(s):
        def wait(r, c):
            for _ in range(2):
                pltpu.make_async_copy(_row_tile(xr.at[s], 0), _row_tile(xs_ref, 0), sem.at[s]).wait()
            return c

        lax.fori_loop(0, tm, wait, 0, unroll=8)

    lax.fori_loop(0, tm, start, 0, unroll=8)

    @pl.when(i > 0)
    def _():
        wait_step(1 - slot)

    @pl.when(i == pl.num_programs(0) - 1)
    def _():
        wait_step(slot)


def _expert_kernel(te_ref, blk_ref, nu_ref, xs_ref, w1_ref, w3_ref, w2_ref, ys_ref, *, tr):
    del te_ref, blk_ref
    used = pl.program_id(0) < nu_ref[0]

    @pl.when(used)
    def _():
        xb = _from_row_tiles(xs_ref, tr).astype(BF16)
        _to_row_tiles(ys_ref, _swiglu_tile(xb, w1_ref[0], w3_ref[0], w2_ref[0]))

    @pl.when(jnp.logical_not(used))
    def _():
        ys_ref[...] = jnp.zeros_like(ys_ref)


def _combine_kernel(d1_ref, d2_ref, x_ref, route_ref, g_ref, b_ref, ys_ref, o_ref, ybuf, sem, *, tm):
    i = pl.program_id(0)
    n_steps = pl.num_programs(0)

    def issue(tile, slot):
        base = tile * tm

        def start(r, c):
            pltpu.make_async_copy(_row_tile(ys_ref, d1_ref[base + r]), _row_tile(ybuf.at[slot, 0], r),
                                  sem.at[slot]).start()
            pltpu.make_async_copy(_row_tile(ys_ref, d2_ref[base + r]), _row_tile(ybuf.at[slot, 1], r),
                                  sem.at[slot]).start(priority=1)
            return c

        lax.fori_loop(0, tm, start, 0, unroll=8)

    @pl.when(i == 0)
    def _():
        issue(0, 0)

    @pl.when(i + 1 < n_steps)
    def _():
        issue(i + 1, (i + 1) % 2)

    slot = i % 2

    def wait(r, c):
        for k in range(2):
            pltpu.make_async_copy(_row_tile(ys_ref, 0), _row_tile(ybuf.at[slot, k], 0), sem.at[slot]).wait()
        return c

    lax.fori_loop(0, tm, wait, 0, unroll=8)
    route = route_ref[...]
    gates = jnp.concatenate([route, jnp.zeros((LANES - route.shape[0], tm), F32)], axis=0).T
    ff = (gates[:, 4:5] * _from_row_tiles(ybuf.at[slot, 0], tm)
          + gates[:, 5:6] * _from_row_tiles(ybuf.at[slot, 1], tm))
    o_ref[...] = _layer_norm(ALPHA * x_ref[...] + ff, g_ref[...], b_ref[...])


def _moe_ffn(x2d, route, counts, w1, w3, w2, g, b):
    n, d = x2d.shape
    n_exp, _, f = w1.shape
    tr = min(TR_MOE, n)
    tm = min(TM_MOE, n)
    n_pad = 2 * n + n_exp * tr
    n_tiles = n_pad // tr
    i32 = jnp.int32

    cnt = counts[:, 0].astype(i32)
    seg = (cnt + tr - 1) // tr * tr
    seg_end = jnp.cumsum(seg)
    seg_start = seg_end - seg
    e1, e2 = route[0].astype(i32), route[1].astype(i32)
    dest1 = seg_start[e1] + route[2].astype(i32)
    dest2 = seg_start[e2] + route[3].astype(i32)
    n_used = jnp.maximum(seg_end[-1] // tr, 1)
    tile = jnp.minimum(jnp.arange(n_tiles, dtype=i32), n_used - 1)
    tile_expert = jnp.minimum(jnp.sum(tile[:, None] * tr >= seg_end[None, :], axis=1), n_exp - 1).astype(i32)

    cparams = pltpu.CompilerParams(dimension_semantics=("arbitrary",), vmem_limit_bytes=VMEM_LIMIT)
    assert d == ROW_TILE * LANES
    xs = pl.pallas_call(
        functools.partial(_dispatch_kernel, tm=tm, tr=tr),
        grid_spec=pltpu.PrefetchScalarGridSpec(
            num_scalar_prefetch=3, grid=(n // tm,),
            in_specs=[pl.BlockSpec((tm, d), lambda i, *_: (i, 0))],
            out_specs=pl.BlockSpec(memory_space=pl.ANY),
            scratch_shapes=[pltpu.VMEM((2, tm * ROW_TILE, LANES), F32), pltpu.VMEM((tr * ROW_TILE, LANES), F32),
                            pltpu.SemaphoreType.DMA((3,))]),
        out_shape=jax.ShapeDtypeStruct((n_pad * ROW_TILE, LANES), F32),
        compiler_params=cparams,
        name="moe_dispatch",
    )(dest1, dest2, seg_end.astype(i32), x2d)

    ys = pl.pallas_call(
        functools.partial(_expert_kernel, tr=tr),
        grid_spec=pltpu.PrefetchScalarGridSpec(
            num_scalar_prefetch=3, grid=(n_tiles,),
            in_specs=[pl.BlockSpec((tr * ROW_TILE, LANES), lambda i, te, blk, nu: (blk[i], 0)),
                      pl.BlockSpec((1, d, f), lambda i, te, blk, nu: (te[i], 0, 0)),
                      pl.BlockSpec((1, d, f), lambda i, te, blk, nu: (te[i], 0, 0)),
                      pl.BlockSpec((1, f, d), lambda i, te, blk, nu: (te[i], 0, 0))],
            out_specs=pl.BlockSpec((tr * ROW_TILE, LANES), lambda i, te, blk, nu: (i, 0))),
        out_shape=jax.ShapeDtypeStruct((n_pad * ROW_TILE, LANES), F32),
        compiler_params=cparams,
        name="moe_experts",
    )(tile_expert, tile, n_used.reshape(1), xs, w1.astype(BF16), w3.astype(BF16), w2.astype(BF16))

    row = lambda a: a.reshape(1, -1)
    return pl.pallas_call(
        functools.partial(_combine_kernel, tm=tm),
        grid_spec=pltpu.PrefetchScalarGridSpec(
            num_scalar_prefetch=2, grid=(n // tm,),
            in_specs=[pl.BlockSpec((tm, d), lambda i, *_: (i, 0)),
                      pl.BlockSpec((N_EXPERTS, tm), lambda i, *_: (0, i)),
                      pl.BlockSpec((1, d), lambda i, *_: (0, 0)),
                      pl.BlockSpec((1, d), lambda i, *_: (0, 0)),
                      pl.BlockSpec(memory_space=pl.ANY)],
            out_specs=pl.BlockSpec((tm, d), lambda i, *_: (i, 0)),
            scratch_shapes=[pltpu.VMEM((2, 2, tm * ROW_TILE, LANES), F32), pltpu.SemaphoreType.DMA((2,))]),
        out_shape=jax.ShapeDtypeStruct((n, d), F32),
        compiler_params=cparams,
        name="moe_combine",
    )(dest1, dest2, x2d, route, row(g), row(b), ys)


def kernel(x, positions, w_in, mla_q_norm_g, w_uq, mla_kv_norm_g, w_ukv, fox_forget_b, conv_w, conv_b,
           conv_norm_g, conv_norm_b, mla_out_norm_g, fox_out_norm_g, w_out, ln1_g, ln1_b, dense_w1,
           dense_w3, dense_w2, router_w, expert_w1, expert_w3, expert_w2, ln2_g, ln2_b):
    batch, seq, d = x.shape
    assert d == D_MODEL and seq % TQ == 0 and seq % min(TM_IN, seq) == 0
    depth = w_in.shape[0]
    tabs = _rope_tables(positions)
    h = x.reshape(batch * seq, d)
    pw_all = jax.vmap(_prep_inproj_weights)(w_in, w_uq, w_ukv, fox_forget_b)
    w_out, dense_w1, dense_w3, dense_w2, expert_w1, expert_w3, expert_w2 = (
        w.astype(BF16) for w in (w_out, dense_w1, dense_w3, dense_w2, expert_w1, expert_w3, expert_w2))
    for layer in range(depth):
        pw = {name: w[layer] for name, w in pw_all.items()}
        q_t, k, v_t, hc = _input_projection(
            h, tabs, pw, mla_q_norm_g[layer], mla_kv_norm_g[layer], conv_w[layer], conv_b[layer],
            conv_norm_g[layer], conv_norm_b[layer], seq)
        o = _attention(q_t, k, v_t, batch, seq)
        j = layer // 2
        if layer % 2 == 0:
            h = _dense_layer(o, hc, h, mla_out_norm_g[layer], fox_out_norm_g[layer], w_out[layer],
                             ln1_g[layer], ln1_b[layer], dense_w1[j], dense_w3[j], dense_w2[j],
                             ln2_g[layer], ln2_b[layer])
        else:
            h, route, counts = _router_layer(o, hc, h, mla_out_norm_g[layer], fox_out_norm_g[layer], w_out[layer],
                                             ln1_g[layer], ln1_b[layer], router_w[j])
            h = _moe_ffn(h, route, counts, expert_w1[j], expert_w3[j], expert_w2[j], ln2_g[layer],
                         ln2_b[layer])
    return h.reshape(batch, seq, d)
```

```python
import functools
import math

import numpy as np
import jax
import jax.numpy as jnp
from jax import lax
from jax.experimental import pallas as pl
from jax.experimental.pallas import tpu as pltpu

F32 = jnp.float32
BF16 = jnp.bfloat16

D_MODEL = 1024
DEPTH = 4
MLA_HEADS = 8
MLA_NOPE = 64
MLA_ROPE = 32
MLA_V = 64
MLA_Q_RANK = 256
MLA_KV_RANK = 128
ROPE_THETA = 10000.0
FOX_HEADS = 4
FOX_DIM = 64
CONV_CH = 256
CONV_GROUPS = 4
CONV_WIDTH = 31
MLA_WIDTH = MLA_HEADS * MLA_V
FOX_WIDTH = FOX_HEADS * FOX_DIM
N_EXPERTS = 8
ALPHA = (2.0 * DEPTH) ** 0.25
NORM_EPS = 1e-5
LOG2E = math.log2(math.e)

LANES = 128
HEAD_PAD = LANES
N_HEADS = MLA_HEADS + FOX_HEADS
V_DIM = 64
V_ROWS = 80
CONV_HALO = 32
VMEM_LIMIT = 56 * 1024 * 1024

TQ = 512
TK = 256
HPS = 2
TN_ROPE = 4096
TM_IN = 1024
TM_OUT = 512
TM_FFN = 512
TR_MOE = 512
TM_MOE = 512
ROW_TILE = 8
CONV_CHUNK = 64
FF_CHUNK = 256
M_INIT = -1e30


def _nt_dot(a, b):
    return lax.dot_general(a, b, (((1,), (1,)), ((), ())), preferred_element_type=F32)


def _dot(a, b):
    return jnp.dot(a, b, preferred_element_type=F32)


def _split2_dot(a, m_bf16):
    hi = a.astype(BF16)
    lo = (a - hi.astype(F32)).astype(BF16)
    return _dot(hi, m_bf16) + _dot(lo, m_bf16)


def _split3(a):
    hi = a.astype(BF16).astype(F32)
    r1 = a - hi
    mid = r1.astype(BF16).astype(F32)
    lo = (r1 - mid).astype(BF16).astype(F32)
    return hi, mid, lo


def _const_spec(shape):
    nd = len(shape)
    return pl.BlockSpec(shape, lambda *_: (0,) * nd, pipeline_mode=pl.Buffered(1))


def _rope_kernel(pos_ref, invf_ref, c_ref, s_ref, ct_ref, st_ref):
    pos = pos_ref[...].astype(F32)
    ang = invf_ref[...] * pos
    cos = jnp.cos(ang)
    sin = jnp.sin(ang)
    tn = pos.shape[1]
    ct = jnp.concatenate([jnp.ones((MLA_NOPE, tn), F32), cos, cos, jnp.zeros((HEAD_PAD - MLA_NOPE - MLA_ROPE, tn), F32)], axis=0)
    st = jnp.concatenate([jnp.zeros((MLA_NOPE, tn), F32), sin, sin, jnp.zeros((HEAD_PAD - MLA_NOPE - MLA_ROPE, tn), F32)], axis=0)
    ct_ref[...] = ct
    st_ref[...] = st
    c_ref[...] = ct.T
    s_ref[...] = st.T


def _rope_tables(positions):
    n = positions.size
    tn = min(TN_ROPE, n)
    inv_freq = ROPE_THETA ** (-jnp.arange(0, MLA_ROPE, 2, dtype=F32) / MLA_ROPE)
    return pl.pallas_call(
        _rope_kernel,
        grid=(n // tn,),
        in_specs=[pl.BlockSpec((1, tn), lambda i: (0, i)),
                  pl.BlockSpec((MLA_ROPE // 2, 1), lambda i: (0, 0))],
        out_specs=[pl.BlockSpec((tn, HEAD_PAD), lambda i: (i, 0)),
                   pl.BlockSpec((tn, HEAD_PAD), lambda i: (i, 0)),
                   pl.BlockSpec((HEAD_PAD, tn), lambda i: (0, i)),
                   pl.BlockSpec((HEAD_PAD, tn), lambda i: (0, i))],
        out_shape=[jax.ShapeDtypeStruct((n, HEAD_PAD), F32),
                   jax.ShapeDtypeStruct((n, HEAD_PAD), F32),
                   jax.ShapeDtypeStruct((HEAD_PAD, n), F32),
                   jax.ShapeDtypeStruct((HEAD_PAD, n), F32)],
        name="rope_tables",
    )(positions.reshape(1, n), inv_freq.reshape(-1, 1))


_A_CQ = 0
_A_CKV = _A_CQ + MLA_Q_RANK
_A_KR = _A_CKV + MLA_KV_RANK
_A_KRR = _A_KR + HEAD_PAD
_A_FK = _A_KRR + HEAD_PAD
_A_CA = _A_FK + FOX_HEADS * HEAD_PAD
_A_CG = _A_CA + CONV_CH
_A_COLS = _A_CG + CONV_CH
_AUG_ROWS = 8
_F_ROWS = 16


def _rms(x, g):
    ms = jnp.mean(jnp.square(x), axis=-1, keepdims=True)
    return x * lax.rsqrt(ms + NORM_EPS) * g


def _inproj_kernel(x_ref, c_ref, s_ref, ct_ref, st_ref, wa_ref, wfq_ref, wfv_ref, wf_ref, fb_ref,
                   gq_ref, wuq_ref, wuqr_ref, gkv_ref, wuk_ref, wuv_ref,
                   cw_ref, cb_ref, cng_ref, cnb_ref, gmat_ref,
                   qt_ref, k_ref, vt_ref, hc_ref,
                   hbuf, hsh, cbuf, fcarry, upper_ref, *, tiles_per_seq, tm):
    i = pl.program_id(0)

    @pl.when(i % tiles_per_seq == 0)
    def _():
        hbuf[0:CONV_HALO, :] = jnp.zeros((CONV_HALO, CONV_CH), F32)
        fcarry[...] = jnp.zeros_like(fcarry)
        r_i = lax.broadcasted_iota(jnp.int32, (tm, tm), 0)
        c_i = lax.broadcasted_iota(jnp.int32, (tm, tm), 1)
        upper_ref[...] = jnp.where(r_i <= c_i, 1.0, 0.0).astype(BF16)

    xb = x_ref[...].astype(BF16)
    p1 = _dot(xb, wa_ref[...])
    cos_t = c_ref[...]
    sin_t = s_ref[...]
    cos_tt = ct_ref[...]
    sin_tt = st_ref[...]

    cqn = _rms(p1[:, _A_CQ:_A_CQ + MLA_Q_RANK], gq_ref[...]).astype(BF16)
    q_t = _nt_dot(wuq_ref[...], cqn)
    q_rot_t = _nt_dot(wuqr_ref[...], cqn)
    mla_scale = (MLA_NOPE + MLA_ROPE) ** -0.5 * LOG2E
    for h in range(MLA_HEADS):
        rows = slice(h * HEAD_PAD, (h + 1) * HEAD_PAD)
        qh = (q_t[rows, :] * cos_tt + q_rot_t[rows, :] * sin_tt) * mla_scale
        for c in range(tm // TQ):
            qt_ref[c, rows, :] = qh[:, c * TQ:(c + 1) * TQ].astype(BF16)

    ckvn = _rms(p1[:, _A_CKV:_A_CKV + MLA_KV_RANK], gkv_ref[...]).astype(BF16)
    k_nope = _dot(ckvn, wuk_ref[...])
    k_rope = p1[:, _A_KR:_A_KR + HEAD_PAD] * cos_t + p1[:, _A_KRR:_A_KRR + HEAD_PAD] * sin_t
    for h in range(MLA_HEADS):
        cols = slice(h * HEAD_PAD, (h + 1) * HEAD_PAD)
        k_ref[:, cols] = (k_nope[:, cols] + k_rope).astype(BF16)
    v_t = _nt_dot(wuv_ref[...], ckvn)
    fv_t = _nt_dot(wfv_ref[...], xb)
    ones_blk = jnp.where(lax.broadcasted_iota(jnp.int32, (V_ROWS - V_DIM, tm), 0) == 0, 1.0, 0.0)
    for h in range(N_HEADS):
        src = v_t if h < MLA_HEADS else fv_t
        r0 = (h if h < MLA_HEADS else h - MLA_HEADS) * V_DIM
        vh = jnp.concatenate([src[r0:r0 + V_DIM, :], ones_blk], axis=0).astype(BF16)
        for c in range(tm // TK):
            vt_ref[c, h * V_ROWS:(h + 1) * V_ROWS, :] = vh[:, c * TK:(c + 1) * TK]

    z = _nt_dot(wf_ref[...], xb) + fb_ref[...]
    logf = (jnp.minimum(z, 0.0) - jnp.log1p(jnp.exp(-jnp.abs(z)))) * LOG2E
    limbs = jnp.concatenate(_split3(logf), axis=0).astype(BF16)
    sums = _dot(limbs, upper_ref[...])
    cum = (sums[0:_F_ROWS] + sums[_F_ROWS:2 * _F_ROWS]) + sums[2 * _F_ROWS:3 * _F_ROWS]
    f_cum = cum + fcarry[:, 0:1]
    fcarry[...] = jnp.broadcast_to(f_cum[:, tm - 1:tm], fcarry.shape)
    f_hi, f_mid, f_lo = _split3(f_cum)

    fq_t = _nt_dot(wfq_ref[...], xb)
    row8 = lax.broadcasted_iota(jnp.int32, (_AUG_ROWS, tm), 0)
    fox_scale = FOX_DIM ** -0.5 * LOG2E
    for h in range(FOX_HEADS):
        bh = lambda a: jnp.broadcast_to(a[h:h + 1, :], (_AUG_ROWS, tm))
        aug_q = jnp.where(row8 == 0, bh(f_hi), jnp.where(row8 == 1, bh(f_mid), jnp.where(
            row8 == 2, bh(f_lo), jnp.where(row8 < 6, 1.0, 0.0))))
        aug_k = jnp.where(row8 < 3, 1.0, jnp.where(row8 == 3, -bh(f_hi), jnp.where(
            row8 == 4, -bh(f_mid), jnp.where(row8 == 5, -bh(f_lo), 0.0))))
        pad = jnp.zeros((HEAD_PAD - FOX_DIM - _AUG_ROWS, tm), F32)
        qh = jnp.concatenate(
            [fq_t[h * HEAD_PAD:h * HEAD_PAD + FOX_DIM, :] * fox_scale, aug_q, pad], axis=0)
        rows = slice((MLA_HEADS + h) * HEAD_PAD, (MLA_HEADS + h + 1) * HEAD_PAD)
        for c in range(tm // TQ):
            qt_ref[c, rows, :] = qh[:, c * TQ:(c + 1) * TQ].astype(BF16)
        kaug_t = jnp.concatenate([jnp.zeros((FOX_DIM, tm), F32), aug_k, pad], axis=0)
        fk = p1[:, _A_FK + h * HEAD_PAD:_A_FK + (h + 1) * HEAD_PAD]
        k_ref[:, rows] = (fk + kaug_t.T).astype(BF16)

    a = p1[:, _A_CA:_A_CA + CONV_CH]
    g = p1[:, _A_CG:_A_CG + CONV_CH]
    hbuf[CONV_HALO:CONV_HALO + tm, :] = a * jax.nn.sigmoid(g)
    chunk = CONV_CHUNK
    first = CONV_HALO - (CONV_WIDTH - 1)
    for r in range(1, 8):
        hsh[r - 1] = hbuf[r:r + tm + CONV_HALO - 8, :]
    for c0 in range(0, tm, chunk):
        acc = jnp.zeros((chunk, CONV_CH), F32)
        for o in range(first, first + CONV_WIDTH):
            r = o % 8
            row = c0 + o - r
            seg = hbuf[row:row + chunk, :] if r == 0 else hsh[r - 1, row:row + chunk, :]
            acc = acc + cw_ref[o - first:o - first + 1, :] * seg
        cbuf[c0:c0 + chunk, :] = acc
    hbuf[0:CONV_HALO, :] = hbuf[tm:tm + CONV_HALO, :]
    hv = cbuf[...] + cb_ref[...]
    gm = gmat_ref[...]
    mu = _split2_dot(hv, gm)
    d = hv - mu
    var = _split2_dot(d * d, gm)
    hn = d * lax.rsqrt(var + NORM_EPS) * cng_ref[...] + cnb_ref[...]
    hc_ref[...] = (hn * jax.nn.sigmoid(hn)).astype(BF16)


def _prep_inproj_weights(w_in, w_uq, w_ukv, fox_forget_b):
    o = np.cumsum((0, MLA_Q_RANK, MLA_KV_RANK, MLA_ROPE, FOX_WIDTH, FOX_WIDTH, FOX_WIDTH, FOX_HEADS,
                   2 * CONV_CH))
    w_cq, w_ckv, w_kr, w_fq, w_fk, w_fv, w_f, w_cv = (w_in[:, o[i]:o[i + 1]] for i in range(8))
    d = w_in.shape[0]
    half = MLA_ROPE // 2

    def rot_cols(w):
        return jnp.concatenate([-w[..., half:], w[..., :half]], axis=-1)

    def rope_block(w):
        return jnp.pad(w, ((0, 0), (MLA_NOPE, HEAD_PAD - MLA_NOPE - MLA_ROPE)))

    w_fk_pad = jnp.pad(w_fk.reshape(d, FOX_HEADS, FOX_DIM), ((0, 0), (0, 0), (0, HEAD_PAD - FOX_DIM)))
    wa = jnp.concatenate([w_cq, w_ckv, rope_block(w_kr), rope_block(rot_cols(w_kr)),
                          w_fk_pad.reshape(d, FOX_HEADS * HEAD_PAD), w_cv], axis=1)
    w_fq_pad = jnp.pad(w_fq.reshape(d, FOX_HEADS, FOX_DIM), ((0, 0), (0, 0), (0, HEAD_PAD - FOX_DIM)))
    wfq_t = w_fq_pad.reshape(d, FOX_HEADS * HEAD_PAD).T
    wfv_t = w_fv.T
    wf_t = jnp.pad(w_f, ((0, 0), (0, _F_ROWS - FOX_HEADS))).T
    fb = jnp.pad(fox_forget_b, (0, _F_ROWS - FOX_HEADS)).reshape(_F_ROWS, 1)

    uq = w_uq.reshape(MLA_Q_RANK, MLA_HEADS, MLA_NOPE + MLA_ROPE)
    uq_nope, uq_rope = uq[..., :MLA_NOPE], uq[..., MLA_NOPE:]
    tail = ((0, 0), (0, 0), (0, HEAD_PAD - MLA_NOPE - MLA_ROPE))
    uq_pad = jnp.pad(jnp.concatenate([uq_nope, uq_rope], axis=-1), tail)
    uq_rot_pad = jnp.pad(jnp.concatenate([jnp.zeros_like(uq_nope), rot_cols(uq_rope)], axis=-1), tail)
    wuq_t = uq_pad.reshape(MLA_Q_RANK, MLA_HEADS * HEAD_PAD).T
    wuqr_t = uq_rot_pad.reshape(MLA_Q_RANK, MLA_HEADS * HEAD_PAD).T
    ukv = w_ukv.reshape(MLA_KV_RANK, MLA_HEADS, MLA_NOPE + MLA_V)
    wuk = jnp.pad(ukv[..., :MLA_NOPE], ((0, 0), (0, 0), (0, HEAD_PAD - MLA_NOPE))).reshape(
        MLA_KV_RANK, MLA_HEADS * HEAD_PAD)
    wuv_t = ukv[..., MLA_NOPE:].reshape(MLA_KV_RANK, MLA_WIDTH).T
    bf = lambda a: a.astype(BF16)
    return dict(wa=bf(wa), wfq=bf(wfq_t), wfv=bf(wfv_t), wf=bf(wf_t), fb=fb, wuq=bf(wuq_t),
                wuqr=bf(wuqr_t), wuk=bf(wuk), wuv=bf(wuv_t))


def _input_projection(x2d, tabs, pw, gq, gkv, conv_w, conv_b, conv_ng, conv_nb, seq):
    n, d = x2d.shape
    tm = min(TM_IN, seq)
    cos_t, sin_t, cos_tt, sin_tt = tabs
    gidx = np.arange(CONV_CH) // (CONV_CH // CONV_GROUPS)
    gmat = jnp.asarray((gidx[:, None] == gidx[None, :]) / (CONV_CH // CONV_GROUPS), BF16)
    cw = jnp.pad(conv_w, ((0, CONV_HALO - CONV_WIDTH), (0, 0)))
    row = lambda a: a.reshape(1, -1)
    tok = lambda w: pl.BlockSpec((tm, w), lambda i: (i, 0))
    tok_t = lambda r: pl.BlockSpec((r, tm), lambda i: (0, i))
    consts = [pw["wa"], pw["wfq"], pw["wfv"], pw["wf"], pw["fb"], row(gq), pw["wuq"], pw["wuqr"],
              row(gkv), pw["wuk"], pw["wuv"], cw, row(conv_b), row(conv_ng), row(conv_nb), gmat]
    kern = functools.partial(_inproj_kernel, tiles_per_seq=seq // tm, tm=tm)
    return pl.pallas_call(
        kern,
        grid=(n // tm,),
        in_specs=[tok(d), tok(HEAD_PAD), tok(HEAD_PAD), tok_t(HEAD_PAD), tok_t(HEAD_PAD)]
        + [_const_spec(c.shape) for c in consts],
        out_specs=[pl.BlockSpec((tm // TQ, N_HEADS * HEAD_PAD, TQ), lambda i: (i, 0, 0)),
                   tok(N_HEADS * HEAD_PAD),
                   pl.BlockSpec((tm // TK, N_HEADS * V_ROWS, TK), lambda i: (i, 0, 0)),
                   tok(CONV_CH)],
        out_shape=[jax.ShapeDtypeStruct((n // TQ, N_HEADS * HEAD_PAD, TQ), BF16),
                   jax.ShapeDtypeStruct((n, N_HEADS * HEAD_PAD), BF16),
                   jax.ShapeDtypeStruct((n // TK, N_HEADS * V_ROWS, TK), BF16),
                   jax.ShapeDtypeStruct((n, CONV_CH), BF16)],
        scratch_shapes=[pltpu.VMEM((CONV_HALO + tm, CONV_CH), F32),
                        pltpu.VMEM((7, tm + CONV_HALO - 8, CONV_CH), F32),
                        pltpu.VMEM((tm, CONV_CH), F32),
                        pltpu.VMEM((_F_ROWS, LANES), F32),
                        pltpu.VMEM((tm, tm), BF16)],
        compiler_params=pltpu.CompilerParams(dimension_semantics=("arbitrary",),
                                             vmem_limit_bytes=VMEM_LIMIT),
        name="input_projection",
    )(x2d, cos_t, sin_t, cos_tt, sin_tt, *consts)


def _attn_kernel(qt_ref, k_ref, vt_ref, o_ref, *scratch, n_tiles):
    assert TQ == 2 * TK
    s_ref = (scratch[0:HPS], scratch[HPS:2 * HPS])
    sd_ref = scratch[2 * HPS:3 * HPS]
    p_ref = (scratch[3 * HPS:4 * HPS], scratch[4 * HPS:5 * HPS])
    pd_ref = scratch[5 * HPS:6 * HPS]
    acc_bufs = (scratch[6 * HPS:7 * HPS], scratch[7 * HPS:8 * HPS])
    diff_ref = scratch[8 * HPS]
    diff_ref[...] = (lax.broadcasted_iota(jnp.int32, (TK, TQ), 1)
                     - lax.broadcasted_iota(jnp.int32, (TK, TQ), 0))
    for h in range(HPS):
        p_ref[1][h][...] = jnp.zeros_like(p_ref[1][h])
        for par in range(2):
            acc_bufs[par][h][...] = jnp.ones_like(acc_bufs[par][h])

    def tile_scores(tile, j, slot):
        row0 = pl.multiple_of(j * TK, TK)
        block_max = []
        for h in range(HPS):
            s = _dot(k_ref[pl.ds(row0, TK), h * HEAD_PAD:(h + 1) * HEAD_PAD],
                     qt_ref[tile, h * HEAD_PAD:(h + 1) * HEAD_PAD, :])
            s_ref[slot][h][...] = s
            block_max.append(jnp.max(s, axis=0, keepdims=True))
        return block_max

    def last_diag_scores(tile):
        row0 = pl.multiple_of((2 * tile + 1) * TK, TK)
        for h in range(HPS):
            sd_ref[h][...] = _dot(k_ref[pl.ds(row0, TK), h * HEAD_PAD:(h + 1) * HEAD_PAD],
                                  qt_ref[tile, h * HEAD_PAD:(h + 1) * HEAD_PAD, TK:])

    def finalize(tile, par):
        out_t = jnp.concatenate([acc_bufs[par][h][0:V_DIM, :] / acc_bufs[par][h][V_DIM:V_DIM + 1, :]
                                 for h in range(HPS)], axis=0)
        o_ref[pl.ds(pl.multiple_of(tile * TQ, TQ), TQ), :] = out_t.T.astype(o_ref.dtype)

    def q_tile(i, par, bm0):
        acc_ref = acc_bufs[par]
        scores = functools.partial(tile_scores, i)

        def softmax(slot, m, block_max):
            m_new = [jnp.maximum(m[h], block_max[h]) for h in range(HPS)]
            for h in range(HPS):
                p_ref[slot][h][...] = jnp.exp2(s_ref[slot][h][...] - m_new[h]).astype(BF16)
            return m_new, [jnp.exp2(m[h] - m_new[h]) for h in range(HPS)]

        def values(j, slot, alpha, gate=None):
            for h in range(HPS):
                pv = _dot(vt_ref[j, h * V_ROWS:(h + 1) * V_ROWS, :], p_ref[slot][h][...])
                acc_ref[h][...] = alpha[h] * acc_ref[h][...] + (pv if gate is None else gate * pv)

        m0 = [jnp.full((1, TQ), M_INIT, F32)] * HPS
        a0 = [jnp.zeros((1, TQ), F32)] * HPS

        def pair(u, state):
            m, alpha, bm_t = list(state[0:HPS]), list(state[HPS:2 * HPS]), list(state[2 * HPS:3 * HPS])
            t = 2 * u
            m, alpha_t = softmax(0, m, bm_t)
            values(jnp.maximum(t - 1, 0), 1, alpha, jnp.where(t > 0, 1.0, 0.0))
            bm_t1 = scores(t + 1, 1)
            m, alpha_t1 = softmax(1, m, bm_t1)
            values(t, 0, alpha_t)
            bm_t2 = scores(t + 2, 0)
            return (*m, *alpha_t1, *bm_t2)

        state = lax.fori_loop(0, i // 2, lambda v, st: pair(2 * v + 1, pair(2 * v, st)), (*m0, *a0, *bm0))
        state = lax.fori_loop(i - i % 2, i, pair, state)
        m, alpha = list(state[0:HPS]), list(state[HPS:2 * HPS])
        d0 = 2 * i
        values(jnp.maximum(d0 - 1, 0), 1, alpha, jnp.where(i > 0, 1.0, 0.0))
        finalize(jnp.where(i > 0, i - 1, n_tiles - 1), 1 - par)
        nxt = jnp.minimum(i + 1, n_tiles - 1)

        m_d0, alpha_d0 = [], []
        for h in range(HPS):
            s = s_ref[0][h][...]
            s = jnp.concatenate([jnp.where(diff_ref[:, 0:TK] >= 0, s[:, 0:TK], -jnp.inf), s[:, TK:]], axis=1)
            m_d0.append(jnp.maximum(m[h], jnp.max(s, axis=0, keepdims=True)))
            p_ref[0][h][...] = jnp.exp2(s - m_d0[h]).astype(BF16)
            alpha_d0.append(jnp.exp2(m[h] - m_d0[h]))
        bm_next = tile_scores(nxt, 0, 0)

        alpha_d1 = []
        for h in range(HPS):
            s = jnp.where(diff_ref[:, 0:TK] >= 0, sd_ref[h][...], -jnp.inf)
            m_old = m_d0[h][:, TK:]
            m_new = jnp.maximum(m_old, jnp.max(s, axis=0, keepdims=True))
            pd_ref[h][...] = jnp.exp2(s - m_new).astype(BF16)
            alpha_d1.append(jnp.exp2(m_old - m_new))
        last_diag_scores(nxt)

        values(d0, 0, alpha_d0)
        for h in range(HPS):
            acc_ref[h][:, TK:] = alpha_d1[h] * acc_ref[h][:, TK:] + _dot(
                vt_ref[d0 + 1, h * V_ROWS:(h + 1) * V_ROWS, :], pd_ref[h][...])
        return tuple(bm_next)

    last_diag_scores(0)
    bm = lax.fori_loop(0, n_tiles // 2, lambda a, st: q_tile(2 * a + 1, 1, q_tile(2 * a, 0, st)),
                       tuple(tile_scores(0, 0, 0)))
    if n_tiles % 2:
        q_tile(n_tiles - 1, 0, bm)
    finalize(n_tiles - 1, (n_tiles - 1) % 2)


def _attention(q_t, k, v_t, batch, seq):
    n = k.shape[0]
    groups = N_HEADS // HPS
    return pl.pallas_call(
        functools.partial(_attn_kernel, n_tiles=seq // TQ),
        grid=(batch, groups),
        in_specs=[pl.BlockSpec((seq // TQ, HPS * HEAD_PAD, TQ), lambda b, p: (b, p, 0)),
                  pl.BlockSpec((seq, HPS * HEAD_PAD), lambda b, p: (b, p)),
                  pl.BlockSpec((seq // TK, HPS * V_ROWS, TK), lambda b, p: (b, p, 0))],
        out_specs=pl.BlockSpec((seq, HPS * V_DIM), lambda b, p: (b, p)),
        out_shape=jax.ShapeDtypeStruct((n, N_HEADS * V_DIM), BF16),
        scratch_shapes=[pltpu.VMEM((TK, TQ), F32)] * (2 * HPS) + [pltpu.VMEM((TK, TK), F32)] * HPS
        + [pltpu.VMEM((TK, TQ), BF16)] * (2 * HPS) + [pltpu.VMEM((TK, TK), BF16)] * HPS
        + [pltpu.VMEM((V_ROWS, TQ), F32)] * (2 * HPS) + [pltpu.VMEM((TK, TQ), jnp.int32)],
        compiler_params=pltpu.CompilerParams(dimension_semantics=("arbitrary", "arbitrary"),
                                             vmem_limit_bytes=VMEM_LIMIT),
        name="attention",
    )(q_t, k, v_t)


def _layer_norm(x, g, b):
    mu = jnp.mean(x, axis=-1, keepdims=True)
    d = x - mu
    var = jnp.mean(jnp.square(d), axis=-1, keepdims=True)
    return d * lax.rsqrt(var + NORM_EPS) * g + b


def _mix_and_norm(o_ref, hc_ref, x_ref, gm_ref, gf_ref, wo_ref, g1_ref, b1_ref):
    o = o_ref[...].astype(F32)
    mla = _rms(o[:, :MLA_WIDTH], gm_ref[...])
    fox = _rms(o[:, MLA_WIDTH:], gf_ref[...])
    mixed = jnp.concatenate([mla.astype(BF16), fox.astype(BF16), hc_ref[...]], axis=-1)
    y = _dot(mixed, wo_ref[...])
    return _layer_norm(ALPHA * x_ref[...] + y, g1_ref[...], b1_ref[...])


def _router_layer_kernel(o_ref, hc_ref, x_ref, gm_ref, gf_ref, wo_ref, g1_ref, b1_ref, rw_ref,
                         x1_ref, route_ref, counts_ref, cnt_ref, upper_ref):
    x1 = _mix_and_norm(o_ref, hc_ref, x_ref, gm_ref, gf_ref, wo_ref, g1_ref, b1_ref)
    x1_ref[...] = x1
    rw = rw_ref[...]
    x_hi = x1.astype(BF16)
    x_lo = (x1 - x_hi.astype(F32)).astype(BF16)
    w_hi = rw.astype(BF16)
    w_lo = (rw - w_hi.astype(F32)).astype(BF16)
    both = _dot(x_hi, jnp.concatenate([w_hi, w_lo], axis=1))
    logits = both[:, :LANES] + (_dot(x_lo, w_hi) + both[:, LANES:])
    tm = logits.shape[0]
    lg = logits.T[0:N_EXPERTS, :]
    row = lax.broadcasted_iota(jnp.int32, lg.shape, 0)
    v1 = jnp.max(lg, axis=0, keepdims=True)
    i1 = jnp.min(jnp.where(lg == v1, row, N_EXPERTS), axis=0, keepdims=True)
    rest_l = jnp.where(row == i1, -jnp.inf, lg)
    v2 = jnp.max(rest_l, axis=0, keepdims=True)
    i2 = jnp.min(jnp.where(rest_l == v2, row, N_EXPERTS), axis=0, keepdims=True)
    e2 = jnp.exp(v2 - v1)
    den = 1.0 + e2

    @pl.when(pl.program_id(0) == 0)
    def _():
        cnt_ref[...] = jnp.zeros_like(cnt_ref)
        r_i = lax.broadcasted_iota(jnp.int32, (tm, tm), 0)
        c_i = lax.broadcasted_iota(jnp.int32, (tm, tm), 1)
        upper_ref[...] = jnp.where(r_i < c_i, 1.0, 0.0).astype(BF16)

    sel = jnp.where(row == i1, 1.0, jnp.where(row == i2, 1.0, 0.0))
    sel16 = jnp.concatenate([sel, jnp.zeros_like(sel)], axis=0).astype(BF16)
    before = cnt_ref[:, 0:1]
    rank = _dot(sel16, upper_ref[...])[0:N_EXPERTS, :] + before
    total = before + jnp.sum(sel, axis=1, keepdims=True)
    cnt_ref[...] = jnp.broadcast_to(total, cnt_ref.shape)
    counts_ref[...] = jnp.broadcast_to(total, counts_ref.shape)
    r1 = jnp.sum(jnp.where(row == i1, rank, 0.0), axis=0, keepdims=True)
    r2 = jnp.sum(jnp.where(row == i2, rank, 0.0), axis=0, keepdims=True)
    rows = (i1.astype(F32), i2.astype(F32), r1, r2, 1.0 / den, e2 / den)
    route = jnp.zeros(lg.shape, F32)
    for c, v in enumerate(rows):
        route = jnp.where(row == c, v, route)
    route_ref[...] = route


def _router_layer(o, hc, x2d, gm, gf, w_out, g1, b1, router_w):
    n, d = x2d.shape
    tm = min(TM_OUT, n)
    row = lambda a: a.reshape(1, -1)
    tok = lambda w: pl.BlockSpec((tm, w), lambda i: (i, 0))
    consts = [row(gm), row(gf), w_out.astype(BF16), row(g1), row(b1),
              jnp.pad(router_w, ((0, 0), (0, LANES - N_EXPERTS)))]
    return pl.pallas_call(
        _router_layer_kernel,
        grid=(n // tm,),
        in_specs=[tok(o.shape[1]), tok(CONV_CH), tok(d)] + [_const_spec(c.shape) for c in consts],
        out_specs=[tok(d), pl.BlockSpec((N_EXPERTS, tm), lambda i: (0, i)),
                   pl.BlockSpec((N_EXPERTS, LANES), lambda i: (0, 0))],
        out_shape=[jax.ShapeDtypeStruct((n, d), F32), jax.ShapeDtypeStruct((N_EXPERTS, n), F32),
                   jax.ShapeDtypeStruct((N_EXPERTS, LANES), F32)],
        scratch_shapes=[pltpu.VMEM((N_EXPERTS, LANES), F32), pltpu.VMEM((tm, tm), BF16)],
        compiler_params=pltpu.CompilerParams(dimension_semantics=("arbitrary",),
                                             vmem_limit_bytes=VMEM_LIMIT),
        name="router_layer",
    )(o, hc, x2d, *consts)


def _swiglu_tile(xb, w1, w3, w2):
    h1 = _dot(xb, w1)
    h3 = _dot(xb, w3)
    hid = (h1 * jax.nn.sigmoid(h1) * h3).astype(BF16)
    return _dot(hid, w2)


def _swiglu_chunked(xb, w1_ref, w3_ref, w2_ref):
    f = w1_ref.shape[1]
    out = None
    for c0 in range(0, f, FF_CHUNK):
        c1 = min(c0 + FF_CHUNK, f)
        part = _swiglu_tile(xb, w1_ref[:, c0:c1], w3_ref[:, c0:c1], w2_ref[c0:c1, :])
        out = part if out is None else out + part
    return out


def _dense_layer_kernel(o_ref, hc_ref, x_ref, gm_ref, gf_ref, wo_ref, g1_ref, b1_ref,
                        w1_ref, w3_ref, w2_ref, g2_ref, b2_ref, out_ref):
    x1 = _mix_and_norm(o_ref, hc_ref, x_ref, gm_ref, gf_ref, wo_ref, g1_ref, b1_ref)
    ff = _swiglu_chunked(x1.astype(BF16), w1_ref, w3_ref, w2_ref)
    out_ref[...] = _layer_norm(ALPHA * x1 + ff, g2_ref[...], b2_ref[...])


def _dense_layer(o, hc, x2d, gm, gf, w_out, g1, b1, w1, w3, w2, g2, b2):
    n, d = x2d.shape
    tm = min(TM_FFN, n)
    row = lambda a: a.reshape(1, -1)
    tok = lambda w: pl.BlockSpec((tm, w), lambda i: (i, 0))
    consts = [row(gm), row(gf), w_out.astype(BF16), row(g1), row(b1),
              w1.astype(BF16), w3.astype(BF16), w2.astype(BF16), row(g2), row(b2)]
    return pl.pallas_call(
        _dense_layer_kernel,
        grid=(n // tm,),
        in_specs=[tok(o.shape[1]), tok(CONV_CH), tok(d)] + [_const_spec(c.shape) for c in consts],
        out_specs=tok(d),
        out_shape=jax.ShapeDtypeStruct((n, d), F32),
        compiler_params=pltpu.CompilerParams(dimension_semantics=("arbitrary",),
                                             vmem_limit_bytes=VMEM_LIMIT),
        name="dense_layer",
    )(o, hc, x2d, *consts)


def _to_row_tiles(ref, x):
    for c in range(ROW_TILE):
        ref[pl.ds(c, x.shape[0], stride=ROW_TILE), :] = x[:, c * LANES:(c + 1) * LANES]


def _from_row_tiles(ref, t):
    return jnp.concatenate([ref[pl.ds(c, t, stride=ROW_TILE), :] for c in range(ROW_TILE)], axis=-1)


def _row_tile(ref, r):
    return ref.at[pl.ds(pl.multiple_of(r * ROW_TILE, ROW_TILE), ROW_TILE)]


def _dispatch_kernel(d1_ref, d2_ref, se_ref, x_ref, xs_ref, xr, zbuf, sem, *, tm, tr):
    i = pl.program_id(0)

    @pl.when(i == 0)
    def _():
        zbuf[...] = jnp.zeros_like(zbuf)
        for e in range(N_EXPERTS):
            end = se_ref[e]
            start_e = se_ref[e - 1] if e else 0

            for first, live in ((end - tr, end > start_e),
                                (se_ref[N_EXPERTS - 1] + e * tr,
                                 (se_ref[N_EXPERTS - 1] + e * tr) * ROW_TILE < xs_ref.shape[0])):
                @pl.when(live)
                def _():
                    rows = pl.ds(pl.multiple_of(first * ROW_TILE, ROW_TILE), tr * ROW_TILE)
                    fill = pltpu.make_async_copy(zbuf, xs_ref.at[rows], sem.at[2])
                    fill.start()
                    fill.wait()

    slot = i % 2
    _to_row_tiles(xr.at[slot], x_ref[...])
    base = i * tm

    def start(r, c):
        src = _row_tile(xr.at[slot], r)
        pltpu.make_async_copy(src, _row_tile(xs_ref, d1_ref[base + r]), sem.at[slot]).start()
        pltpu.make_async_copy(src, _row_tile(xs_ref, d2_ref[base + r]), sem.at[slot]).start(priority=1)
        return c

    def wait_step(s):
        def wait(r, c):
            for _ in range(2):
                pltpu.make_async_copy(_row_tile(xr.at[s], 0), _row_tile(xs_ref, 0), sem.at[s]).wait()
            return c

        lax.fori_loop(0, tm, wait, 0, unroll=8)

    lax.fori_loop(0, tm, start, 0, unroll=8)

    @pl.when(i > 0)
    def _():
        wait_step(1 - slot)

    @pl.when(i == pl.num_programs(0) - 1)
    def _():
        wait_step(slot)


def _expert_kernel(te_ref, blk_ref, nu_ref, xs_ref, w1_ref, w3_ref, w2_ref, ys_ref, *, tr):
    del te_ref, blk_ref
    used = pl.program_id(0) < nu_ref[0]

    @pl.when(used)
    def _():
        xb = _from_row_tiles(xs_ref, tr).astype(BF16)
        _to_row_tiles(ys_ref, _swiglu_chunked(xb, w1_ref.at[0], w3_ref.at[0], w2_ref.at[0]))

    @pl.when(jnp.logical_not(used))
    def _():
        ys_ref[...] = jnp.zeros_like(ys_ref)


def _combine_kernel(d1_ref, d2_ref, x_ref, route_ref, g_ref, b_ref, ys_ref, o_ref, ybuf, sem, *, tm):
    i = pl.program_id(0)
    n_steps = pl.num_programs(0)

    def issue(tile, slot):
        base = tile * tm

        def start(r, c):
            pltpu.make_async_copy(_row_tile(ys_ref, d1_ref[base + r]), _row_tile(ybuf.at[slot, 0], r),
                                  sem.at[slot]).start()
            pltpu.make_async_copy(_row_tile(ys_ref, d2_ref[base + r]), _row_tile(ybuf.at[slot, 1], r),
                                  sem.at[slot]).start(priority=1)
            return c

        lax.fori_loop(0, tm, start, 0, unroll=8)

    @pl.when(i == 0)
    def _():
        issue(0, 0)

    @pl.when(i + 1 < n_steps)
    def _():
        issue(i + 1, (i + 1) % 2)

    slot = i % 2

    def wait(r, c):
        for k in range(2):
            pltpu.make_async_copy(_row_tile(ys_ref, 0), _row_tile(ybuf.at[slot, k], 0), sem.at[slot]).wait()
        return c

    lax.fori_loop(0, tm, wait, 0, unroll=8)
    route = route_ref[...]
    gates = jnp.concatenate([route, jnp.zeros((LANES - route.shape[0], tm), F32)], axis=0).T
    ff = (gates[:, 4:5] * _from_row_tiles(ybuf.at[slot, 0], tm)
          + gates[:, 5:6] * _from_row_tiles(ybuf.at[slot, 1], tm))
    o_ref[...] = _layer_norm(ALPHA * x_ref[...] + ff, g_ref[...], b_ref[...])


def _moe_ffn(x2d, route, counts, w1, w3, w2, g, b):
    n, d = x2d.shape
    n_exp, _, f = w1.shape
    tr = min(TR_MOE, n)
    tm = min(TM_MOE, n)
    n_pad = 2 * n + n_exp * tr
    n_tiles = n_pad // tr
    i32 = jnp.int32

    cnt = counts[:, 0].astype(i32)
    seg = (cnt + tr - 1) // tr * tr
    seg_end = jnp.cumsum(seg)
    seg_start = seg_end - seg
    e1, e2 = route[0].astype(i32), route[1].astype(i32)
    dest1 = seg_start[e1] + route[2].astype(i32)
    dest2 = seg_start[e2] + route[3].astype(i32)
    n_used = jnp.maximum(seg_end[-1] // tr, 1)
    tile = jnp.minimum(jnp.arange(n_tiles, dtype=i32), n_used - 1)
    tile_expert = jnp.minimum(jnp.sum(tile[:, None] * tr >= seg_end[None, :], axis=1), n_exp - 1).astype(i32)

    cparams = pltpu.CompilerParams(dimension_semantics=("arbitrary",), vmem_limit_bytes=VMEM_LIMIT)
    assert d == ROW_TILE * LANES
    xs = pl.pallas_call(
        functools.partial(_dispatch_kernel, tm=tm, tr=tr),
        grid_spec=pltpu.PrefetchScalarGridSpec(
            num_scalar_prefetch=3, grid=(n // tm,),
            in_specs=[pl.BlockSpec((tm, d), lambda i, *_: (i, 0))],
            out_specs=pl.BlockSpec(memory_space=pl.ANY),
            scratch_shapes=[pltpu.VMEM((2, tm * ROW_TILE, LANES), F32), pltpu.VMEM((tr * ROW_TILE, LANES), F32),
                            pltpu.SemaphoreType.DMA((3,))]),
        out_shape=jax.ShapeDtypeStruct((n_pad * ROW_TILE, LANES), F32),
        compiler_params=cparams,
        name="moe_dispatch",
    )(dest1, dest2, seg_end.astype(i32), x2d)

    ys = pl.pallas_call(
        functools.partial(_expert_kernel, tr=tr),
        grid_spec=pltpu.PrefetchScalarGridSpec(
            num_scalar_prefetch=3, grid=(n_tiles,),
            in_specs=[pl.BlockSpec((tr * ROW_TILE, LANES), lambda i, te, blk, nu: (blk[i], 0)),
                      pl.BlockSpec((1, d, f), lambda i, te, blk, nu: (te[i], 0, 0)),
                      pl.BlockSpec((1, d, f), lambda i, te, blk, nu: (te[i], 0, 0)),
                      pl.BlockSpec((1, f, d), lambda i, te, blk, nu: (te[i], 0, 0))],
            out_specs=pl.BlockSpec((tr * ROW_TILE, LANES), lambda i, te, blk, nu: (i, 0))),
        out_shape=jax.ShapeDtypeStruct((n_pad * ROW_TILE, LANES), F32),
        compiler_params=cparams,
        name="moe_experts",
    )(tile_expert, tile, n_used.reshape(1), xs, w1.astype(BF16), w3.astype(BF16), w2.astype(BF16))

    row = lambda a: a.reshape(1, -1)
    return pl.pallas_call(
        functools.partial(_combine_kernel, tm=tm),
        grid_spec=pltpu.PrefetchScalarGridSpec(
            num_scalar_prefetch=2, grid=(n // tm,),
            in_specs=[pl.BlockSpec((tm, d), lambda i, *_: (i, 0)),
                      pl.BlockSpec((N_EXPERTS, tm), lambda i, *_: (0, i)),
                      pl.BlockSpec((1, d), lambda i, *_: (0, 0)),
                      pl.BlockSpec((1, d), lambda i, *_: (0, 0)),
                      pl.BlockSpec(memory_space=pl.ANY)],
            out_specs=pl.BlockSpec((tm, d), lambda i, *_: (i, 0)),
            scratch_shapes=[pltpu.VMEM((2, 2, tm * ROW_TILE, LANES), F32), pltpu.SemaphoreType.DMA((2,))]),
        out_shape=jax.ShapeDtypeStruct((n, d), F32),
        compiler_params=cparams,
        name="moe_combine",
    )(dest1, dest2, x2d, route, row(g), row(b), ys)


def kernel(x, positions, w_in, mla_q_norm_g, w_uq, mla_kv_norm_g, w_ukv, fox_forget_b, conv_w, conv_b,
           conv_norm_g, conv_norm_b, mla_out_norm_g, fox_out_norm_g, w_out, ln1_g, ln1_b, dense_w1,
           dense_w3, dense_w2, router_w, expert_w1, expert_w3, expert_w2, ln2_g, ln2_b):
    batch, seq, d = x.shape
    assert d == D_MODEL and seq % TQ == 0 and seq % min(TM_IN, seq) == 0
    depth = w_in.shape[0]
    tabs = _rope_tables(positions)
    h = x.reshape(batch * seq, d)
    pw_all = jax.vmap(_prep_inproj_weights)(w_in, w_uq, w_ukv, fox_forget_b)
    w_out, dense_w1, dense_w3, dense_w2, expert_w1, expert_w3, expert_w2 = (
        w.astype(BF16) for w in (w_out, dense_w1, dense_w3, dense_w2, expert_w1, expert_w3, expert_w2))
    for layer in range(depth):
        pw = {name: w[layer] for name, w in pw_all.items()}
        q_t, k, v_t, hc = _input_projection(
            h, tabs, pw, mla_q_norm_g[layer], mla_kv_norm_g[layer], conv_w[layer], conv_b[layer],
            conv_norm_g[layer], conv_norm_b[layer], seq)
        o = _attention(q_t, k, v_t, batch, seq)
        j = layer // 2
        if layer % 2 == 0:
            h = _dense_layer(o, hc, h, mla_out_norm_g[layer], fox_out_norm_g[layer], w_out[layer],
                             ln1_g[layer], ln1_b[layer], dense_w1[j], dense_w3[j], dense_w2[j],
                             ln2_g[layer], ln2_b[layer])
        else:
            h, route, counts = _router_layer(o, hc, h, mla_out_norm_g[layer], fox_out_norm_g[layer], w_out[layer],
                                             ln1_g[layer], ln1_b[layer], router_w[j])
            h = _moe_ffn(h, route, counts, expert_w1[j], expert_w3[j], expert_w2[j], ln2_g[layer],
                         ln2_b[layer])
    return h.reshape(batch, seq, d)
```

```python
import functools
import math

import numpy as np
import jax
import jax.numpy as jnp
from jax import lax
from jax.experimental import pallas as pl
from jax.experimental.pallas import tpu as pltpu

F32 = jnp.float32
BF16 = jnp.bfloat16

D_MODEL = 1024
DEPTH = 4
MLA_HEADS = 8
MLA_NOPE = 64
MLA_ROPE = 32
MLA_V = 64
MLA_Q_RANK = 256
MLA_KV_RANK = 128
ROPE_THETA = 10000.0
FOX_HEADS = 4
FOX_DIM = 64
CONV_CH = 256
CONV_GROUPS = 4
CONV_WIDTH = 31
MLA_WIDTH = MLA_HEADS * MLA_V
FOX_WIDTH = FOX_HEADS * FOX_DIM
N_EXPERTS = 8
ALPHA = (2.0 * DEPTH) ** 0.25
NORM_EPS = 1e-5
LOG2E = math.log2(math.e)

LANES = 128
HEAD_PAD = LANES
N_HEADS = MLA_HEADS + FOX_HEADS
V_DIM = 64
V_ROWS = 80
CONV_HALO = 32
VMEM_LIMIT = 56 * 1024 * 1024

TQ = 512
TK = 256
HPS = 2
SOFTMAX_ROWS = 64
TN_ROPE = 4096
TM_IN = 1024
TM_OUT = 512
TM_FFN = 512
TR_MOE = 512
TM_MOE = 512
ROW_TILE = 8
CONV_CHUNK = 64
FF_CHUNK = 256
M_INIT = -1e30


def _nt_dot(a, b):
    return lax.dot_general(a, b, (((1,), (1,)), ((), ())), preferred_element_type=F32)


def _dot(a, b):
    return jnp.dot(a, b, preferred_element_type=F32)


def _split2_dot(a, m_bf16):
    hi = a.astype(BF16)
    lo = (a - hi.astype(F32)).astype(BF16)
    return _dot(hi, m_bf16) + _dot(lo, m_bf16)


def _split3(a):
    hi = a.astype(BF16).astype(F32)
    r1 = a - hi
    mid = r1.astype(BF16).astype(F32)
    lo = (r1 - mid).astype(BF16).astype(F32)
    return hi, mid, lo


def _const_spec(shape):
    nd = len(shape)
    return pl.BlockSpec(shape, lambda *_: (0,) * nd, pipeline_mode=pl.Buffered(1))


def _rope_kernel(pos_ref, invf_ref, c_ref, s_ref, ct_ref, st_ref):
    pos = pos_ref[...].astype(F32)
    ang = invf_ref[...] * pos
    cos = jnp.cos(ang)
    sin = jnp.sin(ang)
    tn = pos.shape[1]
    ct = jnp.concatenate([jnp.ones((MLA_NOPE, tn), F32), cos, cos, jnp.zeros((HEAD_PAD - MLA_NOPE - MLA_ROPE, tn), F32)], axis=0)
    st = jnp.concatenate([jnp.zeros((MLA_NOPE, tn), F32), sin, sin, jnp.zeros((HEAD_PAD - MLA_NOPE - MLA_ROPE, tn), F32)], axis=0)
    ct_ref[...] = ct
    st_ref[...] = st
    c_ref[...] = ct.T
    s_ref[...] = st.T


def _rope_tables(positions):
    n = positions.size
    tn = min(TN_ROPE, n)
    inv_freq = ROPE_THETA ** (-jnp.arange(0, MLA_ROPE, 2, dtype=F32) / MLA_ROPE)
    return pl.pallas_call(
        _rope_kernel,
        grid=(n // tn,),
        in_specs=[pl.BlockSpec((1, tn), lambda i: (0, i)),
                  pl.BlockSpec((MLA_ROPE // 2, 1), lambda i: (0, 0))],
        out_specs=[pl.BlockSpec((tn, HEAD_PAD), lambda i: (i, 0)),
                   pl.BlockSpec((tn, HEAD_PAD), lambda i: (i, 0)),
                   pl.BlockSpec((HEAD_PAD, tn), lambda i: (0, i)),
                   pl.BlockSpec((HEAD_PAD, tn), lambda i: (0, i))],
        out_shape=[jax.ShapeDtypeStruct((n, HEAD_PAD), F32),
                   jax.ShapeDtypeStruct((n, HEAD_PAD), F32),
                   jax.ShapeDtypeStruct((HEAD_PAD, n), F32),
                   jax.ShapeDtypeStruct((HEAD_PAD, n), F32)],
        name="rope_tables",
    )(positions.reshape(1, n), inv_freq.reshape(-1, 1))


_A_CQ = 0
_A_CKV = _A_CQ + MLA_Q_RANK
_A_KR = _A_CKV + MLA_KV_RANK
_A_KRR = _A_KR + HEAD_PAD
_A_FK = _A_KRR + HEAD_PAD
_A_CA = _A_FK + FOX_HEADS * HEAD_PAD
_A_CG = _A_CA + CONV_CH
_A_COLS = _A_CG + CONV_CH
_AUG_ROWS = 8
_F_ROWS = 16


def _rms(x, g):
    ms = jnp.mean(jnp.square(x), axis=-1, keepdims=True)
    return x * lax.rsqrt(ms + NORM_EPS) * g


def _inproj_kernel(x_ref, c_ref, s_ref, ct_ref, st_ref, wa_ref, wfq_ref, wfv_ref, wf_ref, fb_ref,
                   gq_ref, wuq_ref, wuqr_ref, gkv_ref, wuk_ref, wuv_ref,
                   cw_ref, cb_ref, cng_ref, cnb_ref, gmat_ref,
                   qt_ref, k_ref, vt_ref, hc_ref,
                   hbuf, hsh, cbuf, fcarry, upper_ref, *, tiles_per_seq, tm):
    i = pl.program_id(0)

    @pl.when(i % tiles_per_seq == 0)
    def _():
        hbuf[0:CONV_HALO, :] = jnp.zeros((CONV_HALO, CONV_CH), F32)
        fcarry[...] = jnp.zeros_like(fcarry)
        r_i = lax.broadcasted_iota(jnp.int32, (tm, tm), 0)
        c_i = lax.broadcasted_iota(jnp.int32, (tm, tm), 1)
        upper_ref[...] = jnp.where(r_i <= c_i, 1.0, 0.0).astype(BF16)

    xb = x_ref[...].astype(BF16)
    p1 = _dot(xb, wa_ref[...])
    cos_t = c_ref[...]
    sin_t = s_ref[...]
    cos_tt = ct_ref[...]
    sin_tt = st_ref[...]

    cqn = _rms(p1[:, _A_CQ:_A_CQ + MLA_Q_RANK], gq_ref[...]).astype(BF16)
    q_t = _nt_dot(wuq_ref[...], cqn)
    q_rot_t = _nt_dot(wuqr_ref[...], cqn)
    mla_scale = (MLA_NOPE + MLA_ROPE) ** -0.5 * LOG2E
    for h in range(MLA_HEADS):
        rows = slice(h * HEAD_PAD, (h + 1) * HEAD_PAD)
        qh = (q_t[rows, :] * cos_tt + q_rot_t[rows, :] * sin_tt) * mla_scale
        for c in range(tm // TQ):
            qt_ref[c, rows, :] = qh[:, c * TQ:(c + 1) * TQ].astype(BF16)

    ckvn = _rms(p1[:, _A_CKV:_A_CKV + MLA_KV_RANK], gkv_ref[...]).astype(BF16)
    k_nope = _dot(ckvn, wuk_ref[...])
    k_rope = p1[:, _A_KR:_A_KR + HEAD_PAD] * cos_t + p1[:, _A_KRR:_A_KRR + HEAD_PAD] * sin_t
    for h in range(MLA_HEADS):
        cols = slice(h * HEAD_PAD, (h + 1) * HEAD_PAD)
        k_ref[:, cols] = (k_nope[:, cols] + k_rope).astype(BF16)
    v_t = _nt_dot(wuv_ref[...], ckvn)
    fv_t = _nt_dot(wfv_ref[...], xb)
    ones_blk = jnp.where(lax.broadcasted_iota(jnp.int32, (V_ROWS - V_DIM, tm), 0) == 0, 1.0, 0.0)
    for h in range(N_HEADS):
        src = v_t if h < MLA_HEADS else fv_t
        r0 = (h if h < MLA_HEADS else h - MLA_HEADS) * V_DIM
        vh = jnp.concatenate([src[r0:r0 + V_DIM, :], ones_blk], axis=0).astype(BF16)
        for c in range(tm // TK):
            vt_ref[c, h * V_ROWS:(h + 1) * V_ROWS, :] = vh[:, c * TK:(c + 1) * TK]

    z = _nt_dot(wf_ref[...], xb) + fb_ref[...]
    logf = (jnp.minimum(z, 0.0) - jnp.log1p(jnp.exp(-jnp.abs(z)))) * LOG2E
    limbs = jnp.concatenate(_split3(logf), axis=0).astype(BF16)
    sums = _dot(limbs, upper_ref[...])
    cum = (sums[0:_F_ROWS] + sums[_F_ROWS:2 * _F_ROWS]) + sums[2 * _F_ROWS:3 * _F_ROWS]
    f_cum = cum + fcarry[:, 0:1]
    fcarry[...] = jnp.broadcast_to(f_cum[:, tm - 1:tm], fcarry.shape)
    f_hi, f_mid, f_lo = _split3(f_cum)

    fq_t = _nt_dot(wfq_ref[...], xb)
    row8 = lax.broadcasted_iota(jnp.int32, (_AUG_ROWS, tm), 0)
    fox_scale = FOX_DIM ** -0.5 * LOG2E
    for h in range(FOX_HEADS):
        bh = lambda a: jnp.broadcast_to(a[h:h + 1, :], (_AUG_ROWS, tm))
        aug_q = jnp.where(row8 == 0, bh(f_hi), jnp.where(row8 == 1, bh(f_mid), jnp.where(
            row8 == 2, bh(f_lo), jnp.where(row8 < 6, 1.0, 0.0))))
        aug_k = jnp.where(row8 < 3, 1.0, jnp.where(row8 == 3, -bh(f_hi), jnp.where(
            row8 == 4, -bh(f_mid), jnp.where(row8 == 5, -bh(f_lo), 0.0))))
        pad = jnp.zeros((HEAD_PAD - FOX_DIM - _AUG_ROWS, tm), F32)
        qh = jnp.concatenate(
            [fq_t[h * HEAD_PAD:h * HEAD_PAD + FOX_DIM, :] * fox_scale, aug_q, pad], axis=0)
        rows = slice((MLA_HEADS + h) * HEAD_PAD, (MLA_HEADS + h + 1) * HEAD_PAD)
        for c in range(tm // TQ):
            qt_ref[c, rows, :] = qh[:, c * TQ:(c + 1) * TQ].astype(BF16)
        kaug_t = jnp.concatenate([jnp.zeros((FOX_DIM, tm), F32), aug_k, pad], axis=0)
        fk = p1[:, _A_FK + h * HEAD_PAD:_A_FK + (h + 1) * HEAD_PAD]
        k_ref[:, rows] = (fk + kaug_t.T).astype(BF16)

    a = p1[:, _A_CA:_A_CA + CONV_CH]
    g = p1[:, _A_CG:_A_CG + CONV_CH]
    hbuf[CONV_HALO:CONV_HALO + tm, :] = a * jax.nn.sigmoid(g)
    chunk = CONV_CHUNK
    first = CONV_HALO - (CONV_WIDTH - 1)
    for r in range(1, 8):
        hsh[r - 1] = hbuf[r:r + tm + CONV_HALO - 8, :]
    for c0 in range(0, tm, chunk):
        acc = jnp.zeros((chunk, CONV_CH), F32)
        for o in range(first, first + CONV_WIDTH):
            r = o % 8
            row = c0 + o - r
            seg = hbuf[row:row + chunk, :] if r == 0 else hsh[r - 1, row:row + chunk, :]
            acc = acc + cw_ref[o - first:o - first + 1, :] * seg
        cbuf[c0:c0 + chunk, :] = acc
    hbuf[0:CONV_HALO, :] = hbuf[tm:tm + CONV_HALO, :]
    hv = cbuf[...] + cb_ref[...]
    gm = gmat_ref[...]
    mu = _split2_dot(hv, gm)
    d = hv - mu
    var = _split2_dot(d * d, gm)
    hn = d * lax.rsqrt(var + NORM_EPS) * cng_ref[...] + cnb_ref[...]
    hc_ref[...] = (hn * jax.nn.sigmoid(hn)).astype(BF16)


def _prep_inproj_weights(w_in, w_uq, w_ukv, fox_forget_b):
    o = np.cumsum((0, MLA_Q_RANK, MLA_KV_RANK, MLA_ROPE, FOX_WIDTH, FOX_WIDTH, FOX_WIDTH, FOX_HEADS,
                   2 * CONV_CH))
    w_cq, w_ckv, w_kr, w_fq, w_fk, w_fv, w_f, w_cv = (w_in[:, o[i]:o[i + 1]] for i in range(8))
    d = w_in.shape[0]
    half = MLA_ROPE // 2

    def rot_cols(w):
        return jnp.concatenate([-w[..., half:], w[..., :half]], axis=-1)

    def rope_block(w):
        return jnp.pad(w, ((0, 0), (MLA_NOPE, HEAD_PAD - MLA_NOPE - MLA_ROPE)))

    w_fk_pad = jnp.pad(w_fk.reshape(d, FOX_HEADS, FOX_DIM), ((0, 0), (0, 0), (0, HEAD_PAD - FOX_DIM)))
    wa = jnp.concatenate([w_cq, w_ckv, rope_block(w_kr), rope_block(rot_cols(w_kr)),
                          w_fk_pad.reshape(d, FOX_HEADS * HEAD_PAD), w_cv], axis=1)
    w_fq_pad = jnp.pad(w_fq.reshape(d, FOX_HEADS, FOX_DIM), ((0, 0), (0, 0), (0, HEAD_PAD - FOX_DIM)))
    wfq_t = w_fq_pad.reshape(d, FOX_HEADS * HEAD_PAD).T
    wfv_t = w_fv.T
    wf_t = jnp.pad(w_f, ((0, 0), (0, _F_ROWS - FOX_HEADS))).T
    fb = jnp.pad(fox_forget_b, (0, _F_ROWS - FOX_HEADS)).reshape(_F_ROWS, 1)

    uq = w_uq.reshape(MLA_Q_RANK, MLA_HEADS, MLA_NOPE + MLA_ROPE)
    uq_nope, uq_rope = uq[..., :MLA_NOPE], uq[..., MLA_NOPE:]
    tail = ((0, 0), (0, 0), (0, HEAD_PAD - MLA_NOPE - MLA_ROPE))
    uq_pad = jnp.pad(jnp.concatenate([uq_nope, uq_rope], axis=-1), tail)
    uq_rot_pad = jnp.pad(jnp.concatenate([jnp.zeros_like(uq_nope), rot_cols(uq_rope)], axis=-1), tail)
    wuq_t = uq_pad.reshape(MLA_Q_RANK, MLA_HEADS * HEAD_PAD).T
    wuqr_t = uq_rot_pad.reshape(MLA_Q_RANK, MLA_HEADS * HEAD_PAD).T
    ukv = w_ukv.reshape(MLA_KV_RANK, MLA_HEADS, MLA_NOPE + MLA_V)
    wuk = jnp.pad(ukv[..., :MLA_NOPE], ((0, 0), (0, 0), (0, HEAD_PAD - MLA_NOPE))).reshape(
        MLA_KV_RANK, MLA_HEADS * HEAD_PAD)
    wuv_t = ukv[..., MLA_NOPE:].reshape(MLA_KV_RANK, MLA_WIDTH).T
    bf = lambda a: a.astype(BF16)
    return dict(wa=bf(wa), wfq=bf(wfq_t), wfv=bf(wfv_t), wf=bf(wf_t), fb=fb, wuq=bf(wuq_t),
                wuqr=bf(wuqr_t), wuk=bf(wuk), wuv=bf(wuv_t))


def _input_projection(x2d, tabs, pw, gq, gkv, conv_w, conv_b, conv_ng, conv_nb, seq):
    n, d = x2d.shape
    tm = min(TM_IN, seq)
    cos_t, sin_t, cos_tt, sin_tt = tabs
    gidx = np.arange(CONV_CH) // (CONV_CH // CONV_GROUPS)
    gmat = jnp.asarray((gidx[:, None] == gidx[None, :]) / (CONV_CH // CONV_GROUPS), BF16)
    cw = jnp.pad(conv_w, ((0, CONV_HALO - CONV_WIDTH), (0, 0)))
    row = lambda a: a.reshape(1, -1)
    tok = lambda w: pl.BlockSpec((tm, w), lambda i: (i, 0))
    tok_t = lambda r: pl.BlockSpec((r, tm), lambda i: (0, i))
    consts = [pw["wa"], pw["wfq"], pw["wfv"], pw["wf"], pw["fb"], row(gq), pw["wuq"], pw["wuqr"],
              row(gkv), pw["wuk"], pw["wuv"], cw, row(conv_b), row(conv_ng), row(conv_nb), gmat]
    kern = functools.partial(_inproj_kernel, tiles_per_seq=seq // tm, tm=tm)
    return pl.pallas_call(
        kern,
        grid=(n // tm,),
        in_specs=[tok(d), tok(HEAD_PAD), tok(HEAD_PAD), tok_t(HEAD_PAD), tok_t(HEAD_PAD)]
        + [_const_spec(c.shape) for c in consts],
        out_specs=[pl.BlockSpec((tm // TQ, N_HEADS * HEAD_PAD, TQ), lambda i: (i, 0, 0)),
                   tok(N_HEADS * HEAD_PAD),
                   pl.BlockSpec((tm // TK, N_HEADS * V_ROWS, TK), lambda i: (i, 0, 0)),
                   tok(CONV_CH)],
        out_shape=[jax.ShapeDtypeStruct((n // TQ, N_HEADS * HEAD_PAD, TQ), BF16),
                   jax.ShapeDtypeStruct((n, N_HEADS * HEAD_PAD), BF16),
                   jax.ShapeDtypeStruct((n // TK, N_HEADS * V_ROWS, TK), BF16),
                   jax.ShapeDtypeStruct((n, CONV_CH), BF16)],
        scratch_shapes=[pltpu.VMEM((CONV_HALO + tm, CONV_CH), F32),
                        pltpu.VMEM((7, tm + CONV_HALO - 8, CONV_CH), F32),
                        pltpu.VMEM((tm, CONV_CH), F32),
                        pltpu.VMEM((_F_ROWS, LANES), F32),
                        pltpu.VMEM((tm, tm), BF16)],
        compiler_params=pltpu.CompilerParams(dimension_semantics=("arbitrary",),
                                             vmem_limit_bytes=VMEM_LIMIT),
        name="input_projection",
    )(x2d, cos_t, sin_t, cos_tt, sin_tt, *consts)


def _attn_kernel(qt_ref, k_ref, vt_ref, o_ref, *scratch, n_tiles):
    assert TQ == 2 * TK
    s_ref = (scratch[0:HPS], scratch[HPS:2 * HPS])
    sd_ref = scratch[2 * HPS:3 * HPS]
    p_ref = (scratch[3 * HPS:4 * HPS], scratch[4 * HPS:5 * HPS])
    pd_ref = scratch[5 * HPS:6 * HPS]
    acc_bufs = (scratch[6 * HPS:7 * HPS], scratch[7 * HPS:8 * HPS])
    diff_ref = scratch[8 * HPS]
    diff_ref[...] = (lax.broadcasted_iota(jnp.int32, (TK, TQ), 1)
                     - lax.broadcasted_iota(jnp.int32, (TK, TQ), 0))
    for h in range(HPS):
        p_ref[1][h][...] = jnp.zeros_like(p_ref[1][h])
        for par in range(2):
            acc_bufs[par][h][...] = jnp.ones_like(acc_bufs[par][h])

    def tile_scores(tile, j, slot):
        row0 = pl.multiple_of(j * TK, TK)
        block_max = []
        for h in range(HPS):
            s = _dot(k_ref[pl.ds(row0, TK), h * HEAD_PAD:(h + 1) * HEAD_PAD],
                     qt_ref[tile, h * HEAD_PAD:(h + 1) * HEAD_PAD, :])
            s_ref[slot][h][...] = s
            block_max.append(jnp.max(s, axis=0, keepdims=True))
        return block_max

    def last_diag_scores(tile):
        row0 = pl.multiple_of((2 * tile + 1) * TK, TK)
        for h in range(HPS):
            sd_ref[h][...] = _dot(k_ref[pl.ds(row0, TK), h * HEAD_PAD:(h + 1) * HEAD_PAD],
                                  qt_ref[tile, h * HEAD_PAD:(h + 1) * HEAD_PAD, TK:])

    def finalize(tile, par):
        out_t = jnp.concatenate([acc_bufs[par][h][0:V_DIM, :] / acc_bufs[par][h][V_DIM:V_DIM + 1, :]
                                 for h in range(HPS)], axis=0)
        o_ref[pl.ds(pl.multiple_of(tile * TQ, TQ), TQ), :] = out_t.T.astype(o_ref.dtype)

    def q_tile(i, par, bm0):
        acc_ref = acc_bufs[par]
        scores = functools.partial(tile_scores, i)

        def softmax(slot, m, block_max):
            m_new = [jnp.maximum(m[h], block_max[h]) for h in range(HPS)]
            for h in range(HPS):
                for r0 in range(0, TK, SOFTMAX_ROWS):
                    rows = slice(r0, r0 + SOFTMAX_ROWS)
                    p_ref[slot][h][rows, :] = jnp.exp2(s_ref[slot][h][rows, :] - m_new[h]).astype(BF16)
            return m_new, [jnp.exp2(m[h] - m_new[h]) for h in range(HPS)]

        def values(j, slot, alpha, gate=None):
            for h in range(HPS):
                pv = _dot(vt_ref[j, h * V_ROWS:(h + 1) * V_ROWS, :], p_ref[slot][h][...])
                acc_ref[h][...] = alpha[h] * acc_ref[h][...] + (pv if gate is None else gate * pv)

        m0 = [jnp.full((1, TQ), M_INIT, F32)] * HPS
        a0 = [jnp.zeros((1, TQ), F32)] * HPS

        def pair(u, state):
            m, alpha, bm_t = list(state[0:HPS]), list(state[HPS:2 * HPS]), list(state[2 * HPS:3 * HPS])
            t = 2 * u
            m, alpha_t = softmax(0, m, bm_t)
            values(jnp.maximum(t - 1, 0), 1, alpha, jnp.where(t > 0, 1.0, 0.0))
            bm_t1 = scores(t + 1, 1)
            m, alpha_t1 = softmax(1, m, bm_t1)
            values(t, 0, alpha_t)
            bm_t2 = scores(t + 2, 0)
            return (*m, *alpha_t1, *bm_t2)

        state = lax.fori_loop(0, i // 2, lambda v, st: pair(2 * v + 1, pair(2 * v, st)), (*m0, *a0, *bm0))
        state = lax.fori_loop(i - i % 2, i, pair, state)
        m, alpha = list(state[0:HPS]), list(state[HPS:2 * HPS])
        d0 = 2 * i
        values(jnp.maximum(d0 - 1, 0), 1, alpha, jnp.where(i > 0, 1.0, 0.0))
        finalize(jnp.where(i > 0, i - 1, n_tiles - 1), 1 - par)
        nxt = jnp.minimum(i + 1, n_tiles - 1)

        m_d0, alpha_d0 = [], []
        for h in range(HPS):
            s = s_ref[0][h][...]
            s = jnp.concatenate([jnp.where(diff_ref[:, 0:TK] >= 0, s[:, 0:TK], -jnp.inf), s[:, TK:]], axis=1)
            m_d0.append(jnp.maximum(m[h], jnp.max(s, axis=0, keepdims=True)))
            p_ref[0][h][...] = jnp.exp2(s - m_d0[h]).astype(BF16)
            alpha_d0.append(jnp.exp2(m[h] - m_d0[h]))
        bm_next = tile_scores(nxt, 0, 0)

        alpha_d1 = []
        for h in range(HPS):
            s = jnp.where(diff_ref[:, 0:TK] >= 0, sd_ref[h][...], -jnp.inf)
            m_old = m_d0[h][:, TK:]
            m_new = jnp.maximum(m_old, jnp.max(s, axis=0, keepdims=True))
            pd_ref[h][...] = jnp.exp2(s - m_new).astype(BF16)
            alpha_d1.append(jnp.exp2(m_old - m_new))
        last_diag_scores(nxt)

        values(d0, 0, alpha_d0)
        for h in range(HPS):
            acc_ref[h][:, TK:] = alpha_d1[h] * acc_ref[h][:, TK:] + _dot(
                vt_ref[d0 + 1, h * V_ROWS:(h + 1) * V_ROWS, :], pd_ref[h][...])
        return tuple(bm_next)

    last_diag_scores(0)
    bm = lax.fori_loop(0, n_tiles // 2, lambda a, st: q_tile(2 * a + 1, 1, q_tile(2 * a, 0, st)),
                       tuple(tile_scores(0, 0, 0)))
    if n_tiles % 2:
        q_tile(n_tiles - 1, 0, bm)
    finalize(n_tiles - 1, (n_tiles - 1) % 2)


def _attention(q_t, k, v_t, batch, seq):
    n = k.shape[0]
    groups = N_HEADS // HPS
    return pl.pallas_call(
        functools.partial(_attn_kernel, n_tiles=seq // TQ),
        grid=(batch, groups),
        in_specs=[pl.BlockSpec((seq // TQ, HPS * HEAD_PAD, TQ), lambda b, p: (b, p, 0)),
                  pl.BlockSpec((seq, HPS * HEAD_PAD), lambda b, p: (b, p)),
                  pl.BlockSpec((seq // TK, HPS * V_ROWS, TK), lambda b, p: (b, p, 0))],
        out_specs=pl.BlockSpec((seq, HPS * V_DIM), lambda b, p: (b, p)),
        out_shape=jax.ShapeDtypeStruct((n, N_HEADS * V_DIM), BF16),
        scratch_shapes=[pltpu.VMEM((TK, TQ), F32)] * (2 * HPS) + [pltpu.VMEM((TK, TK), F32)] * HPS
        + [pltpu.VMEM((TK, TQ), BF16)] * (2 * HPS) + [pltpu.VMEM((TK, TK), BF16)] * HPS
        + [pltpu.VMEM((V_ROWS, TQ), F32)] * (2 * HPS) + [pltpu.VMEM((TK, TQ), jnp.int32)],
        compiler_params=pltpu.CompilerParams(dimension_semantics=("arbitrary", "arbitrary"),
                                             vmem_limit_bytes=VMEM_LIMIT),
        name="attention",
    )(q_t, k, v_t)


def _layer_norm(x, g, b):
    mu = jnp.mean(x, axis=-1, keepdims=True)
    d = x - mu
    var = jnp.mean(jnp.square(d), axis=-1, keepdims=True)
    return d * lax.rsqrt(var + NORM_EPS) * g + b


def _mix_and_norm(o_ref, hc_ref, x_ref, gm_ref, gf_ref, wo_ref, g1_ref, b1_ref):
    o = o_ref[...].astype(F32)
    mla = _rms(o[:, :MLA_WIDTH], gm_ref[...])
    fox = _rms(o[:, MLA_WIDTH:], gf_ref[...])
    mixed = jnp.concatenate([mla.astype(BF16), fox.astype(BF16), hc_ref[...]], axis=-1)
    y = _dot(mixed, wo_ref[...])
    return _layer_norm(ALPHA * x_ref[...] + y, g1_ref[...], b1_ref[...])


def _router_layer_kernel(o_ref, hc_ref, x_ref, gm_ref, gf_ref, wo_ref, g1_ref, b1_ref, rw_ref,
                         x1_ref, route_ref, counts_ref, cnt_ref, upper_ref):
    x1 = _mix_and_norm(o_ref, hc_ref, x_ref, gm_ref, gf_ref, wo_ref, g1_ref, b1_ref)
    x1_ref[...] = x1
    rw = rw_ref[...]
    x_hi = x1.astype(BF16)
    x_lo = (x1 - x_hi.astype(F32)).astype(BF16)
    w_hi = rw.astype(BF16)
    w_lo = (rw - w_hi.astype(F32)).astype(BF16)
    both = _dot(x_hi, jnp.concatenate([w_hi, w_lo], axis=1))
    logits = both[:, :LANES] + (_dot(x_lo, w_hi) + both[:, LANES:])
    tm = logits.shape[0]
    lg = logits.T[0:N_EXPERTS, :]
    row = lax.broadcasted_iota(jnp.int32, lg.shape, 0)
    v1 = jnp.max(lg, axis=0, keepdims=True)
    i1 = jnp.min(jnp.where(lg == v1, row, N_EXPERTS), axis=0, keepdims=True)
    rest_l = jnp.where(row == i1, -jnp.inf, lg)
    v2 = jnp.max(rest_l, axis=0, keepdims=True)
    i2 = jnp.min(jnp.where(rest_l == v2, row, N_EXPERTS), axis=0, keepdims=True)
    e2 = jnp.exp(v2 - v1)
    den = 1.0 + e2

    @pl.when(pl.program_id(0) == 0)
    def _():
        cnt_ref[...] = jnp.zeros_like(cnt_ref)
        r_i = lax.broadcasted_iota(jnp.int32, (tm, tm), 0)
        c_i = lax.broadcasted_iota(jnp.int32, (tm, tm), 1)
        upper_ref[...] = jnp.where(r_i < c_i, 1.0, 0.0).astype(BF16)

    sel = jnp.where(row == i1, 1.0, jnp.where(row == i2, 1.0, 0.0))
    sel16 = jnp.concatenate([sel, jnp.zeros_like(sel)], axis=0).astype(BF16)
    before = cnt_ref[:, 0:1]
    rank = _dot(sel16, upper_ref[...])[0:N_EXPERTS, :] + before
    total = before + jnp.sum(sel, axis=1, keepdims=True)
    cnt_ref[...] = jnp.broadcast_to(total, cnt_ref.shape)
    counts_ref[...] = jnp.broadcast_to(total, counts_ref.shape)
    r1 = jnp.sum(jnp.where(row == i1, rank, 0.0), axis=0, keepdims=True)
    r2 = jnp.sum(jnp.where(row == i2, rank, 0.0), axis=0, keepdims=True)
    rows = (i1.astype(F32), i2.astype(F32), r1, r2, 1.0 / den, e2 / den)
    route = jnp.zeros(lg.shape, F32)
    for c, v in enumerate(rows):
        route = jnp.where(row == c, v, route)
    route_ref[...] = route


def _router_layer(o, hc, x2d, gm, gf, w_out, g1, b1, router_w):
    n, d = x2d.shape
    tm = min(TM_OUT, n)
    row = lambda a: a.reshape(1, -1)
    tok = lambda w: pl.BlockSpec((tm, w), lambda i: (i, 0))
    consts = [row(gm), row(gf), w_out.astype(BF16), row(g1), row(b1),
              jnp.pad(router_w, ((0, 0), (0, LANES - N_EXPERTS)))]
    return pl.pallas_call(
        _router_layer_kernel,
        grid=(n // tm,),
        in_specs=[tok(o.shape[1]), tok(CONV_CH), tok(d)] + [_const_spec(c.shape) for c in consts],
        out_specs=[tok(d), pl.BlockSpec((N_EXPERTS, tm), lambda i: (0, i)),
                   pl.BlockSpec((N_EXPERTS, LANES), lambda i: (0, 0))],
        out_shape=[jax.ShapeDtypeStruct((n, d), F32), jax.ShapeDtypeStruct((N_EXPERTS, n), F32),
                   jax.ShapeDtypeStruct((N_EXPERTS, LANES), F32)],
        scratch_shapes=[pltpu.VMEM((N_EXPERTS, LANES), F32), pltpu.VMEM((tm, tm), BF16)],
        compiler_params=pltpu.CompilerParams(dimension_semantics=("arbitrary",),
                                             vmem_limit_bytes=VMEM_LIMIT),
        name="router_layer",
    )(o, hc, x2d, *consts)


def _swiglu_tile(xb, w1, w3, w2):
    h1 = _dot(xb, w1)
    h3 = _dot(xb, w3)
    hid = (h1 * jax.nn.sigmoid(h1) * h3).astype(BF16)
    return _dot(hid, w2)


def _swiglu_chunked(xb, w1_ref, w3_ref, w2_ref):
    f = w1_ref.shape[1]
    out = None
    for c0 in range(0, f, FF_CHUNK):
        c1 = min(c0 + FF_CHUNK, f)
        part = _swiglu_tile(xb, w1_ref[:, c0:c1], w3_ref[:, c0:c1], w2_ref[c0:c1, :])
        out = part if out is None else out + part
    return out


def _dense_layer_kernel(o_ref, hc_ref, x_ref, gm_ref, gf_ref, wo_ref, g1_ref, b1_ref,
                        w1_ref, w3_ref, w2_ref, g2_ref, b2_ref, out_ref):
    x1 = _mix_and_norm(o_ref, hc_ref, x_ref, gm_ref, gf_ref, wo_ref, g1_ref, b1_ref)
    ff = _swiglu_chunked(x1.astype(BF16), w1_ref, w3_ref, w2_ref)
    out_ref[...] = _layer_norm(ALPHA * x1 + ff, g2_ref[...], b2_ref[...])


def _dense_layer(o, hc, x2d, gm, gf, w_out, g1, b1, w1, w3, w2, g2, b2):
    n, d = x2d.shape
    tm = min(TM_FFN, n)
    row = lambda a: a.reshape(1, -1)
    tok = lambda w: pl.BlockSpec((tm, w), lambda i: (i, 0))
    consts = [row(gm), row(gf), w_out.astype(BF16), row(g1), row(b1),
              w1.astype(BF16), w3.astype(BF16), w2.astype(BF16), row(g2), row(b2)]
    return pl.pallas_call(
        _dense_layer_kernel,
        grid=(n // tm,),
        in_specs=[tok(o.shape[1]), tok(CONV_CH), tok(d)] + [_const_spec(c.shape) for c in consts],
        out_specs=tok(d),
        out_shape=jax.ShapeDtypeStruct((n, d), F32),
        compiler_params=pltpu.CompilerParams(dimension_semantics=("arbitrary",),
                                             vmem_limit_bytes=VMEM_LIMIT),
        name="dense_layer",
    )(o, hc, x2d, *consts)


def _to_row_tiles(ref, x):
    for c in range(ROW_TILE):
        ref[pl.ds(c, x.shape[0], stride=ROW_TILE), :] = x[:, c * LANES:(c + 1) * LANES]


def _from_row_tiles(ref, t):
    return jnp.concatenate([ref[pl.ds(c, t, stride=ROW_TILE), :] for c in range(ROW_TILE)], axis=-1)


def _row_tile(ref, r):
    return ref.at[pl.ds(pl.multiple_of(r * ROW_TILE, ROW_TILE), ROW_TILE)]


def _dispatch_kernel(d1_ref, d2_ref, se_ref, x_ref, xs_ref, xr, zbuf, sem, *, tm, tr):
    i = pl.program_id(0)

    @pl.when(i == 0)
    def _():
        zbuf[...] = jnp.zeros_like(zbuf)
        for e in range(N_EXPERTS):
            end = se_ref[e]
            start_e = se_ref[e - 1] if e else 0

            for first, live in ((end - tr, end > start_e),
                                (se_ref[N_EXPERTS - 1] + e * tr,
                                 (se_ref[N_EXPERTS - 1] + e * tr) * ROW_TILE < xs_ref.shape[0])):
                @pl.when(live)
                def _():
                    rows = pl.ds(pl.multiple_of(first * ROW_TILE, ROW_TILE), tr * ROW_TILE)
                    fill = pltpu.make_async_copy(zbuf, xs_ref.at[rows], sem.at[2])
                    fill.start()
                    fill.wait()

    slot = i % 2
    _to_row_tiles(xr.at[slot], x_ref[...])
    base = i * tm

    def start(r, c):
        src = _row_tile(xr.at[slot], r)
        pltpu.make_async_copy(src, _row_tile(xs_ref, d1_ref[base + r]), sem.at[slot]).start()
        pltpu.make_async_copy(src, _row_tile(xs_ref, d2_ref[base + r]), sem.at[slot]).start(priority=1)
        return c

    def wait_step(s):
        def wait(r, c):
            for _ in range(2):
                pltpu.make_async_copy(_row_tile(xr.at[s], 0), _row_tile(xs_ref, 0), sem.at[s]).wait()
            return c

        lax.fori_loop(0, tm, wait, 0, unroll=8)

    lax.fori_loop(0, tm, start, 0, unroll=8)

    @pl.when(i > 0)
    def _():
        wait_step(1 - slot)

    @pl.when(i == pl.num_programs(0) - 1)
    def _():
        wait_step(slot)


def _expert_kernel(te_ref, blk_ref, nu_ref, xs_ref, w1_ref, w3_ref, w2_ref, ys_ref, *, tr):
    del te_ref, blk_ref
    used = pl.program_id(0) < nu_ref[0]

    @pl.when(used)
    def _():
        xb = _from_row_tiles(xs_ref, tr).astype(BF16)
        _to_row_tiles(ys_ref, _swiglu_chunked(xb, w1_ref.at[0], w3_ref.at[0], w2_ref.at[0]))

    @pl.when(jnp.logical_not(used))
    def _():
        ys_ref[...] = jnp.zeros_like(ys_ref)


def _combine_kernel(d1_ref, d2_ref, x_ref, route_ref, g_ref, b_ref, ys_ref, o_ref, ybuf, sem, *, tm):
    i = pl.program_id(0)
    n_steps = pl.num_programs(0)

    def issue(tile, slot):
        base = tile * tm

        def start(r, c):
            pltpu.make_async_copy(_row_tile(ys_ref, d1_ref[base + r]), _row_tile(ybuf.at[slot, 0], r),
                                  sem.at[slot]).start()
            pltpu.make_async_copy(_row_tile(ys_ref, d2_ref[base + r]), _row_tile(ybuf.at[slot, 1], r),
                                  sem.at[slot]).start(priority=1)
            return c

        lax.fori_loop(0, tm, start, 0, unroll=8)

    @pl.when(i == 0)
    def _():
        issue(0, 0)

    @pl.when(i + 1 < n_steps)
    def _():
        issue(i + 1, (i + 1) % 2)

    slot = i % 2

    def wait(r, c):
        for k in range(2):
            pltpu.make_async_copy(_row_tile(ys_ref, 0), _row_tile(ybuf.at[slot, k], 0), sem.at[slot]).wait()
        return c

    lax.fori_loop(0, tm, wait, 0, unroll=8)
    route = route_ref[...]
    gates = jnp.concatenate([route, jnp.zeros((LANES - route.shape[0], tm), F32)], axis=0).T
    ff = (gates[:, 4:5] * _from_row_tiles(ybuf.at[slot, 0], tm)
          + gates[:, 5:6] * _from_row_tiles(ybuf.at[slot, 1], tm))
    o_ref[...] = _layer_norm(ALPHA * x_ref[...] + ff, g_ref[...], b_ref[...])


def _moe_ffn(x2d, route, counts, w1, w3, w2, g, b):
    n, d = x2d.shape
    n_exp, _, f = w1.shape
    tr = min(TR_MOE, n)
    tm = min(TM_MOE, n)
    n_pad = 2 * n + n_exp * tr
    n_tiles = n_pad // tr
    i32 = jnp.int32

    cnt = counts[:, 0].astype(i32)
    seg = (cnt + tr - 1) // tr * tr
    seg_end = jnp.cumsum(seg)
    seg_start = seg_end - seg
    e1, e2 = route[0].astype(i32), route[1].astype(i32)
    dest1 = seg_start[e1] + route[2].astype(i32)
    dest2 = seg_start[e2] + route[3].astype(i32)
    n_used = jnp.maximum(seg_end[-1] // tr, 1)
    tile = jnp.minimum(jnp.arange(n_tiles, dtype=i32), n_used - 1)
    tile_expert = jnp.minimum(jnp.sum(tile[:, None] * tr >= seg_end[None, :], axis=1), n_exp - 1).astype(i32)

    cparams = pltpu.CompilerParams(dimension_semantics=("arbitrary",), vmem_limit_bytes=VMEM_LIMIT)
    assert d == ROW_TILE * LANES
    xs = pl.pallas_call(
        functools.partial(_dispatch_kernel, tm=tm, tr=tr),
        grid_spec=pltpu.PrefetchScalarGridSpec(
            num_scalar_prefetch=3, grid=(n // tm,),
            in_specs=[pl.BlockSpec((tm, d), lambda i, *_: (i, 0))],
            out_specs=pl.BlockSpec(memory_space=pl.ANY),
            scratch_shapes=[pltpu.VMEM((2, tm * ROW_TILE, LANES), F32), pltpu.VMEM((tr * ROW_TILE, LANES), F32),
                            pltpu.SemaphoreType.DMA((3,))]),
        out_shape=jax.ShapeDtypeStruct((n_pad * ROW_TILE, LANES), F32),
        compiler_params=cparams,
        name="moe_dispatch",
    )(dest1, dest2, seg_end.astype(i32), x2d)

    ys = pl.pallas_call(
        functools.partial(_expert_kernel, tr=tr),
        grid_spec=pltpu.PrefetchScalarGridSpec(
            num_scalar_prefetch=3, grid=(n_tiles,),
            in_specs=[pl.BlockSpec((tr * ROW_TILE, LANES), lambda i, te, blk, nu: (blk[i], 0)),
                      pl.BlockSpec((1, d, f), lambda i, te, blk, nu: (te[i], 0, 0)),
                      pl.BlockSpec((1, d, f), lambda i, te, blk, nu: (te[i], 0, 0)),
                      pl.BlockSpec((1, f, d), lambda i, te, blk, nu: (te[i], 0, 0))],
            out_specs=pl.BlockSpec((tr * ROW_TILE, LANES), lambda i, te, blk, nu: (i, 0))),
        out_shape=jax.ShapeDtypeStruct((n_pad * ROW_TILE, LANES), F32),
        compiler_params=cparams,
        name="moe_experts",
    )(tile_expert, tile, n_used.reshape(1), xs, w1.astype(BF16), w3.astype(BF16), w2.astype(BF16))

    row = lambda a: a.reshape(1, -1)
    return pl.pallas_call(
        functools.partial(_combine_kernel, tm=tm),
        grid_spec=pltpu.PrefetchScalarGridSpec(
            num_scalar_prefetch=2, grid=(n // tm,),
            in_specs=[pl.BlockSpec((tm, d), lambda i, *_: (i, 0)),
                      pl.BlockSpec((N_EXPERTS, tm), lambda i, *_: (0, i)),
                      pl.BlockSpec((1, d), lambda i, *_: (0, 0)),
                      pl.BlockSpec((1, d), lambda i, *_: (0, 0)),
                      pl.BlockSpec(memory_space=pl.ANY)],
            out_specs=pl.BlockSpec((tm, d), lambda i, *_: (i, 0)),
            scratch_shapes=[pltpu.VMEM((2, 2, tm * ROW_TILE, LANES), F32), pltpu.SemaphoreType.DMA((2,))]),
        out_shape=jax.ShapeDtypeStruct((n, d), F32),
        compiler_params=cparams,
        name="moe_combine",
    )(dest1, dest2, x2d, route, row(g), row(b), ys)


def kernel(x, positions, w_in, mla_q_norm_g, w_uq, mla_kv_norm_g, w_ukv, fox_forget_b, conv_w, conv_b,
           conv_norm_g, conv_norm_b, mla_out_norm_g, fox_out_norm_g, w_out, ln1_g, ln1_b, dense_w1,
           dense_w3, dense_w2, router_w, expert_w1, expert_w3, expert_w2, ln2_g, ln2_b):
    batch, seq, d = x.shape
    assert d == D_MODEL and seq % TQ == 0 and seq % min(TM_IN, seq) == 0
    depth = w_in.shape[0]
    tabs = _rope_tables(positions)
    h = x.reshape(batch * seq, d)
    pw_all = jax.vmap(_prep_inproj_weights)(w_in, w_uq, w_ukv, fox_forget_b)
    w_out, dense_w1, dense_w3, dense_w2, expert_w1, expert_w3, expert_w2 = (
        w.astype(BF16) for w in (w_out, dense_w1, dense_w3, dense_w2, expert_w1, expert_w3, expert_w2))
    for layer in range(depth):
        pw = {name: w[layer] for name, w in pw_all.items()}
        q_t, k, v_t, hc = _input_projection(
            h, tabs, pw, mla_q_norm_g[layer], mla_kv_norm_g[layer], conv_w[layer], conv_b[layer],
            conv_norm_g[layer], conv_norm_b[layer], seq)
        o = _attention(q_t, k, v_t, batch, seq)
        j = layer // 2
        if layer % 2 == 0:
            h = _dense_layer(o, hc, h, mla_out_norm_g[layer], fox_out_norm_g[layer], w_out[layer],
                             ln1_g[layer], ln1_b[layer], dense_w1[j], dense_w3[j], dense_w2[j],
                             ln2_g[layer], ln2_b[layer])
        else:
            h, route, counts = _router_layer(o, hc, h, mla_out_norm_g[layer], fox_out_norm_g[layer], w_out[layer],
                                             ln1_g[layer], ln1_b[layer], router_w[j])
            h = _moe_ffn(h, route, counts, expert_w1[j], expert_w3[j], expert_w2[j], ln2_g[layer],
                         ln2_b[layer])
    return h.reshape(batch, seq, d)
```
